```python
import math
import functools
import jax
import jax.numpy as jnp
from jax import lax
import numpy as np

D_MODEL = 1024
BATCH = 16
SEQ = 2048
DEPTH = 2

GRID_W = 64
CTX_LEN = 256
CHUNK = 64
EPS = 1e-6
N_MOD = 6

HY_WIDTH = D_MODEL // 4
HY_ORDER = 2
HY_SHORT = 3
HY_EMB = 33
HY_HIDDEN = 64
HY_FAST_DECAY = 0.3
HY_SLOW_DECAY = 1.5
HY_TARGET = 1e-2
GDN_HEADS = D_MODEL // 256
GDN_DK = 128
GDN_DV = 128
GDN_SHORT = 3
RET_HEADS = D_MODEL // 256
RET_DK = 64
RET_DV = 64
ROPE_BASE = 10000.0

MIX_WIDTH = HY_WIDTH + GDN_HEADS * GDN_DV + RET_HEADS * RET_DV
FFN_HIDDEN = ((8 * D_MODEL + 3 * 256 - 1) // (3 * 256)) * 256
IN_SIZES = (
    3 * HY_WIDTH,
    2 * GDN_HEADS * GDN_DK + GDN_HEADS * GDN_DV,
    GDN_HEADS * GDN_DV,
    2 * GDN_HEADS,
    2 * GDN_HEADS,
    RET_HEADS * RET_DK,
    RET_HEADS * RET_DK,
    RET_HEADS * RET_DV,
    RET_HEADS * RET_DV,
)
IN_WIDTH = sum(IN_SIZES)

kernel_name = 'hybrid_hyena_gdn_retention_dit'


def rmsnorm(x, g):
    xf = x.astype(jnp.float32)
    y = xf * lax.rsqrt(jnp.mean(xf * xf, axis=-1, keepdims=True) + EPS)
    return (y * g.astype(jnp.float32)).astype(x.dtype)


def modulate(h, shift, scale):
    return h * (1 + scale) + shift


def split_cols(p, sizes):
    parts, start = [], 0
    for s in sizes:
        parts.append(p[..., start:start + s])
        start += s
    return parts


def short_conv(x, w):
    k_w = w.shape[0]
    pad = k_w // 2
    L = x.shape[1]
    xp = jnp.pad(x, ((0, 0), (pad, pad), (0, 0)))
    return sum(xp[:, j:j + L] * w[j] for j in range(k_w))


def to_heads(t, n_heads, d_head):
    b, l, _ = t.shape
    return t.reshape(b, l, n_heads, d_head).transpose(0, 2, 1, 3).astype(jnp.float32)


def l2norm(x):
    return x * lax.rsqrt(jnp.sum(x * x, axis=-1, keepdims=True) + EPS)


def hyena_filters(L, lp):
    pos = jnp.arange(L, dtype=jnp.float32)
    t = jnp.linspace(0.0, 1.0, L, dtype=jnp.float32)
    bands = (HY_EMB - 1) // 2
    f = jnp.linspace(1e-4, bands - 1, bands, dtype=jnp.float32)
    ang = (2.0 * math.pi / L) * pos[:, None] * f[None, :]
    z = jnp.concatenate([t[:, None], jnp.cos(ang), -jnp.sin(ang)], axis=-1)
    h = jnp.sin(lp['hy_f_freq1'] * (z @ lp['hy_f_w1'] + lp['hy_f_b1']))
    h = jnp.sin(lp['hy_f_freq2'] * (h @ lp['hy_f_w2'] + lp['hy_f_b2']))
    h = (h @ lp['hy_f_w3']).astype(jnp.float32).reshape(L, HY_ORDER, 2, HY_WIDTH)
    max_decay = math.log(HY_TARGET) / HY_FAST_DECAY
    min_decay = math.log(HY_TARGET) / HY_SLOW_DECAY
    deltas = jnp.abs(jnp.linspace(min_decay, max_decay, HY_WIDTH, dtype=jnp.float32))
    window = jnp.exp(-t[:, None] * deltas[None, :])
    return h * window[:, None, None, :]


def two_sided_kernel(h_pos, h_neg):
    return jnp.concatenate([h_pos, jnp.zeros_like(h_pos[:1]), h_neg[:0:-1]], axis=0)


def long_conv(u, kern):
    L = u.shape[1]
    uf = jnp.fft.rfft(u, n=2 * L, axis=1)
    kf = jnp.fft.rfft(kern, axis=0)
    return jnp.fft.irfft(uf * kf[None], n=2 * L, axis=1)[:, :L]


def hyena_mixer(p, lp):
    L = p.shape[1]
    p = (short_conv(p, lp['hy_conv_w']) + lp['hy_conv_b']).astype(jnp.float32)
    v, x1, x2 = jnp.split(p, 3, axis=-1)
    h = hyena_filters(L, lp)
    bias_d = lp['hy_bias'].astype(jnp.float32)
    z = v
    for n, gate in enumerate((x1, x2)):
        kern = two_sided_kernel(h[:, n, 0], h[:, n, 1])
        z = gate * (long_conv(z, kern) + z * bias_d[n])
    return z


def gdn_prepare(qkv, a, b, lp):
    qkv = jax.nn.silu(short_conv(qkv, lp['gdn_conv_w']))
    q, k, v = split_cols(qkv, (GDN_HEADS * GDN_DK, GDN_HEADS * GDN_DK, GDN_HEADS * GDN_DV))
    q = l2norm(to_heads(q, GDN_HEADS, GDN_DK)) * (GDN_DK ** -0.5)
    k = l2norm(to_heads(k, GDN_HEADS, GDN_DK))
    v = to_heads(v, GDN_HEADS, GDN_DV)
    bsz, L, _ = a.shape
    a = a.astype(jnp.float32).reshape(bsz, L, 2, GDN_HEADS)
    b = b.astype(jnp.float32).reshape(bsz, L, 2, GDN_HEADS)
    a_log = lp['gdn_a_log'].astype(jnp.float32)
    dt_bias = lp['gdn_dt_bias'].astype(jnp.float32)
    g = -jnp.exp(a_log) * jax.nn.softplus(a + dt_bias)
    beta = jax.nn.sigmoid(b)
    return q, k, v, g.transpose(2, 0, 3, 1), beta.transpose(2, 0, 3, 1)


def gated_delta_chunk(q, k, v, g, beta, s0):
    bsz, nh, L, dk = q.shape
    dv = v.shape[-1]
    n = L // CHUNK
    if s0 is None:
        s0 = jnp.zeros((bsz, nh, dk, dv), jnp.float32)
    q, k, v = (t.reshape(bsz, nh, n, CHUNK, t.shape[-1]) for t in (q, k, v))
    g, beta = (t.reshape(bsz, nh, n, CHUNK) for t in (g, beta))
    gc = jnp.cumsum(g, axis=-1)
    idx = jnp.arange(CHUNK)
    incl = idx[:, None] >= idx[None, :]
    strict = idx[:, None] > idx[None, :]
    decay = jnp.exp(jnp.where(incl, gc[..., :, None] - gc[..., None, :], -jnp.inf))
    kb = k * beta[..., None]
    lmat = jnp.where(strict, jnp.einsum('bhncd,bhnsd->bhncs', kb, k) * decay, 0.0)
    eye = jnp.eye(CHUNK, dtype=jnp.float32)
    t_inv = lax.linalg.triangular_solve(eye + lmat, jnp.broadcast_to(eye, lmat.shape),
                                        left_side=True, lower=True, unit_diagonal=True)
    u = jnp.einsum('bhncs,bhnsv->bhncv', t_inv, v * beta[..., None])
    w = jnp.einsum('bhncs,bhnsd->bhncd', t_inv, kb * jnp.exp(gc)[..., None])
    attn = jnp.einsum('bhncd,bhnsd->bhncs', q, k) * decay
    qg = q * jnp.exp(gc)[..., None]
    kg = k * jnp.exp(gc[..., -1:] - gc)[..., None]
    glast = jnp.exp(gc[..., -1])

    def step(s, xs):
        u_n, w_n, attn_n, qg_n, kg_n, gl_n = xs
        v_new = u_n - jnp.einsum('bhck,bhkv->bhcv', w_n, s)
        o = jnp.einsum('bhck,bhkv->bhcv', qg_n, s) + jnp.einsum('bhcs,bhsv->bhcv', attn_n, v_new)
        s = s * gl_n[..., None, None] + jnp.einsum('bhck,bhcv->bhkv', kg_n, v_new)
        return s, o

    xs = tuple(jnp.moveaxis(t, 2, 0) for t in (u, w, attn, qg, kg, glast))
    s, o = lax.scan(step, s0, xs)
    return jnp.moveaxis(o, 0, 2).reshape(bsz, nh, L, dv), s


def axial_rope_angles(rows_n):
    row = jnp.repeat(jnp.arange(rows_n, dtype=jnp.float32), GRID_W)
    col = jnp.tile(jnp.arange(GRID_W, dtype=jnp.float32), rows_n)
    nf = RET_DK // 4
    inv = ROPE_BASE ** (-jnp.arange(nf, dtype=jnp.float32) / nf)
    ang = jnp.concatenate([row[:, None] * inv, col[:, None] * inv], axis=-1)
    return jnp.cos(ang), jnp.sin(ang)


def apply_rope(x, cos, sin):
    x1, x2 = jnp.split(x, 2, axis=-1)
    return jnp.concatenate([x1 * cos - x2 * sin, x1 * sin + x2 * cos], axis=-1)


def ret_prepare(q, k, v, rope):
    q = to_heads(q, RET_HEADS, RET_DK)
    k = to_heads(k, RET_HEADS, RET_DK) * (RET_DK ** -0.5)
    v = to_heads(v, RET_HEADS, RET_DV)
    if rope is not None:
        q, k = apply_rope(q, *rope), apply_rope(k, *rope)
    return q, k, v


def retention_chunk(q, k, v, s0, log_gamma):
    bsz, nh, L, dk = q.shape
    dv = v.shape[-1]
    n = L // CHUNK
    if s0 is None:
        s0 = jnp.zeros((bsz, nh, dk, dv), jnp.float32)
    q, k, v = (t.reshape(bsz, nh, n, CHUNK, t.shape[-1]) for t in (q, k, v))
    idx = jnp.arange(CHUNK, dtype=jnp.float32)
    rel = idx[:, None] - idx[None, :]
    dmat = jnp.exp(jnp.where(rel >= 0, rel * log_gamma[:, None, None], -jnp.inf))
    scores = jnp.einsum('bhncd,bhnsd->bhncs', q, k) * dmat[None, :, None]
    inner = jnp.einsum('bhncs,bhnsv->bhncv', scores, v)
    qd = q * jnp.exp((idx + 1.0)[None, :] * log_gamma[:, None])[None, :, None, :, None]
    kd = k * jnp.exp((CHUNK - 1.0 - idx)[None, :] * log_gamma[:, None])[None, :, None, :, None]
    g_chunk = jnp.exp(CHUNK * log_gamma)[None, :, None, None]

    def step(s, xs):
        qd_n, kd_n, v_n = xs
        o = jnp.einsum('bhck,bhkv->bhcv', qd_n, s)
        s = s * g_chunk + jnp.einsum('bhck,bhcv->bhkv', kd_n, v_n)
        return s, o

    xs = tuple(jnp.moveaxis(t, 2, 0) for t in (qd, kd, v))
    s, o_cross = lax.scan(step, s0, xs)
    o = inner + jnp.moveaxis(o_cross, 0, 2)
    return o.reshape(bsz, nh, L, dv), s


def prefix_scan(scan_fn, ctx_seq, lat_seq, reverse):
    if reverse:
        ctx_seq = tuple(jnp.flip(t, 2) for t in ctx_seq)
        lat_seq = tuple(jnp.flip(t, 2) for t in lat_seq)
    o_ctx, s_ctx = scan_fn(*ctx_seq, None)
    o_lat, _ = scan_fn(*lat_seq, s_ctx)
    if reverse:
        o_ctx, o_lat = jnp.flip(o_ctx, 2), jnp.flip(o_lat, 2)
    return o_ctx, o_lat


def gdn_output(o, z, norm_g):
    bsz, nh, L, dv = o.shape
    o = o.transpose(0, 2, 1, 3)
    z = z.astype(jnp.float32).reshape(bsz, L, nh, dv)
    y = o * lax.rsqrt(jnp.mean(o * o, axis=-1, keepdims=True) + EPS) * norm_g.astype(jnp.float32)
    return (y * jax.nn.silu(z)).reshape(bsz, L, nh * dv)


def ret_output(o, gate):
    bsz, nh, L, dv = o.shape
    o = o.transpose(0, 2, 1, 3)
    mu = jnp.mean(o, axis=-1, keepdims=True)
    var = jnp.mean(jnp.square(o - mu), axis=-1, keepdims=True)
    y = ((o - mu) * lax.rsqrt(var + EPS)).reshape(bsz, L, nh * dv)
    return y * jax.nn.silu(gate.astype(jnp.float32))


def token_mixers(h_lat, h_ctx, rope, lp, need_ctx):
    lat = split_cols(h_lat @ lp['w_in'], IN_SIZES)
    ctx = split_cols(h_ctx @ lp['w_in'], IN_SIZES)
    hy_lat = hyena_mixer(lat[0], lp)
    q_l, k_l, v_l, g_l, b_l = gdn_prepare(lat[1], lat[3], lat[4], lp)
    q_c, k_c, v_c, g_c, b_c = gdn_prepare(ctx[1], ctx[3], ctx[4], lp)
    gc_f, gl_f = prefix_scan(gated_delta_chunk, (q_c, k_c, v_c, g_c[0], b_c[0]),
                             (q_l, k_l, v_l, g_l[0], b_l[0]), reverse=False)
    gc_b, gl_b = prefix_scan(gated_delta_chunk, (q_c, k_c, v_c, g_c[1], b_c[1]),
                             (q_l, k_l, v_l, g_l[1], b_l[1]), reverse=True)
    log_gamma = jax.nn.log_sigmoid(lp['ret_decay_logit'].astype(jnp.float32))
    rq_l, rk_l, rv_l = ret_prepare(lat[5], lat[6], lat[7], rope)
    rq_c, rk_c, rv_c = ret_prepare(ctx[5], ctx[6], ctx[7], None)
    rc_f, rl_f = prefix_scan(functools.partial(retention_chunk, log_gamma=log_gamma[0]),
                             (rq_c, rk_c, rv_c), (rq_l, rk_l, rv_l), reverse=False)
    rc_b, rl_b = prefix_scan(functools.partial(retention_chunk, log_gamma=log_gamma[1]),
                             (rq_c, rk_c, rv_c), (rq_l, rk_l, rv_l), reverse=True)
    dt = h_lat.dtype
    y_lat = jnp.concatenate([hy_lat,
                             gdn_output(gl_f + gl_b, lat[2], lp['gdn_norm_g']),
                             ret_output(rl_f + rl_b, lat[8])], axis=-1).astype(dt) @ lp['w_out']
    y_ctx = None
    if need_ctx:
        y_ctx = jnp.concatenate([hyena_mixer(ctx[0], lp),
                                 gdn_output(gc_f + gc_b, ctx[2], lp['gdn_norm_g']),
                                 ret_output(rc_f + rc_b, ctx[8])], axis=-1).astype(dt) @ lp['w_out']
    return y_lat, y_ctx


def swiglu(h, lp):
    gate, up = jnp.split(h @ lp['ffn_w_in'], 2, axis=-1)
    return (jax.nn.silu(gate) * up) @ lp['ffn_w_out']


def trunk_layer(x, ctx, c, c_ctx, rope, lp, need_ctx):
    d = x.shape[-1]
    mod_lat = (jax.nn.silu(c) @ lp['mod_w'] + lp['mod_b']).reshape(c.shape[0], N_MOD, 1, d)
    mod_ctx = (jax.nn.silu(c_ctx) @ lp['mod_w'] + lp['mod_b']).reshape(N_MOD, d)
    h_lat = modulate(rmsnorm(x, lp['norm1_g']), mod_lat[:, 0], mod_lat[:, 1])
    h_ctx = modulate(rmsnorm(ctx, lp['norm1_g']), mod_ctx[0], mod_ctx[1])
    y_lat, y_ctx = token_mixers(h_lat, h_ctx, rope, lp, need_ctx)
    x = x + mod_lat[:, 2] * y_lat
    x = x + mod_lat[:, 5] * swiglu(modulate(rmsnorm(x, lp['norm2_g']), mod_lat[:, 3], mod_lat[:, 4]), lp)
    if need_ctx:
        ctx = ctx + mod_ctx[2] * y_ctx
        ctx = ctx + mod_ctx[5] * swiglu(modulate(rmsnorm(ctx, lp['norm2_g']), mod_ctx[3], mod_ctx[4]), lp)
    return x, ctx


def setup_inputs(seed: int = 0) -> dict:
    key = jax.random.key(seed)
    it = iter(jax.random.split(key, 32))
    f32 = jnp.float32

    def nrm(shape, scale):
        return scale * jax.random.normal(next(it), shape, f32)

    D = D_MODEL
    qkv_w = 2 * GDN_HEADS * GDN_DK + GDN_HEADS * GDN_DV
    gamma0 = 1.0 - 2.0 ** (-5.0 - jnp.arange(RET_HEADS, dtype=f32))
    dt0 = jnp.exp(jax.random.uniform(next(it), (DEPTH, 2, GDN_HEADS), f32, math.log(1e-3), math.log(1e-1)))
    return {
        'x': nrm((BATCH, SEQ, D), 1.0),
        'c': nrm((BATCH, D), 1.0),
        'ctx': nrm((BATCH, CTX_LEN, D), 1.0),
        'c_ctx': nrm((D,), 1.0),
        'mod_w': nrm((DEPTH, D, N_MOD * D), 0.5 * D ** -0.5),
        'mod_b': nrm((DEPTH, N_MOD * D), 0.02),
        'norm1_g': 1.0 + nrm((DEPTH, D), 0.02),
        'w_in': nrm((DEPTH, D, IN_WIDTH), D ** -0.5),
        'hy_conv_w': nrm((DEPTH, HY_SHORT, 3 * HY_WIDTH), HY_SHORT ** -0.5),
        'hy_conv_b': nrm((DEPTH, 3 * HY_WIDTH), 0.02),
        'hy_f_w1': nrm((DEPTH, HY_EMB, HY_HIDDEN), HY_EMB ** -0.5),
        'hy_f_b1': nrm((DEPTH, HY_HIDDEN), 0.02),
        'hy_f_freq1': 1.0 + nrm((DEPTH, HY_HIDDEN), 0.02),
        'hy_f_w2': nrm((DEPTH, HY_HIDDEN, HY_HIDDEN), HY_HIDDEN ** -0.5),
        'hy_f_b2': nrm((DEPTH, HY_HIDDEN), 0.02),
        'hy_f_freq2': 1.0 + nrm((DEPTH, HY_HIDDEN), 0.02),
        'hy_f_w3': nrm((DEPTH, HY_HIDDEN, HY_ORDER * 2 * HY_WIDTH), 0.05 * HY_HIDDEN ** -0.5),
        'hy_bias': nrm((DEPTH, HY_ORDER, HY_WIDTH), 0.5),
        'gdn_conv_w': nrm((DEPTH, GDN_SHORT, qkv_w), GDN_SHORT ** -0.5),
        'gdn_a_log': jnp.log(jax.random.uniform(next(it), (DEPTH, 2, GDN_HEADS), f32, 1.0, 16.0)),
        'gdn_dt_bias': dt0 + jnp.log(-jnp.expm1(-dt0)),
        'gdn_norm_g': 1.0 + nrm((DEPTH, GDN_DV), 0.02),
        'ret_decay_logit': jnp.log(gamma0 / (1.0 - gamma0)) + nrm((DEPTH, 2, RET_HEADS), 0.1),
        'w_out': nrm((DEPTH, MIX_WIDTH, D), MIX_WIDTH ** -0.5),
        'norm2_g': 1.0 + nrm((DEPTH, D), 0.02),
        'ffn_w_in': nrm((DEPTH, D, 2 * FFN_HIDDEN), D ** -0.5),
        'ffn_w_out': nrm((DEPTH, FFN_HIDDEN, D), FFN_HIDDEN ** -0.5),
        'final_norm_g': 1.0 + nrm((D,), 0.02),
    }


def reference(x, c, ctx, c_ctx, mod_w, mod_b, norm1_g, w_in, hy_conv_w, hy_conv_b, hy_f_w1, hy_f_b1,
              hy_f_freq1, hy_f_w2, hy_f_b2, hy_f_freq2, hy_f_w3, hy_bias, gdn_conv_w, gdn_a_log,
              gdn_dt_bias, gdn_norm_g, ret_decay_logit, w_out, norm2_g, ffn_w_in, ffn_w_out, final_norm_g):
    rows_n = x.shape[1] // GRID_W
    rope = axial_rope_angles(rows_n)
    for i in range(DEPTH):
        lp = {
            'mod_w': mod_w[i], 'mod_b': mod_b[i], 'norm1_g': norm1_g[i], 'w_in': w_in[i],
            'hy_conv_w': hy_conv_w[i], 'hy_conv_b': hy_conv_b[i],
            'hy_f_w1': hy_f_w1[i], 'hy_f_b1': hy_f_b1[i], 'hy_f_freq1': hy_f_freq1[i],
            'hy_f_w2': hy_f_w2[i], 'hy_f_b2': hy_f_b2[i], 'hy_f_freq2': hy_f_freq2[i],
            'hy_f_w3': hy_f_w3[i], 'hy_bias': hy_bias[i],
            'gdn_conv_w': gdn_conv_w[i], 'gdn_a_log': gdn_a_log[i], 'gdn_dt_bias': gdn_dt_bias[i],
            'gdn_norm_g': gdn_norm_g[i], 'ret_decay_logit': ret_decay_logit[i],
            'w_out': w_out[i], 'norm2_g': norm2_g[i], 'ffn_w_in': ffn_w_in[i], 'ffn_w_out': ffn_w_out[i],
        }
        x, ctx = trunk_layer(x, ctx, c, c_ctx, rope, lp, need_ctx=(i < DEPTH - 1))
    return rmsnorm(x, final_norm_g)
```

```python
import functools
import math

import jax
import jax.numpy as jnp
from jax import lax
from jax.experimental import pallas as pl
from jax.experimental.pallas import tpu as pltpu

D_MODEL = 1024
DEPTH = 2
GRID_W = 64
EPS = 1e-6
N_MOD = 6

HY_WIDTH = D_MODEL // 4
HY_ORDER = 2
HY_EMB = 33
HY_FAST_DECAY = 0.3
HY_SLOW_DECAY = 1.5
HY_TARGET = 1e-2
GDN_HEADS = D_MODEL // 256
GDN_DK = 128
GDN_DV = 128
RET_HEADS = D_MODEL // 256
RET_DK = 64
RET_DV = 64
ROPE_BASE = 10000.0
FFN_HIDDEN = ((8 * D_MODEL + 3 * 256 - 1) // (3 * 256)) * 256

GDN_CHUNK = 64
GDN_ROWS = 256
GDN_GROUP = 4
RET_CHUNK = 256
HY_BLOCK = 256
LANES = 128
SUBLANES = 8

GDN_QKV = 2 * GDN_HEADS * GDN_DK + GDN_HEADS * GDN_DV
OFF_HY = 0
OFF_QKV = OFF_HY + 3 * HY_WIDTH
OFF_Z = OFF_QKV + GDN_QKV
OFF_RET = OFF_Z + GDN_HEADS * GDN_DV
OFF_AB = OFF_RET + 4 * RET_HEADS * RET_DK
IN_PAD = OFF_AB + LANES

VMEM_LIMIT = 56 * 1024 * 1024

f32 = jnp.float32
bf16 = jnp.bfloat16


def _cparams(sem):
    return pltpu.CompilerParams(dimension_semantics=sem, vmem_limit_bytes=VMEM_LIMIT)


def _const_spec(shape):
    nd = len(shape)
    return pl.BlockSpec(shape, lambda *_: (0,) * nd, pipeline_mode=pl.Buffered(1))


def _rms(x, g):
    return x * lax.rsqrt(jnp.mean(x * x, axis=-1, keepdims=True) + EPS) * g


def _silu(x):
    return x * jax.nn.sigmoid(x)


def _dot(a, b):
    return jnp.dot(a.astype(bf16), b.astype(bf16), preferred_element_type=f32)


def _dot_nt(a, b):
    return lax.dot_general(a.astype(bf16), b.astype(bf16), (((1,), (1,)), ((), ())),
                           preferred_element_type=f32)


def _dot_tn(a, b):
    return lax.dot_general(a.astype(bf16), b.astype(bf16), (((0,), (0,)), ((), ())),
                           preferred_element_type=f32)


def _mod_kernel(c_ref, w_ref, b_ref, o_ref):
    o_ref[...] = _dot(_silu(c_ref[...]), w_ref[...]) + b_ref[...]


def _mod_vectors(cs, w16, b):
    rows, d = cs.shape
    n = w16.shape[1]
    tn = 1536
    return pl.pallas_call(
        _mod_kernel,
        grid=(n // tn,),
        in_specs=[pl.BlockSpec((rows, d), lambda j: (0, 0)),
                  pl.BlockSpec((d, tn), lambda j: (0, j)),
                  pl.BlockSpec((1, tn), lambda j: (0, j))],
        out_specs=pl.BlockSpec((rows, tn), lambda j: (0, j)),
        out_shape=jax.ShapeDtypeStruct((rows, n), f32),
        compiler_params=_cparams(("parallel",)),
        name="mod_vectors",
    )(cs, w16, b)


def _in_proj_kernel(x_ref, g_ref, mod_ref, w_ref, o_ref):
    x = x_ref[0]
    mod = mod_ref[0]
    h = _rms(x, g_ref[...]) * (1.0 + mod[1:2]) + mod[0:1]
    h16 = h.astype(bf16)
    n = w_ref.shape[1]
    step = 512
    for j in range(0, n, step):
        e = min(j + step, n)
        o_ref[0, :, j:e] = jnp.dot(h16, w_ref[:, j:e], preferred_element_type=f32)


def _in_proj(x, g, mod, w16, tm):
    b, t, d = x.shape
    n = w16.shape[1]
    return pl.pallas_call(
        _in_proj_kernel,
        grid=(b, t // tm),
        in_specs=[pl.BlockSpec((1, tm, d), lambda i, j: (i, j, 0)),
                  _const_spec((1, d)),
                  pl.BlockSpec((1, N_MOD, d), lambda i, j: (i, 0, 0)),
                  _const_spec((d, n))],
        out_specs=pl.BlockSpec((1, tm, n), lambda i, j: (i, j, 0)),
        out_shape=jax.ShapeDtypeStruct((b, t, n), f32),
        compiler_params=_cparams(("parallel", "parallel")),
        name="in_proj",
    )(x, g, mod, w16)


def _out_ffn_kernel(x_ref, hy_ref, gd_ref, rt_ref, mod_ref, wo_ref, g2_ref, w1_ref, w2_ref, gf_ref, o_ref,
                    *, final):
    x = x_ref[0]
    mod = mod_ref[0]
    o0, o1, o2 = HY_WIDTH, HY_WIDTH + GDN_HEADS * GDN_DV, D_MODEL
    y = (jnp.dot(hy_ref[0].astype(bf16), wo_ref[0:o0, :], preferred_element_type=f32)
         + jnp.dot(gd_ref[0].astype(bf16), wo_ref[o0:o1, :], preferred_element_type=f32)
         + jnp.dot(rt_ref[0].astype(bf16), wo_ref[o1:o2, :], preferred_element_type=f32))
    x = x + mod[2:3] * y
    h16 = (_rms(x, g2_ref[...]) * (1.0 + mod[4:5]) + mod[3:4]).astype(bf16)
    step = 256
    acc = jnp.zeros_like(x)
    for j in range(0, FFN_HIDDEN, step):
        gate = jnp.dot(h16, w1_ref[:, j:j + step], preferred_element_type=f32)
        up = jnp.dot(h16, w1_ref[:, FFN_HIDDEN + j:FFN_HIDDEN + j + step], preferred_element_type=f32)
        acc = acc + jnp.dot((_silu(gate) * up).astype(bf16), w2_ref[j:j + step, :], preferred_element_type=f32)
    x = x + mod[5:6] * acc
    if final:
        x = _rms(x, gf_ref[...])
    o_ref[0] = x


def _out_ffn(x, hy, gd, rt, mod, wo16, g2, w116, w216, gf, tm, final):
    b, t, d = x.shape
    tok = lambda w: pl.BlockSpec((1, tm, w), lambda i, j: (i, j, 0))
    return pl.pallas_call(
        functools.partial(_out_ffn_kernel, final=final),
        grid=(b, t // tm),
        in_specs=[tok(d), tok(hy.shape[-1]), tok(gd.shape[-1]), tok(rt.shape[-1]),
                  pl.BlockSpec((1, N_MOD, d), lambda i, j: (i, 0, 0)),
                  _const_spec(wo16.shape), _const_spec((1, d)), _const_spec(w116.shape),
                  _const_spec(w216.shape), _const_spec((1, d))],
        out_specs=tok(d),
        out_shape=jax.ShapeDtypeStruct((b, t, d), f32),
        compiler_params=_cparams(("parallel", "parallel")),
        name="out_ffn",
    )(x, hy, gd, rt, mod, wo16, g2, w116, w216, gf)


def _hyena_kernel(v_ref, x1_ref, x2_ref, k_ref, b_ref, o_ref, acc_ref, *, nb, cb):
    hb = HY_BLOCK
    rb = v_ref.shape[1] // nb

    def conv(u16, kext):
        for d in range(-(nb - 1), nb):
            dd = d % (2 * nb)
            win = jnp.broadcast_to(kext[:, hb * dd:hb * dd + 2 * hb], (hb, 2 * hb))
            toep = pltpu.roll(win, 0, 1, stride=1, stride_axis=0)[:, hb:].astype(bf16)
            t0, t1 = max(0, d), min(nb - 1, nb - 1 + d) + 1
            contrib = jnp.dot(u16[rb * (t0 - d):rb * (t1 - d)], toep, preferred_element_type=f32)
            if d == -(nb - 1):
                acc_ref[...] = jnp.zeros_like(acc_ref)
            acc_ref[rb * t0:rb * t1, :] += contrib
        return acc_ref[...]

    def body(c, carry):
        z = v_ref[c]
        for n, gate_ref in enumerate((x1_ref, x2_ref)):
            y = conv(z.astype(bf16), k_ref[c, n])
            z = gate_ref[c] * (y + z * b_ref[c, n])
        o_ref[c] = z
        return carry

    lax.fori_loop(0, cb, body, 0)


def _hyena_conv(p3, kext, bias, nb, cb=8):
    _, c, r, hb = p3.shape
    klen = kext.shape[-1]
    sig = lambda s: pl.BlockSpec((None, cb, r, hb), lambda i, s=s: (s, i, 0, 0))
    return pl.pallas_call(
        functools.partial(_hyena_kernel, nb=nb, cb=cb),
        grid=(c // cb,),
        in_specs=[sig(0), sig(1), sig(2),
                  pl.BlockSpec((cb, 2, 1, klen), lambda i: (i, 0, 0, 0)),
                  pl.BlockSpec((cb, 2, 1, hb), lambda i: (i, 0, 0, 0))],
        out_specs=pl.BlockSpec((cb, r, hb), lambda i: (i, 0, 0)),
        out_shape=jax.ShapeDtypeStruct((c, r, hb), f32),
        scratch_shapes=[pltpu.VMEM((r, hb), f32)],
        compiler_params=_cparams(("parallel",)),
        name=f"hyena_conv_nb{nb}",
    )(p3, p3, p3, kext, bias)


def _gdn_kernel(ql_ref, kl_ref, vl_ref, zl_ref, abl_ref, qc_ref, kc_ref, vc_ref, zc_ref, abc_ref,
                cwq_ref, cwk_ref, cwv_ref, pa_ref, pd_ref, ng_ref,
                ol_ref, oc_ref,
                q_s, k_s, v_s, cf_s, u_s, wq_s, kgt_s, at_s, egl_s, o_s, st_s):
    c = GDN_CHUNK
    rb = GDN_ROWS
    tc, tl = qc_ref.shape[1], ql_ref.shape[1]
    ncc, nc = tc // c, (tc + tl) // c
    dk = GDN_DK

    def conv_rows(x_ref, w_ref, r0, t):
        x = x_ref[0, pl.ds(r0, rb), :]
        prev = x_ref[0, pl.ds(jnp.maximum(r0 - SUBLANES, 0), SUBLANES), :][SUBLANES - 1:SUBLANES]
        nxt = x_ref[0, pl.ds(jnp.minimum(r0 + rb, t - SUBLANES), SUBLANES), :][0:1]
        prev = jnp.where(r0 > 0, prev, 0.0)
        nxt = jnp.where(r0 + rb < t, nxt, 0.0)
        row = lax.broadcasted_iota(jnp.int32, x.shape, 0)
        xp = jnp.where(row == 0, prev, pltpu.roll(x, 1, 0))
        xn = jnp.where(row == rb - 1, nxt, pltpu.roll(x, rb - 1, 0))
        w = w_ref[...]
        return _silu(xp * w[0:1] + x * w[1:2] + xn * w[2:3])

    def l2n(x):
        return x * lax.rsqrt(jnp.sum(x * x, axis=-1, keepdims=True) + EPS)

    def prep(q_ref, k_ref, v_ref, ab_ref, t, base):
        def body(i, carry):
            r0 = pl.multiple_of(i * rb, rb)
            dst = pl.ds(pl.multiple_of(base + i * rb, rb), rb)
            q_s[dst, :] = l2n(conv_rows(q_ref, cwq_ref, r0, t)) * (dk ** -0.5)
            k_s[dst, :] = l2n(conv_rows(k_ref, cwk_ref, r0, t))
            v_s[dst, :] = conv_rows(v_ref, cwv_ref, r0, t)
            ab = ab_ref[0, 0, pl.ds(r0, rb), :]
            ab = jnp.concatenate([ab, jnp.zeros((rb, LANES - ab.shape[1]), f32)], axis=1)
            lane = lax.broadcasted_iota(jnp.int32, ab.shape, 1)
            pos = lax.broadcasted_iota(jnp.int32, ab.shape, 0) % c
            g = -jnp.exp(pa_ref[0]) * jax.nn.softplus(ab + pd_ref[0])
            acc_f, acc_b = g, g
            s = 1
            while s < c:
                acc_f = acc_f + jnp.where(pos >= s, pltpu.roll(acc_f, s, 0), 0.0)
                acc_b = acc_b + jnp.where(pos < c - s, pltpu.roll(acc_b, rb - s, 0), 0.0)
                s *= 2
            cf_s[dst, :] = jnp.where(lane == 0, acc_f, jnp.where(lane == 1, acc_b, jax.nn.sigmoid(ab)))
            return carry
        lax.fori_loop(0, t // rb, body, 0)

    prep(qc_ref, kc_ref, vc_ref, abc_ref, tc, 0)
    prep(ql_ref, kl_ref, vl_ref, abl_ref, tl, tc)

    lane = lax.broadcasted_iota(jnp.int32, (c, 2 * c), 1)
    row = lax.broadcasted_iota(jnp.int32, (c, 2 * c), 0)
    fwd = lane < c
    col = lane % c
    incl = (fwd & (row >= col)) | (~fwd & (row <= col))
    strict = (fwd & (row > col)) | (~fwd & (row < col))

    def blockdiag(a):
        return jnp.concatenate([jnp.where(fwd, a, 0.0), jnp.where(fwd, 0.0, a)], axis=0).astype(bf16)

    def chunk_group(cis):
        n = range(len(cis))
        rows = [pl.ds(pl.multiple_of(ci * c, c), c) for ci in cis]
        cf = [cf_s[r, :] for r in rows]
        k = [k_s[r, :] for r in rows]
        q = [q_s[r, :] for r in rows]
        bc = lambda x, j: jnp.broadcast_to(x[:, j:j + 1], (c, LANES))
        gf_c, gb_c = [bc(x, 0) for x in cf], [bc(x, 1) for x in cf]
        bf_c, bb_c = [bc(x, 2) for x in cf], [bc(x, 3) for x in cf]
        cf_t = [jnp.concatenate([x, x], axis=0).T for x in cf]
        g_r = [jnp.where(fwd, jnp.broadcast_to(x[0:1], (c, 2 * c)), jnp.broadcast_to(x[1:2], (c, 2 * c)))
               for x in cf_t]
        decay = [jnp.where(incl, jnp.exp(jnp.where(incl, jnp.where(fwd, gf_c[g], gb_c[g]) - g_r[g], 0.0)), 0.0)
                 for g in n]
        kq = [_dot_nt(jnp.concatenate([k[g], q[g]], axis=0), jnp.concatenate([k[g], k[g]], axis=0))
              for g in n]
        lm = [jnp.where(strict, kq[g][:c] * jnp.where(fwd, bf_c[g], bb_c[g]) * decay[g], 0.0) for g in n]
        for g in n:
            attn = (kq[g][c:] * decay[g]).astype(bf16)
            at_s[0, cis[g]] = attn[:, :c]
            at_s[1, cis[g]] = attn[:, c:]
        a = [jnp.dot(x.astype(bf16), blockdiag(x), preferred_element_type=f32) for x in lm]
        tp = [-x for x in lm]
        p = 2
        while 2 * p < c:
            ta = [jnp.dot(jnp.concatenate([tp[g], a[g]], axis=0).astype(bf16), blockdiag(a[g]),
                          preferred_element_type=f32) for g in n]
            tp = [tp[g] + a[g] + ta[g][:c] for g in n]
            a = [x[c:] for x in ta]
            p *= 2
        tp = [tp[g] + a[g] + jnp.dot(tp[g].astype(bf16), blockdiag(a[g]), preferred_element_type=f32)
              for g in n]
        for g in n:
            ci, r = cis[g], rows[g]
            v = v_s[r, :]
            eg_f, eg_b = jnp.exp(gf_c[g]), jnp.exp(gb_c[g])
            rhs = jnp.concatenate([jnp.concatenate([v * bf_c[g], k[g] * (bf_c[g] * eg_f)], axis=1),
                                   jnp.concatenate([v * bb_c[g], k[g] * (bb_c[g] * eg_b)], axis=1)], axis=0)
            uw = rhs + jnp.dot(blockdiag(tp[g]), rhs.astype(bf16), preferred_element_type=f32)
            u_s[0, r, :] = uw[:c, :dk]
            u_s[1, r, :] = uw[c:, :dk]
            wq_s[0, ci] = jnp.concatenate([uw[:c, dk:], q[g] * eg_f], axis=0).astype(bf16)
            wq_s[1, ci] = jnp.concatenate([uw[c:, dk:], q[g] * eg_b], axis=0).astype(bf16)
            gl_f, gl_b = gf_c[g][c - 1:c], gb_c[g][0:1]
            kgt_s[0, ci] = (k[g] * jnp.exp(gl_f - gf_c[g])).T.astype(bf16)
            kgt_s[1, ci] = (k[g] * jnp.exp(gl_b - gb_c[g])).T.astype(bf16)
            egl_s[ci] = jnp.concatenate([jnp.exp(gl_f), jnp.exp(gl_b),
                                         jnp.zeros((SUBLANES - 2, LANES), f32)], axis=0)
            o_s[r, :] = jnp.zeros((c, GDN_DV), f32)

    def chunk_body(i, carry):
        chunk_group([GDN_GROUP * i + g for g in range(GDN_GROUP)])
        return carry

    lax.fori_loop(0, nc // GDN_GROUP, chunk_body, 0)

    st_s[...] = jnp.zeros_like(st_s)

    def scan_body(i, carry):
        cis = (i, jnp.where(i < ncc, ncc - 1 - i, nc + ncc - 1 - i))
        rows = [pl.ds(pl.multiple_of(ci * c, c), c) for ci in cis]
        s = [st_s[d] for d in range(2)]
        ws = [jnp.dot(wq_s[d, cis[d]], s[d].astype(bf16), preferred_element_type=f32) for d in range(2)]
        v_new = [(u_s[d, rows[d], :] - ws[d][:c]).astype(bf16) for d in range(2)]
        upd = [jnp.dot(kgt_s[d, cis[d]], v_new[d], preferred_element_type=f32) for d in range(2)]
        out = [ws[d][c:] + jnp.dot(at_s[d, cis[d]], v_new[d], preferred_element_type=f32) for d in range(2)]
        for d in range(2):
            st_s[d] = s[d] * egl_s[cis[d]][d:d + 1] + upd[d]
            o_s[rows[d], :] += out[d]
        return carry

    lax.fori_loop(0, nc, scan_body, 0)

    def finish(z_ref, o_ref, t, base):
        def body(i, carry):
            r0 = pl.ds(pl.multiple_of(i * rb, rb), rb)
            o = o_s[pl.ds(pl.multiple_of(base + i * rb, rb), rb), :]
            y = o * lax.rsqrt(jnp.mean(o * o, axis=-1, keepdims=True) + EPS) * ng_ref[...]
            o_ref[0, r0, :] = y * _silu(z_ref[0, r0, :])
            return carry
        lax.fori_loop(0, t // rb, body, 0)

    finish(zc_ref, oc_ref, tc, 0)
    finish(zl_ref, ol_ref, tl, tc)


def _gdn(p_l, ab_l, p_c, ab_c, conv_w, pa, pd, norm_g):
    b, tl, _ = p_l.shape
    tc = p_c.shape[1]
    h, c = GDN_HEADS, GDN_CHUNK
    nc = (tl + tc) // c
    assert tl % GDN_ROWS == 0 and tc % GDN_ROWS == 0 and nc % GDN_GROUP == 0
    qb, zb = OFF_QKV // LANES, OFF_Z // LANES

    def specs(t):
        head = lambda off: pl.BlockSpec((1, t, LANES), lambda i, j, off=off: (i, 0, off + j))
        return [head(qb), head(qb + h), head(qb + 2 * h), head(zb),
                pl.BlockSpec((1, 1, t, 4), lambda i, j: (i, j, 0, 0))]

    cw = lambda off: pl.BlockSpec((3, LANES), lambda i, j, off=off: (0, off + j))
    par = pl.BlockSpec((1, 1, LANES), lambda i, j: (j, 0, 0))
    out = lambda t: pl.BlockSpec((1, t, GDN_DV), lambda i, j: (i, 0, j))
    tt = tl + tc
    return pl.pallas_call(
        _gdn_kernel,
        grid=(b, h),
        in_specs=specs(tl) + specs(tc) + [cw(0), cw(h), cw(2 * h), par, par, _const_spec((1, GDN_DV))],
        out_specs=[out(tl), out(tc)],
        out_shape=[jax.ShapeDtypeStruct((b, tl, h * GDN_DV), f32),
                   jax.ShapeDtypeStruct((b, tc, h * GDN_DV), f32)],
        scratch_shapes=[pltpu.VMEM((tt, GDN_DK), f32), pltpu.VMEM((tt, GDN_DK), f32),
                        pltpu.VMEM((tt, GDN_DV), f32), pltpu.VMEM((tt, LANES), f32),
                        pltpu.VMEM((2, tt, GDN_DV), f32), pltpu.VMEM((2, nc, 2 * c, GDN_DK), bf16),
                        pltpu.VMEM((2, nc, GDN_DK, c), bf16), pltpu.VMEM((2, nc, c, c), bf16),
                        pltpu.VMEM((nc, SUBLANES, LANES), f32), pltpu.VMEM((tt, GDN_DV), f32),
                        pltpu.VMEM((2, GDN_DK, GDN_DV), f32)],
        compiler_params=_cparams(("parallel", "parallel")),
        name="gated_deltanet",
    )(p_l, p_l, p_l, p_l, ab_l, p_c, p_c, p_c, p_c, ab_c, conv_w, conv_w, conv_w, pa, pd, norm_g)


def _ret_kernel(ql_ref, kl_ref, vl_ref, qc_ref, kc_ref, vc_ref, dsum_ref, sc_ref, gch_ref,
                ol_ref, oc_ref, sf_ref, sb_ref):
    c = RET_CHUNK
    w = RET_HEADS * RET_DK
    ri = lax.broadcasted_iota(jnp.int32, (w, w), 0) // RET_DK
    ci = lax.broadcasted_iota(jnp.int32, (w, w), 1) // RET_DK
    blockdiag = ri == ci
    lane_head = lax.broadcasted_iota(jnp.int32, (c, w), 1) // RET_DK

    def intra(q, k, v):
        out = jnp.zeros((c, w), f32)
        for h in range(RET_HEADS):
            m = lane_head == h
            scores = _dot_nt(jnp.where(m, q, 0.0), k) * dsum_ref[h]
            out = out + _dot(scores, jnp.where(m, v, 0.0))
        return out

    def update(s_ref, k, v, d):
        new = jnp.where(blockdiag, _dot_tn(k * sc_ref[2 * d + 1], v), 0.0)
        s_ref[...] = s_ref[...] * gch_ref[d] + new

    sf_ref[...] = jnp.zeros_like(sf_ref)
    sb_ref[...] = jnp.zeros_like(sb_ref)
    qc, kc, vc = qc_ref[0], kc_ref[0], vc_ref[0]
    oc_ref[0] = intra(qc, kc, vc)
    update(sf_ref, kc, vc, 0)
    update(sb_ref, kc, vc, 1)
    n = ql_ref.shape[1] // c
    for i in range(n):
        r = slice(i * c, (i + 1) * c)
        q, k, v = ql_ref[0, r, :], kl_ref[0, r, :], vl_ref[0, r, :]
        ol_ref[0, r, :] = intra(q, k, v) + _dot(q * sc_ref[0], sf_ref[...])
        update(sf_ref, k, v, 0)
    for i in range(n - 1, -1, -1):
        r = slice(i * c, (i + 1) * c)
        q, k, v = ql_ref[0, r, :], kl_ref[0, r, :], vl_ref[0, r, :]
        ol_ref[0, r, :] += _dot(q * sc_ref[2], sb_ref[...])
        update(sb_ref, k, v, 1)


def _retention(q_l, k_l, v_l, q_c, k_c, v_c, dsum, scales, gch):
    b, tl, w = q_l.shape
    tc = q_c.shape[1]
    tok = lambda t: pl.BlockSpec((1, t, w), lambda i: (i, 0, 0))
    return pl.pallas_call(
        _ret_kernel,
        grid=(b,),
        in_specs=[tok(tl)] * 3 + [tok(tc)] * 3 + [_const_spec(dsum.shape), _const_spec(scales.shape),
                                                  _const_spec(gch.shape)],
        out_specs=[tok(tl), tok(tc)],
        out_shape=[jax.ShapeDtypeStruct((b, tl, w), f32), jax.ShapeDtypeStruct((b, tc, w), f32)],
        scratch_shapes=[pltpu.VMEM((w, w), f32), pltpu.VMEM((w, w), f32)],
        compiler_params=_cparams(("parallel",)),
        name="retention",
    )(q_l, k_l, v_l, q_c, k_c, v_c, dsum, scales, gch)


def _short_conv(x, w):
    k_w = w.shape[0]
    pad = k_w // 2
    t = x.shape[1]
    xp = jnp.pad(x, ((0, 0), (pad, pad), (0, 0)))
    return sum(xp[:, j:j + t] * w[j] for j in range(k_w))


def _hyena_filters(t_len, lp):
    pos = jnp.arange(t_len, dtype=f32)
    t = jnp.linspace(0.0, 1.0, t_len, dtype=f32)
    bands = (HY_EMB - 1) // 2
    f = jnp.linspace(1e-4, bands - 1, bands, dtype=f32)
    ang = (2.0 * math.pi / t_len) * pos[:, None] * f[None, :]
    z = jnp.concatenate([t[:, None], jnp.cos(ang), -jnp.sin(ang)], axis=-1)
    h = jnp.sin(lp['hy_f_freq1'] * (z @ lp['hy_f_w1'] + lp['hy_f_b1']))
    h = jnp.sin(lp['hy_f_freq2'] * (h @ lp['hy_f_w2'] + lp['hy_f_b2']))
    h = (h @ lp['hy_f_w3']).reshape(t_len, HY_ORDER, 2, HY_WIDTH)
    max_decay = math.log(HY_TARGET) / HY_FAST_DECAY
    min_decay = math.log(HY_TARGET) / HY_SLOW_DECAY
    deltas = jnp.abs(jnp.linspace(min_decay, max_decay, HY_WIDTH, dtype=f32))
    window = jnp.exp(-t[:, None] * deltas[None, :])
    return h * window[:, None, None, :]


def _hyena(p, lp):
    b, t, _ = p.shape
    hb, c = HY_BLOCK, HY_WIDTH
    nb = t // hb
    p = _short_conv(p, lp['hy_conv_w']) + lp['hy_conv_b']
    p3 = p.reshape(b, nb, hb, 3, c).transpose(3, 4, 1, 0, 2).reshape(3, c, nb * b, hb)
    h = _hyena_filters(t, lp)
    kern = jnp.concatenate([h[:, :, 0], jnp.zeros_like(h[:1, :, 0]), h[:0:-1, :, 1]], axis=0)
    kext = jnp.concatenate([kern[2 * t - hb:], kern], axis=0).transpose(2, 1, 0)[:, :, None, :]
    bias = jnp.broadcast_to(lp['hy_bias'].T[:, :, None, None], (c, HY_ORDER, 1, hb))
    z = _hyena_conv(p3, kext, bias, nb)
    return z.reshape(c, nb, b, hb).transpose(2, 1, 3, 0).reshape(b, t, c)


def _gdn_gate_inputs(p):
    b, t, _ = p.shape
    ab = p[..., OFF_AB:OFF_AB + 4 * GDN_HEADS].reshape(b, t, 4, GDN_HEADS)
    return ab.transpose(0, 3, 1, 2)


def _gdn_gate_params(a_log, dt_bias):
    pad = lambda x: jnp.pad(x.T, ((0, 0), (0, LANES - 2)))[:, None, :]
    return pad(a_log), pad(dt_bias)


def _rope(x, cos, sin):
    b, t, _ = x.shape
    x = x.reshape(b, t, RET_HEADS, RET_DK)
    x1, x2 = jnp.split(x, 2, axis=-1)
    cos, sin = cos[:, None, :], sin[:, None, :]
    return jnp.concatenate([x1 * cos - x2 * sin, x1 * sin + x2 * cos], axis=-1).reshape(b, t, -1)


def _ret_constants(lp):
    c = RET_CHUNK
    lg = jax.nn.log_sigmoid(lp['ret_decay_logit'])
    idx = jnp.arange(c, dtype=f32)
    rel = idx[:, None] - idx[None, :]
    d_f = jnp.exp(jnp.where(rel >= 0, rel * lg[0][:, None, None], -jnp.inf))
    d_b = jnp.exp(jnp.where(rel <= 0, -rel * lg[1][:, None, None], -jnp.inf))
    lane = lambda x: jnp.repeat(x, RET_DK, axis=-1)
    scales = jnp.stack([lane(jnp.exp((idx + 1.0)[:, None] * lg[0])),
                        lane(jnp.exp((c - 1.0 - idx)[:, None] * lg[0])),
                        lane(jnp.exp((c - idx)[:, None] * lg[1])),
                        lane(jnp.exp(idx[:, None] * lg[1]))])
    gch = lane(jnp.exp(c * lg))[:, None, :]
    return d_f + d_b, scales, gch


def _ret_output(o, gate):
    b, t, _ = o.shape
    o = o.reshape(b, t, RET_HEADS, RET_DV)
    mu = jnp.mean(o, axis=-1, keepdims=True)
    var = jnp.mean(jnp.square(o - mu), axis=-1, keepdims=True)
    return ((o - mu) * lax.rsqrt(var + EPS)).reshape(b, t, -1) * _silu(gate)


def _reorder_w_in(w):
    h2 = 2 * GDN_HEADS
    a0 = OFF_Z + GDN_HEADS * GDN_DV
    parts = [w[:, :a0], w[:, a0 + 2 * h2:], w[:, a0:a0 + 2 * h2],
             jnp.zeros((w.shape[0], LANES - 2 * h2), w.dtype)]
    return jnp.concatenate(parts, axis=1)


def _ret_slices(p):
    w = RET_HEADS * RET_DK
    return tuple(p[..., OFF_RET + i * w:OFF_RET + (i + 1) * w] for i in range(4))


def kernel(x, c, ctx, c_ctx, mod_w, mod_b, norm1_g, w_in, hy_conv_w, hy_conv_b, hy_f_w1, hy_f_b1, hy_f_freq1, hy_f_w2, hy_f_b2, hy_f_freq2, hy_f_w3, hy_bias, gdn_conv_w, gdn_a_log, gdn_dt_bias, gdn_norm_g, ret_decay_logit, w_out, norm2_g, ffn_w_in, ffn_w_out, final_norm_g):
    bsz, t_lat, d = x.shape
    rows_n = t_lat // GRID_W
    row = jnp.repeat(jnp.arange(rows_n, dtype=f32), GRID_W)
    colp = jnp.tile(jnp.arange(GRID_W, dtype=f32), rows_n)
    nf = RET_DK // 4
    inv = ROPE_BASE ** (-jnp.arange(nf, dtype=f32) / nf)
    ang = jnp.concatenate([row[:, None] * inv, colp[:, None] * inv], axis=-1)
    cos, sin = jnp.cos(ang), jnp.sin(ang)

    pad_rows = (-(bsz + 1)) % SUBLANES
    cs = jnp.concatenate([c, c_ctx[None], jnp.zeros((pad_rows, d), f32)], axis=0)
    gf = final_norm_g[None]
    for i in range(DEPTH):
        need_ctx = i < DEPTH - 1
        lp = {'hy_conv_w': hy_conv_w[i], 'hy_conv_b': hy_conv_b[i], 'hy_f_w1': hy_f_w1[i],
              'hy_f_b1': hy_f_b1[i], 'hy_f_freq1': hy_f_freq1[i], 'hy_f_w2': hy_f_w2[i],
              'hy_f_b2': hy_f_b2[i], 'hy_f_freq2': hy_f_freq2[i], 'hy_f_w3': hy_f_w3[i],
              'hy_bias': hy_bias[i], 'ret_decay_logit': ret_decay_logit[i]}
        mod = _mod_vectors(cs, mod_w[i].astype(bf16), mod_b[i][None])
        mod_lat = mod[:bsz].reshape(bsz, N_MOD, d)
        mod_ctx = jnp.broadcast_to(mod[bsz].reshape(1, N_MOD, d), (bsz, N_MOD, d))
        w_in16 = _reorder_w_in(w_in[i]).astype(bf16)
        g1 = norm1_g[i][None]
        p_l = _in_proj(x, g1, mod_lat, w_in16, 256)
        p_c = _in_proj(ctx, g1, mod_ctx, w_in16, 256)

        hy_l = _hyena(p_l[..., OFF_HY:OFF_HY + 3 * HY_WIDTH], lp)
        pa, pd = _gdn_gate_params(gdn_a_log[i], gdn_dt_bias[i])
        gd_l, gd_c = _gdn(p_l, _gdn_gate_inputs(p_l), p_c, _gdn_gate_inputs(p_c), gdn_conv_w[i], pa, pd,
                          gdn_norm_g[i][None])

        rq_l, rk_l, rv_l, rg_l = _ret_slices(p_l)
        rq_c, rk_c, rv_c, rg_c = _ret_slices(p_c)
        scale = RET_DK ** -0.5
        dsum, scales, gch = _ret_constants(lp)
        ro_l, ro_c = _retention(_rope(rq_l, cos, sin), _rope(rk_l * scale, cos, sin), rv_l,
                                rq_c, rk_c * scale, rv_c, dsum, scales, gch)
        rt_l = _ret_output(ro_l, rg_l)

        wo16, w116, w216 = w_out[i].astype(bf16), ffn_w_in[i].astype(bf16), ffn_w_out[i].astype(bf16)
        g2 = norm2_g[i][None]
        x = _out_ffn(x, hy_l, gd_l, rt_l, mod_lat, wo16, g2, w116, w216, gf, 256, final=not need_ctx)
        if need_ctx:
            hy_c = _hyena(p_c[..., OFF_HY:OFF_HY + 3 * HY_WIDTH], lp)
            rt_c = _ret_output(ro_c, rg_c)
            ctx = _out_ffn(ctx, hy_c, gd_c, rt_c, mod_ctx, wo16, g2, w116, w216, gf, 256, final=False)
    return x
```

```python
import functools
import math

import jax
import jax.numpy as jnp
from jax import lax
from jax.experimental import pallas as pl
from jax.experimental.pallas import tpu as pltpu

D_MODEL = 1024
DEPTH = 2
GRID_W = 64
EPS = 1e-6
N_MOD = 6

HY_WIDTH = D_MODEL // 4
HY_ORDER = 2
HY_EMB = 33
HY_FAST_DECAY = 0.3
HY_SLOW_DECAY = 1.5
HY_TARGET = 1e-2
GDN_HEADS = D_MODEL // 256
GDN_DK = 128
GDN_DV = 128
RET_HEADS = D_MODEL // 256
RET_DK = 64
RET_DV = 64
ROPE_BASE = 10000.0
FFN_HIDDEN = ((8 * D_MODEL + 3 * 256 - 1) // (3 * 256)) * 256

GDN_CHUNK = 64
GDN_ROWS = 256
GDN_HP = 2
GDN_GROUP = 12
RET_CHUNK = 256
HY_BLOCK = 256
LANES = 128
SUBLANES = 8

GDN_QKV = 2 * GDN_HEADS * GDN_DK + GDN_HEADS * GDN_DV
OFF_HY = 0
OFF_QKV = OFF_HY + 3 * HY_WIDTH
OFF_Z = OFF_QKV + GDN_QKV
OFF_RET = OFF_Z + GDN_HEADS * GDN_DV
OFF_AB = OFF_RET + 4 * RET_HEADS * RET_DK
IN_PAD = OFF_AB + LANES

VMEM_LIMIT = 56 * 1024 * 1024
GDN_VMEM_LIMIT = 60 * 1024 * 1024

f32 = jnp.float32
bf16 = jnp.bfloat16


def _cparams(sem):
    return pltpu.CompilerParams(dimension_semantics=sem, vmem_limit_bytes=VMEM_LIMIT)


def _const_spec(shape):
    nd = len(shape)
    return pl.BlockSpec(shape, lambda *_: (0,) * nd, pipeline_mode=pl.Buffered(1))


def _rms(x, g):
    return x * lax.rsqrt(jnp.mean(x * x, axis=-1, keepdims=True) + EPS) * g


def _silu(x):
    return x * jax.nn.sigmoid(x)


def _dot(a, b):
    return jnp.dot(a.astype(bf16), b.astype(bf16), preferred_element_type=f32)


def _dot_nt(a, b):
    return lax.dot_general(a.astype(bf16), b.astype(bf16), (((1,), (1,)), ((), ())),
                           preferred_element_type=f32)


def _dot_tn(a, b):
    return lax.dot_general(a.astype(bf16), b.astype(bf16), (((0,), (0,)), ((), ())),
                           preferred_element_type=f32)


def _mod_kernel(c_ref, w_ref, b_ref, o_ref):
    o_ref[...] = _dot(_silu(c_ref[...]), w_ref[...]) + b_ref[...]


def _mod_vectors(cs, w16, b):
    rows, d = cs.shape
    n = w16.shape[1]
    tn = 1536
    return pl.pallas_call(
        _mod_kernel,
        grid=(n // tn,),
        in_specs=[pl.BlockSpec((rows, d), lambda j: (0, 0)),
                  pl.BlockSpec((d, tn), lambda j: (0, j)),
                  pl.BlockSpec((1, tn), lambda j: (0, j))],
        out_specs=pl.BlockSpec((rows, tn), lambda j: (0, j)),
        out_shape=jax.ShapeDtypeStruct((rows, n), f32),
        compiler_params=_cparams(("parallel",)),
        name="mod_vectors",
    )(cs, w16, b)


def _in_proj_kernel(x_ref, g_ref, mod_ref, w_ref, o_ref):
    x = x_ref[0]
    mod = mod_ref[0]
    h = _rms(x, g_ref[...]) * (1.0 + mod[1:2]) + mod[0:1]
    h16 = h.astype(bf16)
    n = w_ref.shape[1]
    step = 512
    for j in range(0, n, step):
        e = min(j + step, n)
        o_ref[0, :, j:e] = jnp.dot(h16, w_ref[:, j:e], preferred_element_type=f32)


def _in_proj(x, g, mod, w16, tm):
    b, t, d = x.shape
    n = w16.shape[1]
    return pl.pallas_call(
        _in_proj_kernel,
        grid=(b, t // tm),
        in_specs=[pl.BlockSpec((1, tm, d), lambda i, j: (i, j, 0)),
                  _const_spec((1, d)),
                  pl.BlockSpec((1, N_MOD, d), lambda i, j: (i, 0, 0)),
                  _const_spec((d, n))],
        out_specs=pl.BlockSpec((1, tm, n), lambda i, j: (i, j, 0)),
        out_shape=jax.ShapeDtypeStruct((b, t, n), f32),
        compiler_params=_cparams(("parallel", "parallel")),
        name="in_proj",
    )(x, g, mod, w16)


def _out_ffn_kernel(x_ref, hy_ref, gd_ref, rt_ref, mod_ref, wo_ref, g2_ref, w1_ref, w2_ref, gf_ref, o_ref,
                    *, final):
    x = x_ref[0]
    mod = mod_ref[0]
    o0, o1, o2 = HY_WIDTH, HY_WIDTH + GDN_HEADS * GDN_DV, D_MODEL
    y = (jnp.dot(hy_ref[0].astype(bf16), wo_ref[0:o0, :], preferred_element_type=f32)
         + jnp.dot(gd_ref[0].astype(bf16), wo_ref[o0:o1, :], preferred_element_type=f32)
         + jnp.dot(rt_ref[0].astype(bf16), wo_ref[o1:o2, :], preferred_element_type=f32))
    x = x + mod[2:3] * y
    h16 = (_rms(x, g2_ref[...]) * (1.0 + mod[4:5]) + mod[3:4]).astype(bf16)
    step = 256
    acc = jnp.zeros_like(x)
    for j in range(0, FFN_HIDDEN, step):
        gate = jnp.dot(h16, w1_ref[:, j:j + step], preferred_element_type=f32)
        up = jnp.dot(h16, w1_ref[:, FFN_HIDDEN + j:FFN_HIDDEN + j + step], preferred_element_type=f32)
        acc = acc + jnp.dot((_silu(gate) * up).astype(bf16), w2_ref[j:j + step, :], preferred_element_type=f32)
    x = x + mod[5:6] * acc
    if final:
        x = _rms(x, gf_ref[...])
    o_ref[0] = x


def _out_ffn(x, hy, gd, rt, mod, wo16, g2, w116, w216, gf, tm, final):
    b, t, d = x.shape
    tok = lambda w: pl.BlockSpec((1, tm, w), lambda i, j: (i, j, 0))
    return pl.pallas_call(
        functools.partial(_out_ffn_kernel, final=final),
        grid=(b, t // tm),
        in_specs=[tok(d), tok(hy.shape[-1]), tok(gd.shape[-1]), tok(rt.shape[-1]),
                  pl.BlockSpec((1, N_MOD, d), lambda i, j: (i, 0, 0)),
                  _const_spec(wo16.shape), _const_spec((1, d)), _const_spec(w116.shape),
                  _const_spec(w216.shape), _const_spec((1, d))],
        out_specs=tok(d),
        out_shape=jax.ShapeDtypeStruct((b, t, d), f32),
        compiler_params=_cparams(("parallel", "parallel")),
        name="out_ffn",
    )(x, hy, gd, rt, mod, wo16, g2, w116, w216, gf)


def _hyena_kernel(v_ref, x1_ref, x2_ref, k_ref, b_ref, o_ref, acc_ref, *, nb, cb):
    hb = HY_BLOCK
    rb = v_ref.shape[1] // nb

    def conv(u16, kext):
        for d in range(-(nb - 1), nb):
            dd = d % (2 * nb)
            win = jnp.broadcast_to(kext[:, hb * dd:hb * dd + 2 * hb], (hb, 2 * hb))
            toep = pltpu.roll(win, 0, 1, stride=1, stride_axis=0)[:, hb:].astype(bf16)
            t0, t1 = max(0, d), min(nb - 1, nb - 1 + d) + 1
            contrib = jnp.dot(u16[rb * (t0 - d):rb * (t1 - d)], toep, preferred_element_type=f32)
            if d == -(nb - 1):
                acc_ref[...] = jnp.zeros_like(acc_ref)
            acc_ref[rb * t0:rb * t1, :] += contrib
        return acc_ref[...]

    def body(c, carry):
        z = v_ref[c]
        for n, gate_ref in enumerate((x1_ref, x2_ref)):
            y = conv(z.astype(bf16), k_ref[c, n])
            z = gate_ref[c] * (y + z * b_ref[c, n])
        o_ref[c] = z
        return carry

    lax.fori_loop(0, cb, body, 0)


def _hyena_conv(p3, kext, bias, nb, cb=8):
    _, c, r, hb = p3.shape
    klen = kext.shape[-1]
    sig = lambda s: pl.BlockSpec((None, cb, r, hb), lambda i, s=s: (s, i, 0, 0))
    return pl.pallas_call(
        functools.partial(_hyena_kernel, nb=nb, cb=cb),
        grid=(c // cb,),
        in_specs=[sig(0), sig(1), sig(2),
                  pl.BlockSpec((cb, 2, 1, klen), lambda i: (i, 0, 0, 0)),
                  pl.BlockSpec((cb, 2, 1, hb), lambda i: (i, 0, 0, 0))],
        out_specs=pl.BlockSpec((cb, r, hb), lambda i: (i, 0, 0)),
        out_shape=jax.ShapeDtypeStruct((c, r, hb), f32),
        scratch_shapes=[pltpu.VMEM((r, hb), f32)],
        compiler_params=_cparams(("parallel",)),
        name=f"hyena_conv_nb{nb}",
    )(p3, p3, p3, kext, bias)


def _gdn_kernel(ql_ref, kl_ref, vl_ref, zl_ref, abl_ref, qc_ref, kc_ref, vc_ref, zc_ref, abc_ref,
                cwq_ref, cwk_ref, cwv_ref, pa_ref, pd_ref, ng_ref,
                ol_ref, oc_ref,
                q_s, k_s, v_s, cf_s, u_s, wq_s, kgt_s, at_s, egl_s, o_s, st_s):
    c = GDN_CHUNK
    rb = GDN_ROWS
    tc, tl = qc_ref.shape[1], ql_ref.shape[1]
    ncc, nc = tc // c, (tc + tl) // c
    dk = GDN_DK

    def conv_rows(x_ref, w_ref, r0, t, ls):
        x = x_ref[0, pl.ds(r0, rb), ls]
        before = pl.multiple_of(jnp.maximum(r0 - SUBLANES, 0), SUBLANES)
        after = pl.multiple_of(jnp.minimum(r0 + rb, t - SUBLANES), SUBLANES)
        prev = x_ref[0, pl.ds(before, SUBLANES), ls][SUBLANES - 1:SUBLANES]
        nxt = x_ref[0, pl.ds(after, SUBLANES), ls][0:1]
        prev = jnp.where(r0 > 0, prev, 0.0)
        nxt = jnp.where(r0 + rb < t, nxt, 0.0)
        row = lax.broadcasted_iota(jnp.int32, x.shape, 0)
        xp = jnp.where(row == 0, prev, pltpu.roll(x, 1, 0))
        xn = jnp.where(row == rb - 1, nxt, pltpu.roll(x, rb - 1, 0))
        w = w_ref[:, ls]
        return _silu(xp * w[0:1] + x * w[1:2] + xn * w[2:3])

    def l2n(x):
        return x * lax.rsqrt(jnp.sum(x * x, axis=-1, keepdims=True) + EPS)

    def prep(hh, q_ref, k_ref, v_ref, ab_ref, t, base):
        ls = slice(hh * LANES, (hh + 1) * LANES)

        def body(i, carry):
            r0 = pl.multiple_of(i * rb, rb)
            dst = pl.ds(pl.multiple_of(base + i * rb, rb), rb)
            q_s[dst, :] = l2n(conv_rows(q_ref, cwq_ref, r0, t, ls)) * (dk ** -0.5)
            k_s[dst, :] = l2n(conv_rows(k_ref, cwk_ref, r0, t, ls))
            v_s[dst, :] = conv_rows(v_ref, cwv_ref, r0, t, ls)
            return carry
        lax.fori_loop(0, t // rb, body, 0)

        ab = ab_ref[0, hh]
        ab = jnp.concatenate([ab, jnp.zeros((SUBLANES - ab.shape[0], t), f32)], axis=0)
        rowi = lax.broadcasted_iota(jnp.int32, ab.shape, 0)
        pos = lax.broadcasted_iota(jnp.int32, ab.shape, 1) % c
        tile = lambda p: jnp.concatenate([p] * (t // LANES), axis=1)
        g = -jnp.exp(tile(pa_ref[hh])) * jax.nn.softplus(ab + tile(pd_ref[hh]))
        acc_f, acc_b = g, g
        s = 1
        while s < c:
            acc_f = acc_f + jnp.where(pos >= s, pltpu.roll(acc_f, s, 1), 0.0)
            acc_b = acc_b + jnp.where(pos < c - s, pltpu.roll(acc_b, t - s, 1), 0.0)
            s *= 2
        cf_t = jnp.where(rowi == 0, acc_f, jnp.where(rowi == 1, acc_b, jax.nn.sigmoid(ab)))
        pad = jnp.zeros((LANES - SUBLANES, LANES), f32)
        for j in range(t // LANES):
            blk = jnp.concatenate([cf_t[:, j * LANES:(j + 1) * LANES], pad], axis=0)
            cf_s[base + j * LANES:base + (j + 1) * LANES, :] = blk.T

    lane = lax.broadcasted_iota(jnp.int32, (c, 2 * c), 1)
    row = lax.broadcasted_iota(jnp.int32, (c, 2 * c), 0)
    fwd = lane < c
    col = lane % c
    incl = (fwd & (row >= col)) | (~fwd & (row <= col))
    strict = (fwd & (row > col)) | (~fwd & (row < col))

    def blockdiag(a):
        return jnp.concatenate([jnp.where(fwd, a, 0.0), jnp.where(fwd, 0.0, a)], axis=0).astype(bf16)

    def chunk_group(hh, cis):
        n = range(len(cis))
        xf, xb = 2 * hh, 2 * hh + 1
        rows = [pl.ds(pl.multiple_of(ci * c, c), c) for ci in cis]
        cf = [cf_s[r, :] for r in rows]
        k = [k_s[r, :] for r in rows]
        q = [q_s[r, :] for r in rows]
        bc = lambda x, j: jnp.broadcast_to(x[:, j:j + 1], (c, LANES))
        gf_c, gb_c = [bc(x, 0) for x in cf], [bc(x, 1) for x in cf]
        bf_c, bb_c = [bc(x, 2) for x in cf], [bc(x, 3) for x in cf]
        cf_t = [jnp.concatenate([x, x], axis=0).T for x in cf]
        g_r = [jnp.where(fwd, jnp.broadcast_to(x[0:1], (c, 2 * c)), jnp.broadcast_to(x[1:2], (c, 2 * c)))
               for x in cf_t]
        decay = [jnp.where(incl, jnp.exp(jnp.where(incl, jnp.where(fwd, gf_c[g], gb_c[g]) - g_r[g], 0.0)), 0.0)
                 for g in n]
        kq = [_dot_nt(jnp.concatenate([k[g], q[g]], axis=0), jnp.concatenate([k[g], k[g]], axis=0))
              for g in n]
        lm = [jnp.where(strict, kq[g][:c] * jnp.where(fwd, bf_c[g], bb_c[g]) * decay[g], 0.0) for g in n]
        for g in n:
            attn = (kq[g][c:] * decay[g]).astype(bf16)
            at_s[xf, cis[g]] = attn[:, :c]
            at_s[xb, cis[g]] = attn[:, c:]
        a = [jnp.dot(x.astype(bf16), blockdiag(x), preferred_element_type=f32) for x in lm]
        tp = [-x for x in lm]
        p = 2
        while 2 * p < c:
            ta = [jnp.dot(jnp.concatenate([tp[g], a[g]], axis=0).astype(bf16), blockdiag(a[g]),
                          preferred_element_type=f32) for g in n]
            tp = [tp[g] + a[g] + ta[g][:c] for g in n]
            a = [x[c:] for x in ta]
            p *= 2
        tp = [tp[g] + a[g] + jnp.dot(tp[g].astype(bf16), blockdiag(a[g]), preferred_element_type=f32)
              for g in n]
        for g in n:
            ci, r = cis[g], rows[g]
            v = v_s[r, :]
            eg_f, eg_b = jnp.exp(gf_c[g]), jnp.exp(gb_c[g])
            rhs = jnp.concatenate([jnp.concatenate([v * bf_c[g], k[g] * (bf_c[g] * eg_f)], axis=1),
                                   jnp.concatenate([v * bb_c[g], k[g] * (bb_c[g] * eg_b)], axis=1)], axis=0)
            uw = rhs + jnp.dot(blockdiag(tp[g]), rhs.astype(bf16), preferred_element_type=f32)
            u_s[xf, r, :] = uw[:c, :dk]
            u_s[xb, r, :] = uw[c:, :dk]
            wq_s[xf, ci] = jnp.concatenate([uw[:c, dk:], q[g] * eg_f], axis=0).astype(bf16)
            wq_s[xb, ci] = jnp.concatenate([uw[c:, dk:], q[g] * eg_b], axis=0).astype(bf16)
            gl_f, gl_b = gf_c[g][c - 1:c], gb_c[g][0:1]
            kgt_s[xf, ci] = (k[g] * jnp.exp(gl_f - gf_c[g])).T.astype(bf16)
            kgt_s[xb, ci] = (k[g] * jnp.exp(gl_b - gb_c[g])).T.astype(bf16)
            egl_s[hh, ci] = jnp.concatenate([jnp.exp(gl_f), jnp.exp(gl_b),
                                             jnp.zeros((SUBLANES - 2, LANES), f32)], axis=0)
            o_s[hh, r, :] = jnp.zeros((c, GDN_DV), f32)

    for hh in range(GDN_HP):
        prep(hh, qc_ref, kc_ref, vc_ref, abc_ref, tc, 0)
        prep(hh, ql_ref, kl_ref, vl_ref, abl_ref, tl, tc)

        def chunk_body(i, carry, hh=hh):
            chunk_group(hh, [GDN_GROUP * i + g for g in range(GDN_GROUP)])
            return carry

        lax.fori_loop(0, nc // GDN_GROUP, chunk_body, 0)

    st_s[...] = jnp.zeros_like(st_s)
    chains = [(hh, d) for hh in range(GDN_HP) for d in range(2)]

    def scan_body(i, carry):
        ci = (i, jnp.where(i < ncc, ncc - 1 - i, nc + ncc - 1 - i))
        rows = [pl.ds(pl.multiple_of(x * c, c), c) for x in ci]
        n = range(len(chains))
        s = [st_s[x] for x in n]
        ws = [jnp.dot(wq_s[x, ci[chains[x][1]]], s[x].astype(bf16), preferred_element_type=f32) for x in n]
        v_new = [(u_s[x, rows[chains[x][1]], :] - ws[x][:c]).astype(bf16) for x in n]
        upd = [jnp.dot(kgt_s[x, ci[chains[x][1]]], v_new[x], preferred_element_type=f32) for x in n]
        out = [ws[x][c:] + jnp.dot(at_s[x, ci[chains[x][1]]], v_new[x], preferred_element_type=f32) for x in n]
        for x in n:
            hh, d = chains[x]
            st_s[x] = s[x] * egl_s[hh, ci[d]][d:d + 1] + upd[x]
            o_s[hh, rows[d], :] += out[x]
        return carry

    lax.fori_loop(0, nc, scan_body, 0)

    def finish(hh, z_ref, o_ref, t, base):
        ls = slice(hh * LANES, (hh + 1) * LANES)

        def body(i, carry):
            r0 = pl.ds(pl.multiple_of(i * rb, rb), rb)
            o = o_s[hh, pl.ds(pl.multiple_of(base + i * rb, rb), rb), :]
            y = o * lax.rsqrt(jnp.mean(o * o, axis=-1, keepdims=True) + EPS) * ng_ref[...]
            o_ref[0, r0, ls] = y * _silu(z_ref[0, r0, ls])
            return carry
        lax.fori_loop(0, t // rb, body, 0)

    for hh in range(GDN_HP):
        finish(hh, zc_ref, oc_ref, tc, 0)
        finish(hh, zl_ref, ol_ref, tl, tc)


def _gdn(p_l, ab_l, p_c, ab_c, conv_w, pa, pd, norm_g):
    b, tl, _ = p_l.shape
    tc = p_c.shape[1]
    h, c, hp = GDN_HEADS, GDN_CHUNK, GDN_HP
    nc = (tl + tc) // c
    assert tl % GDN_ROWS == 0 and tc % GDN_ROWS == 0 and nc % GDN_GROUP == 0
    qb, zb = OFF_QKV // LANES, OFF_Z // LANES
    assert h % hp == 0 and qb % hp == 0 and zb % hp == 0
    w = hp * LANES

    def specs(t):
        heads = lambda off: pl.BlockSpec((1, t, w), lambda i, j, off=off // hp: (i, 0, off + j))
        return [heads(qb), heads(qb + h), heads(qb + 2 * h), heads(zb),
                pl.BlockSpec((1, hp, 4, t), lambda i, j: (i, j, 0, 0))]

    cw = lambda off: pl.BlockSpec((3, w), lambda i, j, off=off // hp: (0, off + j))
    par = pl.BlockSpec((hp, SUBLANES, LANES), lambda i, j: (j, 0, 0))
    out = lambda t: pl.BlockSpec((1, t, w), lambda i, j: (i, 0, j))
    tt = tl + tc
    return pl.pallas_call(
        _gdn_kernel,
        grid=(b, h // hp),
        in_specs=specs(tl) + specs(tc) + [cw(0), cw(h), cw(2 * h), par, par, _const_spec((1, GDN_DV))],
        out_specs=[out(tl), out(tc)],
        out_shape=[jax.ShapeDtypeStruct((b, tl, h * GDN_DV), f32),
                   jax.ShapeDtypeStruct((b, tc, h * GDN_DV), f32)],
        scratch_shapes=[pltpu.VMEM((tt, GDN_DK), f32), pltpu.VMEM((tt, GDN_DK), f32),
                        pltpu.VMEM((tt, GDN_DV), f32), pltpu.VMEM((tt, LANES), f32),
                        pltpu.VMEM((2 * hp, tt, GDN_DV), f32), pltpu.VMEM((2 * hp, nc, 2 * c, GDN_DK), bf16),
                        pltpu.VMEM((2 * hp, nc, GDN_DK, c), bf16), pltpu.VMEM((2 * hp, nc, c, c), bf16),
                        pltpu.VMEM((hp, nc, SUBLANES, LANES), f32), pltpu.VMEM((hp, tt, GDN_DV), f32),
                        pltpu.VMEM((2 * hp, GDN_DK, GDN_DV), f32)],
        compiler_params=pltpu.CompilerParams(dimension_semantics=("parallel", "parallel"),
                                             vmem_limit_bytes=GDN_VMEM_LIMIT),
        name="gated_deltanet",
    )(p_l, p_l, p_l, p_l, ab_l, p_c, p_c, p_c, p_c, ab_c, conv_w, conv_w, conv_w, pa, pd, norm_g)


def _ret_kernel(ql_ref, kl_ref, vl_ref, qc_ref, kc_ref, vc_ref, dsum_ref, sc_ref, gch_ref,
                ol_ref, oc_ref, sf_ref, sb_ref):
    c = RET_CHUNK
    w = RET_HEADS * RET_DK
    ri = lax.broadcasted_iota(jnp.int32, (w, w), 0) // RET_DK
    ci = lax.broadcasted_iota(jnp.int32, (w, w), 1) // RET_DK
    blockdiag = ri == ci
    lane_head = lax.broadcasted_iota(jnp.int32, (c, w), 1) // RET_DK

    def intra(q, k, v):
        out = jnp.zeros((c, w), f32)
        for h in range(RET_HEADS):
            m = lane_head == h
            scores = _dot_nt(jnp.where(m, q, 0.0), k) * dsum_ref[h]
            out = out + _dot(scores, jnp.where(m, v, 0.0))
        return out

    def update(s_ref, k, v, d):
        new = jnp.where(blockdiag, _dot_tn(k * sc_ref[2 * d + 1], v), 0.0)
        s_ref[...] = s_ref[...] * gch_ref[d] + new

    sf_ref[...] = jnp.zeros_like(sf_ref)
    sb_ref[...] = jnp.zeros_like(sb_ref)
    qc, kc, vc = qc_ref[0], kc_ref[0], vc_ref[0]
    oc_ref[0] = intra(qc, kc, vc)
    update(sf_ref, kc, vc, 0)
    update(sb_ref, kc, vc, 1)
    n = ql_ref.shape[1] // c
    for i in range(n):
        r = slice(i * c, (i + 1) * c)
        q, k, v = ql_ref[0, r, :], kl_ref[0, r, :], vl_ref[0, r, :]
        ol_ref[0, r, :] = intra(q, k, v) + _dot(q * sc_ref[0], sf_ref[...])
        update(sf_ref, k, v, 0)
    for i in range(n - 1, -1, -1):
        r = slice(i * c, (i + 1) * c)
        q, k, v = ql_ref[0, r, :], kl_ref[0, r, :], vl_ref[0, r, :]
        ol_ref[0, r, :] += _dot(q * sc_ref[2], sb_ref[...])
        update(sb_ref, k, v, 1)


def _retention(q_l, k_l, v_l, q_c, k_c, v_c, dsum, scales, gch):
    b, tl, w = q_l.shape
    tc = q_c.shape[1]
    tok = lambda t: pl.BlockSpec((1, t, w), lambda i: (i, 0, 0))
    return pl.pallas_call(
        _ret_kernel,
        grid=(b,),
        in_specs=[tok(tl)] * 3 + [tok(tc)] * 3 + [_const_spec(dsum.shape), _const_spec(scales.shape),
                                                  _const_spec(gch.shape)],
        out_specs=[tok(tl), tok(tc)],
        out_shape=[jax.ShapeDtypeStruct((b, tl, w), f32), jax.ShapeDtypeStruct((b, tc, w), f32)],
        scratch_shapes=[pltpu.VMEM((w, w), f32), pltpu.VMEM((w, w), f32)],
        compiler_params=_cparams(("parallel",)),
        name="retention",
    )(q_l, k_l, v_l, q_c, k_c, v_c, dsum, scales, gch)


def _short_conv(x, w):
    k_w = w.shape[0]
    pad = k_w // 2
    t = x.shape[1]
    xp = jnp.pad(x, ((0, 0), (pad, pad), (0, 0)))
    return sum(xp[:, j:j + t] * w[j] for j in range(k_w))


def _hyena_filters(t_len, lp):
    pos = jnp.arange(t_len, dtype=f32)
    t = jnp.linspace(0.0, 1.0, t_len, dtype=f32)
    bands = (HY_EMB - 1) // 2
    f = jnp.linspace(1e-4, bands - 1, bands, dtype=f32)
    ang = (2.0 * math.pi / t_len) * pos[:, None] * f[None, :]
    z = jnp.concatenate([t[:, None], jnp.cos(ang), -jnp.sin(ang)], axis=-1)
    h = jnp.sin(lp['hy_f_freq1'] * (z @ lp['hy_f_w1'] + lp['hy_f_b1']))
    h = jnp.sin(lp['hy_f_freq2'] * (h @ lp['hy_f_w2'] + lp['hy_f_b2']))
    h = (h @ lp['hy_f_w3']).reshape(t_len, HY_ORDER, 2, HY_WIDTH)
    max_decay = math.log(HY_TARGET) / HY_FAST_DECAY
    min_decay = math.log(HY_TARGET) / HY_SLOW_DECAY
    deltas = jnp.abs(jnp.linspace(min_decay, max_decay, HY_WIDTH, dtype=f32))
    window = jnp.exp(-t[:, None] * deltas[None, :])
    return h * window[:, None, None, :]


def _hyena(p, lp):
    b, t, _ = p.shape
    hb, c = HY_BLOCK, HY_WIDTH
    nb = t // hb
    p = _short_conv(p, lp['hy_conv_w']) + lp['hy_conv_b']
    p3 = p.reshape(b, nb, hb, 3, c).transpose(3, 4, 1, 0, 2).reshape(3, c, nb * b, hb)
    h = _hyena_filters(t, lp)
    kern = jnp.concatenate([h[:, :, 0], jnp.zeros_like(h[:1, :, 0]), h[:0:-1, :, 1]], axis=0)
    kext = jnp.concatenate([kern[2 * t - hb:], kern], axis=0).transpose(2, 1, 0)[:, :, None, :]
    bias = jnp.broadcast_to(lp['hy_bias'].T[:, :, None, None], (c, HY_ORDER, 1, hb))
    z = _hyena_conv(p3, kext, bias, nb)
    return z.reshape(c, nb, b, hb).transpose(2, 1, 3, 0).reshape(b, t, c)


def _gdn_gate_inputs(p):
    b, t, _ = p.shape
    ab = p[..., OFF_AB:OFF_AB + 4 * GDN_HEADS].reshape(b, t, 4, GDN_HEADS)
    return ab.transpose(0, 3, 2, 1)


def _gdn_gate_params(a_log, dt_bias):
    def rows(x):
        x = jnp.pad(x.T, ((0, 0), (0, SUBLANES - 2)))
        return jnp.broadcast_to(x[:, :, None], (GDN_HEADS, SUBLANES, LANES))
    return rows(a_log), rows(dt_bias)


def _rope(x, cos, sin):
    b, t, _ = x.shape
    x = x.reshape(b, t, RET_HEADS, RET_DK)
    x1, x2 = jnp.split(x, 2, axis=-1)
    cos, sin = cos[:, None, :], sin[:, None, :]
    return jnp.concatenate([x1 * cos - x2 * sin, x1 * sin + x2 * cos], axis=-1).reshape(b, t, -1)


def _ret_constants(lp):
    c = RET_CHUNK
    lg = jax.nn.log_sigmoid(lp['ret_decay_logit'])
    idx = jnp.arange(c, dtype=f32)
    rel = idx[:, None] - idx[None, :]
    d_f = jnp.exp(jnp.where(rel >= 0, rel * lg[0][:, None, None], -jnp.inf))
    d_b = jnp.exp(jnp.where(rel <= 0, -rel * lg[1][:, None, None], -jnp.inf))
    lane = lambda x: jnp.repeat(x, RET_DK, axis=-1)
    scales = jnp.stack([lane(jnp.exp((idx + 1.0)[:, None] * lg[0])),
                        lane(jnp.exp((c - 1.0 - idx)[:, None] * lg[0])),
                        lane(jnp.exp((c - idx)[:, None] * lg[1])),
                        lane(jnp.exp(idx[:, None] * lg[1]))])
    gch = lane(jnp.exp(c * lg))[:, None, :]
    return d_f + d_b, scales, gch


def _ret_output(o, gate):
    b, t, _ = o.shape
    o = o.reshape(b, t, RET_HEADS, RET_DV)
    mu = jnp.mean(o, axis=-1, keepdims=True)
    var = jnp.mean(jnp.square(o - mu), axis=-1, keepdims=True)
    return ((o - mu) * lax.rsqrt(var + EPS)).reshape(b, t, -1) * _silu(gate)


def _reorder_w_in(w):
    h2 = 2 * GDN_HEADS
    a0 = OFF_Z + GDN_HEADS * GDN_DV
    parts = [w[:, :a0], w[:, a0 + 2 * h2:], w[:, a0:a0 + 2 * h2],
             jnp.zeros((w.shape[0], LANES - 2 * h2), w.dtype)]
    return jnp.concatenate(parts, axis=1)


def _ret_slices(p):
    w = RET_HEADS * RET_DK
    return tuple(p[..., OFF_RET + i * w:OFF_RET + (i + 1) * w] for i in range(4))


def kernel(x, c, ctx, c_ctx, mod_w, mod_b, norm1_g, w_in, hy_conv_w, hy_conv_b, hy_f_w1, hy_f_b1, hy_f_freq1, hy_f_w2, hy_f_b2, hy_f_freq2, hy_f_w3, hy_bias, gdn_conv_w, gdn_a_log, gdn_dt_bias, gdn_norm_g, ret_decay_logit, w_out, norm2_g, ffn_w_in, ffn_w_out, final_norm_g):
    bsz, t_lat, d = x.shape
    rows_n = t_lat // GRID_W
    row = jnp.repeat(jnp.arange(rows_n, dtype=f32), GRID_W)
    colp = jnp.tile(jnp.arange(GRID_W, dtype=f32), rows_n)
    nf = RET_DK // 4
    inv = ROPE_BASE ** (-jnp.arange(nf, dtype=f32) / nf)
    ang = jnp.concatenate([row[:, None] * inv, colp[:, None] * inv], axis=-1)
    cos, sin = jnp.cos(ang), jnp.sin(ang)

    pad_rows = (-(bsz + 1)) % SUBLANES
    cs = jnp.concatenate([c, c_ctx[None], jnp.zeros((pad_rows, d), f32)], axis=0)
    gf = final_norm_g[None]
    for i in range(DEPTH):
        need_ctx = i < DEPTH - 1
        lp = {'hy_conv_w': hy_conv_w[i], 'hy_conv_b': hy_conv_b[i], 'hy_f_w1': hy_f_w1[i],
              'hy_f_b1': hy_f_b1[i], 'hy_f_freq1': hy_f_freq1[i], 'hy_f_w2': hy_f_w2[i],
              'hy_f_b2': hy_f_b2[i], 'hy_f_freq2': hy_f_freq2[i], 'hy_f_w3': hy_f_w3[i],
              'hy_bias': hy_bias[i], 'ret_decay_logit': ret_decay_logit[i]}
        mod = _mod_vectors(cs, mod_w[i].astype(bf16), mod_b[i][None])
        mod_lat = mod[:bsz].reshape(bsz, N_MOD, d)
        mod_ctx = jnp.broadcast_to(mod[bsz].reshape(1, N_MOD, d), (bsz, N_MOD, d))
        w_in16 = _reorder_w_in(w_in[i]).astype(bf16)
        g1 = norm1_g[i][None]
        p_l = _in_proj(x, g1, mod_lat, w_in16, 256)
        p_c = _in_proj(ctx, g1, mod_ctx, w_in16, 256)

        hy_l = _hyena(p_l[..., OFF_HY:OFF_HY + 3 * HY_WIDTH], lp)
        pa, pd = _gdn_gate_params(gdn_a_log[i], gdn_dt_bias[i])
        gd_l, gd_c = _gdn(p_l, _gdn_gate_inputs(p_l), p_c, _gdn_gate_inputs(p_c), gdn_conv_w[i], pa, pd,
                          gdn_norm_g[i][None])

        rq_l, rk_l, rv_l, rg_l = _ret_slices(p_l)
        rq_c, rk_c, rv_c, rg_c = _ret_slices(p_c)
        scale = RET_DK ** -0.5
        dsum, scales, gch = _ret_constants(lp)
        ro_l, ro_c = _retention(_rope(rq_l, cos, sin), _rope(rk_l * scale, cos, sin), rv_l,
                                rq_c, rk_c * scale, rv_c, dsum, scales, gch)
        rt_l = _ret_output(ro_l, rg_l)

        wo16, w116, w216 = w_out[i].astype(bf16), ffn_w_in[i].astype(bf16), ffn_w_out[i].astype(bf16)
        g2 = norm2_g[i][None]
        x = _out_ffn(x, hy_l, gd_l, rt_l, mod_lat, wo16, g2, w116, w216, gf, 256, final=not need_ctx)
        if need_ctx:
            hy_c = _hyena(p_c[..., OFF_HY:OFF_HY + 3 * HY_WIDTH], lp)
            rt_c = _ret_output(ro_c, rg_c)
            ctx = _out_ffn(ctx, hy_c, gd_c, rt_c, mod_ctx, wo16, g2, w116, w216, gf, 256, final=False)
    return x
```

```python
import functools
import math

import jax
import jax.numpy as jnp
from jax import lax
from jax.experimental import pallas as pl
from jax.experimental.pallas import tpu as pltpu

D_MODEL = 1024
DEPTH = 2
GRID_W = 64
EPS = 1e-6
N_MOD = 6

HY_WIDTH = D_MODEL // 4
HY_ORDER = 2
HY_EMB = 33
HY_FAST_DECAY = 0.3
HY_SLOW_DECAY = 1.5
HY_TARGET = 1e-2
GDN_HEADS = D_MODEL // 256
GDN_DK = 128
GDN_DV = 128
RET_HEADS = D_MODEL // 256
RET_DK = 64
RET_DV = 64
ROPE_BASE = 10000.0
FFN_HIDDEN = ((8 * D_MODEL + 3 * 256 - 1) // (3 * 256)) * 256

GDN_CHUNK = 64
GDN_ROWS = 256
GDN_HP = 2
GDN_GROUP = 12
RET_CHUNK = 256
HY_BLOCK = 256
LANES = 128
SUBLANES = 8

GDN_QKV = 2 * GDN_HEADS * GDN_DK + GDN_HEADS * GDN_DV
OFF_HY = 0
OFF_QKV = OFF_HY + 3 * HY_WIDTH
OFF_Z = OFF_QKV + GDN_QKV
OFF_RET = OFF_Z + GDN_HEADS * GDN_DV
OFF_AB = OFF_RET + 4 * RET_HEADS * RET_DK
IN_PAD = OFF_AB + LANES

VMEM_LIMIT = 56 * 1024 * 1024
GDN_VMEM_LIMIT = 60 * 1024 * 1024

f32 = jnp.float32
bf16 = jnp.bfloat16


def _cparams(sem):
    return pltpu.CompilerParams(dimension_semantics=sem, vmem_limit_bytes=VMEM_LIMIT)


def _const_spec(shape):
    nd = len(shape)
    return pl.BlockSpec(shape, lambda *_: (0,) * nd, pipeline_mode=pl.Buffered(1))


def _rms(x, g):
    return x * lax.rsqrt(jnp.mean(x * x, axis=-1, keepdims=True) + EPS) * g


def _silu(x):
    return x * jax.nn.sigmoid(x)


def _dot(a, b):
    return jnp.dot(a.astype(bf16), b.astype(bf16), preferred_element_type=f32)


def _dot_nt(a, b):
    return lax.dot_general(a.astype(bf16), b.astype(bf16), (((1,), (1,)), ((), ())),
                           preferred_element_type=f32)


def _dot_tn(a, b):
    return lax.dot_general(a.astype(bf16), b.astype(bf16), (((0,), (0,)), ((), ())),
                           preferred_element_type=f32)


def _mod_kernel(c_ref, w_ref, b_ref, o_ref):
    o_ref[...] = _dot(_silu(c_ref[...]), w_ref[...]) + b_ref[...]


def _mod_vectors(cs, w16, b):
    rows, d = cs.shape
    n = w16.shape[1]
    tn = 1536
    return pl.pallas_call(
        _mod_kernel,
        grid=(n // tn,),
        in_specs=[pl.BlockSpec((rows, d), lambda j: (0, 0)),
                  pl.BlockSpec((d, tn), lambda j: (0, j)),
                  pl.BlockSpec((1, tn), lambda j: (0, j))],
        out_specs=pl.BlockSpec((rows, tn), lambda j: (0, j)),
        out_shape=jax.ShapeDtypeStruct((rows, n), f32),
        compiler_params=_cparams(("parallel",)),
        name="mod_vectors",
    )(cs, w16, b)


def _in_proj_kernel(x_ref, g_ref, mod_ref, w_ref, o_ref):
    x = x_ref[0]
    mod = mod_ref[0]
    h = _rms(x, g_ref[...]) * (1.0 + mod[1:2]) + mod[0:1]
    h16 = h.astype(bf16)
    n = w_ref.shape[1]
    step = 512
    for j in range(0, n, step):
        e = min(j + step, n)
        o_ref[0, :, j:e] = jnp.dot(h16, w_ref[:, j:e], preferred_element_type=f32)


def _in_proj(x, g, mod, w16, tm):
    b, t, d = x.shape
    n = w16.shape[1]
    return pl.pallas_call(
        _in_proj_kernel,
        grid=(b, t // tm),
        in_specs=[pl.BlockSpec((1, tm, d), lambda i, j: (i, j, 0)),
                  _const_spec((1, d)),
                  pl.BlockSpec((1, N_MOD, d), lambda i, j: (i, 0, 0)),
                  _const_spec((d, n))],
        out_specs=pl.BlockSpec((1, tm, n), lambda i, j: (i, j, 0)),
        out_shape=jax.ShapeDtypeStruct((b, t, n), f32),
        compiler_params=_cparams(("parallel", "parallel")),
        name="in_proj",
    )(x, g, mod, w16)


def _out_ffn_kernel(x_ref, hy_ref, gd_ref, rt_ref, mod_ref, wo_ref, g2_ref, w1_ref, w2_ref, gf_ref, o_ref,
                    *, final):
    x = x_ref[0]
    mod = mod_ref[0]
    o0, o1, o2 = HY_WIDTH, HY_WIDTH + GDN_HEADS * GDN_DV, D_MODEL
    y = (jnp.dot(hy_ref[0].astype(bf16), wo_ref[0:o0, :], preferred_element_type=f32)
         + jnp.dot(gd_ref[0].astype(bf16), wo_ref[o0:o1, :], preferred_element_type=f32)
         + jnp.dot(rt_ref[0].astype(bf16), wo_ref[o1:o2, :], preferred_element_type=f32))
    x = x + mod[2:3] * y
    h16 = (_rms(x, g2_ref[...]) * (1.0 + mod[4:5]) + mod[3:4]).astype(bf16)
    step = 256
    acc = jnp.zeros_like(x)
    for j in range(0, FFN_HIDDEN, step):
        gate = jnp.dot(h16, w1_ref[:, j:j + step], preferred_element_type=f32)
        up = jnp.dot(h16, w1_ref[:, FFN_HIDDEN + j:FFN_HIDDEN + j + step], preferred_element_type=f32)
        acc = acc + jnp.dot((_silu(gate) * up).astype(bf16), w2_ref[j:j + step, :], preferred_element_type=f32)
    x = x + mod[5:6] * acc
    if final:
        x = _rms(x, gf_ref[...])
    o_ref[0] = x


def _out_ffn(x, hy, gd, rt, mod, wo16, g2, w116, w216, gf, tm, final):
    b, t, d = x.shape
    tok = lambda w: pl.BlockSpec((1, tm, w), lambda i, j: (i, j, 0))
    return pl.pallas_call(
        functools.partial(_out_ffn_kernel, final=final),
        grid=(b, t // tm),
        in_specs=[tok(d), tok(hy.shape[-1]), tok(gd.shape[-1]), tok(rt.shape[-1]),
                  pl.BlockSpec((1, N_MOD, d), lambda i, j: (i, 0, 0)),
                  _const_spec(wo16.shape), _const_spec((1, d)), _const_spec(w116.shape),
                  _const_spec(w216.shape), _const_spec((1, d))],
        out_specs=tok(d),
        out_shape=jax.ShapeDtypeStruct((b, t, d), f32),
        compiler_params=_cparams(("parallel", "parallel")),
        name="out_ffn",
    )(x, hy, gd, rt, mod, wo16, g2, w116, w216, gf)


def _hyena_kernel(v_ref, x1_ref, x2_ref, k_ref, b_ref, o_ref, acc_ref, *, nb, cb):
    hb = HY_BLOCK
    rb = v_ref.shape[1] // nb

    def conv(u16, kext):
        for d in range(-(nb - 1), nb):
            dd = d % (2 * nb)
            win = jnp.broadcast_to(kext[:, hb * dd:hb * dd + 2 * hb], (hb, 2 * hb))
            toep = pltpu.roll(win, 0, 1, stride=1, stride_axis=0)[:, hb:].astype(bf16)
            t0, t1 = max(0, d), min(nb - 1, nb - 1 + d) + 1
            contrib = jnp.dot(u16[rb * (t0 - d):rb * (t1 - d)], toep, preferred_element_type=f32)
            if d == -(nb - 1):
                acc_ref[...] = jnp.zeros_like(acc_ref)
            acc_ref[rb * t0:rb * t1, :] += contrib
        return acc_ref[...]

    def body(c, carry):
        z = v_ref[c]
        for n, gate_ref in enumerate((x1_ref, x2_ref)):
            y = conv(z.astype(bf16), k_ref[c, n])
            z = gate_ref[c] * (y + z * b_ref[c, n])
        o_ref[c] = z
        return carry

    lax.fori_loop(0, cb, body, 0)


def _hyena_conv(p3, kext, bias, nb, cb=8):
    _, c, r, hb = p3.shape
    klen = kext.shape[-1]
    sig = lambda s: pl.BlockSpec((None, cb, r, hb), lambda i, s=s: (s, i, 0, 0))
    return pl.pallas_call(
        functools.partial(_hyena_kernel, nb=nb, cb=cb),
        grid=(c // cb,),
        in_specs=[sig(0), sig(1), sig(2),
                  pl.BlockSpec((cb, 2, 1, klen), lambda i: (i, 0, 0, 0)),
                  pl.BlockSpec((cb, 2, 1, hb), lambda i: (i, 0, 0, 0))],
        out_specs=pl.BlockSpec((cb, r, hb), lambda i: (i, 0, 0)),
        out_shape=jax.ShapeDtypeStruct((c, r, hb), f32),
        scratch_shapes=[pltpu.VMEM((r, hb), f32)],
        compiler_params=_cparams(("parallel",)),
        name=f"hyena_conv_nb{nb}",
    )(p3, p3, p3, kext, bias)


def _gdn_kernel(ql_ref, kl_ref, vl_ref, zl_ref, abl_ref, qc_ref, kc_ref, vc_ref, zc_ref, abc_ref,
                cwq_ref, cwk_ref, cwv_ref, pa_ref, pd_ref, ng_ref,
                ol_ref, oc_ref,
                q_s, k_s, v_s, cf_s, u_s, wq_s, kgt_s, at_s, egl_s, o_s, st_s):
    c = GDN_CHUNK
    rb = GDN_ROWS
    tc, tl = qc_ref.shape[1], ql_ref.shape[1]
    ncc, nc = tc // c, (tc + tl) // c
    dk = GDN_DK

    def conv_rows(x_ref, w_ref, r0, t, ls):
        x = x_ref[0, pl.ds(r0, rb), ls]
        before = pl.multiple_of(jnp.maximum(r0 - SUBLANES, 0), SUBLANES)
        after = pl.multiple_of(jnp.minimum(r0 + rb, t - SUBLANES), SUBLANES)
        prev = x_ref[0, pl.ds(before, SUBLANES), ls][SUBLANES - 1:SUBLANES]
        nxt = x_ref[0, pl.ds(after, SUBLANES), ls][0:1]
        prev = jnp.where(r0 > 0, prev, 0.0)
        nxt = jnp.where(r0 + rb < t, nxt, 0.0)
        row = lax.broadcasted_iota(jnp.int32, x.shape, 0)
        xp = jnp.where(row == 0, prev, pltpu.roll(x, 1, 0))
        xn = jnp.where(row == rb - 1, nxt, pltpu.roll(x, rb - 1, 0))
        w = w_ref[:, ls]
        return _silu(xp * w[0:1] + x * w[1:2] + xn * w[2:3])

    def l2n(x):
        return x * lax.rsqrt(jnp.sum(x * x, axis=-1, keepdims=True) + EPS)

    def prep(hh, q_ref, k_ref, v_ref, ab_ref, t, base):
        ls = slice(hh * LANES, (hh + 1) * LANES)

        def body(i, carry):
            r0 = pl.multiple_of(i * rb, rb)
            dst = pl.ds(pl.multiple_of(base + i * rb, rb), rb)
            q_s[dst, :] = l2n(conv_rows(q_ref, cwq_ref, r0, t, ls)) * (dk ** -0.5)
            k_s[dst, :] = l2n(conv_rows(k_ref, cwk_ref, r0, t, ls))
            v_s[dst, :] = conv_rows(v_ref, cwv_ref, r0, t, ls)
            return carry
        lax.fori_loop(0, t // rb, body, 0)

        ab = ab_ref[0, hh]
        ab = jnp.concatenate([ab, jnp.zeros((SUBLANES - ab.shape[0], t), f32)], axis=0)
        rowi = lax.broadcasted_iota(jnp.int32, ab.shape, 0)
        pos = lax.broadcasted_iota(jnp.int32, ab.shape, 1) % c
        tile = lambda p: jnp.concatenate([p] * (t // LANES), axis=1)
        g = -jnp.exp(tile(pa_ref[hh])) * jax.nn.softplus(ab + tile(pd_ref[hh]))
        acc_f, acc_b = g, g
        s = 1
        while s < c:
            acc_f = acc_f + jnp.where(pos >= s, pltpu.roll(acc_f, s, 1), 0.0)
            acc_b = acc_b + jnp.where(pos < c - s, pltpu.roll(acc_b, t - s, 1), 0.0)
            s *= 2
        cf_t = jnp.where(rowi == 0, acc_f, jnp.where(rowi == 1, acc_b, jax.nn.sigmoid(ab)))
        pad = jnp.zeros((LANES - SUBLANES, LANES), f32)
        for j in range(t // LANES):
            blk = jnp.concatenate([cf_t[:, j * LANES:(j + 1) * LANES], pad], axis=0)
            cf_s[base + j * LANES:base + (j + 1) * LANES, :] = blk.T

    lane = lax.broadcasted_iota(jnp.int32, (c, 2 * c), 1)
    row = lax.broadcasted_iota(jnp.int32, (c, 2 * c), 0)
    fwd = lane < c
    col = lane % c
    incl = (fwd & (row >= col)) | (~fwd & (row <= col))
    strict = (fwd & (row > col)) | (~fwd & (row < col))

    def blockdiag(a):
        return jnp.concatenate([jnp.where(fwd, a, 0.0), jnp.where(fwd, 0.0, a)], axis=0).astype(bf16)

    def chunk_group(hh, cis):
        n = range(len(cis))
        xf, xb = 2 * hh, 2 * hh + 1
        rows = [pl.ds(pl.multiple_of(ci * c, c), c) for ci in cis]
        cf = [cf_s[r, :] for r in rows]
        k = [k_s[r, :] for r in rows]
        q = [q_s[r, :] for r in rows]
        bc = lambda x, j: jnp.broadcast_to(x[:, j:j + 1], (c, LANES))
        gf_c, gb_c = [bc(x, 0) for x in cf], [bc(x, 1) for x in cf]
        bf_c, bb_c = [bc(x, 2) for x in cf], [bc(x, 3) for x in cf]
        cf_t = [jnp.concatenate([x, x], axis=0).T for x in cf]
        g_r = [jnp.where(fwd, jnp.broadcast_to(x[0:1], (c, 2 * c)), jnp.broadcast_to(x[1:2], (c, 2 * c)))
               for x in cf_t]
        decay = [jnp.where(incl, jnp.exp(jnp.where(incl, jnp.where(fwd, gf_c[g], gb_c[g]) - g_r[g], 0.0)), 0.0)
                 for g in n]
        kq = [_dot_nt(jnp.concatenate([k[g], q[g]], axis=0), jnp.concatenate([k[g], k[g]], axis=0))
              for g in n]
        lm = [jnp.where(strict, kq[g][:c] * jnp.where(fwd, bf_c[g], bb_c[g]) * decay[g], 0.0) for g in n]
        for g in n:
            attn = (kq[g][c:] * decay[g]).astype(bf16)
            at_s[xf, cis[g]] = attn[:, :c]
            at_s[xb, cis[g]] = attn[:, c:]
        a = [jnp.dot(x.astype(bf16), blockdiag(x), preferred_element_type=f32) for x in lm]
        tp = [-x for x in lm]
        p = 2
        while 2 * p < c:
            ta = [jnp.dot(jnp.concatenate([tp[g], a[g]], axis=0).astype(bf16), blockdiag(a[g]),
                          preferred_element_type=f32) for g in n]
            tp = [tp[g] + a[g] + ta[g][:c] for g in n]
            a = [x[c:] for x in ta]
            p *= 2
        tp = [tp[g] + a[g] + jnp.dot(tp[g].astype(bf16), blockdiag(a[g]), preferred_element_type=f32)
              for g in n]
        for g in n:
            ci, r = cis[g], rows[g]
            v = v_s[r, :]
            eg_f, eg_b = jnp.exp(gf_c[g]), jnp.exp(gb_c[g])
            rhs = jnp.concatenate([jnp.concatenate([v * bf_c[g], k[g] * (bf_c[g] * eg_f)], axis=1),
                                   jnp.concatenate([v * bb_c[g], k[g] * (bb_c[g] * eg_b)], axis=1)], axis=0)
            uw = rhs + jnp.dot(blockdiag(tp[g]), rhs.astype(bf16), preferred_element_type=f32)
            u_s[xf, r, :] = uw[:c, :dk]
            u_s[xb, r, :] = uw[c:, :dk]
            wq_s[xf, ci] = jnp.concatenate([uw[:c, dk:], q[g] * eg_f], axis=0).astype(bf16)
            wq_s[xb, ci] = jnp.concatenate([uw[c:, dk:], q[g] * eg_b], axis=0).astype(bf16)
            gl_f, gl_b = gf_c[g][c - 1:c], gb_c[g][0:1]
            kgt_s[xf, ci] = (k[g] * jnp.exp(gl_f - gf_c[g])).T.astype(bf16)
            kgt_s[xb, ci] = (k[g] * jnp.exp(gl_b - gb_c[g])).T.astype(bf16)
            egl_s[hh, ci] = jnp.concatenate([jnp.exp(gl_f), jnp.exp(gl_b),
                                             jnp.zeros((SUBLANES - 2, LANES), f32)], axis=0)
            o_s[hh, r, :] = jnp.zeros((c, GDN_DV), f32)

    for hh in range(GDN_HP):
        prep(hh, qc_ref, kc_ref, vc_ref, abc_ref, tc, 0)
        prep(hh, ql_ref, kl_ref, vl_ref, abl_ref, tl, tc)

        def chunk_body(i, carry, hh=hh):
            chunk_group(hh, [GDN_GROUP * i + g for g in range(GDN_GROUP)])
            return carry

        lax.fori_loop(0, nc // GDN_GROUP, chunk_body, 0)

    st_s[...] = jnp.zeros_like(st_s)
    chains = [(hh, d) for hh in range(GDN_HP) for d in range(2)]

    def scan_body(i, carry):
        ci = (i, jnp.where(i < ncc, ncc - 1 - i, nc + ncc - 1 - i))
        rows = [pl.ds(pl.multiple_of(x * c, c), c) for x in ci]
        n = range(len(chains))
        s = [st_s[x] for x in n]
        ws = [jnp.dot(wq_s[x, ci[chains[x][1]]], s[x].astype(bf16), preferred_element_type=f32) for x in n]
        v_new = [(u_s[x, rows[chains[x][1]], :] - ws[x][:c]).astype(bf16) for x in n]
        upd = [jnp.dot(kgt_s[x, ci[chains[x][1]]], v_new[x], preferred_element_type=f32) for x in n]
        out = [ws[x][c:] + jnp.dot(at_s[x, ci[chains[x][1]]], v_new[x], preferred_element_type=f32) for x in n]
        for x in n:
            hh, d = chains[x]
            st_s[x] = s[x] * egl_s[hh, ci[d]][d:d + 1] + upd[x]
            o_s[hh, rows[d], :] += out[x]
        return carry

    lax.fori_loop(0, nc, scan_body, 0)

    def finish(hh, z_ref, o_ref, t, base):
        ls = slice(hh * LANES, (hh + 1) * LANES)

        def body(i, carry):
            r0 = pl.ds(pl.multiple_of(i * rb, rb), rb)
            o = o_s[hh, pl.ds(pl.multiple_of(base + i * rb, rb), rb), :]
            y = o * lax.rsqrt(jnp.mean(o * o, axis=-1, keepdims=True) + EPS) * ng_ref[...]
            o_ref[0, r0, ls] = y * _silu(z_ref[0, r0, ls])
            return carry
        lax.fori_loop(0, t // rb, body, 0)

    for hh in range(GDN_HP):
        finish(hh, zc_ref, oc_ref, tc, 0)
        finish(hh, zl_ref, ol_ref, tl, tc)


def _gdn(p_l, ab_l, p_c, ab_c, conv_w, pa, pd, norm_g):
    b, tl, _ = p_l.shape
    tc = p_c.shape[1]
    h, c, hp = GDN_HEADS, GDN_CHUNK, GDN_HP
    nc = (tl + tc) // c
    assert tl % GDN_ROWS == 0 and tc % GDN_ROWS == 0 and nc % GDN_GROUP == 0
    qb, zb = OFF_QKV // LANES, OFF_Z // LANES
    assert h % hp == 0 and qb % hp == 0 and zb % hp == 0
    w = hp * LANES

    def specs(t):
        heads = lambda off: pl.BlockSpec((1, t, w), lambda i, j, off=off // hp: (i, 0, off + j))
        return [heads(qb), heads(qb + h), heads(qb + 2 * h), heads(zb),
                pl.BlockSpec((1, hp, 4, t), lambda i, j: (i, j, 0, 0))]

    cw = lambda off: pl.BlockSpec((3, w), lambda i, j, off=off // hp: (0, off + j))
    par = pl.BlockSpec((hp, SUBLANES, LANES), lambda i, j: (j, 0, 0))
    out = lambda t: pl.BlockSpec((1, t, w), lambda i, j: (i, 0, j))
    tt = tl + tc
    return pl.pallas_call(
        _gdn_kernel,
        grid=(b, h // hp),
        in_specs=specs(tl) + specs(tc) + [cw(0), cw(h), cw(2 * h), par, par, _const_spec((1, GDN_DV))],
        out_specs=[out(tl), out(tc)],
        out_shape=[jax.ShapeDtypeStruct((b, tl, h * GDN_DV), f32),
                   jax.ShapeDtypeStruct((b, tc, h * GDN_DV), f32)],
        scratch_shapes=[pltpu.VMEM((tt, GDN_DK), f32), pltpu.VMEM((tt, GDN_DK), f32),
                        pltpu.VMEM((tt, GDN_DV), f32), pltpu.VMEM((tt, LANES), f32),
                        pltpu.VMEM((2 * hp, tt, GDN_DV), f32), pltpu.VMEM((2 * hp, nc, 2 * c, GDN_DK), bf16),
                        pltpu.VMEM((2 * hp, nc, GDN_DK, c), bf16), pltpu.VMEM((2 * hp, nc, c, c), bf16),
                        pltpu.VMEM((hp, nc, SUBLANES, LANES), f32), pltpu.VMEM((hp, tt, GDN_DV), f32),
                        pltpu.VMEM((2 * hp, GDN_DK, GDN_DV), f32)],
        compiler_params=pltpu.CompilerParams(dimension_semantics=("parallel", "parallel"),
                                             vmem_limit_bytes=GDN_VMEM_LIMIT),
        name="gated_deltanet",
    )(p_l, p_l, p_l, p_l, ab_l, p_c, p_c, p_c, p_c, ab_c, conv_w, conv_w, conv_w, pa, pd, norm_g)


def _ret_kernel(ql_ref, kl_ref, vl_ref, gl_ref, qc_ref, kc_ref, vc_ref, gc_ref,
                cos_ref, sin_ref, dsum_ref, sc_ref, gch_ref,
                ol_ref, oc_ref, sf_ref, sb_ref, q_s, k_s, o_s):
    c = RET_CHUNK
    w = RET_HEADS * RET_DK
    ri = lax.broadcasted_iota(jnp.int32, (w, w), 0) // RET_DK
    ci = lax.broadcasted_iota(jnp.int32, (w, w), 1) // RET_DK
    blockdiag = ri == ci
    lane_head = lax.broadcasted_iota(jnp.int32, (c, w), 1) // RET_DK

    def intra(q, k, v):
        out = jnp.zeros((c, w), f32)
        for h in range(RET_HEADS):
            m = lane_head == h
            scores = _dot_nt(jnp.where(m, q, 0.0), k) * dsum_ref[h]
            out = out + _dot(scores, jnp.where(m, v, 0.0))
        return out

    def update(s_ref, k, v, d):
        new = jnp.where(blockdiag, _dot_tn(k * sc_ref[2 * d + 1], v), 0.0)
        s_ref[...] = s_ref[...] * gch_ref[d] + new

    first_half = lax.broadcasted_iota(jnp.int32, (c, w), 1) % RET_DK < RET_DK // 2
    avg = jnp.where(blockdiag, 1.0 / RET_DV, 0.0).astype(bf16)

    def rope(x, r):
        swapped = jnp.where(first_half, pltpu.roll(x, w - RET_DK // 2, 1), pltpu.roll(x, RET_DK // 2, 1))
        return x * cos_ref[r, :] + swapped * sin_ref[r, :]

    def head_mean(x):
        hi = x.astype(bf16)
        lo = (x - hi.astype(f32)).astype(bf16)
        return jnp.dot(hi, avg, preferred_element_type=f32) + jnp.dot(lo, avg, preferred_element_type=f32)

    def finish(o, gate):
        d = o - head_mean(o)
        return d * lax.rsqrt(head_mean(d * d) + EPS) * _silu(gate)

    sf_ref[...] = jnp.zeros_like(sf_ref)
    sb_ref[...] = jnp.zeros_like(sb_ref)
    qc, kc, vc = qc_ref[0], kc_ref[0] * (RET_DK ** -0.5), vc_ref[0]
    oc_ref[0] = finish(intra(qc, kc, vc), gc_ref[0])
    update(sf_ref, kc, vc, 0)
    update(sb_ref, kc, vc, 1)
    n = ql_ref.shape[1] // c
    for i in range(n):
        r = slice(i * c, (i + 1) * c)
        q = rope(ql_ref[0, r, :], r)
        k = rope(kl_ref[0, r, :] * (RET_DK ** -0.5), r)
        v = vl_ref[0, r, :]
        q_s[r, :] = q
        k_s[r, :] = k
        o_s[r, :] = intra(q, k, v) + _dot(q * sc_ref[0], sf_ref[...])
        update(sf_ref, k, v, 0)
    for i in range(n - 1, -1, -1):
        r = slice(i * c, (i + 1) * c)
        o = o_s[r, :] + _dot(q_s[r, :] * sc_ref[2], sb_ref[...])
        ol_ref[0, r, :] = finish(o, gl_ref[0, r, :])
        update(sb_ref, k_s[r, :], vl_ref[0, r, :], 1)


def _retention(p_l, p_c, cos, sin, dsum, scales, gch):
    b, tl, _ = p_l.shape
    tc = p_c.shape[1]
    w = RET_HEADS * RET_DK
    assert OFF_RET % w == 0
    blk = lambda t, j: pl.BlockSpec((1, t, w), lambda i, j=j: (i, 0, OFF_RET // w + j))
    tok = lambda t: pl.BlockSpec((1, t, w), lambda i: (i, 0, 0))
    return pl.pallas_call(
        _ret_kernel,
        grid=(b,),
        in_specs=[blk(tl, j) for j in range(4)] + [blk(tc, j) for j in range(4)]
        + [_const_spec(x.shape) for x in (cos, sin, dsum, scales, gch)],
        out_specs=[tok(tl), tok(tc)],
        out_shape=[jax.ShapeDtypeStruct((b, tl, w), f32), jax.ShapeDtypeStruct((b, tc, w), f32)],
        scratch_shapes=[pltpu.VMEM((w, w), f32), pltpu.VMEM((w, w), f32),
                        pltpu.VMEM((tl, w), f32), pltpu.VMEM((tl, w), f32), pltpu.VMEM((tl, w), f32)],
        compiler_params=_cparams(("parallel",)),
        name="retention",
    )(p_l, p_l, p_l, p_l, p_c, p_c, p_c, p_c, cos, sin, dsum, scales, gch)


def _short_conv(x, w):
    k_w = w.shape[0]
    pad = k_w // 2
    t = x.shape[1]
    xp = jnp.pad(x, ((0, 0), (pad, pad), (0, 0)))
    return sum(xp[:, j:j + t] * w[j] for j in range(k_w))


def _hyena_filters(t_len, lp):
    pos = jnp.arange(t_len, dtype=f32)
    t = jnp.linspace(0.0, 1.0, t_len, dtype=f32)
    bands = (HY_EMB - 1) // 2
    f = jnp.linspace(1e-4, bands - 1, bands, dtype=f32)
    ang = (2.0 * math.pi / t_len) * pos[:, None] * f[None, :]
    z = jnp.concatenate([t[:, None], jnp.cos(ang), -jnp.sin(ang)], axis=-1)
    h = jnp.sin(lp['hy_f_freq1'] * (z @ lp['hy_f_w1'] + lp['hy_f_b1']))
    h = jnp.sin(lp['hy_f_freq2'] * (h @ lp['hy_f_w2'] + lp['hy_f_b2']))
    h = (h @ lp['hy_f_w3']).reshape(t_len, HY_ORDER, 2, HY_WIDTH)
    max_decay = math.log(HY_TARGET) / HY_FAST_DECAY
    min_decay = math.log(HY_TARGET) / HY_SLOW_DECAY
    deltas = jnp.abs(jnp.linspace(min_decay, max_decay, HY_WIDTH, dtype=f32))
    window = jnp.exp(-t[:, None] * deltas[None, :])
    return h * window[:, None, None, :]


def _hyena(p, lp):
    b, t, _ = p.shape
    hb, c = HY_BLOCK, HY_WIDTH
    nb = t // hb
    p = _short_conv(p, lp['hy_conv_w']) + lp['hy_conv_b']
    p3 = p.reshape(b, nb, hb, 3, c).transpose(3, 4, 1, 0, 2).reshape(3, c, nb * b, hb)
    h = _hyena_filters(t, lp)
    kern = jnp.concatenate([h[:, :, 0], jnp.zeros_like(h[:1, :, 0]), h[:0:-1, :, 1]], axis=0)
    kext = jnp.concatenate([kern[2 * t - hb:], kern], axis=0).transpose(2, 1, 0)[:, :, None, :]
    bias = jnp.broadcast_to(lp['hy_bias'].T[:, :, None, None], (c, HY_ORDER, 1, hb))
    z = _hyena_conv(p3, kext, bias, nb)
    return z.reshape(c, nb, b, hb).transpose(2, 1, 3, 0).reshape(b, t, c)


def _gdn_gate_inputs(p):
    b, t, _ = p.shape
    ab = p[..., OFF_AB:OFF_AB + 4 * GDN_HEADS].reshape(b, t, 4, GDN_HEADS)
    return ab.transpose(0, 3, 2, 1)


def _gdn_gate_params(a_log, dt_bias):
    def rows(x):
        x = jnp.pad(x.T, ((0, 0), (0, SUBLANES - 2)))
        return jnp.broadcast_to(x[:, :, None], (GDN_HEADS, SUBLANES, LANES))
    return rows(a_log), rows(dt_bias)


def _rope_tables(t_lat):
    rows_n = t_lat // GRID_W
    row = jnp.repeat(jnp.arange(rows_n, dtype=f32), GRID_W)
    colp = jnp.tile(jnp.arange(GRID_W, dtype=f32), rows_n)
    nf = RET_DK // 4
    inv = ROPE_BASE ** (-jnp.arange(nf, dtype=f32) / nf)
    ang = jnp.concatenate([row[:, None] * inv, colp[:, None] * inv], axis=-1)
    cos, sin = jnp.cos(ang), jnp.sin(ang)
    return (jnp.tile(jnp.concatenate([cos, cos], axis=-1), (1, RET_HEADS)),
            jnp.tile(jnp.concatenate([-sin, sin], axis=-1), (1, RET_HEADS)))


def _ret_constants(lp):
    c = RET_CHUNK
    lg = jax.nn.log_sigmoid(lp['ret_decay_logit'])
    idx = jnp.arange(c, dtype=f32)
    rel = idx[:, None] - idx[None, :]
    d_f = jnp.exp(jnp.where(rel >= 0, rel * lg[0][:, None, None], -jnp.inf))
    d_b = jnp.exp(jnp.where(rel <= 0, -rel * lg[1][:, None, None], -jnp.inf))
    lane = lambda x: jnp.repeat(x, RET_DK, axis=-1)
    scales = jnp.stack([lane(jnp.exp((idx + 1.0)[:, None] * lg[0])),
                        lane(jnp.exp((c - 1.0 - idx)[:, None] * lg[0])),
                        lane(jnp.exp((c - idx)[:, None] * lg[1])),
                        lane(jnp.exp(idx[:, None] * lg[1]))])
    gch = lane(jnp.exp(c * lg))[:, None, :]
    return d_f + d_b, scales, gch


def _reorder_w_in(w):
    h2 = 2 * GDN_HEADS
    a0 = OFF_Z + GDN_HEADS * GDN_DV
    parts = [w[:, :a0], w[:, a0 + 2 * h2:], w[:, a0:a0 + 2 * h2],
             jnp.zeros((w.shape[0], LANES - 2 * h2), w.dtype)]
    return jnp.concatenate(parts, axis=1)


def kernel(x, c, ctx, c_ctx, mod_w, mod_b, norm1_g, w_in, hy_conv_w, hy_conv_b, hy_f_w1, hy_f_b1, hy_f_freq1, hy_f_w2, hy_f_b2, hy_f_freq2, hy_f_w3, hy_bias, gdn_conv_w, gdn_a_log, gdn_dt_bias, gdn_norm_g, ret_decay_logit, w_out, norm2_g, ffn_w_in, ffn_w_out, final_norm_g):
    bsz, t_lat, d = x.shape
    cos, sin = _rope_tables(t_lat)
    pad_rows = (-(bsz + 1)) % SUBLANES
    cs = jnp.concatenate([c, c_ctx[None], jnp.zeros((pad_rows, d), f32)], axis=0)
    gf = final_norm_g[None]
    for i in range(DEPTH):
        need_ctx = i < DEPTH - 1
        lp = {'hy_conv_w': hy_conv_w[i], 'hy_conv_b': hy_conv_b[i], 'hy_f_w1': hy_f_w1[i],
              'hy_f_b1': hy_f_b1[i], 'hy_f_freq1': hy_f_freq1[i], 'hy_f_w2': hy_f_w2[i],
              'hy_f_b2': hy_f_b2[i], 'hy_f_freq2': hy_f_freq2[i], 'hy_f_w3': hy_f_w3[i],
              'hy_bias': hy_bias[i], 'ret_decay_logit': ret_decay_logit[i]}
        mod = _mod_vectors(cs, mod_w[i].astype(bf16), mod_b[i][None])
        mod_lat = mod[:bsz].reshape(bsz, N_MOD, d)
        mod_ctx = jnp.broadcast_to(mod[bsz].reshape(1, N_MOD, d), (bsz, N_MOD, d))
        w_in16 = _reorder_w_in(w_in[i]).astype(bf16)
        g1 = norm1_g[i][None]
        p_l = _in_proj(x, g1, mod_lat, w_in16, 256)
        p_c = _in_proj(ctx, g1, mod_ctx, w_in16, 256)

        hy_l = _hyena(p_l[..., OFF_HY:OFF_HY + 3 * HY_WIDTH], lp)
        pa, pd = _gdn_gate_params(gdn_a_log[i], gdn_dt_bias[i])
        gd_l, gd_c = _gdn(p_l, _gdn_gate_inputs(p_l), p_c, _gdn_gate_inputs(p_c), gdn_conv_w[i], pa, pd,
                          gdn_norm_g[i][None])

        dsum, scales, gch = _ret_constants(lp)
        rt_l, rt_c = _retention(p_l, p_c, cos, sin, dsum, scales, gch)

        wo16, w116, w216 = w_out[i].astype(bf16), ffn_w_in[i].astype(bf16), ffn_w_out[i].astype(bf16)
        g2 = norm2_g[i][None]
        x = _out_ffn(x, hy_l, gd_l, rt_l, mod_lat, wo16, g2, w116, w216, gf, 256, final=not need_ctx)
        if need_ctx:
            hy_c = _hyena(p_c[..., OFF_HY:OFF_HY + 3 * HY_WIDTH], lp)
            ctx = _out_ffn(ctx, hy_c, gd_c, rt_c, mod_ctx, wo16, g2, w116, w216, gf, 256, final=False)
    return x
```

```python
import functools
import math

import jax
import jax.numpy as jnp
from jax import lax
from jax.experimental import pallas as pl
from jax.experimental.pallas import tpu as pltpu

D_MODEL = 1024
DEPTH = 2
GRID_W = 64
EPS = 1e-6
N_MOD = 6

HY_WIDTH = D_MODEL // 4
HY_ORDER = 2
HY_EMB = 33
HY_FAST_DECAY = 0.3
HY_SLOW_DECAY = 1.5
HY_TARGET = 1e-2
GDN_HEADS = D_MODEL // 256
GDN_DK = 128
GDN_DV = 128
RET_HEADS = D_MODEL // 256
RET_DK = 64
RET_DV = 64
ROPE_BASE = 10000.0
FFN_HIDDEN = ((8 * D_MODEL + 3 * 256 - 1) // (3 * 256)) * 256

GDN_CHUNK = 64
GDN_ROWS = 256
GDN_HP = 2
GDN_GROUP = 12
RET_CHUNK = 256
HY_BLOCK = 256
LANES = 128
SUBLANES = 8

GDN_QKV = 2 * GDN_HEADS * GDN_DK + GDN_HEADS * GDN_DV
OFF_HY = 0
OFF_QKV = OFF_HY + 3 * HY_WIDTH
OFF_Z = OFF_QKV + GDN_QKV
OFF_RET = OFF_Z + GDN_HEADS * GDN_DV
OFF_AB = OFF_RET + 4 * RET_HEADS * RET_DK
IN_PAD = OFF_AB + LANES

VMEM_LIMIT = 56 * 1024 * 1024
GDN_VMEM_LIMIT = 60 * 1024 * 1024

f32 = jnp.float32
bf16 = jnp.bfloat16


def _cparams(sem):
    return pltpu.CompilerParams(dimension_semantics=sem, vmem_limit_bytes=VMEM_LIMIT)


def _const_spec(shape):
    nd = len(shape)
    return pl.BlockSpec(shape, lambda *_: (0,) * nd, pipeline_mode=pl.Buffered(1))


def _rms(x, g):
    return x * lax.rsqrt(jnp.mean(x * x, axis=-1, keepdims=True) + EPS) * g


def _silu(x):
    return x * jax.nn.sigmoid(x)


def _dot(a, b):
    return jnp.dot(a.astype(bf16), b.astype(bf16), preferred_element_type=f32)


def _dot_nt(a, b):
    return lax.dot_general(a.astype(bf16), b.astype(bf16), (((1,), (1,)), ((), ())),
                           preferred_element_type=f32)


def _dot_tn(a, b):
    return lax.dot_general(a.astype(bf16), b.astype(bf16), (((0,), (0,)), ((), ())),
                           preferred_element_type=f32)


def _mod_kernel(c_ref, w_ref, b_ref, o_ref):
    o_ref[...] = _dot(_silu(c_ref[...]), w_ref[...]) + b_ref[...]


def _mod_vectors(cs, w16, b):
    rows, d = cs.shape
    n = w16.shape[1]
    tn = 1536
    return pl.pallas_call(
        _mod_kernel,
        grid=(n // tn,),
        in_specs=[pl.BlockSpec((rows, d), lambda j: (0, 0)),
                  pl.BlockSpec((d, tn), lambda j: (0, j)),
                  pl.BlockSpec((1, tn), lambda j: (0, j))],
        out_specs=pl.BlockSpec((rows, tn), lambda j: (0, j)),
        out_shape=jax.ShapeDtypeStruct((rows, n), f32),
        compiler_params=_cparams(("parallel",)),
        name="mod_vectors",
    )(cs, w16, b)


def _in_proj_kernel(x_ref, g_ref, mod_ref, w_ref, o_ref):
    x = x_ref[0]
    mod = mod_ref[0]
    h = _rms(x, g_ref[...]) * (1.0 + mod[1:2]) + mod[0:1]
    h16 = h.astype(bf16)
    n = w_ref.shape[1]
    step = 512
    for j in range(0, n, step):
        e = min(j + step, n)
        o_ref[0, :, j:e] = jnp.dot(h16, w_ref[:, j:e], preferred_element_type=f32)


def _in_proj(x, g, mod, w16, tm):
    b, t, d = x.shape
    n = w16.shape[1]
    return pl.pallas_call(
        _in_proj_kernel,
        grid=(b, t // tm),
        in_specs=[pl.BlockSpec((1, tm, d), lambda i, j: (i, j, 0)),
                  _const_spec((1, d)),
                  pl.BlockSpec((1, N_MOD, d), lambda i, j: (i, 0, 0)),
                  _const_spec((d, n))],
        out_specs=pl.BlockSpec((1, tm, n), lambda i, j: (i, j, 0)),
        out_shape=jax.ShapeDtypeStruct((b, t, n), f32),
        compiler_params=_cparams(("parallel", "parallel")),
        name="in_proj",
    )(x, g, mod, w16)


def _out_ffn_kernel(x_ref, hy_ref, gd_ref, rt_ref, mod_ref, wo_ref, g2_ref, w1_ref, w2_ref, gf_ref, o_ref,
                    *, final):
    x = x_ref[0]
    mod = mod_ref[0]
    o0, o1, o2 = HY_WIDTH, HY_WIDTH + GDN_HEADS * GDN_DV, D_MODEL
    y = (jnp.dot(hy_ref[0].astype(bf16), wo_ref[0:o0, :], preferred_element_type=f32)
         + jnp.dot(gd_ref[0].astype(bf16), wo_ref[o0:o1, :], preferred_element_type=f32)
         + jnp.dot(rt_ref[0].astype(bf16), wo_ref[o1:o2, :], preferred_element_type=f32))
    x = x + mod[2:3] * y
    h16 = (_rms(x, g2_ref[...]) * (1.0 + mod[4:5]) + mod[3:4]).astype(bf16)
    step = 256
    acc = jnp.zeros_like(x)
    for j in range(0, FFN_HIDDEN, step):
        gate = jnp.dot(h16, w1_ref[:, j:j + step], preferred_element_type=f32)
        up = jnp.dot(h16, w1_ref[:, FFN_HIDDEN + j:FFN_HIDDEN + j + step], preferred_element_type=f32)
        acc = acc + jnp.dot((_silu(gate) * up).astype(bf16), w2_ref[j:j + step, :], preferred_element_type=f32)
    x = x + mod[5:6] * acc
    if final:
        x = _rms(x, gf_ref[...])
    o_ref[0] = x


def _out_ffn(x, hy, gd, rt, mod, wo16, g2, w116, w216, gf, tm, final):
    b, t, d = x.shape
    tok = lambda w: pl.BlockSpec((1, tm, w), lambda i, j: (i, j, 0))
    return pl.pallas_call(
        functools.partial(_out_ffn_kernel, final=final),
        grid=(b, t // tm),
        in_specs=[tok(d), tok(hy.shape[-1]), tok(gd.shape[-1]), tok(rt.shape[-1]),
                  pl.BlockSpec((1, N_MOD, d), lambda i, j: (i, 0, 0)),
                  _const_spec(wo16.shape), _const_spec((1, d)), _const_spec(w116.shape),
                  _const_spec(w216.shape), _const_spec((1, d))],
        out_specs=tok(d),
        out_shape=jax.ShapeDtypeStruct((b, t, d), f32),
        compiler_params=_cparams(("parallel", "parallel")),
        name="out_ffn",
    )(x, hy, gd, rt, mod, wo16, g2, w116, w216, gf)


def _hyena_kernel(v_ref, x1_ref, x2_ref, k_ref, b_ref, o_ref, acc_ref, *, nb, cb):
    hb = HY_BLOCK
    rb = v_ref.shape[1] // nb

    def conv(u16, kext):
        for d in range(-(nb - 1), nb):
            dd = d % (2 * nb)
            win = jnp.broadcast_to(kext[:, hb * dd:hb * dd + 2 * hb], (hb, 2 * hb))
            toep = pltpu.roll(win, 0, 1, stride=1, stride_axis=0)[:, hb:].astype(bf16)
            t0, t1 = max(0, d), min(nb - 1, nb - 1 + d) + 1
            contrib = jnp.dot(u16[rb * (t0 - d):rb * (t1 - d)], toep, preferred_element_type=f32)
            if d == -(nb - 1):
                acc_ref[...] = jnp.zeros_like(acc_ref)
            acc_ref[rb * t0:rb * t1, :] += contrib
        return acc_ref[...]

    def body(c, carry):
        z = v_ref[c]
        for n, gate_ref in enumerate((x1_ref, x2_ref)):
            y = conv(z.astype(bf16), k_ref[n, pl.ds(c, 1), :])
            z = gate_ref[c] * (y + z * b_ref[n, pl.ds(c, 1), :])
        o_ref[c] = z
        return carry

    lax.fori_loop(0, cb, body, 0)


def _hyena_conv(p3, kext, bias, nb, cb=SUBLANES):
    c3, r, hb = p3.shape
    c = c3 // 3
    klen = kext.shape[-1]
    sig = lambda s: pl.BlockSpec((cb, r, hb), lambda i, s=s: (s * (c // cb) + i, 0, 0))
    return pl.pallas_call(
        functools.partial(_hyena_kernel, nb=nb, cb=cb),
        grid=(c // cb,),
        in_specs=[sig(0), sig(1), sig(2),
                  pl.BlockSpec((HY_ORDER, cb, klen), lambda i: (0, i, 0)),
                  pl.BlockSpec((HY_ORDER, cb, hb), lambda i: (0, i, 0))],
        out_specs=pl.BlockSpec((cb, r, hb), lambda i: (i, 0, 0)),
        out_shape=jax.ShapeDtypeStruct((c, r, hb), f32),
        scratch_shapes=[pltpu.VMEM((r, hb), f32)],
        compiler_params=_cparams(("parallel",)),
        name=f"hyena_conv_nb{nb}",
    )(p3, p3, p3, kext, bias)


def _hyena_prep_kernel(x_ref, w_ref, b_ref, o_ref):
    hb = HY_BLOCK
    t = x_ref.shape[1]
    w = w_ref[...]
    row = lax.broadcasted_iota(jnp.int32, (hb, LANES), 0)
    zero = jnp.zeros((1, LANES), f32)
    for s in range(t // hb):
        x = x_ref[0, s * hb:(s + 1) * hb, :]
        prev = x_ref[0, s * hb - 1:s * hb, :] if s > 0 else zero
        nxt = x_ref[0, (s + 1) * hb:(s + 1) * hb + 1, :] if (s + 1) * hb < t else zero
        xp = jnp.where(row == 0, prev, pltpu.roll(x, 1, 0))
        xn = jnp.where(row == hb - 1, nxt, pltpu.roll(x, hb - 1, 0))
        y = xp * w[0:1] + x * w[1:2] + xn * w[2:3] + b_ref[...]
        o_ref[:, s, :] = y.T


def _hyena_prep(p, conv_w, conv_b):
    b, t, _ = p.shape
    hb = HY_BLOCK
    n = 3 * HY_WIDTH
    assert OFF_HY == 0
    return pl.pallas_call(
        _hyena_prep_kernel,
        grid=(n // LANES, b),
        in_specs=[pl.BlockSpec((1, t, LANES), lambda j, i: (i, 0, j)),
                  pl.BlockSpec((3, LANES), lambda j, i: (0, j)),
                  pl.BlockSpec((1, LANES), lambda j, i: (0, j))],
        out_specs=pl.BlockSpec((LANES, t // hb, hb), lambda j, i: (j, 0, i)),
        out_shape=jax.ShapeDtypeStruct((n, t // hb, b * hb), f32),
        compiler_params=_cparams(("parallel", "parallel")),
        name="hyena_prep",
    )(p, conv_w, conv_b)


def _hyena_post_kernel(z_ref, o_ref):
    hb = HY_BLOCK
    for s in range(z_ref.shape[1]):
        o_ref[0, s * hb:(s + 1) * hb, :] = z_ref[:, s, :].T


def _hyena_post(z, b):
    c, nb, _ = z.shape
    hb = HY_BLOCK
    return pl.pallas_call(
        _hyena_post_kernel,
        grid=(c // LANES, b),
        in_specs=[pl.BlockSpec((LANES, nb, hb), lambda j, i: (j, 0, i))],
        out_specs=pl.BlockSpec((1, nb * hb, LANES), lambda j, i: (i, 0, j)),
        out_shape=jax.ShapeDtypeStruct((b, nb * hb, c), f32),
        compiler_params=_cparams(("parallel", "parallel")),
        name="hyena_post",
    )(z)


def _hyena_filter_kernel(z_ref, w1_ref, b1_ref, f1_ref, w2_ref, b2_ref, f2_ref, w3_ref, win_ref, o_ref):
    hi = lax.Precision.HIGHEST
    hb = HY_BLOCK
    t = z_ref.shape[1]
    for side in range(2):
        h = jnp.sin(f1_ref[...] * (jnp.dot(z_ref[side], w1_ref[...], precision=hi,
                                           preferred_element_type=f32) + b1_ref[...]))
        h = jnp.sin(f2_ref[...] * (jnp.dot(h, w2_ref[...], precision=hi, preferred_element_type=f32)
                                   + b2_ref[...]))
        w3 = w3_ref[:, side * HY_WIDTH:(side + 1) * HY_WIDTH]
        k = lax.dot_general(w3, h, (((0,), (1,)), ((), ())), precision=hi,
                            preferred_element_type=f32) * win_ref[side]
        o_ref[0, :, hb + side * t:hb + (side + 1) * t] = k
        if side == 1:
            o_ref[0, :, 0:hb] = k[:, t - hb:]


def _hyena_filters(t_len, lp):
    pos = jnp.arange(t_len, dtype=f32)
    t = jnp.linspace(0.0, 1.0, t_len, dtype=f32)
    bands = (HY_EMB - 1) // 2
    f = jnp.linspace(1e-4, bands - 1, bands, dtype=f32)
    ang = (2.0 * math.pi / t_len) * pos[:, None] * f[None, :]
    z = jnp.concatenate([t[:, None], jnp.cos(ang), -jnp.sin(ang)], axis=-1)
    max_decay = math.log(HY_TARGET) / HY_FAST_DECAY
    min_decay = math.log(HY_TARGET) / HY_SLOW_DECAY
    deltas = jnp.abs(jnp.linspace(min_decay, max_decay, HY_WIDTH, dtype=f32))
    window = jnp.exp(-t[:, None] * deltas[None, :])
    neg = lambda x: jnp.concatenate([x[:1], x[:0:-1]], axis=0)
    zs = jnp.pad(jnp.stack([z, neg(z)]), ((0, 0), (0, 0), (0, LANES - HY_EMB)))
    win = jnp.stack([window, neg(window).at[0].set(0.0)]).transpose(0, 2, 1)
    w1 = jnp.pad(lp['hy_f_w1'], ((0, LANES - HY_EMB), (0, 0)))
    row = lambda x: x[None]
    args = (zs, w1, row(lp['hy_f_b1']), row(lp['hy_f_freq1']), lp['hy_f_w2'], row(lp['hy_f_b2']),
            row(lp['hy_f_freq2']), lp['hy_f_w3'], win)
    specs = [_const_spec(a.shape) for a in args]
    specs[7] = pl.BlockSpec((lp['hy_f_w3'].shape[0], 2 * HY_WIDTH), lambda n: (0, n))
    klen = 2 * t_len + HY_BLOCK
    return pl.pallas_call(
        _hyena_filter_kernel,
        grid=(HY_ORDER,),
        in_specs=specs,
        out_specs=pl.BlockSpec((1, HY_WIDTH, klen), lambda n: (n, 0, 0)),
        out_shape=jax.ShapeDtypeStruct((HY_ORDER, HY_WIDTH, klen), f32),
        compiler_params=_cparams(("parallel",)),
        name="hyena_filters",
    )(*args)


def _gdn_kernel(ql_ref, kl_ref, vl_ref, zl_ref, abl_ref, qc_ref, kc_ref, vc_ref, zc_ref, abc_ref,
                cwq_ref, cwk_ref, cwv_ref, pa_ref, pd_ref, ng_ref,
                ol_ref, oc_ref,
                q_s, k_s, v_s, cf_s, u_s, wq_s, kgt_s, at_s, egl_s, o_s, st_s):
    c = GDN_CHUNK
    rb = GDN_ROWS
    tc, tl = qc_ref.shape[1], ql_ref.shape[1]
    ncc, nc = tc // c, (tc + tl) // c
    dk = GDN_DK

    def conv_rows(x_ref, w_ref, r0, t, ls):
        x = x_ref[0, pl.ds(r0, rb), ls]
        before = pl.multiple_of(jnp.maximum(r0 - SUBLANES, 0), SUBLANES)
        after = pl.multiple_of(jnp.minimum(r0 + rb, t - SUBLANES), SUBLANES)
        prev = x_ref[0, pl.ds(before, SUBLANES), ls][SUBLANES - 1:SUBLANES]
        nxt = x_ref[0, pl.ds(after, SUBLANES), ls][0:1]
        prev = jnp.where(r0 > 0, prev, 0.0)
        nxt = jnp.where(r0 + rb < t, nxt, 0.0)
        row = lax.broadcasted_iota(jnp.int32, x.shape, 0)
        xp = jnp.where(row == 0, prev, pltpu.roll(x, 1, 0))
        xn = jnp.where(row == rb - 1, nxt, pltpu.roll(x, rb - 1, 0))
        w = w_ref[:, ls]
        return _silu(xp * w[0:1] + x * w[1:2] + xn * w[2:3])

    def l2n(x):
        return x * lax.rsqrt(jnp.sum(x * x, axis=-1, keepdims=True) + EPS)

    def prep(hh, q_ref, k_ref, v_ref, ab_ref, t, base):
        ls = slice(hh * LANES, (hh + 1) * LANES)

        def body(i, carry):
            r0 = pl.multiple_of(i * rb, rb)
            dst = pl.ds(pl.multiple_of(base + i * rb, rb), rb)
            q_s[dst, :] = l2n(conv_rows(q_ref, cwq_ref, r0, t, ls)) * (dk ** -0.5)
            k_s[dst, :] = l2n(conv_rows(k_ref, cwk_ref, r0, t, ls))
            v_s[dst, :] = conv_rows(v_ref, cwv_ref, r0, t, ls)
            return carry
        lax.fori_loop(0, t // rb, body, 0)

        ab = ab_ref[0, hh]
        ab = jnp.concatenate([ab, jnp.zeros((SUBLANES - ab.shape[0], t), f32)], axis=0)
        rowi = lax.broadcasted_iota(jnp.int32, ab.shape, 0)
        pos = lax.broadcasted_iota(jnp.int32, ab.shape, 1) % c
        tile = lambda p: jnp.concatenate([p] * (t // LANES), axis=1)
        g = -jnp.exp(tile(pa_ref[hh])) * jax.nn.softplus(ab + tile(pd_ref[hh]))
        acc_f, acc_b = g, g
        s = 1
        while s < c:
            acc_f = acc_f + jnp.where(pos >= s, pltpu.roll(acc_f, s, 1), 0.0)
            acc_b = acc_b + jnp.where(pos < c - s, pltpu.roll(acc_b, t - s, 1), 0.0)
            s *= 2
        cf_t = jnp.where(rowi == 0, acc_f, jnp.where(rowi == 1, acc_b, jax.nn.sigmoid(ab)))
        pad = jnp.zeros((LANES - SUBLANES, LANES), f32)
        for j in range(t // LANES):
            blk = jnp.concatenate([cf_t[:, j * LANES:(j + 1) * LANES], pad], axis=0)
            cf_s[base + j * LANES:base + (j + 1) * LANES, :] = blk.T

    lane = lax.broadcasted_iota(jnp.int32, (c, 2 * c), 1)
    row = lax.broadcasted_iota(jnp.int32, (c, 2 * c), 0)
    fwd = lane < c
    col = lane % c
    incl = (fwd & (row >= col)) | (~fwd & (row <= col))
    strict = (fwd & (row > col)) | (~fwd & (row < col))

    def blockdiag(a):
        return jnp.concatenate([jnp.where(fwd, a, 0.0), jnp.where(fwd, 0.0, a)], axis=0).astype(bf16)

    def chunk_group(hh, cis):
        n = range(len(cis))
        xf, xb = 2 * hh, 2 * hh + 1
        rows = [pl.ds(pl.multiple_of(ci * c, c), c) for ci in cis]
        cf = [cf_s[r, :] for r in rows]
        k = [k_s[r, :] for r in rows]
        q = [q_s[r, :] for r in rows]
        bc = lambda x, j: jnp.broadcast_to(x[:, j:j + 1], (c, LANES))
        gf_c, gb_c = [bc(x, 0) for x in cf], [bc(x, 1) for x in cf]
        bf_c, bb_c = [bc(x, 2) for x in cf], [bc(x, 3) for x in cf]
        cf_t = [jnp.concatenate([x, x], axis=0).T for x in cf]
        g_r = [jnp.where(fwd, jnp.broadcast_to(x[0:1], (c, 2 * c)), jnp.broadcast_to(x[1:2], (c, 2 * c)))
               for x in cf_t]
        decay = [jnp.where(incl, jnp.exp(jnp.where(incl, jnp.where(fwd, gf_c[g], gb_c[g]) - g_r[g], 0.0)), 0.0)
                 for g in n]
        kq = [_dot_nt(jnp.concatenate([k[g], q[g]], axis=0), jnp.concatenate([k[g], k[g]], axis=0))
              for g in n]
        lm = [jnp.where(strict, kq[g][:c] * jnp.where(fwd, bf_c[g], bb_c[g]) * decay[g], 0.0) for g in n]
        for g in n:
            attn = (kq[g][c:] * decay[g]).astype(bf16)
            at_s[xf, cis[g]] = attn[:, :c]
            at_s[xb, cis[g]] = attn[:, c:]
        a = [jnp.dot(x.astype(bf16), blockdiag(x), preferred_element_type=f32) for x in lm]
        tp = [-x for x in lm]
        p = 2
        while 2 * p < c:
            ta = [jnp.dot(jnp.concatenate([tp[g], a[g]], axis=0).astype(bf16), blockdiag(a[g]),
                          preferred_element_type=f32) for g in n]
            tp = [tp[g] + a[g] + ta[g][:c] for g in n]
            a = [x[c:] for x in ta]
            p *= 2
        tp = [tp[g] + a[g] + jnp.dot(tp[g].astype(bf16), blockdiag(a[g]), preferred_element_type=f32)
              for g in n]
        for g in n:
            ci, r = cis[g], rows[g]
            v = v_s[r, :]
            eg_f, eg_b = jnp.exp(gf_c[g]), jnp.exp(gb_c[g])
            rhs = jnp.concatenate([jnp.concatenate([v * bf_c[g], k[g] * (bf_c[g] * eg_f)], axis=1),
                                   jnp.concatenate([v * bb_c[g], k[g] * (bb_c[g] * eg_b)], axis=1)], axis=0)
            uw = rhs + jnp.dot(blockdiag(tp[g]), rhs.astype(bf16), preferred_element_type=f32)
            u_s[xf, r, :] = uw[:c, :dk]
            u_s[xb, r, :] = uw[c:, :dk]
            wq_s[xf, ci] = jnp.concatenate([uw[:c, dk:], q[g] * eg_f], axis=0).astype(bf16)
            wq_s[xb, ci] = jnp.concatenate([uw[c:, dk:], q[g] * eg_b], axis=0).astype(bf16)
            gl_f, gl_b = gf_c[g][c - 1:c], gb_c[g][0:1]
            kgt_s[xf, ci] = (k[g] * jnp.exp(gl_f - gf_c[g])).T.astype(bf16)
            kgt_s[xb, ci] = (k[g] * jnp.exp(gl_b - gb_c[g])).T.astype(bf16)
            egl_s[hh, ci] = jnp.concatenate([jnp.exp(gl_f), jnp.exp(gl_b),
                                             jnp.zeros((SUBLANES - 2, LANES), f32)], axis=0)
            o_s[hh, r, :] = jnp.zeros((c, GDN_DV), f32)

    for hh in range(GDN_HP):
        prep(hh, qc_ref, kc_ref, vc_ref, abc_ref, tc, 0)
        prep(hh, ql_ref, kl_ref, vl_ref, abl_ref, tl, tc)

        def chunk_body(i, carry, hh=hh):
            chunk_group(hh, [GDN_GROUP * i + g for g in range(GDN_GROUP)])
            return carry

        lax.fori_loop(0, nc // GDN_GROUP, chunk_body, 0)

    st_s[...] = jnp.zeros_like(st_s)
    chains = [(hh, d) for hh in range(GDN_HP) for d in range(2)]

    def scan_body(i, carry):
        ci = (i, jnp.where(i < ncc, ncc - 1 - i, nc + ncc - 1 - i))
        rows = [pl.ds(pl.multiple_of(x * c, c), c) for x in ci]
        n = range(len(chains))
        s = [st_s[x] for x in n]
        ws = [jnp.dot(wq_s[x, ci[chains[x][1]]], s[x].astype(bf16), preferred_element_type=f32) for x in n]
        v_new = [(u_s[x, rows[chains[x][1]], :] - ws[x][:c]).astype(bf16) for x in n]
        upd = [jnp.dot(kgt_s[x, ci[chains[x][1]]], v_new[x], preferred_element_type=f32) for x in n]
        out = [ws[x][c:] + jnp.dot(at_s[x, ci[chains[x][1]]], v_new[x], preferred_element_type=f32) for x in n]
        for x in n:
            hh, d = chains[x]
            st_s[x] = s[x] * egl_s[hh, ci[d]][d:d + 1] + upd[x]
            o_s[hh, rows[d], :] += out[x]
        return carry

    lax.fori_loop(0, nc, scan_body, 0)

    def finish(hh, z_ref, o_ref, t, base):
        ls = slice(hh * LANES, (hh + 1) * LANES)

        def body(i, carry):
            r0 = pl.ds(pl.multiple_of(i * rb, rb), rb)
            o = o_s[hh, pl.ds(pl.multiple_of(base + i * rb, rb), rb), :]
            y = o * lax.rsqrt(jnp.mean(o * o, axis=-1, keepdims=True) + EPS) * ng_ref[...]
            o_ref[0, r0, ls] = y * _silu(z_ref[0, r0, ls])
            return carry
        lax.fori_loop(0, t // rb, body, 0)

    for hh in range(GDN_HP):
        finish(hh, zc_ref, oc_ref, tc, 0)
        finish(hh, zl_ref, ol_ref, tl, tc)


def _gdn(p_l, ab_l, p_c, ab_c, conv_w, pa, pd, norm_g):
    b, tl, _ = p_l.shape
    tc = p_c.shape[1]
    h, c, hp = GDN_HEADS, GDN_CHUNK, GDN_HP
    nc = (tl + tc) // c
    assert tl % GDN_ROWS == 0 and tc % GDN_ROWS == 0 and nc % GDN_GROUP == 0
    qb, zb = OFF_QKV // LANES, OFF_Z // LANES
    assert h % hp == 0 and qb % hp == 0 and zb % hp == 0
    w = hp * LANES

    def specs(t):
        heads = lambda off: pl.BlockSpec((1, t, w), lambda i, j, off=off // hp: (i, 0, off + j))
        return [heads(qb), heads(qb + h), heads(qb + 2 * h), heads(zb),
                pl.BlockSpec((1, hp, 4, t), lambda i, j: (i, j, 0, 0))]

    cw = lambda off: pl.BlockSpec((3, w), lambda i, j, off=off // hp: (0, off + j))
    par = pl.BlockSpec((hp, SUBLANES, LANES), lambda i, j: (j, 0, 0))
    out = lambda t: pl.BlockSpec((1, t, w), lambda i, j: (i, 0, j))
    tt = tl + tc
    return pl.pallas_call(
        _gdn_kernel,
        grid=(b, h // hp),
        in_specs=specs(tl) + specs(tc) + [cw(0), cw(h), cw(2 * h), par, par, _const_spec((1, GDN_DV))],
        out_specs=[out(tl), out(tc)],
        out_shape=[jax.ShapeDtypeStruct((b, tl, h * GDN_DV), f32),
                   jax.ShapeDtypeStruct((b, tc, h * GDN_DV), f32)],
        scratch_shapes=[pltpu.VMEM((tt, GDN_DK), f32), pltpu.VMEM((tt, GDN_DK), f32),
                        pltpu.VMEM((tt, GDN_DV), f32), pltpu.VMEM((tt, LANES), f32),
                        pltpu.VMEM((2 * hp, tt, GDN_DV), f32), pltpu.VMEM((2 * hp, nc, 2 * c, GDN_DK), bf16),
                        pltpu.VMEM((2 * hp, nc, GDN_DK, c), bf16), pltpu.VMEM((2 * hp, nc, c, c), bf16),
                        pltpu.VMEM((hp, nc, SUBLANES, LANES), f32), pltpu.VMEM((hp, tt, GDN_DV), f32),
                        pltpu.VMEM((2 * hp, GDN_DK, GDN_DV), f32)],
        compiler_params=pltpu.CompilerParams(dimension_semantics=("parallel", "parallel"),
                                             vmem_limit_bytes=GDN_VMEM_LIMIT),
        name="gated_deltanet",
    )(p_l, p_l, p_l, p_l, ab_l, p_c, p_c, p_c, p_c, ab_c, conv_w, conv_w, conv_w, pa, pd, norm_g)


def _ret_kernel(ql_ref, kl_ref, vl_ref, gl_ref, qc_ref, kc_ref, vc_ref, gc_ref,
                cos_ref, sin_ref, dsum_ref, sc_ref, gch_ref,
                ol_ref, oc_ref, sf_ref, sb_ref, q_s, k_s, o_s):
    c = RET_CHUNK
    w = RET_HEADS * RET_DK
    ri = lax.broadcasted_iota(jnp.int32, (w, w), 0) // RET_DK
    ci = lax.broadcasted_iota(jnp.int32, (w, w), 1) // RET_DK
    blockdiag = ri == ci
    lane_head = lax.broadcasted_iota(jnp.int32, (c, w), 1) // RET_DK

    def intra(q, k, v):
        out = jnp.zeros((c, w), f32)
        for h in range(RET_HEADS):
            m = lane_head == h
            scores = _dot_nt(jnp.where(m, q, 0.0), k) * dsum_ref[h]
            out = out + _dot(scores, jnp.where(m, v, 0.0))
        return out

    def update(s_ref, k, v, d):
        new = jnp.where(blockdiag, _dot_tn(k * sc_ref[2 * d + 1], v), 0.0)
        s_ref[...] = s_ref[...] * gch_ref[d] + new

    first_half = lax.broadcasted_iota(jnp.int32, (c, w), 1) % RET_DK < RET_DK // 2
    avg = jnp.where(blockdiag, 1.0 / RET_DV, 0.0).astype(bf16)

    def rope(x, r):
        swapped = jnp.where(first_half, pltpu.roll(x, w - RET_DK // 2, 1), pltpu.roll(x, RET_DK // 2, 1))
        return x * cos_ref[r, :] + swapped * sin_ref[r, :]

    def head_mean(x):
        hi = x.astype(bf16)
        lo = (x - hi.astype(f32)).astype(bf16)
        return jnp.dot(hi, avg, preferred_element_type=f32) + jnp.dot(lo, avg, preferred_element_type=f32)

    def finish(o, gate):
        d = o - head_mean(o)
        return d * lax.rsqrt(head_mean(d * d) + EPS) * _silu(gate)

    sf_ref[...] = jnp.zeros_like(sf_ref)
    sb_ref[...] = jnp.zeros_like(sb_ref)
    qc, kc, vc = qc_ref[0], kc_ref[0] * (RET_DK ** -0.5), vc_ref[0]
    oc_ref[0] = finish(intra(qc, kc, vc), gc_ref[0])
    update(sf_ref, kc, vc, 0)
    update(sb_ref, kc, vc, 1)
    n = ql_ref.shape[1] // c
    for i in range(n):
        r = slice(i * c, (i + 1) * c)
        q = rope(ql_ref[0, r, :], r)
        k = rope(kl_ref[0, r, :] * (RET_DK ** -0.5), r)
        v = vl_ref[0, r, :]
        q_s[r, :] = q
        k_s[r, :] = k
        o_s[r, :] = intra(q, k, v) + _dot(q * sc_ref[0], sf_ref[...])
        update(sf_ref, k, v, 0)
    for i in range(n - 1, -1, -1):
        r = slice(i * c, (i + 1) * c)
        o = o_s[r, :] + _dot(q_s[r, :] * sc_ref[2], sb_ref[...])
        ol_ref[0, r, :] = finish(o, gl_ref[0, r, :])
        update(sb_ref, k_s[r, :], vl_ref[0, r, :], 1)


def _retention(p_l, p_c, cos, sin, dsum, scales, gch):
    b, tl, _ = p_l.shape
    tc = p_c.shape[1]
    w = RET_HEADS * RET_DK
    assert OFF_RET % w == 0
    blk = lambda t, j: pl.BlockSpec((1, t, w), lambda i, j=j: (i, 0, OFF_RET // w + j))
    tok = lambda t: pl.BlockSpec((1, t, w), lambda i: (i, 0, 0))
    return pl.pallas_call(
        _ret_kernel,
        grid=(b,),
        in_specs=[blk(tl, j) for j in range(4)] + [blk(tc, j) for j in range(4)]
        + [_const_spec(x.shape) for x in (cos, sin, dsum, scales, gch)],
        out_specs=[tok(tl), tok(tc)],
        out_shape=[jax.ShapeDtypeStruct((b, tl, w), f32), jax.ShapeDtypeStruct((b, tc, w), f32)],
        scratch_shapes=[pltpu.VMEM((w, w), f32), pltpu.VMEM((w, w), f32),
                        pltpu.VMEM((tl, w), f32), pltpu.VMEM((tl, w), f32), pltpu.VMEM((tl, w), f32)],
        compiler_params=_cparams(("parallel",)),
        name="retention",
    )(p_l, p_l, p_l, p_l, p_c, p_c, p_c, p_c, cos, sin, dsum, scales, gch)


def _hyena(p, lp):
    b, t, _ = p.shape
    hb, c = HY_BLOCK, HY_WIDTH
    nb = t // hb
    p3 = _hyena_prep(p, lp['hy_conv_w'], lp['hy_conv_b'][None]).reshape(3 * c, nb * b, hb)
    bias = jnp.broadcast_to(lp['hy_bias'][:, :, None], (HY_ORDER, c, hb))
    z = _hyena_conv(p3, _hyena_filters(t, lp), bias, nb)
    return _hyena_post(z.reshape(c, nb, b * hb), b)


def _gdn_gate_inputs(p):
    b, t, _ = p.shape
    ab = p[..., OFF_AB:OFF_AB + 4 * GDN_HEADS].reshape(b, t, 4, GDN_HEADS)
    return ab.transpose(0, 3, 2, 1)


def _gdn_gate_params(a_log, dt_bias):
    def rows(x):
        x = jnp.pad(x.T, ((0, 0), (0, SUBLANES - 2)))
        return jnp.broadcast_to(x[:, :, None], (GDN_HEADS, SUBLANES, LANES))
    return rows(a_log), rows(dt_bias)


def _rope_tables(t_lat):
    rows_n = t_lat // GRID_W
    row = jnp.repeat(jnp.arange(rows_n, dtype=f32), GRID_W)
    colp = jnp.tile(jnp.arange(GRID_W, dtype=f32), rows_n)
    nf = RET_DK // 4
    inv = ROPE_BASE ** (-jnp.arange(nf, dtype=f32) / nf)
    ang = jnp.concatenate([row[:, None] * inv, colp[:, None] * inv], axis=-1)
    cos, sin = jnp.cos(ang), jnp.sin(ang)
    return (jnp.tile(jnp.concatenate([cos, cos], axis=-1), (1, RET_HEADS)),
            jnp.tile(jnp.concatenate([-sin, sin], axis=-1), (1, RET_HEADS)))


def _ret_constants(lp):
    c = RET_CHUNK
    lg = jax.nn.log_sigmoid(lp['ret_decay_logit'])
    idx = jnp.arange(c, dtype=f32)
    rel = idx[:, None] - idx[None, :]
    d_f = jnp.exp(jnp.where(rel >= 0, rel * lg[0][:, None, None], -jnp.inf))
    d_b = jnp.exp(jnp.where(rel <= 0, -rel * lg[1][:, None, None], -jnp.inf))
    lane = lambda x: jnp.repeat(x, RET_DK, axis=-1)
    scales = jnp.stack([lane(jnp.exp((idx + 1.0)[:, None] * lg[0])),
                        lane(jnp.exp((c - 1.0 - idx)[:, None] * lg[0])),
                        lane(jnp.exp((c - idx)[:, None] * lg[1])),
                        lane(jnp.exp(idx[:, None] * lg[1]))])
    gch = lane(jnp.exp(c * lg))[:, None, :]
    return d_f + d_b, scales, gch


def _reorder_w_in(w):
    h2 = 2 * GDN_HEADS
    a0 = OFF_Z + GDN_HEADS * GDN_DV
    parts = [w[:, :a0], w[:, a0 + 2 * h2:], w[:, a0:a0 + 2 * h2],
             jnp.zeros((w.shape[0], LANES - 2 * h2), w.dtype)]
    return jnp.concatenate(parts, axis=1)


def kernel(x, c, ctx, c_ctx, mod_w, mod_b, norm1_g, w_in, hy_conv_w, hy_conv_b, hy_f_w1, hy_f_b1, hy_f_freq1, hy_f_w2, hy_f_b2, hy_f_freq2, hy_f_w3, hy_bias, gdn_conv_w, gdn_a_log, gdn_dt_bias, gdn_norm_g, ret_decay_logit, w_out, norm2_g, ffn_w_in, ffn_w_out, final_norm_g):
    bsz, t_lat, d = x.shape
    cos, sin = _rope_tables(t_lat)
    pad_rows = (-(bsz + 1)) % SUBLANES
    cs = jnp.concatenate([c, c_ctx[None], jnp.zeros((pad_rows, d), f32)], axis=0)
    gf = final_norm_g[None]
    for i in range(DEPTH):
        need_ctx = i < DEPTH - 1
        lp = {'hy_conv_w': hy_conv_w[i], 'hy_conv_b': hy_conv_b[i], 'hy_f_w1': hy_f_w1[i],
              'hy_f_b1': hy_f_b1[i], 'hy_f_freq1': hy_f_freq1[i], 'hy_f_w2': hy_f_w2[i],
              'hy_f_b2': hy_f_b2[i], 'hy_f_freq2': hy_f_freq2[i], 'hy_f_w3': hy_f_w3[i],
              'hy_bias': hy_bias[i], 'ret_decay_logit': ret_decay_logit[i]}
        mod = _mod_vectors(cs, mod_w[i].astype(bf16), mod_b[i][None])
        mod_lat = mod[:bsz].reshape(bsz, N_MOD, d)
        mod_ctx = jnp.broadcast_to(mod[bsz].reshape(1, N_MOD, d), (bsz, N_MOD, d))
        w_in16 = _reorder_w_in(w_in[i]).astype(bf16)
        g1 = norm1_g[i][None]
        p_l = _in_proj(x, g1, mod_lat, w_in16, 256)
        p_c = _in_proj(ctx, g1, mod_ctx, w_in16, 256)

        hy_l = _hyena(p_l, lp)
        pa, pd = _gdn_gate_params(gdn_a_log[i], gdn_dt_bias[i])
        gd_l, gd_c = _gdn(p_l, _gdn_gate_inputs(p_l), p_c, _gdn_gate_inputs(p_c), gdn_conv_w[i], pa, pd,
                          gdn_norm_g[i][None])

        dsum, scales, gch = _ret_constants(lp)
        rt_l, rt_c = _retention(p_l, p_c, cos, sin, dsum, scales, gch)

        wo16, w116, w216 = w_out[i].astype(bf16), ffn_w_in[i].astype(bf16), ffn_w_out[i].astype(bf16)
        g2 = norm2_g[i][None]
        x = _out_ffn(x, hy_l, gd_l, rt_l, mod_lat, wo16, g2, w116, w216, gf, 256, final=not need_ctx)
        if need_ctx:
            hy_c = _hyena(p_c, lp)
            ctx = _out_ffn(ctx, hy_c, gd_c, rt_c, mod_ctx, wo16, g2, w116, w216, gf, 256, final=False)
    return x
```

```python
import functools
import math

import jax
import jax.numpy as jnp
from jax import lax
from jax.experimental import pallas as pl
from jax.experimental.pallas import tpu as pltpu

D_MODEL = 1024
DEPTH = 2
GRID_W = 64
EPS = 1e-6
N_MOD = 6

HY_WIDTH = D_MODEL // 4
HY_ORDER = 2
HY_EMB = 33
HY_FAST_DECAY = 0.3
HY_SLOW_DECAY = 1.5
HY_TARGET = 1e-2
GDN_HEADS = D_MODEL // 256
GDN_DK = 128
GDN_DV = 128
RET_HEADS = D_MODEL // 256
RET_DK = 64
RET_DV = 64
ROPE_BASE = 10000.0
FFN_HIDDEN = ((8 * D_MODEL + 3 * 256 - 1) // (3 * 256)) * 256

GDN_CHUNK = 64
GDN_ROWS = 256
GDN_HP = 2
GDN_GROUP = 12
RET_CHUNK = 256
HY_BLOCK = 256
LANES = 128
SUBLANES = 8

GDN_QKV = 2 * GDN_HEADS * GDN_DK + GDN_HEADS * GDN_DV
OFF_HY = 0
OFF_QKV = OFF_HY + 3 * HY_WIDTH
OFF_Z = OFF_QKV + GDN_QKV
OFF_RET = OFF_Z + GDN_HEADS * GDN_DV
OFF_AB = OFF_RET + 4 * RET_HEADS * RET_DK
IN_PAD = OFF_AB + LANES

VMEM_LIMIT = 56 * 1024 * 1024
GDN_VMEM_LIMIT = 60 * 1024 * 1024

f32 = jnp.float32
bf16 = jnp.bfloat16


def _cparams(sem):
    return pltpu.CompilerParams(dimension_semantics=sem, vmem_limit_bytes=VMEM_LIMIT)


def _const_spec(shape):
    nd = len(shape)
    return pl.BlockSpec(shape, lambda *_: (0,) * nd, pipeline_mode=pl.Buffered(1))


def _rms(x, g):
    return x * lax.rsqrt(jnp.mean(x * x, axis=-1, keepdims=True) + EPS) * g


def _silu(x):
    return x * jax.nn.sigmoid(x)


def _dot(a, b):
    return jnp.dot(a.astype(bf16), b.astype(bf16), preferred_element_type=f32)


def _dot_nt(a, b):
    return lax.dot_general(a.astype(bf16), b.astype(bf16), (((1,), (1,)), ((), ())),
                           preferred_element_type=f32)


def _dot_tn(a, b):
    return lax.dot_general(a.astype(bf16), b.astype(bf16), (((0,), (0,)), ((), ())),
                           preferred_element_type=f32)


def _mod_kernel(c_ref, w_ref, b_ref, o_ref):
    o_ref[...] = _dot(_silu(c_ref[...]), w_ref[...]) + b_ref[...]


def _mod_vectors(cs, w16, b):
    rows, d = cs.shape
    n = w16.shape[1]
    tn = 1536
    return pl.pallas_call(
        _mod_kernel,
        grid=(n // tn,),
        in_specs=[pl.BlockSpec((rows, d), lambda j: (0, 0)),
                  pl.BlockSpec((d, tn), lambda j: (0, j)),
                  pl.BlockSpec((1, tn), lambda j: (0, j))],
        out_specs=pl.BlockSpec((rows, tn), lambda j: (0, j)),
        out_shape=jax.ShapeDtypeStruct((rows, n), f32),
        compiler_params=_cparams(("parallel",)),
        name="mod_vectors",
    )(cs, w16, b)


def _in_proj_kernel(x_ref, g_ref, mod_ref, w_ref, o_ref):
    x = x_ref[0]
    mod = mod_ref[0]
    h = _rms(x, g_ref[...]) * (1.0 + mod[1:2]) + mod[0:1]
    h16 = h.astype(bf16)
    n = w_ref.shape[1]
    step = 512
    for j in range(0, n, step):
        e = min(j + step, n)
        o_ref[0, :, j:e] = jnp.dot(h16, w_ref[:, j:e], preferred_element_type=f32)


def _in_proj(x, g, mod, w16, tm):
    b, t, d = x.shape
    n = w16.shape[1]
    return pl.pallas_call(
        _in_proj_kernel,
        grid=(b, t // tm),
        in_specs=[pl.BlockSpec((1, tm, d), lambda i, j: (i, j, 0)),
                  _const_spec((1, d)),
                  pl.BlockSpec((1, N_MOD, d), lambda i, j: (i, 0, 0)),
                  _const_spec((d, n))],
        out_specs=pl.BlockSpec((1, tm, n), lambda i, j: (i, j, 0)),
        out_shape=jax.ShapeDtypeStruct((b, t, n), f32),
        compiler_params=_cparams(("parallel", "parallel")),
        name="in_proj",
    )(x, g, mod, w16)


def _out_ffn_kernel(x_ref, hy_ref, gd_ref, rt_ref, mod_ref, wo_ref, g2_ref, w1_ref, w2_ref, gf_ref, o_ref,
                    *, final):
    x = x_ref[0]
    mod = mod_ref[0]
    o0, o1, o2 = HY_WIDTH, HY_WIDTH + GDN_HEADS * GDN_DV, D_MODEL
    y = (jnp.dot(hy_ref[0].astype(bf16), wo_ref[0:o0, :], preferred_element_type=f32)
         + jnp.dot(gd_ref[0].astype(bf16), wo_ref[o0:o1, :], preferred_element_type=f32)
         + jnp.dot(rt_ref[0].astype(bf16), wo_ref[o1:o2, :], preferred_element_type=f32))
    x = x + mod[2:3] * y
    h16 = (_rms(x, g2_ref[...]) * (1.0 + mod[4:5]) + mod[3:4]).astype(bf16)
    step = 256
    acc = jnp.zeros_like(x)
    for j in range(0, FFN_HIDDEN, step):
        gate = jnp.dot(h16, w1_ref[:, j:j + step], preferred_element_type=f32)
        up = jnp.dot(h16, w1_ref[:, FFN_HIDDEN + j:FFN_HIDDEN + j + step], preferred_element_type=f32)
        acc = acc + jnp.dot((_silu(gate) * up).astype(bf16), w2_ref[j:j + step, :], preferred_element_type=f32)
    x = x + mod[5:6] * acc
    if final:
        x = _rms(x, gf_ref[...])
    o_ref[0] = x


def _out_ffn(x, hy, gd, rt, mod, wo16, g2, w116, w216, gf, tm, final):
    b, t, d = x.shape
    tok = lambda w: pl.BlockSpec((1, tm, w), lambda i, j: (i, j, 0))
    return pl.pallas_call(
        functools.partial(_out_ffn_kernel, final=final),
        grid=(b, t // tm),
        in_specs=[tok(d), tok(hy.shape[-1]), tok(gd.shape[-1]), tok(rt.shape[-1]),
                  pl.BlockSpec((1, N_MOD, d), lambda i, j: (i, 0, 0)),
                  _const_spec(wo16.shape), _const_spec((1, d)), _const_spec(w116.shape),
                  _const_spec(w216.shape), _const_spec((1, d))],
        out_specs=tok(d),
        out_shape=jax.ShapeDtypeStruct((b, t, d), f32),
        compiler_params=_cparams(("parallel", "parallel")),
        name="out_ffn",
    )(x, hy, gd, rt, mod, wo16, g2, w116, w216, gf)


def _hyena_kernel(v_ref, x1_ref, x2_ref, k_ref, b_ref, o_ref, acc_ref, *, nb, cb):
    hb = HY_BLOCK
    rb = v_ref.shape[1] // nb

    def conv(u16, kext):
        for d in range(-(nb - 1), nb):
            dd = d % (2 * nb)
            win = jnp.broadcast_to(kext[:, hb * dd:hb * dd + 2 * hb], (hb, 2 * hb))
            toep = pltpu.roll(win, 0, 1, stride=1, stride_axis=0)[:, hb:].astype(bf16)
            t0, t1 = max(0, d), min(nb - 1, nb - 1 + d) + 1
            contrib = jnp.dot(u16[rb * (t0 - d):rb * (t1 - d)], toep, preferred_element_type=f32)
            if d == -(nb - 1):
                acc_ref[...] = jnp.zeros_like(acc_ref)
            acc_ref[rb * t0:rb * t1, :] += contrib
        return acc_ref[...]

    def body(c, carry):
        z = v_ref[c]
        for n, gate_ref in enumerate((x1_ref, x2_ref)):
            y = conv(z.astype(bf16), k_ref[n, pl.ds(c, 1), :])
            z = gate_ref[c] * (y + z * b_ref[n, pl.ds(c, 1), :])
        o_ref[c] = z
        return carry

    lax.fori_loop(0, cb, body, 0)


def _hyena_conv(p3, kext, bias, nb, cb=SUBLANES):
    c3, r, hb = p3.shape
    c = c3 // 3
    klen = kext.shape[-1]
    sig = lambda s: pl.BlockSpec((cb, r, hb), lambda i, s=s: (s * (c // cb) + i, 0, 0))
    return pl.pallas_call(
        functools.partial(_hyena_kernel, nb=nb, cb=cb),
        grid=(c // cb,),
        in_specs=[sig(0), sig(1), sig(2),
                  pl.BlockSpec((HY_ORDER, cb, klen), lambda i: (0, i, 0)),
                  pl.BlockSpec((HY_ORDER, cb, hb), lambda i: (0, i, 0))],
        out_specs=pl.BlockSpec((cb, r, hb), lambda i: (i, 0, 0)),
        out_shape=jax.ShapeDtypeStruct((c, r, hb), f32),
        scratch_shapes=[pltpu.VMEM((r, hb), f32)],
        compiler_params=_cparams(("parallel",)),
        name=f"hyena_conv_nb{nb}",
    )(p3, p3, p3, kext, bias)


def _hyena_prep_kernel(x_ref, before_ref, after_ref, w_ref, b_ref, o_ref):
    hb = HY_BLOCK
    w = w_ref[...]
    row = lax.broadcasted_iota(jnp.int32, (hb, LANES), 0)
    s = pl.program_id(1)
    has_before = s > 0
    has_after = s < pl.num_programs(1) - 1
    for i in range(x_ref.shape[0]):
        x = x_ref[i]
        prev = jnp.where(has_before, before_ref[i, SUBLANES - 1:SUBLANES, :], 0.0)
        nxt = jnp.where(has_after, after_ref[i, 0:1, :], 0.0)
        xp = jnp.where(row == 0, prev, pltpu.roll(x, 1, 0))
        xn = jnp.where(row == hb - 1, nxt, pltpu.roll(x, hb - 1, 0))
        y = xp * w[0:1] + x * w[1:2] + xn * w[2:3] + b_ref[...]
        o_ref[:, i, :] = y.T


def _hyena_prep(p, conv_w, conv_b):
    b, t, _ = p.shape
    hb = HY_BLOCK
    n = 3 * HY_WIDTH
    per = hb // SUBLANES
    last = t // SUBLANES - 1
    assert OFF_HY == 0 and b % SUBLANES == 0
    return pl.pallas_call(
        _hyena_prep_kernel,
        grid=(n // LANES, t // hb),
        in_specs=[pl.BlockSpec((b, hb, LANES), lambda j, s: (0, s, j)),
                  pl.BlockSpec((b, SUBLANES, LANES), lambda j, s: (0, jnp.maximum(s * per - 1, 0), j)),
                  pl.BlockSpec((b, SUBLANES, LANES), lambda j, s: (0, jnp.minimum((s + 1) * per, last), j)),
                  pl.BlockSpec((3, LANES), lambda j, s: (0, j)),
                  pl.BlockSpec((1, LANES), lambda j, s: (0, j))],
        out_specs=pl.BlockSpec((LANES, b, hb), lambda j, s: (j, s, 0)),
        out_shape=jax.ShapeDtypeStruct((n, (t // hb) * b, hb), f32),
        compiler_params=_cparams(("parallel", "parallel")),
        name="hyena_prep",
    )(p, p, p, conv_w, conv_b)


def _hyena_post_kernel(z_ref, o_ref):
    for i in range(z_ref.shape[1]):
        o_ref[i] = z_ref[:, i, :].T


def _hyena_post(z, b):
    c, r, hb = z.shape
    nb = r // b
    return pl.pallas_call(
        _hyena_post_kernel,
        grid=(c // LANES, nb),
        in_specs=[pl.BlockSpec((LANES, b, hb), lambda j, s: (j, s, 0))],
        out_specs=pl.BlockSpec((b, hb, LANES), lambda j, s: (0, s, j)),
        out_shape=jax.ShapeDtypeStruct((b, nb * hb, c), f32),
        compiler_params=_cparams(("parallel", "parallel")),
        name="hyena_post",
    )(z)


def _hyena_filter_kernel(z_ref, w1_ref, b1_ref, f1_ref, w2_ref, b2_ref, f2_ref, w3_ref, win_ref, o_ref):
    hi = lax.Precision.HIGHEST
    hb = HY_BLOCK
    t = z_ref.shape[1]
    for side in range(2):
        h = jnp.sin(f1_ref[...] * (jnp.dot(z_ref[side], w1_ref[...], precision=hi,
                                           preferred_element_type=f32) + b1_ref[...]))
        h = jnp.sin(f2_ref[...] * (jnp.dot(h, w2_ref[...], precision=hi, preferred_element_type=f32)
                                   + b2_ref[...]))
        w3 = w3_ref[:, side * HY_WIDTH:(side + 1) * HY_WIDTH]
        k = lax.dot_general(w3, h, (((0,), (1,)), ((), ())), precision=hi,
                            preferred_element_type=f32) * win_ref[side]
        o_ref[0, :, hb + side * t:hb + (side + 1) * t] = k
        if side == 1:
            o_ref[0, :, 0:hb] = k[:, t - hb:]


def _hyena_filters(t_len, lp):
    pos = jnp.arange(t_len, dtype=f32)
    t = jnp.linspace(0.0, 1.0, t_len, dtype=f32)
    bands = (HY_EMB - 1) // 2
    f = jnp.linspace(1e-4, bands - 1, bands, dtype=f32)
    ang = (2.0 * math.pi / t_len) * pos[:, None] * f[None, :]
    z = jnp.concatenate([t[:, None], jnp.cos(ang), -jnp.sin(ang)], axis=-1)
    max_decay = math.log(HY_TARGET) / HY_FAST_DECAY
    min_decay = math.log(HY_TARGET) / HY_SLOW_DECAY
    deltas = jnp.abs(jnp.linspace(min_decay, max_decay, HY_WIDTH, dtype=f32))
    window = jnp.exp(-t[:, None] * deltas[None, :])
    neg = lambda x: jnp.concatenate([x[:1], x[:0:-1]], axis=0)
    zs = jnp.pad(jnp.stack([z, neg(z)]), ((0, 0), (0, 0), (0, LANES - HY_EMB)))
    win = jnp.stack([window, neg(window).at[0].set(0.0)]).transpose(0, 2, 1)
    w1 = jnp.pad(lp['hy_f_w1'], ((0, LANES - HY_EMB), (0, 0)))
    row = lambda x: x[None]
    args = (zs, w1, row(lp['hy_f_b1']), row(lp['hy_f_freq1']), lp['hy_f_w2'], row(lp['hy_f_b2']),
            row(lp['hy_f_freq2']), lp['hy_f_w3'], win)
    specs = [_const_spec(a.shape) for a in args]
    specs[7] = pl.BlockSpec((lp['hy_f_w3'].shape[0], 2 * HY_WIDTH), lambda n: (0, n))
    klen = 2 * t_len + HY_BLOCK
    return pl.pallas_call(
        _hyena_filter_kernel,
        grid=(HY_ORDER,),
        in_specs=specs,
        out_specs=pl.BlockSpec((1, HY_WIDTH, klen), lambda n: (n, 0, 0)),
        out_shape=jax.ShapeDtypeStruct((HY_ORDER, HY_WIDTH, klen), f32),
        compiler_params=_cparams(("parallel",)),
        name="hyena_filters",
    )(*args)


def _gdn_kernel(ql_ref, kl_ref, vl_ref, zl_ref, abl_ref, qc_ref, kc_ref, vc_ref, zc_ref, abc_ref,
                cwq_ref, cwk_ref, cwv_ref, pa_ref, pd_ref, ng_ref,
                ol_ref, oc_ref,
                q_s, k_s, v_s, cf_s, u_s, wq_s, kgt_s, at_s, egl_s, o_s, st_s):
    c = GDN_CHUNK
    rb = GDN_ROWS
    tc, tl = qc_ref.shape[1], ql_ref.shape[1]
    ncc, nc = tc // c, (tc + tl) // c
    dk = GDN_DK

    def conv_rows(x_ref, w_ref, r0, t, ls):
        x = x_ref[0, pl.ds(r0, rb), ls]
        before = pl.multiple_of(jnp.maximum(r0 - SUBLANES, 0), SUBLANES)
        after = pl.multiple_of(jnp.minimum(r0 + rb, t - SUBLANES), SUBLANES)
        prev = x_ref[0, pl.ds(before, SUBLANES), ls][SUBLANES - 1:SUBLANES]
        nxt = x_ref[0, pl.ds(after, SUBLANES), ls][0:1]
        prev = jnp.where(r0 > 0, prev, 0.0)
        nxt = jnp.where(r0 + rb < t, nxt, 0.0)
        row = lax.broadcasted_iota(jnp.int32, x.shape, 0)
        xp = jnp.where(row == 0, prev, pltpu.roll(x, 1, 0))
        xn = jnp.where(row == rb - 1, nxt, pltpu.roll(x, rb - 1, 0))
        w = w_ref[:, ls]
        return _silu(xp * w[0:1] + x * w[1:2] + xn * w[2:3])

    def l2n(x):
        return x * lax.rsqrt(jnp.sum(x * x, axis=-1, keepdims=True) + EPS)

    def prep(hh, q_ref, k_ref, v_ref, ab_ref, t, base):
        ls = slice(hh * LANES, (hh + 1) * LANES)

        def body(i, carry):
            r0 = pl.multiple_of(i * rb, rb)
            dst = pl.ds(pl.multiple_of(base + i * rb, rb), rb)
            q_s[dst, :] = l2n(conv_rows(q_ref, cwq_ref, r0, t, ls)) * (dk ** -0.5)
            k_s[dst, :] = l2n(conv_rows(k_ref, cwk_ref, r0, t, ls))
            v_s[dst, :] = conv_rows(v_ref, cwv_ref, r0, t, ls)
            return carry
        lax.fori_loop(0, t // rb, body, 0)

        ab = ab_ref[0, hh]
        ab = jnp.concatenate([ab, jnp.zeros((SUBLANES - ab.shape[0], t), f32)], axis=0)
        rowi = lax.broadcasted_iota(jnp.int32, ab.shape, 0)
        pos = lax.broadcasted_iota(jnp.int32, ab.shape, 1) % c
        tile = lambda p: jnp.concatenate([p] * (t // LANES), axis=1)
        g = -jnp.exp(tile(pa_ref[hh])) * jax.nn.softplus(ab + tile(pd_ref[hh]))
        acc_f, acc_b = g, g
        s = 1
        while s < c:
            acc_f = acc_f + jnp.where(pos >= s, pltpu.roll(acc_f, s, 1), 0.0)
            acc_b = acc_b + jnp.where(pos < c - s, pltpu.roll(acc_b, t - s, 1), 0.0)
            s *= 2
        cf_t = jnp.where(rowi == 0, acc_f, jnp.where(rowi == 1, acc_b, jax.nn.sigmoid(ab)))
        pad = jnp.zeros((LANES - SUBLANES, LANES), f32)
        for j in range(t // LANES):
            blk = jnp.concatenate([cf_t[:, j * LANES:(j + 1) * LANES], pad], axis=0)
            cf_s[base + j * LANES:base + (j + 1) * LANES, :] = blk.T

    lane = lax.broadcasted_iota(jnp.int32, (c, 2 * c), 1)
    row = lax.broadcasted_iota(jnp.int32, (c, 2 * c), 0)
    fwd = lane < c
    col = lane % c
    incl = (fwd & (row >= col)) | (~fwd & (row <= col))
    strict = (fwd & (row > col)) | (~fwd & (row < col))

    def blockdiag(a):
        return jnp.concatenate([jnp.where(fwd, a, 0.0), jnp.where(fwd, 0.0, a)], axis=0).astype(bf16)

    def chunk_group(hh, cis):
        n = range(len(cis))
        xf, xb = 2 * hh, 2 * hh + 1
        rows = [pl.ds(pl.multiple_of(ci * c, c), c) for ci in cis]
        cf = [cf_s[r, :] for r in rows]
        k = [k_s[r, :] for r in rows]
        q = [q_s[r, :] for r in rows]
        bc = lambda x, j: jnp.broadcast_to(x[:, j:j + 1], (c, LANES))
        gf_c, gb_c = [bc(x, 0) for x in cf], [bc(x, 1) for x in cf]
        bf_c, bb_c = [bc(x, 2) for x in cf], [bc(x, 3) for x in cf]
        cf_t = [jnp.concatenate([x, x], axis=0).T for x in cf]
        g_r = [jnp.where(fwd, jnp.broadcast_to(x[0:1], (c, 2 * c)), jnp.broadcast_to(x[1:2], (c, 2 * c)))
               for x in cf_t]
        decay = [jnp.where(incl, jnp.exp(jnp.where(incl, jnp.where(fwd, gf_c[g], gb_c[g]) - g_r[g], 0.0)), 0.0)
                 for g in n]
        kq = [_dot_nt(jnp.concatenate([k[g], q[g]], axis=0), jnp.concatenate([k[g], k[g]], axis=0))
              for g in n]
        lm = [jnp.where(strict, kq[g][:c] * jnp.where(fwd, bf_c[g], bb_c[g]) * decay[g], 0.0) for g in n]
        for g in n:
            attn = (kq[g][c:] * decay[g]).astype(bf16)
            at_s[xf, cis[g]] = attn[:, :c]
            at_s[xb, cis[g]] = attn[:, c:]
        a = [jnp.dot(x.astype(bf16), blockdiag(x), preferred_element_type=f32) for x in lm]
        tp = [-x for x in lm]
        p = 2
        while 2 * p < c:
            ta = [jnp.dot(jnp.concatenate([tp[g], a[g]], axis=0).astype(bf16), blockdiag(a[g]),
                          preferred_element_type=f32) for g in n]
            tp = [tp[g] + a[g] + ta[g][:c] for g in n]
            a = [x[c:] for x in ta]
            p *= 2
        tp = [tp[g] + a[g] + jnp.dot(tp[g].astype(bf16), blockdiag(a[g]), preferred_element_type=f32)
              for g in n]
        for g in n:
            ci, r = cis[g], rows[g]
            v = v_s[r, :]
            eg_f, eg_b = jnp.exp(gf_c[g]), jnp.exp(gb_c[g])
            rhs = jnp.concatenate([jnp.concatenate([v * bf_c[g], k[g] * (bf_c[g] * eg_f)], axis=1),
                                   jnp.concatenate([v * bb_c[g], k[g] * (bb_c[g] * eg_b)], axis=1)], axis=0)
            uw = rhs + jnp.dot(blockdiag(tp[g]), rhs.astype(bf16), preferred_element_type=f32)
            u_s[xf, r, :] = uw[:c, :dk]
            u_s[xb, r, :] = uw[c:, :dk]
            wq_s[xf, ci] = jnp.concatenate([uw[:c, dk:], q[g] * eg_f], axis=0).astype(bf16)
            wq_s[xb, ci] = jnp.concatenate([uw[c:, dk:], q[g] * eg_b], axis=0).astype(bf16)
            gl_f, gl_b = gf_c[g][c - 1:c], gb_c[g][0:1]
            kgt_s[xf, ci] = (k[g] * jnp.exp(gl_f - gf_c[g])).T.astype(bf16)
            kgt_s[xb, ci] = (k[g] * jnp.exp(gl_b - gb_c[g])).T.astype(bf16)
            egl_s[hh, ci] = jnp.concatenate([jnp.exp(gl_f), jnp.exp(gl_b),
                                             jnp.zeros((SUBLANES - 2, LANES), f32)], axis=0)
            o_s[hh, r, :] = jnp.zeros((c, GDN_DV), f32)

    for hh in range(GDN_HP):
        prep(hh, qc_ref, kc_ref, vc_ref, abc_ref, tc, 0)
        prep(hh, ql_ref, kl_ref, vl_ref, abl_ref, tl, tc)

        def chunk_body(i, carry, hh=hh):
            chunk_group(hh, [GDN_GROUP * i + g for g in range(GDN_GROUP)])
            return carry

        lax.fori_loop(0, nc // GDN_GROUP, chunk_body, 0)

    st_s[...] = jnp.zeros_like(st_s)
    chains = [(hh, d) for hh in range(GDN_HP) for d in range(2)]

    def scan_body(i, carry):
        ci = (i, jnp.where(i < ncc, ncc - 1 - i, nc + ncc - 1 - i))
        rows = [pl.ds(pl.multiple_of(x * c, c), c) for x in ci]
        n = range(len(chains))
        s = [st_s[x] for x in n]
        ws = [jnp.dot(wq_s[x, ci[chains[x][1]]], s[x].astype(bf16), preferred_element_type=f32) for x in n]
        v_new = [(u_s[x, rows[chains[x][1]], :] - ws[x][:c]).astype(bf16) for x in n]
        upd = [jnp.dot(kgt_s[x, ci[chains[x][1]]], v_new[x], preferred_element_type=f32) for x in n]
        out = [ws[x][c:] + jnp.dot(at_s[x, ci[chains[x][1]]], v_new[x], preferred_element_type=f32) for x in n]
        for x in n:
            hh, d = chains[x]
            st_s[x] = s[x] * egl_s[hh, ci[d]][d:d + 1] + upd[x]
            o_s[hh, rows[d], :] += out[x]
        return carry

    lax.fori_loop(0, nc, scan_body, 0)

    def finish(hh, z_ref, o_ref, t, base):
        ls = slice(hh * LANES, (hh + 1) * LANES)

        def body(i, carry):
            r0 = pl.ds(pl.multiple_of(i * rb, rb), rb)
            o = o_s[hh, pl.ds(pl.multiple_of(base + i * rb, rb), rb), :]
            y = o * lax.rsqrt(jnp.mean(o * o, axis=-1, keepdims=True) + EPS) * ng_ref[...]
            o_ref[0, r0, ls] = y * _silu(z_ref[0, r0, ls])
            return carry
        lax.fori_loop(0, t // rb, body, 0)

    for hh in range(GDN_HP):
        finish(hh, zc_ref, oc_ref, tc, 0)
        finish(hh, zl_ref, ol_ref, tl, tc)


def _gdn(p_l, ab_l, p_c, ab_c, conv_w, pa, pd, norm_g):
    b, tl, _ = p_l.shape
    tc = p_c.shape[1]
    h, c, hp = GDN_HEADS, GDN_CHUNK, GDN_HP
    nc = (tl + tc) // c
    assert tl % GDN_ROWS == 0 and tc % GDN_ROWS == 0 and nc % GDN_GROUP == 0
    qb, zb = OFF_QKV // LANES, OFF_Z // LANES
    assert h % hp == 0 and qb % hp == 0 and zb % hp == 0
    w = hp * LANES

    def specs(t):
        heads = lambda off: pl.BlockSpec((1, t, w), lambda i, j, off=off // hp: (i, 0, off + j))
        return [heads(qb), heads(qb + h), heads(qb + 2 * h), heads(zb),
                pl.BlockSpec((1, hp, 4, t), lambda i, j: (i, j, 0, 0))]

    cw = lambda off: pl.BlockSpec((3, w), lambda i, j, off=off // hp: (0, off + j))
    par = pl.BlockSpec((hp, SUBLANES, LANES), lambda i, j: (j, 0, 0))
    out = lambda t: pl.BlockSpec((1, t, w), lambda i, j: (i, 0, j))
    tt = tl + tc
    return pl.pallas_call(
        _gdn_kernel,
        grid=(b, h // hp),
        in_specs=specs(tl) + specs(tc) + [cw(0), cw(h), cw(2 * h), par, par, _const_spec((1, GDN_DV))],
        out_specs=[out(tl), out(tc)],
        out_shape=[jax.ShapeDtypeStruct((b, tl, h * GDN_DV), f32),
                   jax.ShapeDtypeStruct((b, tc, h * GDN_DV), f32)],
        scratch_shapes=[pltpu.VMEM((tt, GDN_DK), f32), pltpu.VMEM((tt, GDN_DK), f32),
                        pltpu.VMEM((tt, GDN_DV), f32), pltpu.VMEM((tt, LANES), f32),
                        pltpu.VMEM((2 * hp, tt, GDN_DV), f32), pltpu.VMEM((2 * hp, nc, 2 * c, GDN_DK), bf16),
                        pltpu.VMEM((2 * hp, nc, GDN_DK, c), bf16), pltpu.VMEM((2 * hp, nc, c, c), bf16),
                        pltpu.VMEM((hp, nc, SUBLANES, LANES), f32), pltpu.VMEM((hp, tt, GDN_DV), f32),
                        pltpu.VMEM((2 * hp, GDN_DK, GDN_DV), f32)],
        compiler_params=pltpu.CompilerParams(dimension_semantics=("parallel", "parallel"),
                                             vmem_limit_bytes=GDN_VMEM_LIMIT),
        name="gated_deltanet",
    )(p_l, p_l, p_l, p_l, ab_l, p_c, p_c, p_c, p_c, ab_c, conv_w, conv_w, conv_w, pa, pd, norm_g)


def _ret_kernel(ql_ref, kl_ref, vl_ref, gl_ref, qc_ref, kc_ref, vc_ref, gc_ref,
                cos_ref, sin_ref, dsum_ref, sc_ref, gch_ref,
                ol_ref, oc_ref, sf_ref, sb_ref, q_s, k_s, o_s):
    c = RET_CHUNK
    w = RET_HEADS * RET_DK
    ri = lax.broadcasted_iota(jnp.int32, (w, w), 0) // RET_DK
    ci = lax.broadcasted_iota(jnp.int32, (w, w), 1) // RET_DK
    blockdiag = ri == ci
    lane_head = lax.broadcasted_iota(jnp.int32, (c, w), 1) // RET_DK

    def intra(q, k, v):
        out = jnp.zeros((c, w), f32)
        for h in range(RET_HEADS):
            m = lane_head == h
            scores = _dot_nt(jnp.where(m, q, 0.0), k) * dsum_ref[h]
            out = out + _dot(scores, jnp.where(m, v, 0.0))
        return out

    def update(s_ref, k, v, d):
        new = jnp.where(blockdiag, _dot_tn(k * sc_ref[2 * d + 1], v), 0.0)
        s_ref[...] = s_ref[...] * gch_ref[d] + new

    first_half = lax.broadcasted_iota(jnp.int32, (c, w), 1) % RET_DK < RET_DK // 2
    avg = jnp.where(blockdiag, 1.0 / RET_DV, 0.0).astype(bf16)

    def rope(x, r):
        swapped = jnp.where(first_half, pltpu.roll(x, w - RET_DK // 2, 1), pltpu.roll(x, RET_DK // 2, 1))
        return x * cos_ref[r, :] + swapped * sin_ref[r, :]

    def head_mean(x):
        hi = x.astype(bf16)
        lo = (x - hi.astype(f32)).astype(bf16)
        return jnp.dot(hi, avg, preferred_element_type=f32) + jnp.dot(lo, avg, preferred_element_type=f32)

    def finish(o, gate):
        d = o - head_mean(o)
        return d * lax.rsqrt(head_mean(d * d) + EPS) * _silu(gate)

    sf_ref[...] = jnp.zeros_like(sf_ref)
    sb_ref[...] = jnp.zeros_like(sb_ref)
    qc, kc, vc = qc_ref[0], kc_ref[0] * (RET_DK ** -0.5), vc_ref[0]
    oc_ref[0] = finish(intra(qc, kc, vc), gc_ref[0])
    update(sf_ref, kc, vc, 0)
    update(sb_ref, kc, vc, 1)
    n = ql_ref.shape[1] // c
    for i in range(n):
        r = slice(i * c, (i + 1) * c)
        q = rope(ql_ref[0, r, :], r)
        k = rope(kl_ref[0, r, :] * (RET_DK ** -0.5), r)
        v = vl_ref[0, r, :]
        q_s[r, :] = q
        k_s[r, :] = k
        o_s[r, :] = intra(q, k, v) + _dot(q * sc_ref[0], sf_ref[...])
        update(sf_ref, k, v, 0)
    for i in range(n - 1, -1, -1):
        r = slice(i * c, (i + 1) * c)
        o = o_s[r, :] + _dot(q_s[r, :] * sc_ref[2], sb_ref[...])
        ol_ref[0, r, :] = finish(o, gl_ref[0, r, :])
        update(sb_ref, k_s[r, :], vl_ref[0, r, :], 1)


def _retention(p_l, p_c, cos, sin, dsum, scales, gch):
    b, tl, _ = p_l.shape
    tc = p_c.shape[1]
    w = RET_HEADS * RET_DK
    assert OFF_RET % w == 0
    blk = lambda t, j: pl.BlockSpec((1, t, w), lambda i, j=j: (i, 0, OFF_RET // w + j))
    tok = lambda t: pl.BlockSpec((1, t, w), lambda i: (i, 0, 0))
    return pl.pallas_call(
        _ret_kernel,
        grid=(b,),
        in_specs=[blk(tl, j) for j in range(4)] + [blk(tc, j) for j in range(4)]
        + [_const_spec(x.shape) for x in (cos, sin, dsum, scales, gch)],
        out_specs=[tok(tl), tok(tc)],
        out_shape=[jax.ShapeDtypeStruct((b, tl, w), f32), jax.ShapeDtypeStruct((b, tc, w), f32)],
        scratch_shapes=[pltpu.VMEM((w, w), f32), pltpu.VMEM((w, w), f32),
                        pltpu.VMEM((tl, w), f32), pltpu.VMEM((tl, w), f32), pltpu.VMEM((tl, w), f32)],
        compiler_params=_cparams(("parallel",)),
        name="retention",
    )(p_l, p_l, p_l, p_l, p_c, p_c, p_c, p_c, cos, sin, dsum, scales, gch)


def _hyena(p, lp):
    b, t, _ = p.shape
    hb, c = HY_BLOCK, HY_WIDTH
    nb = t // hb
    p3 = _hyena_prep(p, lp['hy_conv_w'], lp['hy_conv_b'][None])
    bias = jnp.broadcast_to(lp['hy_bias'][:, :, None], (HY_ORDER, c, hb))
    z = _hyena_conv(p3, _hyena_filters(t, lp), bias, nb)
    return _hyena_post(z, b)


def _gdn_gate_inputs(p):
    b, t, _ = p.shape
    ab = p[..., OFF_AB:OFF_AB + 4 * GDN_HEADS].reshape(b, t, 4, GDN_HEADS)
    return ab.transpose(0, 3, 2, 1)


def _gdn_gate_params(a_log, dt_bias):
    def rows(x):
        x = jnp.pad(x.T, ((0, 0), (0, SUBLANES - 2)))
        return jnp.broadcast_to(x[:, :, None], (GDN_HEADS, SUBLANES, LANES))
    return rows(a_log), rows(dt_bias)


def _rope_tables(t_lat):
    rows_n = t_lat // GRID_W
    row = jnp.repeat(jnp.arange(rows_n, dtype=f32), GRID_W)
    colp = jnp.tile(jnp.arange(GRID_W, dtype=f32), rows_n)
    nf = RET_DK // 4
    inv = ROPE_BASE ** (-jnp.arange(nf, dtype=f32) / nf)
    ang = jnp.concatenate([row[:, None] * inv, colp[:, None] * inv], axis=-1)
    cos, sin = jnp.cos(ang), jnp.sin(ang)
    return (jnp.tile(jnp.concatenate([cos, cos], axis=-1), (1, RET_HEADS)),
            jnp.tile(jnp.concatenate([-sin, sin], axis=-1), (1, RET_HEADS)))


def _ret_constants(lp):
    c = RET_CHUNK
    lg = jax.nn.log_sigmoid(lp['ret_decay_logit'])
    idx = jnp.arange(c, dtype=f32)
    rel = idx[:, None] - idx[None, :]
    d_f = jnp.exp(jnp.where(rel >= 0, rel * lg[0][:, None, None], -jnp.inf))
    d_b = jnp.exp(jnp.where(rel <= 0, -rel * lg[1][:, None, None], -jnp.inf))
    lane = lambda x: jnp.repeat(x, RET_DK, axis=-1)
    scales = jnp.stack([lane(jnp.exp((idx + 1.0)[:, None] * lg[0])),
                        lane(jnp.exp((c - 1.0 - idx)[:, None] * lg[0])),
                        lane(jnp.exp((c - idx)[:, None] * lg[1])),
                        lane(jnp.exp(idx[:, None] * lg[1]))])
    gch = lane(jnp.exp(c * lg))[:, None, :]
    return d_f + d_b, scales, gch


def _reorder_w_in(w):
    h2 = 2 * GDN_HEADS
    a0 = OFF_Z + GDN_HEADS * GDN_DV
    parts = [w[:, :a0], w[:, a0 + 2 * h2:], w[:, a0:a0 + 2 * h2],
             jnp.zeros((w.shape[0], LANES - 2 * h2), w.dtype)]
    return jnp.concatenate(parts, axis=1)


def kernel(x, c, ctx, c_ctx, mod_w, mod_b, norm1_g, w_in, hy_conv_w, hy_conv_b, hy_f_w1, hy_f_b1, hy_f_freq1, hy_f_w2, hy_f_b2, hy_f_freq2, hy_f_w3, hy_bias, gdn_conv_w, gdn_a_log, gdn_dt_bias, gdn_norm_g, ret_decay_logit, w_out, norm2_g, ffn_w_in, ffn_w_out, final_norm_g):
    bsz, t_lat, d = x.shape
    cos, sin = _rope_tables(t_lat)
    pad_rows = (-(bsz + 1)) % SUBLANES
    cs = jnp.concatenate([c, c_ctx[None], jnp.zeros((pad_rows, d), f32)], axis=0)
    gf = final_norm_g[None]
    for i in range(DEPTH):
        need_ctx = i < DEPTH - 1
        lp = {'hy_conv_w': hy_conv_w[i], 'hy_conv_b': hy_conv_b[i], 'hy_f_w1': hy_f_w1[i],
              'hy_f_b1': hy_f_b1[i], 'hy_f_freq1': hy_f_freq1[i], 'hy_f_w2': hy_f_w2[i],
              'hy_f_b2': hy_f_b2[i], 'hy_f_freq2': hy_f_freq2[i], 'hy_f_w3': hy_f_w3[i],
              'hy_bias': hy_bias[i], 'ret_decay_logit': ret_decay_logit[i]}
        mod = _mod_vectors(cs, mod_w[i].astype(bf16), mod_b[i][None])
        mod_lat = mod[:bsz].reshape(bsz, N_MOD, d)
        mod_ctx = jnp.broadcast_to(mod[bsz].reshape(1, N_MOD, d), (bsz, N_MOD, d))
        w_in16 = _reorder_w_in(w_in[i]).astype(bf16)
        g1 = norm1_g[i][None]
        p_l = _in_proj(x, g1, mod_lat, w_in16, 256)
        p_c = _in_proj(ctx, g1, mod_ctx, w_in16, 256)

        hy_l = _hyena(p_l, lp)
        pa, pd = _gdn_gate_params(gdn_a_log[i], gdn_dt_bias[i])
        gd_l, gd_c = _gdn(p_l, _gdn_gate_inputs(p_l), p_c, _gdn_gate_inputs(p_c), gdn_conv_w[i], pa, pd,
                          gdn_norm_g[i][None])

        dsum, scales, gch = _ret_constants(lp)
        rt_l, rt_c = _retention(p_l, p_c, cos, sin, dsum, scales, gch)

        wo16, w116, w216 = w_out[i].astype(bf16), ffn_w_in[i].astype(bf16), ffn_w_out[i].astype(bf16)
        g2 = norm2_g[i][None]
        x = _out_ffn(x, hy_l, gd_l, rt_l, mod_lat, wo16, g2, w116, w216, gf, 256, final=not need_ctx)
        if need_ctx:
            hy_c = _hyena(p_c, lp)
            ctx = _out_ffn(ctx, hy_c, gd_c, rt_c, mod_ctx, wo16, g2, w116, w216, gf, 256, final=False)
    return x
```

```python
import functools
import math

import jax
import jax.numpy as jnp
from jax import lax
from jax.experimental import pallas as pl
from jax.experimental.pallas import tpu as pltpu

D_MODEL = 1024
DEPTH = 2
GRID_W = 64
EPS = 1e-6
N_MOD = 6

HY_WIDTH = D_MODEL // 4
HY_ORDER = 2
HY_EMB = 33
HY_FAST_DECAY = 0.3
HY_SLOW_DECAY = 1.5
HY_TARGET = 1e-2
GDN_HEADS = D_MODEL // 256
GDN_DK = 128
GDN_DV = 128
RET_HEADS = D_MODEL // 256
RET_DK = 64
RET_DV = 64
ROPE_BASE = 10000.0
FFN_HIDDEN = ((8 * D_MODEL + 3 * 256 - 1) // (3 * 256)) * 256

GDN_CHUNK = 64
GDN_ROWS = 256
GDN_HP = 2
GDN_GROUP = 12
RET_CHUNK = 256
HY_BLOCK = 256
HY_PAIR = 2
LANES = 128
SUBLANES = 8

GDN_QKV = 2 * GDN_HEADS * GDN_DK + GDN_HEADS * GDN_DV
OFF_HY = 0
OFF_QKV = OFF_HY + 3 * HY_WIDTH
OFF_Z = OFF_QKV + GDN_QKV
OFF_RET = OFF_Z + GDN_HEADS * GDN_DV
OFF_AB = OFF_RET + 4 * RET_HEADS * RET_DK
IN_PAD = OFF_AB + LANES

VMEM_LIMIT = 56 * 1024 * 1024
GDN_VMEM_LIMIT = 60 * 1024 * 1024

f32 = jnp.float32
bf16 = jnp.bfloat16


def _cparams(sem):
    return pltpu.CompilerParams(dimension_semantics=sem, vmem_limit_bytes=VMEM_LIMIT)


def _const_spec(shape):
    nd = len(shape)
    return pl.BlockSpec(shape, lambda *_: (0,) * nd, pipeline_mode=pl.Buffered(1))


def _rms(x, g):
    return x * lax.rsqrt(jnp.mean(x * x, axis=-1, keepdims=True) + EPS) * g


def _silu(x):
    return x * jax.nn.sigmoid(x)


def _dot(a, b):
    return jnp.dot(a.astype(bf16), b.astype(bf16), preferred_element_type=f32)


def _dot_nt(a, b):
    return lax.dot_general(a.astype(bf16), b.astype(bf16), (((1,), (1,)), ((), ())),
                           preferred_element_type=f32)


def _dot_tn(a, b):
    return lax.dot_general(a.astype(bf16), b.astype(bf16), (((0,), (0,)), ((), ())),
                           preferred_element_type=f32)


def _mod_kernel(c_ref, w_ref, b_ref, o_ref):
    o_ref[...] = _dot(_silu(c_ref[...]), w_ref[...]) + b_ref[...]


def _mod_vectors(cs, w16, b):
    rows, d = cs.shape
    n = w16.shape[1]
    tn = 1536
    return pl.pallas_call(
        _mod_kernel,
        grid=(n // tn,),
        in_specs=[pl.BlockSpec((rows, d), lambda j: (0, 0)),
                  pl.BlockSpec((d, tn), lambda j: (0, j)),
                  pl.BlockSpec((1, tn), lambda j: (0, j))],
        out_specs=pl.BlockSpec((rows, tn), lambda j: (0, j)),
        out_shape=jax.ShapeDtypeStruct((rows, n), f32),
        compiler_params=_cparams(("parallel",)),
        name="mod_vectors",
    )(cs, w16, b)


def _in_proj_kernel(x_ref, g_ref, mod_ref, w_ref, o_ref):
    x = x_ref[0]
    mod = mod_ref[0]
    h = _rms(x, g_ref[...]) * (1.0 + mod[1:2]) + mod[0:1]
    h16 = h.astype(bf16)
    n = w_ref.shape[1]
    step = 512
    for j in range(0, n, step):
        e = min(j + step, n)
        o_ref[0, :, j:e] = jnp.dot(h16, w_ref[:, j:e], preferred_element_type=f32)


def _in_proj(x, g, mod, w16, tm):
    b, t, d = x.shape
    n = w16.shape[1]
    return pl.pallas_call(
        _in_proj_kernel,
        grid=(b, t // tm),
        in_specs=[pl.BlockSpec((1, tm, d), lambda i, j: (i, j, 0)),
                  _const_spec((1, d)),
                  pl.BlockSpec((1, N_MOD, d), lambda i, j: (i, 0, 0)),
                  _const_spec((d, n))],
        out_specs=pl.BlockSpec((1, tm, n), lambda i, j: (i, j, 0)),
        out_shape=jax.ShapeDtypeStruct((b, t, n), f32),
        compiler_params=_cparams(("parallel", "parallel")),
        name="in_proj",
    )(x, g, mod, w16)


def _out_ffn_kernel(x_ref, hy_ref, gd_ref, rt_ref, mod_ref, wo_ref, g2_ref, w1_ref, w2_ref, gf_ref, o_ref,
                    *, final):
    x = x_ref[0]
    mod = mod_ref[0]
    o0, o1, o2 = HY_WIDTH, HY_WIDTH + GDN_HEADS * GDN_DV, D_MODEL
    y = (jnp.dot(hy_ref[0].astype(bf16), wo_ref[0:o0, :], preferred_element_type=f32)
         + jnp.dot(gd_ref[0].astype(bf16), wo_ref[o0:o1, :], preferred_element_type=f32)
         + jnp.dot(rt_ref[0].astype(bf16), wo_ref[o1:o2, :], preferred_element_type=f32))
    x = x + mod[2:3] * y
    h16 = (_rms(x, g2_ref[...]) * (1.0 + mod[4:5]) + mod[3:4]).astype(bf16)
    step = 256
    acc = jnp.zeros_like(x)
    for j in range(0, FFN_HIDDEN, step):
        gate = jnp.dot(h16, w1_ref[:, j:j + step], preferred_element_type=f32)
        up = jnp.dot(h16, w1_ref[:, FFN_HIDDEN + j:FFN_HIDDEN + j + step], preferred_element_type=f32)
        acc = acc + jnp.dot((_silu(gate) * up).astype(bf16), w2_ref[j:j + step, :], preferred_element_type=f32)
    x = x + mod[5:6] * acc
    if final:
        x = _rms(x, gf_ref[...])
    o_ref[0] = x


def _out_ffn(x, hy, gd, rt, mod, wo16, g2, w116, w216, gf, tm, final):
    b, t, d = x.shape
    tok = lambda w: pl.BlockSpec((1, tm, w), lambda i, j: (i, j, 0))
    return pl.pallas_call(
        functools.partial(_out_ffn_kernel, final=final),
        grid=(b, t // tm),
        in_specs=[tok(d), tok(hy.shape[-1]), tok(gd.shape[-1]), tok(rt.shape[-1]),
                  pl.BlockSpec((1, N_MOD, d), lambda i, j: (i, 0, 0)),
                  _const_spec(wo16.shape), _const_spec((1, d)), _const_spec(w116.shape),
                  _const_spec(w216.shape), _const_spec((1, d))],
        out_specs=tok(d),
        out_shape=jax.ShapeDtypeStruct((b, t, d), f32),
        compiler_params=_cparams(("parallel", "parallel")),
        name="out_ffn",
    )(x, hy, gd, rt, mod, wo16, g2, w116, w216, gf)


def _hyena_kernel(v_ref, x1_ref, x2_ref, k_ref, b_ref, o_ref, acc_ref, *, nb, cb):
    hb = HY_BLOCK
    rb = v_ref.shape[1] // nb

    def conv(u16, kext):
        m = range(len(u16))
        for d in range(-(nb - 1), nb):
            dd = d % (2 * nb)
            t0, t1 = max(0, d), min(nb - 1, nb - 1 + d) + 1
            for i in m:
                win = jnp.broadcast_to(kext[i][:, hb * dd:hb * dd + 2 * hb], (hb, 2 * hb))
                toep = pltpu.roll(win, 0, 1, stride=1, stride_axis=0)[:, hb:].astype(bf16)
                contrib = jnp.dot(u16[i][rb * (t0 - d):rb * (t1 - d)], toep, preferred_element_type=f32)
                if d == -(nb - 1):
                    acc_ref[i] = jnp.zeros(acc_ref.shape[1:], f32)
                acc_ref[i, rb * t0:rb * t1, :] += contrib
        return [acc_ref[i] for i in m]

    def body(g, carry):
        cs = [HY_PAIR * g + i for i in range(HY_PAIR)]
        z = [v_ref[c] for c in cs]
        for n, gate_ref in enumerate((x1_ref, x2_ref)):
            y = conv([x.astype(bf16) for x in z], [k_ref[n, pl.ds(c, 1), :] for c in cs])
            z = [gate_ref[c] * (y[i] + z[i] * b_ref[n, pl.ds(c, 1), :]) for i, c in enumerate(cs)]
        for i, c in enumerate(cs):
            o_ref[c] = z[i]
        return carry

    lax.fori_loop(0, cb // HY_PAIR, body, 0)


def _hyena_conv(p3, kext, bias, nb, cb=SUBLANES):
    c3, r, hb = p3.shape
    c = c3 // 3
    klen = kext.shape[-1]
    sig = lambda s: pl.BlockSpec((cb, r, hb), lambda i, s=s: (s * (c // cb) + i, 0, 0))
    return pl.pallas_call(
        functools.partial(_hyena_kernel, nb=nb, cb=cb),
        grid=(c // cb,),
        in_specs=[sig(0), sig(1), sig(2),
                  pl.BlockSpec((HY_ORDER, cb, klen), lambda i: (0, i, 0)),
                  pl.BlockSpec((HY_ORDER, cb, hb), lambda i: (0, i, 0))],
        out_specs=pl.BlockSpec((cb, r, hb), lambda i: (i, 0, 0)),
        out_shape=jax.ShapeDtypeStruct((c, r, hb), f32),
        scratch_shapes=[pltpu.VMEM((HY_PAIR, r, hb), f32)],
        compiler_params=_cparams(("parallel",)),
        name=f"hyena_conv_nb{nb}",
    )(p3, p3, p3, kext, bias)


def _hyena_prep_kernel(x_ref, before_ref, after_ref, w_ref, b_ref, o_ref):
    hb = HY_BLOCK
    w = w_ref[...]
    row = lax.broadcasted_iota(jnp.int32, (hb, LANES), 0)
    s = pl.program_id(1)
    has_before = s > 0
    has_after = s < pl.num_programs(1) - 1
    for i in range(x_ref.shape[0]):
        x = x_ref[i]
        prev = jnp.where(has_before, before_ref[i, SUBLANES - 1:SUBLANES, :], 0.0)
        nxt = jnp.where(has_after, after_ref[i, 0:1, :], 0.0)
        xp = jnp.where(row == 0, prev, pltpu.roll(x, 1, 0))
        xn = jnp.where(row == hb - 1, nxt, pltpu.roll(x, hb - 1, 0))
        y = xp * w[0:1] + x * w[1:2] + xn * w[2:3] + b_ref[...]
        o_ref[:, i, :] = y.T


def _hyena_prep(p, conv_w, conv_b):
    b, t, _ = p.shape
    hb = HY_BLOCK
    n = 3 * HY_WIDTH
    per = hb // SUBLANES
    last = t // SUBLANES - 1
    assert OFF_HY == 0 and b % SUBLANES == 0
    return pl.pallas_call(
        _hyena_prep_kernel,
        grid=(n // LANES, t // hb),
        in_specs=[pl.BlockSpec((b, hb, LANES), lambda j, s: (0, s, j)),
                  pl.BlockSpec((b, SUBLANES, LANES), lambda j, s: (0, jnp.maximum(s * per - 1, 0), j)),
                  pl.BlockSpec((b, SUBLANES, LANES), lambda j, s: (0, jnp.minimum((s + 1) * per, last), j)),
                  pl.BlockSpec((3, LANES), lambda j, s: (0, j)),
                  pl.BlockSpec((1, LANES), lambda j, s: (0, j))],
        out_specs=pl.BlockSpec((LANES, b, hb), lambda j, s: (j, s, 0)),
        out_shape=jax.ShapeDtypeStruct((n, (t // hb) * b, hb), f32),
        compiler_params=_cparams(("parallel", "parallel")),
        name="hyena_prep",
    )(p, p, p, conv_w, conv_b)


def _hyena_post_kernel(z_ref, o_ref):
    for i in range(z_ref.shape[1]):
        o_ref[i] = z_ref[:, i, :].T


def _hyena_post(z, b):
    c, r, hb = z.shape
    nb = r // b
    return pl.pallas_call(
        _hyena_post_kernel,
        grid=(c // LANES, nb),
        in_specs=[pl.BlockSpec((LANES, b, hb), lambda j, s: (j, s, 0))],
        out_specs=pl.BlockSpec((b, hb, LANES), lambda j, s: (0, s, j)),
        out_shape=jax.ShapeDtypeStruct((b, nb * hb, c), f32),
        compiler_params=_cparams(("parallel", "parallel")),
        name="hyena_post",
    )(z)


def _hyena_filter_kernel(z_ref, w1_ref, b1_ref, f1_ref, w2_ref, b2_ref, f2_ref, w3_ref, win_ref, o_ref):
    hi = lax.Precision.HIGHEST
    hb = HY_BLOCK
    t = z_ref.shape[1]
    for side in range(2):
        h = jnp.sin(f1_ref[...] * (jnp.dot(z_ref[side], w1_ref[...], precision=hi,
                                           preferred_element_type=f32) + b1_ref[...]))
        h = jnp.sin(f2_ref[...] * (jnp.dot(h, w2_ref[...], precision=hi, preferred_element_type=f32)
                                   + b2_ref[...]))
        w3 = w3_ref[:, side * HY_WIDTH:(side + 1) * HY_WIDTH]
        k = lax.dot_general(w3, h, (((0,), (1,)), ((), ())), precision=hi,
                            preferred_element_type=f32) * win_ref[side]
        o_ref[0, :, hb + side * t:hb + (side + 1) * t] = k
        if side == 1:
            o_ref[0, :, 0:hb] = k[:, t - hb:]


def _hyena_filters(t_len, lp):
    pos = jnp.arange(t_len, dtype=f32)
    t = jnp.linspace(0.0, 1.0, t_len, dtype=f32)
    bands = (HY_EMB - 1) // 2
    f = jnp.linspace(1e-4, bands - 1, bands, dtype=f32)
    ang = (2.0 * math.pi / t_len) * pos[:, None] * f[None, :]
    z = jnp.concatenate([t[:, None], jnp.cos(ang), -jnp.sin(ang)], axis=-1)
    max_decay = math.log(HY_TARGET) / HY_FAST_DECAY
    min_decay = math.log(HY_TARGET) / HY_SLOW_DECAY
    deltas = jnp.abs(jnp.linspace(min_decay, max_decay, HY_WIDTH, dtype=f32))
    window = jnp.exp(-t[:, None] * deltas[None, :])
    neg = lambda x: jnp.concatenate([x[:1], x[:0:-1]], axis=0)
    zs = jnp.pad(jnp.stack([z, neg(z)]), ((0, 0), (0, 0), (0, LANES - HY_EMB)))
    win = jnp.stack([window, neg(window).at[0].set(0.0)]).transpose(0, 2, 1)
    w1 = jnp.pad(lp['hy_f_w1'], ((0, LANES - HY_EMB), (0, 0)))
    row = lambda x: x[None]
    args = (zs, w1, row(lp['hy_f_b1']), row(lp['hy_f_freq1']), lp['hy_f_w2'], row(lp['hy_f_b2']),
            row(lp['hy_f_freq2']), lp['hy_f_w3'], win)
    specs = [_const_spec(a.shape) for a in args]
    specs[7] = pl.BlockSpec((lp['hy_f_w3'].shape[0], 2 * HY_WIDTH), lambda n: (0, n))
    klen = 2 * t_len + HY_BLOCK
    return pl.pallas_call(
        _hyena_filter_kernel,
        grid=(HY_ORDER,),
        in_specs=specs,
        out_specs=pl.BlockSpec((1, HY_WIDTH, klen), lambda n: (n, 0, 0)),
        out_shape=jax.ShapeDtypeStruct((HY_ORDER, HY_WIDTH, klen), f32),
        compiler_params=_cparams(("parallel",)),
        name="hyena_filters",
    )(*args)


def _gdn_kernel(ql_ref, kl_ref, vl_ref, zl_ref, abl_ref, qc_ref, kc_ref, vc_ref, zc_ref, abc_ref,
                cwq_ref, cwk_ref, cwv_ref, pa_ref, pd_ref, ng_ref,
                ol_ref, oc_ref,
                q_s, k_s, v_s, cf_s, u_s, wq_s, kgt_s, at_s, egl_s, o_s, st_s):
    c = GDN_CHUNK
    rb = GDN_ROWS
    tc, tl = qc_ref.shape[1], ql_ref.shape[1]
    ncc, nc = tc // c, (tc + tl) // c
    dk = GDN_DK

    def conv_rows(x_ref, w_ref, r0, t, ls):
        x = x_ref[0, pl.ds(r0, rb), ls]
        before = pl.multiple_of(jnp.maximum(r0 - SUBLANES, 0), SUBLANES)
        after = pl.multiple_of(jnp.minimum(r0 + rb, t - SUBLANES), SUBLANES)
        prev = x_ref[0, pl.ds(before, SUBLANES), ls][SUBLANES - 1:SUBLANES]
        nxt = x_ref[0, pl.ds(after, SUBLANES), ls][0:1]
        prev = jnp.where(r0 > 0, prev, 0.0)
        nxt = jnp.where(r0 + rb < t, nxt, 0.0)
        row = lax.broadcasted_iota(jnp.int32, x.shape, 0)
        xp = jnp.where(row == 0, prev, pltpu.roll(x, 1, 0))
        xn = jnp.where(row == rb - 1, nxt, pltpu.roll(x, rb - 1, 0))
        w = w_ref[:, ls]
        return _silu(xp * w[0:1] + x * w[1:2] + xn * w[2:3])

    def l2n(x):
        return x * lax.rsqrt(jnp.sum(x * x, axis=-1, keepdims=True) + EPS)

    def prep(hh, q_ref, k_ref, v_ref, ab_ref, t, base):
        ls = slice(hh * LANES, (hh + 1) * LANES)

        def body(i, carry):
            r0 = pl.multiple_of(i * rb, rb)
            dst = pl.ds(pl.multiple_of(base + i * rb, rb), rb)
            q_s[dst, :] = l2n(conv_rows(q_ref, cwq_ref, r0, t, ls)) * (dk ** -0.5)
            k_s[dst, :] = l2n(conv_rows(k_ref, cwk_ref, r0, t, ls))
            v_s[dst, :] = conv_rows(v_ref, cwv_ref, r0, t, ls)
            return carry
        lax.fori_loop(0, t // rb, body, 0)

        ab = ab_ref[0, hh]
        ab = jnp.concatenate([ab, jnp.zeros((SUBLANES - ab.shape[0], t), f32)], axis=0)
        rowi = lax.broadcasted_iota(jnp.int32, ab.shape, 0)
        pos = lax.broadcasted_iota(jnp.int32, ab.shape, 1) % c
        tile = lambda p: jnp.concatenate([p] * (t // LANES), axis=1)
        g = -jnp.exp(tile(pa_ref[hh])) * jax.nn.softplus(ab + tile(pd_ref[hh]))
        acc_f, acc_b = g, g
        s = 1
        while s < c:
            acc_f = acc_f + jnp.where(pos >= s, pltpu.roll(acc_f, s, 1), 0.0)
            acc_b = acc_b + jnp.where(pos < c - s, pltpu.roll(acc_b, t - s, 1), 0.0)
            s *= 2
        cf_t = jnp.where(rowi == 0, acc_f, jnp.where(rowi == 1, acc_b, jax.nn.sigmoid(ab)))
        pad = jnp.zeros((LANES - SUBLANES, LANES), f32)
        for j in range(t // LANES):
            blk = jnp.concatenate([cf_t[:, j * LANES:(j + 1) * LANES], pad], axis=0)
            cf_s[base + j * LANES:base + (j + 1) * LANES, :] = blk.T

    lane = lax.broadcasted_iota(jnp.int32, (c, 2 * c), 1)
    row = lax.broadcasted_iota(jnp.int32, (c, 2 * c), 0)
    fwd = lane < c
    col = lane % c
    incl = (fwd & (row >= col)) | (~fwd & (row <= col))
    strict = (fwd & (row > col)) | (~fwd & (row < col))

    def blockdiag(a):
        return jnp.concatenate([jnp.where(fwd, a, 0.0), jnp.where(fwd, 0.0, a)], axis=0).astype(bf16)

    def chunk_group(hh, cis):
        n = range(len(cis))
        xf, xb = 2 * hh, 2 * hh + 1
        rows = [pl.ds(pl.multiple_of(ci * c, c), c) for ci in cis]
        cf = [cf_s[r, :] for r in rows]
        k = [k_s[r, :] for r in rows]
        q = [q_s[r, :] for r in rows]
        bc = lambda x, j: jnp.broadcast_to(x[:, j:j + 1], (c, LANES))
        gf_c, gb_c = [bc(x, 0) for x in cf], [bc(x, 1) for x in cf]
        bf_c, bb_c = [bc(x, 2) for x in cf], [bc(x, 3) for x in cf]
        cf_t = [jnp.concatenate([x, x], axis=0).T for x in cf]
        g_r = [jnp.where(fwd, jnp.broadcast_to(x[0:1], (c, 2 * c)), jnp.broadcast_to(x[1:2], (c, 2 * c)))
               for x in cf_t]
        decay = [jnp.where(incl, jnp.exp(jnp.where(incl, jnp.where(fwd, gf_c[g], gb_c[g]) - g_r[g], 0.0)), 0.0)
                 for g in n]
        kq = [_dot_nt(jnp.concatenate([k[g], q[g]], axis=0), jnp.concatenate([k[g], k[g]], axis=0))
              for g in n]
        lm = [jnp.where(strict, kq[g][:c] * jnp.where(fwd, bf_c[g], bb_c[g]) * decay[g], 0.0) for g in n]
        for g in n:
            attn = (kq[g][c:] * decay[g]).astype(bf16)
            at_s[xf, cis[g]] = attn[:, :c]
            at_s[xb, cis[g]] = attn[:, c:]
        a = [jnp.dot(x.astype(bf16), blockdiag(x), preferred_element_type=f32) for x in lm]
        tp = [-x for x in lm]
        p = 2
        while 2 * p < c:
            ta = [jnp.dot(jnp.concatenate([tp[g], a[g]], axis=0).astype(bf16), blockdiag(a[g]),
                          preferred_element_type=f32) for g in n]
            tp = [tp[g] + a[g] + ta[g][:c] for g in n]
            a = [x[c:] for x in ta]
            p *= 2
        tp = [tp[g] + a[g] + jnp.dot(tp[g].astype(bf16), blockdiag(a[g]), preferred_element_type=f32)
              for g in n]
        for g in n:
            ci, r = cis[g], rows[g]
            v = v_s[r, :]
            eg_f, eg_b = jnp.exp(gf_c[g]), jnp.exp(gb_c[g])
            rhs = jnp.concatenate([jnp.concatenate([v * bf_c[g], k[g] * (bf_c[g] * eg_f)], axis=1),
                                   jnp.concatenate([v * bb_c[g], k[g] * (bb_c[g] * eg_b)], axis=1)], axis=0)
            uw = rhs + jnp.dot(blockdiag(tp[g]), rhs.astype(bf16), preferred_element_type=f32)
            u_s[xf, r, :] = uw[:c, :dk]
            u_s[xb, r, :] = uw[c:, :dk]
            wq_s[xf, ci] = jnp.concatenate([uw[:c, dk:], q[g] * eg_f], axis=0).astype(bf16)
            wq_s[xb, ci] = jnp.concatenate([uw[c:, dk:], q[g] * eg_b], axis=0).astype(bf16)
            gl_f, gl_b = gf_c[g][c - 1:c], gb_c[g][0:1]
            kgt_s[xf, ci] = (k[g] * jnp.exp(gl_f - gf_c[g])).T.astype(bf16)
            kgt_s[xb, ci] = (k[g] * jnp.exp(gl_b - gb_c[g])).T.astype(bf16)
            egl_s[hh, ci] = jnp.concatenate([jnp.exp(gl_f), jnp.exp(gl_b),
                                             jnp.zeros((SUBLANES - 2, LANES), f32)], axis=0)
            o_s[hh, r, :] = jnp.zeros((c, GDN_DV), f32)

    for hh in range(GDN_HP):
        prep(hh, qc_ref, kc_ref, vc_ref, abc_ref, tc, 0)
        prep(hh, ql_ref, kl_ref, vl_ref, abl_ref, tl, tc)

        def chunk_body(i, carry, hh=hh):
            chunk_group(hh, [GDN_GROUP * i + g for g in range(GDN_GROUP)])
            return carry

        lax.fori_loop(0, nc // GDN_GROUP, chunk_body, 0)

    st_s[...] = jnp.zeros_like(st_s)
    chains = [(hh, d) for hh in range(GDN_HP) for d in range(2)]

    def scan_body(i, carry):
        ci = (i, jnp.where(i < ncc, ncc - 1 - i, nc + ncc - 1 - i))
        rows = [pl.ds(pl.multiple_of(x * c, c), c) for x in ci]
        n = range(len(chains))
        s = [st_s[x] for x in n]
        ws = [jnp.dot(wq_s[x, ci[chains[x][1]]], s[x].astype(bf16), preferred_element_type=f32) for x in n]
        v_new = [(u_s[x, rows[chains[x][1]], :] - ws[x][:c]).astype(bf16) for x in n]
        upd = [jnp.dot(kgt_s[x, ci[chains[x][1]]], v_new[x], preferred_element_type=f32) for x in n]
        out = [ws[x][c:] + jnp.dot(at_s[x, ci[chains[x][1]]], v_new[x], preferred_element_type=f32) for x in n]
        for x in n:
            hh, d = chains[x]
            st_s[x] = s[x] * egl_s[hh, ci[d]][d:d + 1] + upd[x]
            o_s[hh, rows[d], :] += out[x]
        return carry

    lax.fori_loop(0, nc, scan_body, 0)

    def finish(hh, z_ref, o_ref, t, base):
        ls = slice(hh * LANES, (hh + 1) * LANES)

        def body(i, carry):
            r0 = pl.ds(pl.multiple_of(i * rb, rb), rb)
            o = o_s[hh, pl.ds(pl.multiple_of(base + i * rb, rb), rb), :]
            y = o * lax.rsqrt(jnp.mean(o * o, axis=-1, keepdims=True) + EPS) * ng_ref[...]
            o_ref[0, r0, ls] = y * _silu(z_ref[0, r0, ls])
            return carry
        lax.fori_loop(0, t // rb, body, 0)

    for hh in range(GDN_HP):
        finish(hh, zc_ref, oc_ref, tc, 0)
        finish(hh, zl_ref, ol_ref, tl, tc)


def _gdn(p_l, ab_l, p_c, ab_c, conv_w, pa, pd, norm_g):
    b, tl, _ = p_l.shape
    tc = p_c.shape[1]
    h, c, hp = GDN_HEADS, GDN_CHUNK, GDN_HP
    nc = (tl + tc) // c
    assert tl % GDN_ROWS == 0 and tc % GDN_ROWS == 0 and nc % GDN_GROUP == 0
    qb, zb = OFF_QKV // LANES, OFF_Z // LANES
    assert h % hp == 0 and qb % hp == 0 and zb % hp == 0
    w = hp * LANES

    def specs(t):
        heads = lambda off: pl.BlockSpec((1, t, w), lambda i, j, off=off // hp: (i, 0, off + j))
        return [heads(qb), heads(qb + h), heads(qb + 2 * h), heads(zb),
                pl.BlockSpec((1, hp, 4, t), lambda i, j: (i, j, 0, 0))]

    cw = lambda off: pl.BlockSpec((3, w), lambda i, j, off=off // hp: (0, off + j))
    par = pl.BlockSpec((hp, SUBLANES, LANES), lambda i, j: (j, 0, 0))
    out = lambda t: pl.BlockSpec((1, t, w), lambda i, j: (i, 0, j))
    tt = tl + tc
    return pl.pallas_call(
        _gdn_kernel,
        grid=(b, h // hp),
        in_specs=specs(tl) + specs(tc) + [cw(0), cw(h), cw(2 * h), par, par, _const_spec((1, GDN_DV))],
        out_specs=[out(tl), out(tc)],
        out_shape=[jax.ShapeDtypeStruct((b, tl, h * GDN_DV), f32),
                   jax.ShapeDtypeStruct((b, tc, h * GDN_DV), f32)],
        scratch_shapes=[pltpu.VMEM((tt, GDN_DK), f32), pltpu.VMEM((tt, GDN_DK), f32),
                        pltpu.VMEM((tt, GDN_DV), f32), pltpu.VMEM((tt, LANES), f32),
                        pltpu.VMEM((2 * hp, tt, GDN_DV), f32), pltpu.VMEM((2 * hp, nc, 2 * c, GDN_DK), bf16),
                        pltpu.VMEM((2 * hp, nc, GDN_DK, c), bf16), pltpu.VMEM((2 * hp, nc, c, c), bf16),
                        pltpu.VMEM((hp, nc, SUBLANES, LANES), f32), pltpu.VMEM((hp, tt, GDN_DV), f32),
                        pltpu.VMEM((2 * hp, GDN_DK, GDN_DV), f32)],
        compiler_params=pltpu.CompilerParams(dimension_semantics=("parallel", "parallel"),
                                             vmem_limit_bytes=GDN_VMEM_LIMIT),
        name="gated_deltanet",
    )(p_l, p_l, p_l, p_l, ab_l, p_c, p_c, p_c, p_c, ab_c, conv_w, conv_w, conv_w, pa, pd, norm_g)


def _ret_kernel(ql_ref, kl_ref, vl_ref, gl_ref, qc_ref, kc_ref, vc_ref, gc_ref,
                cos_ref, sin_ref, dsum_ref, sc_ref, gch_ref,
                ol_ref, oc_ref, sf_ref, sb_ref, q_s, k_s, o_s):
    c = RET_CHUNK
    w = RET_HEADS * RET_DK
    ri = lax.broadcasted_iota(jnp.int32, (w, w), 0) // RET_DK
    ci = lax.broadcasted_iota(jnp.int32, (w, w), 1) // RET_DK
    blockdiag = ri == ci
    lane_head = lax.broadcasted_iota(jnp.int32, (c, w), 1) // RET_DK

    def intra(q, k, v):
        out = jnp.zeros((c, w), f32)
        for h in range(RET_HEADS):
            m = lane_head == h
            scores = _dot_nt(jnp.where(m, q, 0.0), k) * dsum_ref[h]
            out = out + _dot(scores, jnp.where(m, v, 0.0))
        return out

    def update(s_ref, k, v, d):
        new = jnp.where(blockdiag, _dot_tn(k * sc_ref[2 * d + 1], v), 0.0)
        s_ref[...] = s_ref[...] * gch_ref[d] + new

    first_half = lax.broadcasted_iota(jnp.int32, (c, w), 1) % RET_DK < RET_DK // 2
    avg = jnp.where(blockdiag, 1.0 / RET_DV, 0.0).astype(bf16)

    def rope(x, r):
        swapped = jnp.where(first_half, pltpu.roll(x, w - RET_DK // 2, 1), pltpu.roll(x, RET_DK // 2, 1))
        return x * cos_ref[r, :] + swapped * sin_ref[r, :]

    def head_mean(x):
        hi = x.astype(bf16)
        lo = (x - hi.astype(f32)).astype(bf16)
        return jnp.dot(hi, avg, preferred_element_type=f32) + jnp.dot(lo, avg, preferred_element_type=f32)

    def finish(o, gate):
        d = o - head_mean(o)
        return d * lax.rsqrt(head_mean(d * d) + EPS) * _silu(gate)

    sf_ref[...] = jnp.zeros_like(sf_ref)
    sb_ref[...] = jnp.zeros_like(sb_ref)
    qc, kc, vc = qc_ref[0], kc_ref[0] * (RET_DK ** -0.5), vc_ref[0]
    oc_ref[0] = finish(intra(qc, kc, vc), gc_ref[0])
    update(sf_ref, kc, vc, 0)
    update(sb_ref, kc, vc, 1)
    n = ql_ref.shape[1] // c
    for i in range(n):
        r = slice(i * c, (i + 1) * c)
        q = rope(ql_ref[0, r, :], r)
        k = rope(kl_ref[0, r, :] * (RET_DK ** -0.5), r)
        v = vl_ref[0, r, :]
        q_s[r, :] = q
        k_s[r, :] = k
        o_s[r, :] = intra(q, k, v) + _dot(q * sc_ref[0], sf_ref[...])
        update(sf_ref, k, v, 0)
    for i in range(n - 1, -1, -1):
        r = slice(i * c, (i + 1) * c)
        o = o_s[r, :] + _dot(q_s[r, :] * sc_ref[2], sb_ref[...])
        ol_ref[0, r, :] = finish(o, gl_ref[0, r, :])
        update(sb_ref, k_s[r, :], vl_ref[0, r, :], 1)


def _retention(p_l, p_c, cos, sin, dsum, scales, gch):
    b, tl, _ = p_l.shape
    tc = p_c.shape[1]
    w = RET_HEADS * RET_DK
    assert OFF_RET % w == 0
    blk = lambda t, j: pl.BlockSpec((1, t, w), lambda i, j=j: (i, 0, OFF_RET // w + j))
    tok = lambda t: pl.BlockSpec((1, t, w), lambda i: (i, 0, 0))
    return pl.pallas_call(
        _ret_kernel,
        grid=(b,),
        in_specs=[blk(tl, j) for j in range(4)] + [blk(tc, j) for j in range(4)]
        + [_const_spec(x.shape) for x in (cos, sin, dsum, scales, gch)],
        out_specs=[tok(tl), tok(tc)],
        out_shape=[jax.ShapeDtypeStruct((b, tl, w), f32), jax.ShapeDtypeStruct((b, tc, w), f32)],
        scratch_shapes=[pltpu.VMEM((w, w), f32), pltpu.VMEM((w, w), f32),
                        pltpu.VMEM((tl, w), f32), pltpu.VMEM((tl, w), f32), pltpu.VMEM((tl, w), f32)],
        compiler_params=_cparams(("parallel",)),
        name="retention",
    )(p_l, p_l, p_l, p_l, p_c, p_c, p_c, p_c, cos, sin, dsum, scales, gch)


def _hyena(p, lp):
    b, t, _ = p.shape
    hb, c = HY_BLOCK, HY_WIDTH
    nb = t // hb
    p3 = _hyena_prep(p, lp['hy_conv_w'], lp['hy_conv_b'][None])
    bias = jnp.broadcast_to(lp['hy_bias'][:, :, None], (HY_ORDER, c, hb))
    z = _hyena_conv(p3, _hyena_filters(t, lp), bias, nb)
    return _hyena_post(z, b)


def _gdn_gate_inputs(p):
    b, t, _ = p.shape
    ab = p[..., OFF_AB:OFF_AB + 4 * GDN_HEADS].reshape(b, t, 4, GDN_HEADS)
    return ab.transpose(0, 3, 2, 1)


def _gdn_gate_params(a_log, dt_bias):
    def rows(x):
        x = jnp.pad(x.T, ((0, 0), (0, SUBLANES - 2)))
        return jnp.broadcast_to(x[:, :, None], (GDN_HEADS, SUBLANES, LANES))
    return rows(a_log), rows(dt_bias)


def _rope_tables(t_lat):
    rows_n = t_lat // GRID_W
    row = jnp.repeat(jnp.arange(rows_n, dtype=f32), GRID_W)
    colp = jnp.tile(jnp.arange(GRID_W, dtype=f32), rows_n)
    nf = RET_DK // 4
    inv = ROPE_BASE ** (-jnp.arange(nf, dtype=f32) / nf)
    ang = jnp.concatenate([row[:, None] * inv, colp[:, None] * inv], axis=-1)
    cos, sin = jnp.cos(ang), jnp.sin(ang)
    return (jnp.tile(jnp.concatenate([cos, cos], axis=-1), (1, RET_HEADS)),
            jnp.tile(jnp.concatenate([-sin, sin], axis=-1), (1, RET_HEADS)))


def _ret_constants(lp):
    c = RET_CHUNK
    lg = jax.nn.log_sigmoid(lp['ret_decay_logit'])
    idx = jnp.arange(c, dtype=f32)
    rel = idx[:, None] - idx[None, :]
    d_f = jnp.exp(jnp.where(rel >= 0, rel * lg[0][:, None, None], -jnp.inf))
    d_b = jnp.exp(jnp.where(rel <= 0, -rel * lg[1][:, None, None], -jnp.inf))
    lane = lambda x: jnp.repeat(x, RET_DK, axis=-1)
    scales = jnp.stack([lane(jnp.exp((idx + 1.0)[:, None] * lg[0])),
                        lane(jnp.exp((c - 1.0 - idx)[:, None] * lg[0])),
                        lane(jnp.exp((c - idx)[:, None] * lg[1])),
                        lane(jnp.exp(idx[:, None] * lg[1]))])
    gch = lane(jnp.exp(c * lg))[:, None, :]
    return d_f + d_b, scales, gch


def _reorder_w_in(w):
    h2 = 2 * GDN_HEADS
    a0 = OFF_Z + GDN_HEADS * GDN_DV
    parts = [w[:, :a0], w[:, a0 + 2 * h2:], w[:, a0:a0 + 2 * h2],
             jnp.zeros((w.shape[0], LANES - 2 * h2), w.dtype)]
    return jnp.concatenate(parts, axis=1)


def kernel(x, c, ctx, c_ctx, mod_w, mod_b, norm1_g, w_in, hy_conv_w, hy_conv_b, hy_f_w1, hy_f_b1, hy_f_freq1, hy_f_w2, hy_f_b2, hy_f_freq2, hy_f_w3, hy_bias, gdn_conv_w, gdn_a_log, gdn_dt_bias, gdn_norm_g, ret_decay_logit, w_out, norm2_g, ffn_w_in, ffn_w_out, final_norm_g):
    bsz, t_lat, d = x.shape
    cos, sin = _rope_tables(t_lat)
    pad_rows = (-(bsz + 1)) % SUBLANES
    cs = jnp.concatenate([c, c_ctx[None], jnp.zeros((pad_rows, d), f32)], axis=0)
    gf = final_norm_g[None]
    for i in range(DEPTH):
        need_ctx = i < DEPTH - 1
        lp = {'hy_conv_w': hy_conv_w[i], 'hy_conv_b': hy_conv_b[i], 'hy_f_w1': hy_f_w1[i],
              'hy_f_b1': hy_f_b1[i], 'hy_f_freq1': hy_f_freq1[i], 'hy_f_w2': hy_f_w2[i],
              'hy_f_b2': hy_f_b2[i], 'hy_f_freq2': hy_f_freq2[i], 'hy_f_w3': hy_f_w3[i],
              'hy_bias': hy_bias[i], 'ret_decay_logit': ret_decay_logit[i]}
        mod = _mod_vectors(cs, mod_w[i].astype(bf16), mod_b[i][None])
        mod_lat = mod[:bsz].reshape(bsz, N_MOD, d)
        mod_ctx = jnp.broadcast_to(mod[bsz].reshape(1, N_MOD, d), (bsz, N_MOD, d))
        w_in16 = _reorder_w_in(w_in[i]).astype(bf16)
        g1 = norm1_g[i][None]
        p_l = _in_proj(x, g1, mod_lat, w_in16, 512)
        p_c = _in_proj(ctx, g1, mod_ctx, w_in16, 256)

        hy_l = _hyena(p_l, lp)
        pa, pd = _gdn_gate_params(gdn_a_log[i], gdn_dt_bias[i])
        gd_l, gd_c = _gdn(p_l, _gdn_gate_inputs(p_l), p_c, _gdn_gate_inputs(p_c), gdn_conv_w[i], pa, pd,
                          gdn_norm_g[i][None])

        dsum, scales, gch = _ret_constants(lp)
        rt_l, rt_c = _retention(p_l, p_c, cos, sin, dsum, scales, gch)

        wo16, w116, w216 = w_out[i].astype(bf16), ffn_w_in[i].astype(bf16), ffn_w_out[i].astype(bf16)
        g2 = norm2_g[i][None]
        x = _out_ffn(x, hy_l, gd_l, rt_l, mod_lat, wo16, g2, w116, w216, gf, 512, final=not need_ctx)
        if need_ctx:
            hy_c = _hyena(p_c, lp)
            ctx = _out_ffn(ctx, hy_c, gd_c, rt_c, mod_ctx, wo16, g2, w116, w216, gf, 256, final=False)
    return x
```

```python
import functools
import math

import jax
import jax.numpy as jnp
from jax import lax
from jax.experimental import pallas as pl
from jax.experimental.pallas import tpu as pltpu

D_MODEL = 1024
DEPTH = 2
GRID_W = 64
EPS = 1e-6
N_MOD = 6

HY_WIDTH = D_MODEL // 4
HY_ORDER = 2
HY_EMB = 33
HY_FAST_DECAY = 0.3
HY_SLOW_DECAY = 1.5
HY_TARGET = 1e-2
GDN_HEADS = D_MODEL // 256
GDN_DK = 128
GDN_DV = 128
RET_HEADS = D_MODEL // 256
RET_DK = 64
RET_DV = 64
ROPE_BASE = 10000.0
FFN_HIDDEN = ((8 * D_MODEL + 3 * 256 - 1) // (3 * 256)) * 256

GDN_CHUNK = 64
GDN_ROWS = 256
GDN_HP = 2
GDN_GROUP = 12
RET_CHUNK = 256
HY_BLOCK = 256
HY_PAIR = 2
LANES = 128
SUBLANES = 8

GDN_QKV = 2 * GDN_HEADS * GDN_DK + GDN_HEADS * GDN_DV
OFF_HY = 0
OFF_QKV = OFF_HY + 3 * HY_WIDTH
OFF_Z = OFF_QKV + GDN_QKV
OFF_RET = OFF_Z + GDN_HEADS * GDN_DV
OFF_AB = OFF_RET + 4 * RET_HEADS * RET_DK
IN_PAD = OFF_AB + LANES

VMEM_LIMIT = 56 * 1024 * 1024
GDN_VMEM_LIMIT = 60 * 1024 * 1024

f32 = jnp.float32
bf16 = jnp.bfloat16


def _cparams(sem):
    return pltpu.CompilerParams(dimension_semantics=sem, vmem_limit_bytes=VMEM_LIMIT)


def _const_spec(shape):
    nd = len(shape)
    return pl.BlockSpec(shape, lambda *_: (0,) * nd, pipeline_mode=pl.Buffered(1))


def _rms(x, g):
    return x * lax.rsqrt(jnp.mean(x * x, axis=-1, keepdims=True) + EPS) * g


def _silu(x):
    return x * jax.nn.sigmoid(x)


def _dot(a, b):
    return jnp.dot(a.astype(bf16), b.astype(bf16), preferred_element_type=f32)


def _dot_nt(a, b):
    return lax.dot_general(a.astype(bf16), b.astype(bf16), (((1,), (1,)), ((), ())),
                           preferred_element_type=f32)


def _dot_tn(a, b):
    return lax.dot_general(a.astype(bf16), b.astype(bf16), (((0,), (0,)), ((), ())),
                           preferred_element_type=f32)


def _mod_kernel(c_ref, w_ref, b_ref, o_ref):
    o_ref[...] = _dot(_silu(c_ref[...]), w_ref[...]) + b_ref[...]


def _mod_vectors(cs, w16, b):
    rows, d = cs.shape
    n = w16.shape[1]
    tn = 1536
    return pl.pallas_call(
        _mod_kernel,
        grid=(n // tn,),
        in_specs=[pl.BlockSpec((rows, d), lambda j: (0, 0)),
                  pl.BlockSpec((d, tn), lambda j: (0, j)),
                  pl.BlockSpec((1, tn), lambda j: (0, j))],
        out_specs=pl.BlockSpec((rows, tn), lambda j: (0, j)),
        out_shape=jax.ShapeDtypeStruct((rows, n), f32),
        compiler_params=_cparams(("parallel",)),
        name="mod_vectors",
    )(cs, w16, b)


def _in_proj_kernel(x_ref, g_ref, mod_ref, w_ref, o_ref):
    x = x_ref[0]
    mod = mod_ref[0]
    h = _rms(x, g_ref[...]) * (1.0 + mod[1:2]) + mod[0:1]
    h16 = h.astype(bf16)
    n = w_ref.shape[1]
    step = 512
    for j in range(0, n, step):
        e = min(j + step, n)
        o_ref[0, :, j:e] = jnp.dot(h16, w_ref[:, j:e], preferred_element_type=f32)


def _in_proj(x, g, mod, w16, tm):
    b, t, d = x.shape
    n = w16.shape[1]
    return pl.pallas_call(
        _in_proj_kernel,
        grid=(b, t // tm),
        in_specs=[pl.BlockSpec((1, tm, d), lambda i, j: (i, j, 0)),
                  _const_spec((1, d)),
                  pl.BlockSpec((1, N_MOD, d), lambda i, j: (i, 0, 0)),
                  _const_spec((d, n))],
        out_specs=pl.BlockSpec((1, tm, n), lambda i, j: (i, j, 0)),
        out_shape=jax.ShapeDtypeStruct((b, t, n), f32),
        compiler_params=_cparams(("parallel", "parallel")),
        name="in_proj",
    )(x, g, mod, w16)


def _out_ffn_kernel(x_ref, hy_ref, gd_ref, rt_ref, mod_ref, wo_ref, g2_ref, w1_ref, w2_ref, gf_ref, o_ref,
                    *, final):
    x = x_ref[0]
    mod = mod_ref[0]
    o0, o1, o2 = HY_WIDTH, HY_WIDTH + GDN_HEADS * GDN_DV, D_MODEL
    y = (jnp.dot(hy_ref[0].astype(bf16), wo_ref[0:o0, :], preferred_element_type=f32)
         + jnp.dot(gd_ref[0].astype(bf16), wo_ref[o0:o1, :], preferred_element_type=f32)
         + jnp.dot(rt_ref[0].astype(bf16), wo_ref[o1:o2, :], preferred_element_type=f32))
    x = x + mod[2:3] * y
    h16 = (_rms(x, g2_ref[...]) * (1.0 + mod[4:5]) + mod[3:4]).astype(bf16)
    step = 256
    acc = jnp.zeros_like(x)
    for j in range(0, FFN_HIDDEN, step):
        gate = jnp.dot(h16, w1_ref[:, j:j + step], preferred_element_type=f32)
        up = jnp.dot(h16, w1_ref[:, FFN_HIDDEN + j:FFN_HIDDEN + j + step], preferred_element_type=f32)
        acc = acc + jnp.dot((_silu(gate) * up).astype(bf16), w2_ref[j:j + step, :], preferred_element_type=f32)
    x = x + mod[5:6] * acc
    if final:
        x = _rms(x, gf_ref[...])
    o_ref[0] = x


def _out_ffn(x, hy, gd, rt, mod, wo16, g2, w116, w216, gf, tm, final):
    b, t, d = x.shape
    tok = lambda w: pl.BlockSpec((1, tm, w), lambda i, j: (i, j, 0))
    return pl.pallas_call(
        functools.partial(_out_ffn_kernel, final=final),
        grid=(b, t // tm),
        in_specs=[tok(d), tok(hy.shape[-1]), tok(gd.shape[-1]), tok(rt.shape[-1]),
                  pl.BlockSpec((1, N_MOD, d), lambda i, j: (i, 0, 0)),
                  _const_spec(wo16.shape), _const_spec((1, d)), _const_spec(w116.shape),
                  _const_spec(w216.shape), _const_spec((1, d))],
        out_specs=tok(d),
        out_shape=jax.ShapeDtypeStruct((b, t, d), f32),
        compiler_params=_cparams(("parallel", "parallel")),
        name="out_ffn",
    )(x, hy, gd, rt, mod, wo16, g2, w116, w216, gf)


def _hyena_kernel(v_ref, x1_ref, x2_ref, k_ref, b_ref, o_ref, acc_ref, *, nb, cb):
    hb = HY_BLOCK
    rb = v_ref.shape[1] // nb

    def conv(u16, kext):
        m = range(len(u16))
        for d in range(-(nb - 1), nb):
            dd = d % (2 * nb)
            t0, t1 = max(0, d), min(nb - 1, nb - 1 + d) + 1
            for i in m:
                win = jnp.broadcast_to(kext[i][:, hb * dd:hb * dd + 2 * hb], (hb, 2 * hb))
                toep = pltpu.roll(win, 0, 1, stride=1, stride_axis=0)[:, hb:].astype(bf16)
                contrib = jnp.dot(u16[i][rb * (t0 - d):rb * (t1 - d)], toep, preferred_element_type=f32)
                if d == -(nb - 1):
                    acc_ref[i] = jnp.zeros(acc_ref.shape[1:], f32)
                acc_ref[i, rb * t0:rb * t1, :] += contrib
        return [acc_ref[i] for i in m]

    def body(g, carry):
        cs = [HY_PAIR * g + i for i in range(HY_PAIR)]
        z = [v_ref[c] for c in cs]
        for n, gate_ref in enumerate((x1_ref, x2_ref)):
            y = conv([x.astype(bf16) for x in z], [k_ref[n, pl.ds(c, 1), :] for c in cs])
            z = [gate_ref[c] * (y[i] + z[i] * b_ref[n, pl.ds(c, 1), :]) for i, c in enumerate(cs)]
        for i, c in enumerate(cs):
            o_ref[c] = z[i]
        return carry

    lax.fori_loop(0, cb // HY_PAIR, body, 0)


def _hyena_conv(p3, kext, bias, nb, cb=SUBLANES):
    c3, r, hb = p3.shape
    c = c3 // 3
    klen = kext.shape[-1]
    sig = lambda s: pl.BlockSpec((cb, r, hb), lambda i, s=s: (s * (c // cb) + i, 0, 0))
    return pl.pallas_call(
        functools.partial(_hyena_kernel, nb=nb, cb=cb),
        grid=(c // cb,),
        in_specs=[sig(0), sig(1), sig(2),
                  pl.BlockSpec((HY_ORDER, cb, klen), lambda i: (0, i, 0)),
                  pl.BlockSpec((HY_ORDER, cb, hb), lambda i: (0, i, 0))],
        out_specs=pl.BlockSpec((cb, r, hb), lambda i: (i, 0, 0)),
        out_shape=jax.ShapeDtypeStruct((c, r, hb), f32),
        scratch_shapes=[pltpu.VMEM((HY_PAIR, r, hb), f32)],
        compiler_params=_cparams(("parallel",)),
        name=f"hyena_conv_nb{nb}",
    )(p3, p3, p3, kext, bias)


def _hyena_prep_kernel(x_ref, before_ref, after_ref, w_ref, b_ref, o_ref):
    hb = HY_BLOCK
    w = w_ref[...]
    row = lax.broadcasted_iota(jnp.int32, (hb, LANES), 0)
    s = pl.program_id(1)
    has_before = s > 0
    has_after = s < pl.num_programs(1) - 1
    for i in range(x_ref.shape[0]):
        x = x_ref[i]
        prev = jnp.where(has_before, before_ref[i, SUBLANES - 1:SUBLANES, :], 0.0)
        nxt = jnp.where(has_after, after_ref[i, 0:1, :], 0.0)
        xp = jnp.where(row == 0, prev, pltpu.roll(x, 1, 0))
        xn = jnp.where(row == hb - 1, nxt, pltpu.roll(x, hb - 1, 0))
        y = xp * w[0:1] + x * w[1:2] + xn * w[2:3] + b_ref[...]
        o_ref[:, i, :] = y.T


def _hyena_prep(p, conv_w, conv_b):
    b, t, _ = p.shape
    hb = HY_BLOCK
    n = 3 * HY_WIDTH
    per = hb // SUBLANES
    last = t // SUBLANES - 1
    assert OFF_HY == 0 and b % SUBLANES == 0
    return pl.pallas_call(
        _hyena_prep_kernel,
        grid=(n // LANES, t // hb),
        in_specs=[pl.BlockSpec((b, hb, LANES), lambda j, s: (0, s, j)),
                  pl.BlockSpec((b, SUBLANES, LANES), lambda j, s: (0, jnp.maximum(s * per - 1, 0), j)),
                  pl.BlockSpec((b, SUBLANES, LANES), lambda j, s: (0, jnp.minimum((s + 1) * per, last), j)),
                  pl.BlockSpec((3, LANES), lambda j, s: (0, j)),
                  pl.BlockSpec((1, LANES), lambda j, s: (0, j))],
        out_specs=pl.BlockSpec((LANES, b, hb), lambda j, s: (j, s, 0)),
        out_shape=jax.ShapeDtypeStruct((n, (t // hb) * b, hb), f32),
        compiler_params=_cparams(("parallel", "parallel")),
        name="hyena_prep",
    )(p, p, p, conv_w, conv_b)


def _hyena_post_kernel(z_ref, o_ref):
    for i in range(z_ref.shape[1]):
        o_ref[i] = z_ref[:, i, :].T


def _hyena_post(z, b):
    c, r, hb = z.shape
    nb = r // b
    return pl.pallas_call(
        _hyena_post_kernel,
        grid=(c // LANES, nb),
        in_specs=[pl.BlockSpec((LANES, b, hb), lambda j, s: (j, s, 0))],
        out_specs=pl.BlockSpec((b, hb, LANES), lambda j, s: (0, s, j)),
        out_shape=jax.ShapeDtypeStruct((b, nb * hb, c), f32),
        compiler_params=_cparams(("parallel", "parallel")),
        name="hyena_post",
    )(z)


def _hyena_filter_kernel(z_ref, w1_ref, b1_ref, f1_ref, w2_ref, b2_ref, f2_ref, w3_ref, win_ref, o_ref):
    hi = lax.Precision.HIGHEST
    hb = HY_BLOCK
    t = z_ref.shape[1]
    for side in range(2):
        h = jnp.sin(f1_ref[...] * (jnp.dot(z_ref[side], w1_ref[...], precision=hi,
                                           preferred_element_type=f32) + b1_ref[...]))
        h = jnp.sin(f2_ref[...] * (jnp.dot(h, w2_ref[...], precision=hi, preferred_element_type=f32)
                                   + b2_ref[...]))
        w3 = w3_ref[:, side * HY_WIDTH:(side + 1) * HY_WIDTH]
        k = lax.dot_general(w3, h, (((0,), (1,)), ((), ())), precision=hi,
                            preferred_element_type=f32) * win_ref[side]
        o_ref[0, :, hb + side * t:hb + (side + 1) * t] = k
        if side == 1:
            o_ref[0, :, 0:hb] = k[:, t - hb:]


def _hyena_filters(t_len, lp):
    pos = jnp.arange(t_len, dtype=f32)
    t = jnp.linspace(0.0, 1.0, t_len, dtype=f32)
    bands = (HY_EMB - 1) // 2
    f = jnp.linspace(1e-4, bands - 1, bands, dtype=f32)
    ang = (2.0 * math.pi / t_len) * pos[:, None] * f[None, :]
    z = jnp.concatenate([t[:, None], jnp.cos(ang), -jnp.sin(ang)], axis=-1)
    max_decay = math.log(HY_TARGET) / HY_FAST_DECAY
    min_decay = math.log(HY_TARGET) / HY_SLOW_DECAY
    deltas = jnp.abs(jnp.linspace(min_decay, max_decay, HY_WIDTH, dtype=f32))
    window = jnp.exp(-t[:, None] * deltas[None, :])
    neg = lambda x: jnp.concatenate([x[:1], x[:0:-1]], axis=0)
    zs = jnp.pad(jnp.stack([z, neg(z)]), ((0, 0), (0, 0), (0, LANES - HY_EMB)))
    win = jnp.stack([window, neg(window).at[0].set(0.0)]).transpose(0, 2, 1)
    w1 = jnp.pad(lp['hy_f_w1'], ((0, LANES - HY_EMB), (0, 0)))
    row = lambda x: x[None]
    args = (zs, w1, row(lp['hy_f_b1']), row(lp['hy_f_freq1']), lp['hy_f_w2'], row(lp['hy_f_b2']),
            row(lp['hy_f_freq2']), lp['hy_f_w3'], win)
    specs = [_const_spec(a.shape) for a in args]
    specs[7] = pl.BlockSpec((lp['hy_f_w3'].shape[0], 2 * HY_WIDTH), lambda n: (0, n))
    klen = 2 * t_len + HY_BLOCK
    return pl.pallas_call(
        _hyena_filter_kernel,
        grid=(HY_ORDER,),
        in_specs=specs,
        out_specs=pl.BlockSpec((1, HY_WIDTH, klen), lambda n: (n, 0, 0)),
        out_shape=jax.ShapeDtypeStruct((HY_ORDER, HY_WIDTH, klen), f32),
        compiler_params=_cparams(("parallel",)),
        name="hyena_filters",
    )(*args)


def _gdn_kernel(ql_ref, kl_ref, vl_ref, zl_ref, abl_ref, qc_ref, kc_ref, vc_ref, zc_ref, abc_ref,
                cwq_ref, cwk_ref, cwv_ref, pa_ref, pd_ref, ng_ref,
                ol_ref, oc_ref,
                q_s, k_s, v_s, cf_s, at_s, n_s, mp_s, egl_s, o_s, st_s):
    c = GDN_CHUNK
    rb = GDN_ROWS
    tc, tl = qc_ref.shape[1], ql_ref.shape[1]
    ncc, nc = tc // c, (tc + tl) // c
    dk = GDN_DK

    def conv_rows(x_ref, w_ref, r0, t, ls):
        x = x_ref[0, pl.ds(r0, rb), ls]
        before = pl.multiple_of(jnp.maximum(r0 - SUBLANES, 0), SUBLANES)
        after = pl.multiple_of(jnp.minimum(r0 + rb, t - SUBLANES), SUBLANES)
        prev = x_ref[0, pl.ds(before, SUBLANES), ls][SUBLANES - 1:SUBLANES]
        nxt = x_ref[0, pl.ds(after, SUBLANES), ls][0:1]
        prev = jnp.where(r0 > 0, prev, 0.0)
        nxt = jnp.where(r0 + rb < t, nxt, 0.0)
        row = lax.broadcasted_iota(jnp.int32, x.shape, 0)
        xp = jnp.where(row == 0, prev, pltpu.roll(x, 1, 0))
        xn = jnp.where(row == rb - 1, nxt, pltpu.roll(x, rb - 1, 0))
        w = w_ref[:, ls]
        return _silu(xp * w[0:1] + x * w[1:2] + xn * w[2:3])

    def l2n(x):
        return x * lax.rsqrt(jnp.sum(x * x, axis=-1, keepdims=True) + EPS)

    def prep(hh, q_ref, k_ref, v_ref, ab_ref, t, base):
        ls = slice(hh * LANES, (hh + 1) * LANES)

        def body(i, carry):
            r0 = pl.multiple_of(i * rb, rb)
            dst = pl.ds(pl.multiple_of(base + i * rb, rb), rb)
            q_s[dst, :] = l2n(conv_rows(q_ref, cwq_ref, r0, t, ls)) * (dk ** -0.5)
            k_s[dst, :] = l2n(conv_rows(k_ref, cwk_ref, r0, t, ls))
            v_s[dst, :] = conv_rows(v_ref, cwv_ref, r0, t, ls)
            return carry
        lax.fori_loop(0, t // rb, body, 0)

        ab = ab_ref[0, hh]
        ab = jnp.concatenate([ab, jnp.zeros((SUBLANES - ab.shape[0], t), f32)], axis=0)
        rowi = lax.broadcasted_iota(jnp.int32, ab.shape, 0)
        pos = lax.broadcasted_iota(jnp.int32, ab.shape, 1) % c
        tile = lambda p: jnp.concatenate([p] * (t // LANES), axis=1)
        g = -jnp.exp(tile(pa_ref[hh])) * jax.nn.softplus(ab + tile(pd_ref[hh]))
        acc_f, acc_b = g, g
        s = 1
        while s < c:
            acc_f = acc_f + jnp.where(pos >= s, pltpu.roll(acc_f, s, 1), 0.0)
            acc_b = acc_b + jnp.where(pos < c - s, pltpu.roll(acc_b, t - s, 1), 0.0)
            s *= 2
        cf_t = jnp.where(rowi == 0, acc_f, jnp.where(rowi == 1, acc_b, jax.nn.sigmoid(ab)))
        pad = jnp.zeros((LANES - SUBLANES, LANES), f32)
        for j in range(t // LANES):
            blk = jnp.concatenate([cf_t[:, j * LANES:(j + 1) * LANES], pad], axis=0)
            cf_s[base + j * LANES:base + (j + 1) * LANES, :] = blk.T

    lane = lax.broadcasted_iota(jnp.int32, (c, 2 * c), 1)
    row = lax.broadcasted_iota(jnp.int32, (c, 2 * c), 0)
    fwd = lane < c
    col = lane % c
    incl = (fwd & (row >= col)) | (~fwd & (row <= col))
    strict = (fwd & (row > col)) | (~fwd & (row < col))

    def blockdiag(a):
        return jnp.concatenate([jnp.where(fwd, a, 0.0), jnp.where(fwd, 0.0, a)], axis=0).astype(bf16)

    def chunk_group(hh, cis):
        n = range(len(cis))
        xf, xb = 2 * hh, 2 * hh + 1
        rows = [pl.ds(pl.multiple_of(ci * c, c), c) for ci in cis]
        cf = [cf_s[r, :] for r in rows]
        k = [k_s[r, :] for r in rows]
        q = [q_s[r, :] for r in rows]
        bc = lambda x, j: jnp.broadcast_to(x[:, j:j + 1], (c, LANES))
        gf_c, gb_c = [bc(x, 0) for x in cf], [bc(x, 1) for x in cf]
        bf_c, bb_c = [bc(x, 2) for x in cf], [bc(x, 3) for x in cf]
        cf_t = [jnp.concatenate([x, x], axis=0).T for x in cf]
        g_r = [jnp.where(fwd, jnp.broadcast_to(x[0:1], (c, 2 * c)), jnp.broadcast_to(x[1:2], (c, 2 * c)))
               for x in cf_t]
        decay = [jnp.where(incl, jnp.exp(jnp.where(incl, jnp.where(fwd, gf_c[g], gb_c[g]) - g_r[g], 0.0)), 0.0)
                 for g in n]
        kq = [_dot_nt(jnp.concatenate([k[g], q[g]], axis=0), jnp.concatenate([k[g], k[g]], axis=0))
              for g in n]
        lm = [jnp.where(strict, kq[g][:c] * jnp.where(fwd, bf_c[g], bb_c[g]) * decay[g], 0.0) for g in n]
        for g in n:
            at_s[g] = (kq[g][c:] * decay[g]).astype(bf16)
        a = [jnp.dot(x.astype(bf16), blockdiag(x), preferred_element_type=f32) for x in lm]
        tp = [-x for x in lm]
        p = 2
        while 2 * p < c:
            ta = [jnp.dot(jnp.concatenate([tp[g], a[g]], axis=0).astype(bf16), blockdiag(a[g]),
                          preferred_element_type=f32) for g in n]
            tp = [tp[g] + a[g] + ta[g][:c] for g in n]
            a = [x[c:] for x in ta]
            p *= 2
        tp = [tp[g] + a[g] + jnp.dot(tp[g].astype(bf16), blockdiag(a[g]), preferred_element_type=f32)
              for g in n]
        eg_f, eg_b = [jnp.exp(x) for x in gf_c], [jnp.exp(x) for x in gb_c]
        uw = []
        for g in n:
            v = v_s[rows[g], :]
            rhs = jnp.concatenate([jnp.concatenate([v * bf_c[g], k[g] * (bf_c[g] * eg_f[g])], axis=1),
                                   jnp.concatenate([v * bb_c[g], k[g] * (bb_c[g] * eg_b[g])], axis=1)], axis=0)
            uw.append(rhs + jnp.dot(blockdiag(tp[g]), rhs.astype(bf16), preferred_element_type=f32))
        gl = [(gf_c[g][c - 1:c], gb_c[g][0:1]) for g in n]
        res = []
        for g in n:
            attn = at_s[g]
            for d, (half, g_c) in enumerate(((slice(0, c), gf_c[g]), (slice(c, 2 * c), gb_c[g]))):
                kgt = (k[g] * jnp.exp(gl[g][d] - g_c)).T.astype(bf16)
                res.append(jnp.dot(jnp.concatenate([kgt, attn[:, half]], axis=0), uw[g][half].astype(bf16),
                                   preferred_element_type=f32))
        for g in n:
            ci = cis[g]
            for d, (x, eg) in enumerate(((xf, eg_f[g]), (xb, eg_b[g]))):
                r = res[2 * g + d]
                n_s[x, ci] = r[:dk, :dk]
                mp_s[x, ci] = jnp.concatenate([r[:dk, dk:], q[g] * eg - r[dk:, dk:]], axis=0).astype(bf16)
            egl_s[hh, ci] = jnp.concatenate([jnp.exp(gl[g][0]), jnp.exp(gl[g][1]),
                                             jnp.zeros((SUBLANES - 2, LANES), f32)], axis=0)
            o_s[hh, rows[g], :] = res[2 * g][dk:, :dk] + res[2 * g + 1][dk:, :dk]

    for hh in range(GDN_HP):
        prep(hh, qc_ref, kc_ref, vc_ref, abc_ref, tc, 0)
        prep(hh, ql_ref, kl_ref, vl_ref, abl_ref, tl, tc)

        def chunk_body(i, carry, hh=hh):
            chunk_group(hh, [GDN_GROUP * i + g for g in range(GDN_GROUP)])
            return carry

        lax.fori_loop(0, nc // GDN_GROUP, chunk_body, 0)

    st_s[...] = jnp.zeros_like(st_s)
    chains = [(hh, d) for hh in range(GDN_HP) for d in range(2)]

    def scan_body(i, carry):
        ci = (i, jnp.where(i < ncc, ncc - 1 - i, nc + ncc - 1 - i))
        rows = [pl.ds(pl.multiple_of(x * c, c), c) for x in ci]
        n = range(len(chains))
        s = [st_s[x] for x in n]
        res = [jnp.dot(mp_s[x, ci[chains[x][1]]], s[x].astype(bf16), preferred_element_type=f32) for x in n]
        for x in n:
            hh, d = chains[x]
            st_s[x] = s[x] * egl_s[hh, ci[d]][d:d + 1] + n_s[x, ci[d]] - res[x][:dk]
            o_s[hh, rows[d], :] += res[x][dk:]
        return carry

    lax.fori_loop(0, nc, scan_body, 0)

    def finish(hh, z_ref, o_ref, t, base):
        ls = slice(hh * LANES, (hh + 1) * LANES)

        def body(i, carry):
            r0 = pl.ds(pl.multiple_of(i * rb, rb), rb)
            o = o_s[hh, pl.ds(pl.multiple_of(base + i * rb, rb), rb), :]
            y = o * lax.rsqrt(jnp.mean(o * o, axis=-1, keepdims=True) + EPS) * ng_ref[...]
            o_ref[0, r0, ls] = y * _silu(z_ref[0, r0, ls])
            return carry
        lax.fori_loop(0, t // rb, body, 0)

    for hh in range(GDN_HP):
        finish(hh, zc_ref, oc_ref, tc, 0)
        finish(hh, zl_ref, ol_ref, tl, tc)


def _gdn(p_l, ab_l, p_c, ab_c, conv_w, pa, pd, norm_g):
    b, tl, _ = p_l.shape
    tc = p_c.shape[1]
    h, c, hp = GDN_HEADS, GDN_CHUNK, GDN_HP
    nc = (tl + tc) // c
    assert tl % GDN_ROWS == 0 and tc % GDN_ROWS == 0 and nc % GDN_GROUP == 0
    qb, zb = OFF_QKV // LANES, OFF_Z // LANES
    assert h % hp == 0 and qb % hp == 0 and zb % hp == 0
    w = hp * LANES

    def specs(t):
        heads = lambda off: pl.BlockSpec((1, t, w), lambda i, j, off=off // hp: (i, 0, off + j))
        return [heads(qb), heads(qb + h), heads(qb + 2 * h), heads(zb),
                pl.BlockSpec((1, hp, 4, t), lambda i, j: (i, j, 0, 0))]

    cw = lambda off: pl.BlockSpec((3, w), lambda i, j, off=off // hp: (0, off + j))
    par = pl.BlockSpec((hp, SUBLANES, LANES), lambda i, j: (j, 0, 0))
    out = lambda t: pl.BlockSpec((1, t, w), lambda i, j: (i, 0, j))
    tt = tl + tc
    return pl.pallas_call(
        _gdn_kernel,
        grid=(b, h // hp),
        in_specs=specs(tl) + specs(tc) + [cw(0), cw(h), cw(2 * h), par, par, _const_spec((1, GDN_DV))],
        out_specs=[out(tl), out(tc)],
        out_shape=[jax.ShapeDtypeStruct((b, tl, h * GDN_DV), f32),
                   jax.ShapeDtypeStruct((b, tc, h * GDN_DV), f32)],
        scratch_shapes=[pltpu.VMEM((tt, GDN_DK), f32), pltpu.VMEM((tt, GDN_DK), f32),
                        pltpu.VMEM((tt, GDN_DV), f32), pltpu.VMEM((tt, LANES), f32),
                        pltpu.VMEM((GDN_GROUP, c, 2 * c), bf16),
                        pltpu.VMEM((2 * hp, nc, GDN_DK, GDN_DV), f32),
                        pltpu.VMEM((2 * hp, nc, GDN_DK + c, GDN_DV), bf16),
                        pltpu.VMEM((hp, nc, SUBLANES, LANES), f32), pltpu.VMEM((hp, tt, GDN_DV), f32),
                        pltpu.VMEM((2 * hp, GDN_DK, GDN_DV), f32)],
        compiler_params=pltpu.CompilerParams(dimension_semantics=("parallel", "parallel"),
                                             vmem_limit_bytes=GDN_VMEM_LIMIT),
        name="gated_deltanet",
    )(p_l, p_l, p_l, p_l, ab_l, p_c, p_c, p_c, p_c, ab_c, conv_w, conv_w, conv_w, pa, pd, norm_g)


def _ret_kernel(ql_ref, kl_ref, vl_ref, gl_ref, qc_ref, kc_ref, vc_ref, gc_ref,
                cos_ref, sin_ref, dsum_ref, sc_ref, gch_ref,
                ol_ref, oc_ref, sf_ref, sb_ref, q_s, k_s, o_s):
    c = RET_CHUNK
    w = RET_HEADS * RET_DK
    ri = lax.broadcasted_iota(jnp.int32, (w, w), 0) // RET_DK
    ci = lax.broadcasted_iota(jnp.int32, (w, w), 1) // RET_DK
    blockdiag = ri == ci
    lane_head = lax.broadcasted_iota(jnp.int32, (c, w), 1) // RET_DK

    def intra(q, k, v):
        out = jnp.zeros((c, w), f32)
        for h in range(RET_HEADS):
            m = lane_head == h
            scores = _dot_nt(jnp.where(m, q, 0.0), k) * dsum_ref[h]
            out = out + _dot(scores, jnp.where(m, v, 0.0))
        return out

    def update(s_ref, k, v, d):
        new = jnp.where(blockdiag, _dot_tn(k * sc_ref[2 * d + 1], v), 0.0)
        s_ref[...] = s_ref[...] * gch_ref[d] + new

    first_half = lax.broadcasted_iota(jnp.int32, (c, w), 1) % RET_DK < RET_DK // 2
    avg = jnp.where(blockdiag, 1.0 / RET_DV, 0.0).astype(bf16)

    def rope(x, r):
        swapped = jnp.where(first_half, pltpu.roll(x, w - RET_DK // 2, 1), pltpu.roll(x, RET_DK // 2, 1))
        return x * cos_ref[r, :] + swapped * sin_ref[r, :]

    def head_mean(x):
        hi = x.astype(bf16)
        lo = (x - hi.astype(f32)).astype(bf16)
        return jnp.dot(hi, avg, preferred_element_type=f32) + jnp.dot(lo, avg, preferred_element_type=f32)

    def finish(o, gate):
        d = o - head_mean(o)
        return d * lax.rsqrt(head_mean(d * d) + EPS) * _silu(gate)

    sf_ref[...] = jnp.zeros_like(sf_ref)
    sb_ref[...] = jnp.zeros_like(sb_ref)
    qc, kc, vc = qc_ref[0], kc_ref[0] * (RET_DK ** -0.5), vc_ref[0]
    oc_ref[0] = finish(intra(qc, kc, vc), gc_ref[0])
    update(sf_ref, kc, vc, 0)
    update(sb_ref, kc, vc, 1)
    n = ql_ref.shape[1] // c
    for i in range(n):
        r = slice(i * c, (i + 1) * c)
        q = rope(ql_ref[0, r, :], r)
        k = rope(kl_ref[0, r, :] * (RET_DK ** -0.5), r)
        v = vl_ref[0, r, :]
        q_s[r, :] = q
        k_s[r, :] = k
        o_s[r, :] = intra(q, k, v) + _dot(q * sc_ref[0], sf_ref[...])
        update(sf_ref, k, v, 0)
    for i in range(n - 1, -1, -1):
        r = slice(i * c, (i + 1) * c)
        o = o_s[r, :] + _dot(q_s[r, :] * sc_ref[2], sb_ref[...])
        ol_ref[0, r, :] = finish(o, gl_ref[0, r, :])
        update(sb_ref, k_s[r, :], vl_ref[0, r, :], 1)


def _retention(p_l, p_c, cos, sin, dsum, scales, gch):
    b, tl, _ = p_l.shape
    tc = p_c.shape[1]
    w = RET_HEADS * RET_DK
    assert OFF_RET % w == 0
    blk = lambda t, j: pl.BlockSpec((1, t, w), lambda i, j=j: (i, 0, OFF_RET // w + j))
    tok = lambda t: pl.BlockSpec((1, t, w), lambda i: (i, 0, 0))
    return pl.pallas_call(
        _ret_kernel,
        grid=(b,),
        in_specs=[blk(tl, j) for j in range(4)] + [blk(tc, j) for j in range(4)]
        + [_const_spec(x.shape) for x in (cos, sin, dsum, scales, gch)],
        out_specs=[tok(tl), tok(tc)],
        out_shape=[jax.ShapeDtypeStruct((b, tl, w), f32), jax.ShapeDtypeStruct((b, tc, w), f32)],
        scratch_shapes=[pltpu.VMEM((w, w), f32), pltpu.VMEM((w, w), f32),
                        pltpu.VMEM((tl, w), f32), pltpu.VMEM((tl, w), f32), pltpu.VMEM((tl, w), f32)],
        compiler_params=_cparams(("parallel",)),
        name="retention",
    )(p_l, p_l, p_l, p_l, p_c, p_c, p_c, p_c, cos, sin, dsum, scales, gch)


def _hyena(p, lp):
    b, t, _ = p.shape
    hb, c = HY_BLOCK, HY_WIDTH
    nb = t // hb
    p3 = _hyena_prep(p, lp['hy_conv_w'], lp['hy_conv_b'][None])
    bias = jnp.broadcast_to(lp['hy_bias'][:, :, None], (HY_ORDER, c, hb))
    z = _hyena_conv(p3, _hyena_filters(t, lp), bias, nb)
    return _hyena_post(z, b)


def _gdn_gate_inputs(p):
    b, t, _ = p.shape
    ab = p[..., OFF_AB:OFF_AB + 4 * GDN_HEADS].reshape(b, t, 4, GDN_HEADS)
    return ab.transpose(0, 3, 2, 1)


def _gdn_gate_params(a_log, dt_bias):
    def rows(x):
        x = jnp.pad(x.T, ((0, 0), (0, SUBLANES - 2)))
        return jnp.broadcast_to(x[:, :, None], (GDN_HEADS, SUBLANES, LANES))
    return rows(a_log), rows(dt_bias)


def _rope_tables(t_lat):
    rows_n = t_lat // GRID_W
    row = jnp.repeat(jnp.arange(rows_n, dtype=f32), GRID_W)
    colp = jnp.tile(jnp.arange(GRID_W, dtype=f32), rows_n)
    nf = RET_DK // 4
    inv = ROPE_BASE ** (-jnp.arange(nf, dtype=f32) / nf)
    ang = jnp.concatenate([row[:, None] * inv, colp[:, None] * inv], axis=-1)
    cos, sin = jnp.cos(ang), jnp.sin(ang)
    return (jnp.tile(jnp.concatenate([cos, cos], axis=-1), (1, RET_HEADS)),
            jnp.tile(jnp.concatenate([-sin, sin], axis=-1), (1, RET_HEADS)))


def _ret_constants(lp):
    c = RET_CHUNK
    lg = jax.nn.log_sigmoid(lp['ret_decay_logit'])
    idx = jnp.arange(c, dtype=f32)
    rel = idx[:, None] - idx[None, :]
    d_f = jnp.exp(jnp.where(rel >= 0, rel * lg[0][:, None, None], -jnp.inf))
    d_b = jnp.exp(jnp.where(rel <= 0, -rel * lg[1][:, None, None], -jnp.inf))
    lane = lambda x: jnp.repeat(x, RET_DK, axis=-1)
    scales = jnp.stack([lane(jnp.exp((idx + 1.0)[:, None] * lg[0])),
                        lane(jnp.exp((c - 1.0 - idx)[:, None] * lg[0])),
                        lane(jnp.exp((c - idx)[:, None] * lg[1])),
                        lane(jnp.exp(idx[:, None] * lg[1]))])
    gch = lane(jnp.exp(c * lg))[:, None, :]
    return d_f + d_b, scales, gch


def _reorder_w_in(w):
    h2 = 2 * GDN_HEADS
    a0 = OFF_Z + GDN_HEADS * GDN_DV
    parts = [w[:, :a0], w[:, a0 + 2 * h2:], w[:, a0:a0 + 2 * h2],
             jnp.zeros((w.shape[0], LANES - 2 * h2), w.dtype)]
    return jnp.concatenate(parts, axis=1)


def kernel(x, c, ctx, c_ctx, mod_w, mod_b, norm1_g, w_in, hy_conv_w, hy_conv_b, hy_f_w1, hy_f_b1, hy_f_freq1, hy_f_w2, hy_f_b2, hy_f_freq2, hy_f_w3, hy_bias, gdn_conv_w, gdn_a_log, gdn_dt_bias, gdn_norm_g, ret_decay_logit, w_out, norm2_g, ffn_w_in, ffn_w_out, final_norm_g):
    bsz, t_lat, d = x.shape
    cos, sin = _rope_tables(t_lat)
    pad_rows = (-(bsz + 1)) % SUBLANES
    cs = jnp.concatenate([c, c_ctx[None], jnp.zeros((pad_rows, d), f32)], axis=0)
    gf = final_norm_g[None]
    for i in range(DEPTH):
        need_ctx = i < DEPTH - 1
        lp = {'hy_conv_w': hy_conv_w[i], 'hy_conv_b': hy_conv_b[i], 'hy_f_w1': hy_f_w1[i],
              'hy_f_b1': hy_f_b1[i], 'hy_f_freq1': hy_f_freq1[i], 'hy_f_w2': hy_f_w2[i],
              'hy_f_b2': hy_f_b2[i], 'hy_f_freq2': hy_f_freq2[i], 'hy_f_w3': hy_f_w3[i],
              'hy_bias': hy_bias[i], 'ret_decay_logit': ret_decay_logit[i]}
        mod = _mod_vectors(cs, mod_w[i].astype(bf16), mod_b[i][None])
        mod_lat = mod[:bsz].reshape(bsz, N_MOD, d)
        mod_ctx = jnp.broadcast_to(mod[bsz].reshape(1, N_MOD, d), (bsz, N_MOD, d))
        w_in16 = _reorder_w_in(w_in[i]).astype(bf16)
        g1 = norm1_g[i][None]
        p_l = _in_proj(x, g1, mod_lat, w_in16, 512)
        p_c = _in_proj(ctx, g1, mod_ctx, w_in16, 256)

        hy_l = _hyena(p_l, lp)
        pa, pd = _gdn_gate_params(gdn_a_log[i], gdn_dt_bias[i])
        gd_l, gd_c = _gdn(p_l, _gdn_gate_inputs(p_l), p_c, _gdn_gate_inputs(p_c), gdn_conv_w[i], pa, pd,
                          gdn_norm_g[i][None])

        dsum, scales, gch = _ret_constants(lp)
        rt_l, rt_c = _retention(p_l, p_c, cos, sin, dsum, scales, gch)

        wo16, w116, w216 = w_out[i].astype(bf16), ffn_w_in[i].astype(bf16), ffn_w_out[i].astype(bf16)
        g2 = norm2_g[i][None]
        x = _out_ffn(x, hy_l, gd_l, rt_l, mod_lat, wo16, g2, w116, w216, gf, 512, final=not need_ctx)
        if need_ctx:
            hy_c = _hyena(p_c, lp)
            ctx = _out_ffn(ctx, hy_c, gd_c, rt_c, mod_ctx, wo16, g2, w116, w216, gf, 256, final=False)
    return x
```

```python
import functools
import math

import jax
import jax.numpy as jnp
import numpy as np
from jax import lax
from jax.experimental import pallas as pl
from jax.experimental.pallas import tpu as pltpu

D_MODEL = 1024
DEPTH = 2
GRID_W = 64
EPS = 1e-6
N_MOD = 6

HY_WIDTH = D_MODEL // 4
HY_ORDER = 2
HY_EMB = 33
HY_FAST_DECAY = 0.3
HY_SLOW_DECAY = 1.5
HY_TARGET = 1e-2
GDN_HEADS = D_MODEL // 256
GDN_DK = 128
GDN_DV = 128
RET_HEADS = D_MODEL // 256
RET_DK = 64
RET_DV = 64
ROPE_BASE = 10000.0
FFN_HIDDEN = ((8 * D_MODEL + 3 * 256 - 1) // (3 * 256)) * 256

GDN_CHUNK = 64
GDN_ROWS = 256
GDN_HP = 2
GDN_GROUP = 18
RET_CHUNK = 256
HY_BLOCK = 256
HY_PAIR = 2
LANES = 128
SUBLANES = 8

GDN_QKV = 2 * GDN_HEADS * GDN_DK + GDN_HEADS * GDN_DV
OFF_HY = 0
OFF_QKV = OFF_HY + 3 * HY_WIDTH
OFF_Z = OFF_QKV + GDN_QKV
OFF_RET = OFF_Z + GDN_HEADS * GDN_DV
OFF_AB = OFF_RET + 4 * RET_HEADS * RET_DK
IN_PAD = OFF_AB + LANES

VMEM_LIMIT = 56 * 1024 * 1024
GDN_VMEM_LIMIT = 60 * 1024 * 1024

f32 = jnp.float32
bf16 = jnp.bfloat16


def _cparams(sem):
    return pltpu.CompilerParams(dimension_semantics=sem, vmem_limit_bytes=VMEM_LIMIT)


def _const_spec(shape):
    nd = len(shape)
    return pl.BlockSpec(shape, lambda *_: (0,) * nd, pipeline_mode=pl.Buffered(1))


def _rms(x, g):
    return x * lax.rsqrt(jnp.mean(x * x, axis=-1, keepdims=True) + EPS) * g


def _silu(x):
    return x * jax.nn.sigmoid(x)


def _dot(a, b):
    return jnp.dot(a.astype(bf16), b.astype(bf16), preferred_element_type=f32)


def _dot_nt(a, b):
    return lax.dot_general(a.astype(bf16), b.astype(bf16), (((1,), (1,)), ((), ())),
                           preferred_element_type=f32)


def _dot_tn(a, b):
    return lax.dot_general(a.astype(bf16), b.astype(bf16), (((0,), (0,)), ((), ())),
                           preferred_element_type=f32)


def _mod_kernel(c_ref, w_ref, b_ref, o_ref):
    o_ref[...] = _dot(_silu(c_ref[...]), w_ref[...]) + b_ref[...]


def _mod_vectors(cs, w16, b):
    rows, d = cs.shape
    n = w16.shape[1]
    tn = 1536
    return pl.pallas_call(
        _mod_kernel,
        grid=(n // tn,),
        in_specs=[pl.BlockSpec((rows, d), lambda j: (0, 0)),
                  pl.BlockSpec((d, tn), lambda j: (0, j)),
                  pl.BlockSpec((1, tn), lambda j: (0, j))],
        out_specs=pl.BlockSpec((rows, tn), lambda j: (0, j)),
        out_shape=jax.ShapeDtypeStruct((rows, n), f32),
        compiler_params=_cparams(("parallel",)),
        name="mod_vectors",
    )(cs, w16, b)


def _in_proj_kernel(x_ref, g_ref, mod_ref, w_ref, o_ref):
    x = x_ref[0]
    mod = mod_ref[0]
    h = _rms(x, g_ref[...]) * (1.0 + mod[1:2]) + mod[0:1]
    h16 = h.astype(bf16)
    n = w_ref.shape[1]
    step = 512
    for j in range(0, n, step):
        e = min(j + step, n)
        o_ref[0, :, j:e] = jnp.dot(h16, w_ref[:, j:e], preferred_element_type=f32)


def _in_proj(x, g, mod, w16, tm):
    b, t, d = x.shape
    n = w16.shape[1]
    return pl.pallas_call(
        _in_proj_kernel,
        grid=(b, t // tm),
        in_specs=[pl.BlockSpec((1, tm, d), lambda i, j: (i, j, 0)),
                  _const_spec((1, d)),
                  pl.BlockSpec((1, N_MOD, d), lambda i, j: (i, 0, 0)),
                  _const_spec((d, n))],
        out_specs=pl.BlockSpec((1, tm, n), lambda i, j: (i, j, 0)),
        out_shape=jax.ShapeDtypeStruct((b, t, n), f32),
        compiler_params=_cparams(("parallel", "parallel")),
        name="in_proj",
    )(x, g, mod, w16)


def _out_ffn_kernel(x_ref, hy_ref, gd_ref, rt_ref, mod_ref, wo_ref, g2_ref, w1_ref, w2_ref, gf_ref, o_ref,
                    *, final):
    x = x_ref[0]
    mod = mod_ref[0]
    o0, o1, o2 = HY_WIDTH, HY_WIDTH + GDN_HEADS * GDN_DV, D_MODEL
    y = (jnp.dot(hy_ref[0].astype(bf16), wo_ref[0:o0, :], preferred_element_type=f32)
         + jnp.dot(gd_ref[0].astype(bf16), wo_ref[o0:o1, :], preferred_element_type=f32)
         + jnp.dot(rt_ref[0].astype(bf16), wo_ref[o1:o2, :], preferred_element_type=f32))
    x = x + mod[2:3] * y
    h16 = (_rms(x, g2_ref[...]) * (1.0 + mod[4:5]) + mod[3:4]).astype(bf16)
    step = 256
    acc = jnp.zeros_like(x)
    for j in range(0, FFN_HIDDEN, step):
        gate = jnp.dot(h16, w1_ref[:, j:j + step], preferred_element_type=f32)
        up = jnp.dot(h16, w1_ref[:, FFN_HIDDEN + j:FFN_HIDDEN + j + step], preferred_element_type=f32)
        acc = acc + jnp.dot((_silu(gate) * up).astype(bf16), w2_ref[j:j + step, :], preferred_element_type=f32)
    x = x + mod[5:6] * acc
    if final:
        x = _rms(x, gf_ref[...])
    o_ref[0] = x


def _out_ffn(x, hy, gd, rt, mod, wo16, g2, w116, w216, gf, tm, final):
    b, t, d = x.shape
    tok = lambda w: pl.BlockSpec((1, tm, w), lambda i, j: (i, j, 0))
    return pl.pallas_call(
        functools.partial(_out_ffn_kernel, final=final),
        grid=(b, t // tm),
        in_specs=[tok(d), tok(hy.shape[-1]), tok(gd.shape[-1]), tok(rt.shape[-1]),
                  pl.BlockSpec((1, N_MOD, d), lambda i, j: (i, 0, 0)),
                  _const_spec(wo16.shape), _const_spec((1, d)), _const_spec(w116.shape),
                  _const_spec(w216.shape), _const_spec((1, d))],
        out_specs=tok(d),
        out_shape=jax.ShapeDtypeStruct((b, t, d), f32),
        compiler_params=_cparams(("parallel", "parallel")),
        name="out_ffn",
    )(x, hy, gd, rt, mod, wo16, g2, w116, w216, gf)


def _hyena_kernel(v_ref, x1_ref, x2_ref, k_ref, b_ref, o_ref, acc_ref, *, nb, cb):
    hb = HY_BLOCK
    rb = v_ref.shape[1] // nb

    def conv(u16, kext):
        m = range(len(u16))
        for d in range(-(nb - 1), nb):
            dd = d % (2 * nb)
            t0, t1 = max(0, d), min(nb - 1, nb - 1 + d) + 1
            for i in m:
                win = jnp.broadcast_to(kext[i][:, hb * dd:hb * dd + 2 * hb], (hb, 2 * hb))
                toep = pltpu.roll(win, 0, 1, stride=1, stride_axis=0)[:, hb:].astype(bf16)
                contrib = jnp.dot(u16[i][rb * (t0 - d):rb * (t1 - d)], toep, preferred_element_type=f32)
                if d == -(nb - 1):
                    acc_ref[i] = jnp.zeros(acc_ref.shape[1:], f32)
                acc_ref[i, rb * t0:rb * t1, :] += contrib
        return [acc_ref[i] for i in m]

    def body(g, carry):
        cs = [HY_PAIR * g + i for i in range(HY_PAIR)]
        z = [v_ref[c] for c in cs]
        for n, gate_ref in enumerate((x1_ref, x2_ref)):
            y = conv([x.astype(bf16) for x in z], [k_ref[n, pl.ds(c, 1), :] for c in cs])
            z = [gate_ref[c] * (y[i] + z[i] * b_ref[n, pl.ds(c, 1), :]) for i, c in enumerate(cs)]
        for i, c in enumerate(cs):
            o_ref[c] = z[i]
        return carry

    lax.fori_loop(0, cb // HY_PAIR, body, 0)


def _hyena_conv(p3, kext, bias, nb, cb=SUBLANES):
    c3, r, hb = p3.shape
    c = c3 // 3
    klen = kext.shape[-1]
    sig = lambda s: pl.BlockSpec((cb, r, hb), lambda i, s=s: (s * (c // cb) + i, 0, 0))
    return pl.pallas_call(
        functools.partial(_hyena_kernel, nb=nb, cb=cb),
        grid=(c // cb,),
        in_specs=[sig(0), sig(1), sig(2),
                  pl.BlockSpec((HY_ORDER, cb, klen), lambda i: (0, i, 0)),
                  pl.BlockSpec((HY_ORDER, cb, hb), lambda i: (0, i, 0))],
        out_specs=pl.BlockSpec((cb, r, hb), lambda i: (i, 0, 0)),
        out_shape=jax.ShapeDtypeStruct((c, r, hb), f32),
        scratch_shapes=[pltpu.VMEM((HY_PAIR, r, hb), f32)],
        compiler_params=_cparams(("parallel",)),
        name=f"hyena_conv_nb{nb}",
    )(p3, p3, p3, kext, bias)


def _hyena_prep_kernel(x_ref, before_ref, after_ref, w_ref, b_ref, o_ref):
    hb = HY_BLOCK
    w = w_ref[...]
    row = lax.broadcasted_iota(jnp.int32, (hb, LANES), 0)
    s = pl.program_id(1)
    has_before = s > 0
    has_after = s < pl.num_programs(1) - 1
    for i in range(x_ref.shape[0]):
        x = x_ref[i]
        prev = jnp.where(has_before, before_ref[i, SUBLANES - 1:SUBLANES, :], 0.0)
        nxt = jnp.where(has_after, after_ref[i, 0:1, :], 0.0)
        xp = jnp.where(row == 0, prev, pltpu.roll(x, 1, 0))
        xn = jnp.where(row == hb - 1, nxt, pltpu.roll(x, hb - 1, 0))
        y = xp * w[0:1] + x * w[1:2] + xn * w[2:3] + b_ref[...]
        o_ref[:, i, :] = y.T


def _hyena_prep(p, conv_w, conv_b):
    b, t, _ = p.shape
    hb = HY_BLOCK
    n = 3 * HY_WIDTH
    per = hb // SUBLANES
    last = t // SUBLANES - 1
    assert OFF_HY == 0 and b % SUBLANES == 0
    return pl.pallas_call(
        _hyena_prep_kernel,
        grid=(n // LANES, t // hb),
        in_specs=[pl.BlockSpec((b, hb, LANES), lambda j, s: (0, s, j)),
                  pl.BlockSpec((b, SUBLANES, LANES), lambda j, s: (0, jnp.maximum(s * per - 1, 0), j)),
                  pl.BlockSpec((b, SUBLANES, LANES), lambda j, s: (0, jnp.minimum((s + 1) * per, last), j)),
                  pl.BlockSpec((3, LANES), lambda j, s: (0, j)),
                  pl.BlockSpec((1, LANES), lambda j, s: (0, j))],
        out_specs=pl.BlockSpec((LANES, b, hb), lambda j, s: (j, s, 0)),
        out_shape=jax.ShapeDtypeStruct((n, (t // hb) * b, hb), f32),
        compiler_params=_cparams(("parallel", "parallel")),
        name="hyena_prep",
    )(p, p, p, conv_w, conv_b)


def _hyena_post_kernel(z_ref, o_ref):
    for i in range(z_ref.shape[1]):
        o_ref[i] = z_ref[:, i, :].T


def _hyena_post(z, b):
    c, r, hb = z.shape
    nb = r // b
    return pl.pallas_call(
        _hyena_post_kernel,
        grid=(c // LANES, nb),
        in_specs=[pl.BlockSpec((LANES, b, hb), lambda j, s: (j, s, 0))],
        out_specs=pl.BlockSpec((b, hb, LANES), lambda j, s: (0, s, j)),
        out_shape=jax.ShapeDtypeStruct((b, nb * hb, c), f32),
        compiler_params=_cparams(("parallel", "parallel")),
        name="hyena_post",
    )(z)


def _hyena_filter_kernel(z_ref, w1_ref, b1_ref, f1_ref, w2_ref, b2_ref, f2_ref, w3_ref, win_ref, o_ref):
    hi = lax.Precision.HIGHEST
    hb = HY_BLOCK
    t = z_ref.shape[1]
    for side in range(2):
        h = jnp.sin(f1_ref[...] * (jnp.dot(z_ref[side], w1_ref[...], precision=hi,
                                           preferred_element_type=f32) + b1_ref[...]))
        h = jnp.sin(f2_ref[...] * (jnp.dot(h, w2_ref[...], precision=hi, preferred_element_type=f32)
                                   + b2_ref[...]))
        for n in range(HY_ORDER):
            w3 = w3_ref[:, (2 * n + side) * HY_WIDTH:(2 * n + side + 1) * HY_WIDTH]
            k = lax.dot_general(w3, h, (((0,), (1,)), ((), ())), precision=hi,
                                preferred_element_type=f32) * win_ref[side]
            o_ref[n, :, hb + side * t:hb + (side + 1) * t] = k
            if side == 1:
                o_ref[n, :, 0:hb] = k[:, t - hb:]


def _hyena_tables(t_len):
    pos = np.arange(t_len, dtype=np.float64)
    t = np.linspace(0.0, 1.0, t_len)
    bands = (HY_EMB - 1) // 2
    f = np.linspace(1e-4, bands - 1, bands)
    ang = (2.0 * math.pi / t_len) * pos[:, None] * f[None, :]
    z = np.concatenate([t[:, None], np.cos(ang), -np.sin(ang)], axis=-1)
    max_decay = math.log(HY_TARGET) / HY_FAST_DECAY
    min_decay = math.log(HY_TARGET) / HY_SLOW_DECAY
    deltas = np.abs(np.linspace(min_decay, max_decay, HY_WIDTH))
    window = np.exp(-t[:, None] * deltas[None, :])
    neg = lambda x: np.concatenate([x[:1], x[:0:-1]], axis=0)
    zs = np.pad(np.stack([z, neg(z)]), ((0, 0), (0, 0), (0, LANES - HY_EMB)))
    wneg = neg(window)
    wneg[0] = 0.0
    win = np.stack([window, wneg]).transpose(0, 2, 1)
    return jnp.asarray(zs, f32), jnp.asarray(win, f32)


def _hyena_filters(t_len, lp):
    zs, win = _hyena_tables(t_len)
    w1 = jnp.pad(lp['hy_f_w1'], ((0, LANES - HY_EMB), (0, 0)))
    row = lambda x: x[None]
    args = (zs, w1, row(lp['hy_f_b1']), row(lp['hy_f_freq1']), lp['hy_f_w2'], row(lp['hy_f_b2']),
            row(lp['hy_f_freq2']), lp['hy_f_w3'], win)
    klen = 2 * t_len + HY_BLOCK
    return pl.pallas_call(
        _hyena_filter_kernel,
        grid=(1,),
        in_specs=[_const_spec(a.shape) for a in args],
        out_specs=pl.BlockSpec((HY_ORDER, HY_WIDTH, klen), lambda i: (0, 0, 0)),
        out_shape=jax.ShapeDtypeStruct((HY_ORDER, HY_WIDTH, klen), f32),
        compiler_params=_cparams(("arbitrary",)),
        name="hyena_filters",
    )(*args)


def _gdn_kernel(ql_ref, kl_ref, vl_ref, zl_ref, abl_ref, qc_ref, kc_ref, vc_ref, zc_ref, abc_ref,
                cwq_ref, cwk_ref, cwv_ref, pa_ref, pd_ref, ng_ref,
                ol_ref, oc_ref,
                q_s, k_s, v_s, cf_s, at_s, n_s, mp_s, egl_s, o_s, st_s):
    c = GDN_CHUNK
    rb = GDN_ROWS
    tc, tl = qc_ref.shape[1], ql_ref.shape[1]
    ncc, nc = tc // c, (tc + tl) // c
    dk = GDN_DK

    def conv_rows(x_ref, w_ref, r0, t, ls):
        x = x_ref[0, pl.ds(r0, rb), ls]
        before = pl.multiple_of(jnp.maximum(r0 - SUBLANES, 0), SUBLANES)
        after = pl.multiple_of(jnp.minimum(r0 + rb, t - SUBLANES), SUBLANES)
        prev = x_ref[0, pl.ds(before, SUBLANES), ls][SUBLANES - 1:SUBLANES]
        nxt = x_ref[0, pl.ds(after, SUBLANES), ls][0:1]
        prev = jnp.where(r0 > 0, prev, 0.0)
        nxt = jnp.where(r0 + rb < t, nxt, 0.0)
        row = lax.broadcasted_iota(jnp.int32, x.shape, 0)
        xp = jnp.where(row == 0, prev, pltpu.roll(x, 1, 0))
        xn = jnp.where(row == rb - 1, nxt, pltpu.roll(x, rb - 1, 0))
        w = w_ref[:, ls]
        return _silu(xp * w[0:1] + x * w[1:2] + xn * w[2:3])

    def l2n(x):
        return x * lax.rsqrt(jnp.sum(x * x, axis=-1, keepdims=True) + EPS)

    def prep(hh, q_ref, k_ref, v_ref, ab_ref, t, base):
        ls = slice(hh * LANES, (hh + 1) * LANES)

        def body(i, carry):
            r0 = pl.multiple_of(i * rb, rb)
            dst = pl.ds(pl.multiple_of(base + i * rb, rb), rb)
            q_s[dst, :] = l2n(conv_rows(q_ref, cwq_ref, r0, t, ls)) * (dk ** -0.5)
            k_s[dst, :] = l2n(conv_rows(k_ref, cwk_ref, r0, t, ls))
            v_s[dst, :] = conv_rows(v_ref, cwv_ref, r0, t, ls)
            return carry
        lax.fori_loop(0, t // rb, body, 0)

        ab = ab_ref[0, hh]
        ab = jnp.concatenate([ab, jnp.zeros((SUBLANES - ab.shape[0], t), f32)], axis=0)
        rowi = lax.broadcasted_iota(jnp.int32, ab.shape, 0)
        pos = lax.broadcasted_iota(jnp.int32, ab.shape, 1) % c
        tile = lambda p: jnp.concatenate([p] * (t // LANES), axis=1)
        g = -jnp.exp(tile(pa_ref[hh])) * jax.nn.softplus(ab + tile(pd_ref[hh]))
        acc_f, acc_b = g, g
        s = 1
        while s < c:
            acc_f = acc_f + jnp.where(pos >= s, pltpu.roll(acc_f, s, 1), 0.0)
            acc_b = acc_b + jnp.where(pos < c - s, pltpu.roll(acc_b, t - s, 1), 0.0)
            s *= 2
        cf_t = jnp.where(rowi == 0, acc_f, jnp.where(rowi == 1, acc_b, jax.nn.sigmoid(ab)))
        pad = jnp.zeros((LANES - SUBLANES, LANES), f32)
        for j in range(t // LANES):
            blk = jnp.concatenate([cf_t[:, j * LANES:(j + 1) * LANES], pad], axis=0)
            cf_s[base + j * LANES:base + (j + 1) * LANES, :] = blk.T

    lane = lax.broadcasted_iota(jnp.int32, (c, 2 * c), 1)
    row = lax.broadcasted_iota(jnp.int32, (c, 2 * c), 0)
    fwd = lane < c
    col = lane % c
    incl = (fwd & (row >= col)) | (~fwd & (row <= col))
    strict = (fwd & (row > col)) | (~fwd & (row < col))

    def blockdiag(a):
        return jnp.concatenate([jnp.where(fwd, a, 0.0), jnp.where(fwd, 0.0, a)], axis=0).astype(bf16)

    def chunk_group(hh, cis):
        n = range(len(cis))
        xf, xb = 2 * hh, 2 * hh + 1
        rows = [pl.ds(pl.multiple_of(ci * c, c), c) for ci in cis]
        cf = [cf_s[r, :] for r in rows]
        k = [k_s[r, :] for r in rows]
        q = [q_s[r, :] for r in rows]
        bc = lambda x, j: jnp.broadcast_to(x[:, j:j + 1], (c, LANES))
        gf_c, gb_c = [bc(x, 0) for x in cf], [bc(x, 1) for x in cf]
        bf_c, bb_c = [bc(x, 2) for x in cf], [bc(x, 3) for x in cf]
        cf_t = [jnp.concatenate([x, x], axis=0).T for x in cf]
        g_r = [jnp.where(fwd, jnp.broadcast_to(x[0:1], (c, 2 * c)), jnp.broadcast_to(x[1:2], (c, 2 * c)))
               for x in cf_t]
        decay = [jnp.where(incl, jnp.exp(jnp.where(incl, jnp.where(fwd, gf_c[g], gb_c[g]) - g_r[g], 0.0)), 0.0)
                 for g in n]
        kq = [_dot_nt(jnp.concatenate([k[g], q[g]], axis=0), jnp.concatenate([k[g], k[g]], axis=0))
              for g in n]
        lm = [jnp.where(strict, kq[g][:c] * jnp.where(fwd, bf_c[g], bb_c[g]) * decay[g], 0.0) for g in n]
        for g in n:
            at_s[g] = (kq[g][c:] * decay[g]).astype(bf16)
        a = [jnp.dot(x.astype(bf16), blockdiag(x), preferred_element_type=f32) for x in lm]
        tp = [-x for x in lm]
        p = 2
        while 2 * p < c:
            ta = [jnp.dot(jnp.concatenate([tp[g], a[g]], axis=0).astype(bf16), blockdiag(a[g]),
                          preferred_element_type=f32) for g in n]
            tp = [tp[g] + a[g] + ta[g][:c] for g in n]
            a = [x[c:] for x in ta]
            p *= 2
        tp = [tp[g] + a[g] + jnp.dot(tp[g].astype(bf16), blockdiag(a[g]), preferred_element_type=f32)
              for g in n]
        eg_f, eg_b = [jnp.exp(x) for x in gf_c], [jnp.exp(x) for x in gb_c]
        uw = []
        for g in n:
            v = v_s[rows[g], :]
            rhs = jnp.concatenate([jnp.concatenate([v * bf_c[g], k[g] * (bf_c[g] * eg_f[g])], axis=1),
                                   jnp.concatenate([v * bb_c[g], k[g] * (bb_c[g] * eg_b[g])], axis=1)], axis=0)
            uw.append(rhs + jnp.dot(blockdiag(tp[g]), rhs.astype(bf16), preferred_element_type=f32))
        gl = [(gf_c[g][c - 1:c], gb_c[g][0:1]) for g in n]
        res = []
        for g in n:
            attn = at_s[g]
            for d, (half, g_c) in enumerate(((slice(0, c), gf_c[g]), (slice(c, 2 * c), gb_c[g]))):
                kgt = (k[g] * jnp.exp(gl[g][d] - g_c)).T.astype(bf16)
                res.append(jnp.dot(jnp.concatenate([kgt, attn[:, half]], axis=0), uw[g][half].astype(bf16),
                                   preferred_element_type=f32))
        for g in n:
            ci = cis[g]
            for d, (x, eg) in enumerate(((xf, eg_f[g]), (xb, eg_b[g]))):
                r = res[2 * g + d]
                n_s[x, ci] = r[:dk, :dk]
                mp_s[x, ci] = jnp.concatenate([r[:dk, dk:], q[g] * eg - r[dk:, dk:]], axis=0).astype(bf16)
            egl_s[hh, ci] = jnp.concatenate([jnp.exp(gl[g][0]), jnp.exp(gl[g][1]),
                                             jnp.zeros((SUBLANES - 2, LANES), f32)], axis=0)
            o_s[hh, rows[g], :] = res[2 * g][dk:, :dk] + res[2 * g + 1][dk:, :dk]

    for hh in range(GDN_HP):
        prep(hh, qc_ref, kc_ref, vc_ref, abc_ref, tc, 0)
        prep(hh, ql_ref, kl_ref, vl_ref, abl_ref, tl, tc)

        def chunk_body(i, carry, hh=hh):
            chunk_group(hh, [GDN_GROUP * i + g for g in range(GDN_GROUP)])
            return carry

        lax.fori_loop(0, nc // GDN_GROUP, chunk_body, 0)

    st_s[...] = jnp.zeros_like(st_s)
    chains = [(hh, d) for hh in range(GDN_HP) for d in range(2)]

    def scan_body(i, carry):
        ci = (i, jnp.where(i < ncc, ncc - 1 - i, nc + ncc - 1 - i))
        rows = [pl.ds(pl.multiple_of(x * c, c), c) for x in ci]
        n = range(len(chains))
        s = [st_s[x] for x in n]
        res = [jnp.dot(mp_s[x, ci[chains[x][1]]], s[x].astype(bf16), preferred_element_type=f32) for x in n]
        for x in n:
            hh, d = chains[x]
            st_s[x] = s[x] * egl_s[hh, ci[d]][d:d + 1] + n_s[x, ci[d]] - res[x][:dk]
            o_s[hh, rows[d], :] += res[x][dk:]
        return carry

    lax.fori_loop(0, nc, scan_body, 0)

    def finish(hh, z_ref, o_ref, t, base):
        ls = slice(hh * LANES, (hh + 1) * LANES)

        def body(i, carry):
            r0 = pl.ds(pl.multiple_of(i * rb, rb), rb)
            o = o_s[hh, pl.ds(pl.multiple_of(base + i * rb, rb), rb), :]
            y = o * lax.rsqrt(jnp.mean(o * o, axis=-1, keepdims=True) + EPS) * ng_ref[...]
            o_ref[0, r0, ls] = y * _silu(z_ref[0, r0, ls])
            return carry
        lax.fori_loop(0, t // rb, body, 0)

    for hh in range(GDN_HP):
        finish(hh, zc_ref, oc_ref, tc, 0)
        finish(hh, zl_ref, ol_ref, tl, tc)


def _gdn(p_l, ab_l, p_c, ab_c, conv_w, pa, pd, norm_g):
    b, tl, _ = p_l.shape
    tc = p_c.shape[1]
    h, c, hp = GDN_HEADS, GDN_CHUNK, GDN_HP
    nc = (tl + tc) // c
    assert tl % GDN_ROWS == 0 and tc % GDN_ROWS == 0 and nc % GDN_GROUP == 0
    qb, zb = OFF_QKV // LANES, OFF_Z // LANES
    assert h % hp == 0 and qb % hp == 0 and zb % hp == 0
    w = hp * LANES

    def specs(t):
        heads = lambda off: pl.BlockSpec((1, t, w), lambda i, j, off=off // hp: (i, 0, off + j))
        return [heads(qb), heads(qb + h), heads(qb + 2 * h), heads(zb),
                pl.BlockSpec((1, hp, 4, t), lambda i, j: (i, j, 0, 0))]

    cw = lambda off: pl.BlockSpec((3, w), lambda i, j, off=off // hp: (0, off + j))
    par = pl.BlockSpec((hp, SUBLANES, LANES), lambda i, j: (j, 0, 0))
    out = lambda t: pl.BlockSpec((1, t, w), lambda i, j: (i, 0, j))
    tt = tl + tc
    return pl.pallas_call(
        _gdn_kernel,
        grid=(b, h // hp),
        in_specs=specs(tl) + specs(tc) + [cw(0), cw(h), cw(2 * h), par, par, _const_spec((1, GDN_DV))],
        out_specs=[out(tl), out(tc)],
        out_shape=[jax.ShapeDtypeStruct((b, tl, h * GDN_DV), f32),
                   jax.ShapeDtypeStruct((b, tc, h * GDN_DV), f32)],
        scratch_shapes=[pltpu.VMEM((tt, GDN_DK), f32), pltpu.VMEM((tt, GDN_DK), f32),
                        pltpu.VMEM((tt, GDN_DV), f32), pltpu.VMEM((tt, LANES), f32),
                        pltpu.VMEM((GDN_GROUP, c, 2 * c), bf16),
                        pltpu.VMEM((2 * hp, nc, GDN_DK, GDN_DV), f32),
                        pltpu.VMEM((2 * hp, nc, GDN_DK + c, GDN_DV), bf16),
                        pltpu.VMEM((hp, nc, SUBLANES, LANES), f32), pltpu.VMEM((hp, tt, GDN_DV), f32),
                        pltpu.VMEM((2 * hp, GDN_DK, GDN_DV), f32)],
        compiler_params=pltpu.CompilerParams(dimension_semantics=("parallel", "parallel"),
                                             vmem_limit_bytes=GDN_VMEM_LIMIT),
        name="gated_deltanet",
    )(p_l, p_l, p_l, p_l, ab_l, p_c, p_c, p_c, p_c, ab_c, conv_w, conv_w, conv_w, pa, pd, norm_g)


def _ret_kernel(ql_ref, kl_ref, vl_ref, gl_ref, qc_ref, kc_ref, vc_ref, gc_ref,
                cos_ref, sin_ref, dsum_ref, sc_ref, gch_ref,
                ol_ref, oc_ref, sf_ref, sb_ref, q_s, k_s, o_s):
    c = RET_CHUNK
    w = RET_HEADS * RET_DK
    ri = lax.broadcasted_iota(jnp.int32, (w, w), 0) // RET_DK
    ci = lax.broadcasted_iota(jnp.int32, (w, w), 1) // RET_DK
    blockdiag = ri == ci
    lane_head = lax.broadcasted_iota(jnp.int32, (c, w), 1) // RET_DK

    def intra(q, k, v):
        out = jnp.zeros((c, w), f32)
        for h in range(RET_HEADS):
            m = lane_head == h
            scores = _dot_nt(jnp.where(m, q, 0.0), k) * dsum_ref[h]
            out = out + _dot(scores, jnp.where(m, v, 0.0))
        return out

    def update(s_ref, k, v, d):
        new = jnp.where(blockdiag, _dot_tn(k * sc_ref[2 * d + 1], v), 0.0)
        s_ref[...] = s_ref[...] * gch_ref[d] + new

    first_half = lax.broadcasted_iota(jnp.int32, (c, w), 1) % RET_DK < RET_DK // 2
    avg = jnp.where(blockdiag, 1.0 / RET_DV, 0.0).astype(bf16)

    def rope(x, r):
        swapped = jnp.where(first_half, pltpu.roll(x, w - RET_DK // 2, 1), pltpu.roll(x, RET_DK // 2, 1))
        return x * cos_ref[r, :] + swapped * sin_ref[r, :]

    def head_mean(x):
        hi = x.astype(bf16)
        lo = (x - hi.astype(f32)).astype(bf16)
        return jnp.dot(hi, avg, preferred_element_type=f32) + jnp.dot(lo, avg, preferred_element_type=f32)

    def finish(o, gate):
        d = o - head_mean(o)
        return d * lax.rsqrt(head_mean(d * d) + EPS) * _silu(gate)

    sf_ref[...] = jnp.zeros_like(sf_ref)
    sb_ref[...] = jnp.zeros_like(sb_ref)
    qc, kc, vc = qc_ref[0], kc_ref[0] * (RET_DK ** -0.5), vc_ref[0]
    oc_ref[0] = finish(intra(qc, kc, vc), gc_ref[0])
    update(sf_ref, kc, vc, 0)
    update(sb_ref, kc, vc, 1)
    n = ql_ref.shape[1] // c
    for i in range(n):
        r = slice(i * c, (i + 1) * c)
        q = rope(ql_ref[0, r, :], r)
        k = rope(kl_ref[0, r, :] * (RET_DK ** -0.5), r)
        v = vl_ref[0, r, :]
        q_s[r, :] = q
        k_s[r, :] = k
        o_s[r, :] = intra(q, k, v) + _dot(q * sc_ref[0], sf_ref[...])
        update(sf_ref, k, v, 0)
    for i in range(n - 1, -1, -1):
        r = slice(i * c, (i + 1) * c)
        o = o_s[r, :] + _dot(q_s[r, :] * sc_ref[2], sb_ref[...])
        ol_ref[0, r, :] = finish(o, gl_ref[0, r, :])
        update(sb_ref, k_s[r, :], vl_ref[0, r, :], 1)


def _retention(p_l, p_c, cos, sin, dsum, scales, gch):
    b, tl, _ = p_l.shape
    tc = p_c.shape[1]
    w = RET_HEADS * RET_DK
    assert OFF_RET % w == 0
    blk = lambda t, j: pl.BlockSpec((1, t, w), lambda i, j=j: (i, 0, OFF_RET // w + j))
    tok = lambda t: pl.BlockSpec((1, t, w), lambda i: (i, 0, 0))
    return pl.pallas_call(
        _ret_kernel,
        grid=(b,),
        in_specs=[blk(tl, j) for j in range(4)] + [blk(tc, j) for j in range(4)]
        + [_const_spec(x.shape) for x in (cos, sin, dsum, scales, gch)],
        out_specs=[tok(tl), tok(tc)],
        out_shape=[jax.ShapeDtypeStruct((b, tl, w), f32), jax.ShapeDtypeStruct((b, tc, w), f32)],
        scratch_shapes=[pltpu.VMEM((w, w), f32), pltpu.VMEM((w, w), f32),
                        pltpu.VMEM((tl, w), f32), pltpu.VMEM((tl, w), f32), pltpu.VMEM((tl, w), f32)],
        compiler_params=_cparams(("parallel",)),
        name="retention",
    )(p_l, p_l, p_l, p_l, p_c, p_c, p_c, p_c, cos, sin, dsum, scales, gch)


def _hyena(p, lp):
    b, t, _ = p.shape
    hb, c = HY_BLOCK, HY_WIDTH
    nb = t // hb
    p3 = _hyena_prep(p, lp['hy_conv_w'], lp['hy_conv_b'][None])
    bias = jnp.broadcast_to(lp['hy_bias'][:, :, None], (HY_ORDER, c, hb))
    z = _hyena_conv(p3, _hyena_filters(t, lp), bias, nb)
    return _hyena_post(z, b)


def _gdn_gate_inputs(p):
    b, t, _ = p.shape
    ab = p[..., OFF_AB:OFF_AB + 4 * GDN_HEADS].reshape(b, t, 4, GDN_HEADS)
    return ab.transpose(0, 3, 2, 1)


def _gdn_gate_params(a_log, dt_bias):
    def rows(x):
        x = jnp.pad(x.T, ((0, 0), (0, SUBLANES - 2)))
        return jnp.broadcast_to(x[:, :, None], (GDN_HEADS, SUBLANES, LANES))
    return rows(a_log), rows(dt_bias)


def _rope_tables(t_lat):
    rows_n = t_lat // GRID_W
    row = np.repeat(np.arange(rows_n, dtype=np.float64), GRID_W)
    colp = np.tile(np.arange(GRID_W, dtype=np.float64), rows_n)
    nf = RET_DK // 4
    inv = ROPE_BASE ** (-np.arange(nf, dtype=np.float64) / nf)
    ang = np.concatenate([row[:, None] * inv, colp[:, None] * inv], axis=-1)
    cos, sin = np.cos(ang), np.sin(ang)
    return (jnp.asarray(np.tile(np.concatenate([cos, cos], axis=-1), (1, RET_HEADS)), f32),
            jnp.asarray(np.tile(np.concatenate([-sin, sin], axis=-1), (1, RET_HEADS)), f32))


def _ret_constants(lp):
    c = RET_CHUNK
    lg = jax.nn.log_sigmoid(lp['ret_decay_logit'])
    idx = jnp.arange(c, dtype=f32)
    rel = idx[:, None] - idx[None, :]
    d_f = jnp.exp(jnp.where(rel >= 0, rel * lg[0][:, None, None], -jnp.inf))
    d_b = jnp.exp(jnp.where(rel <= 0, -rel * lg[1][:, None, None], -jnp.inf))
    lane = lambda x: jnp.repeat(x, RET_DK, axis=-1)
    scales = jnp.stack([lane(jnp.exp((idx + 1.0)[:, None] * lg[0])),
                        lane(jnp.exp((c - 1.0 - idx)[:, None] * lg[0])),
                        lane(jnp.exp((c - idx)[:, None] * lg[1])),
                        lane(jnp.exp(idx[:, None] * lg[1]))])
    gch = lane(jnp.exp(c * lg))[:, None, :]
    return d_f + d_b, scales, gch


def _reorder_w_in(w):
    h2 = 2 * GDN_HEADS
    a0 = OFF_Z + GDN_HEADS * GDN_DV
    parts = [w[:, :a0], w[:, a0 + 2 * h2:], w[:, a0:a0 + 2 * h2],
             jnp.zeros((w.shape[0], LANES - 2 * h2), w.dtype)]
    return jnp.concatenate(parts, axis=1)


def kernel(x, c, ctx, c_ctx, mod_w, mod_b, norm1_g, w_in, hy_conv_w, hy_conv_b, hy_f_w1, hy_f_b1, hy_f_freq1, hy_f_w2, hy_f_b2, hy_f_freq2, hy_f_w3, hy_bias, gdn_conv_w, gdn_a_log, gdn_dt_bias, gdn_norm_g, ret_decay_logit, w_out, norm2_g, ffn_w_in, ffn_w_out, final_norm_g):
    bsz, t_lat, d = x.shape
    cos, sin = _rope_tables(t_lat)
    pad_rows = (-(bsz + 1)) % SUBLANES
    cs = jnp.concatenate([c, c_ctx[None], jnp.zeros((pad_rows, d), f32)], axis=0)
    gf = final_norm_g[None]
    for i in range(DEPTH):
        need_ctx = i < DEPTH - 1
        lp = {'hy_conv_w': hy_conv_w[i], 'hy_conv_b': hy_conv_b[i], 'hy_f_w1': hy_f_w1[i],
              'hy_f_b1': hy_f_b1[i], 'hy_f_freq1': hy_f_freq1[i], 'hy_f_w2': hy_f_w2[i],
              'hy_f_b2': hy_f_b2[i], 'hy_f_freq2': hy_f_freq2[i], 'hy_f_w3': hy_f_w3[i],
              'hy_bias': hy_bias[i], 'ret_decay_logit': ret_decay_logit[i]}
        mod = _mod_vectors(cs, mod_w[i].astype(bf16), mod_b[i][None])
        mod_lat = mod[:bsz].reshape(bsz, N_MOD, d)
        mod_ctx = jnp.broadcast_to(mod[bsz].reshape(1, N_MOD, d), (bsz, N_MOD, d))
        w_in16 = _reorder_w_in(w_in[i]).astype(bf16)
        g1 = norm1_g[i][None]
        p_l = _in_proj(x, g1, mod_lat, w_in16, 512)
        p_c = _in_proj(ctx, g1, mod_ctx, w_in16, 256)

        hy_l = _hyena(p_l, lp)
        pa, pd = _gdn_gate_params(gdn_a_log[i], gdn_dt_bias[i])
        gd_l, gd_c = _gdn(p_l, _gdn_gate_inputs(p_l), p_c, _gdn_gate_inputs(p_c), gdn_conv_w[i], pa, pd,
                          gdn_norm_g[i][None])

        dsum, scales, gch = _ret_constants(lp)
        rt_l, rt_c = _retention(p_l, p_c, cos, sin, dsum, scales, gch)

        wo16, w116, w216 = w_out[i].astype(bf16), ffn_w_in[i].astype(bf16), ffn_w_out[i].astype(bf16)
        g2 = norm2_g[i][None]
        x = _out_ffn(x, hy_l, gd_l, rt_l, mod_lat, wo16, g2, w116, w216, gf, 512, final=not need_ctx)
        if need_ctx:
            hy_c = _hyena(p_c, lp)
            ctx = _out_ffn(ctx, hy_c, gd_c, rt_c, mod_ctx, wo16, g2, w116, w216, gf, 256, final=False)
    return x
```

```python
import functools
import math

import jax
import jax.numpy as jnp
import numpy as np
from jax import lax
from jax.experimental import pallas as pl
from jax.experimental.pallas import tpu as pltpu

D_MODEL = 1024
DEPTH = 2
GRID_W = 64
EPS = 1e-6
N_MOD = 6

HY_WIDTH = D_MODEL // 4
HY_ORDER = 2
HY_EMB = 33
HY_FAST_DECAY = 0.3
HY_SLOW_DECAY = 1.5
HY_TARGET = 1e-2
GDN_HEADS = D_MODEL // 256
GDN_DK = 128
GDN_DV = 128
RET_HEADS = D_MODEL // 256
RET_DK = 64
RET_DV = 64
ROPE_BASE = 10000.0
FFN_HIDDEN = ((8 * D_MODEL + 3 * 256 - 1) // (3 * 256)) * 256

GDN_CHUNK = 64
GDN_ROWS = 256
GDN_HP = 2
GDN_GROUP = 18
RET_CHUNK = 256
HY_BLOCK = 256
HY_PAIR = 2
HY_IO = 256
LANES = 128
SUBLANES = 8

GDN_QKV = 2 * GDN_HEADS * GDN_DK + GDN_HEADS * GDN_DV
OFF_HY = 0
OFF_QKV = OFF_HY + 3 * HY_WIDTH
OFF_Z = OFF_QKV + GDN_QKV
OFF_RET = OFF_Z + GDN_HEADS * GDN_DV
OFF_AB = OFF_RET + 4 * RET_HEADS * RET_DK
IN_PAD = OFF_AB + LANES

VMEM_LIMIT = 56 * 1024 * 1024
GDN_VMEM_LIMIT = 60 * 1024 * 1024

f32 = jnp.float32
bf16 = jnp.bfloat16


def _cparams(sem):
    return pltpu.CompilerParams(dimension_semantics=sem, vmem_limit_bytes=VMEM_LIMIT)


def _const_spec(shape):
    nd = len(shape)
    return pl.BlockSpec(shape, lambda *_: (0,) * nd, pipeline_mode=pl.Buffered(1))


def _rms(x, g):
    return x * lax.rsqrt(jnp.mean(x * x, axis=-1, keepdims=True) + EPS) * g


def _silu(x):
    return x * jax.nn.sigmoid(x)


def _dot(a, b):
    return jnp.dot(a.astype(bf16), b.astype(bf16), preferred_element_type=f32)


def _dot_nt(a, b):
    return lax.dot_general(a.astype(bf16), b.astype(bf16), (((1,), (1,)), ((), ())),
                           preferred_element_type=f32)


def _dot_tn(a, b):
    return lax.dot_general(a.astype(bf16), b.astype(bf16), (((0,), (0,)), ((), ())),
                           preferred_element_type=f32)


def _mod_kernel(c_ref, w_ref, b_ref, o_ref):
    o_ref[...] = _dot(_silu(c_ref[...]), w_ref[...]) + b_ref[...]


def _mod_vectors(cs, w16, b):
    rows, d = cs.shape
    n = w16.shape[1]
    tn = 1536
    return pl.pallas_call(
        _mod_kernel,
        grid=(n // tn,),
        in_specs=[pl.BlockSpec((rows, d), lambda j: (0, 0)),
                  pl.BlockSpec((d, tn), lambda j: (0, j)),
                  pl.BlockSpec((1, tn), lambda j: (0, j))],
        out_specs=pl.BlockSpec((rows, tn), lambda j: (0, j)),
        out_shape=jax.ShapeDtypeStruct((rows, n), f32),
        compiler_params=_cparams(("parallel",)),
        name="mod_vectors",
    )(cs, w16, b)


def _in_proj_kernel(x_ref, g_ref, mod_ref, w_ref, o_ref):
    x = x_ref[0]
    mod = mod_ref[0]
    h = _rms(x, g_ref[...]) * (1.0 + mod[1:2]) + mod[0:1]
    h16 = h.astype(bf16)
    n = w_ref.shape[1]
    step = 512
    for j in range(0, n, step):
        e = min(j + step, n)
        o_ref[0, :, j:e] = jnp.dot(h16, w_ref[:, j:e], preferred_element_type=f32)


def _in_proj(x, g, mod, w16, tm):
    b, t, d = x.shape
    n = w16.shape[1]
    return pl.pallas_call(
        _in_proj_kernel,
        grid=(b, t // tm),
        in_specs=[pl.BlockSpec((1, tm, d), lambda i, j: (i, j, 0)),
                  _const_spec((1, d)),
                  pl.BlockSpec((1, N_MOD, d), lambda i, j: (i, 0, 0)),
                  _const_spec((d, n))],
        out_specs=pl.BlockSpec((1, tm, n), lambda i, j: (i, j, 0)),
        out_shape=jax.ShapeDtypeStruct((b, t, n), f32),
        compiler_params=_cparams(("parallel", "parallel")),
        name="in_proj",
    )(x, g, mod, w16)


def _out_ffn_kernel(x_ref, hy_ref, gd_ref, rt_ref, mod_ref, wo_ref, g2_ref, w1_ref, w2_ref, gf_ref, o_ref,
                    *, final):
    x = x_ref[0]
    mod = mod_ref[0]
    o0, o1, o2 = HY_WIDTH, HY_WIDTH + GDN_HEADS * GDN_DV, D_MODEL
    y = (jnp.dot(hy_ref[0].astype(bf16), wo_ref[0:o0, :], preferred_element_type=f32)
         + jnp.dot(gd_ref[0].astype(bf16), wo_ref[o0:o1, :], preferred_element_type=f32)
         + jnp.dot(rt_ref[0].astype(bf16), wo_ref[o1:o2, :], preferred_element_type=f32))
    x = x + mod[2:3] * y
    h16 = (_rms(x, g2_ref[...]) * (1.0 + mod[4:5]) + mod[3:4]).astype(bf16)
    step = 256
    acc = jnp.zeros_like(x)
    for j in range(0, FFN_HIDDEN, step):
        gate = jnp.dot(h16, w1_ref[:, j:j + step], preferred_element_type=f32)
        up = jnp.dot(h16, w1_ref[:, FFN_HIDDEN + j:FFN_HIDDEN + j + step], preferred_element_type=f32)
        acc = acc + jnp.dot((_silu(gate) * up).astype(bf16), w2_ref[j:j + step, :], preferred_element_type=f32)
    x = x + mod[5:6] * acc
    if final:
        x = _rms(x, gf_ref[...])
    o_ref[0] = x


def _out_ffn(x, hy, gd, rt, mod, wo16, g2, w116, w216, gf, tm, final):
    b, t, d = x.shape
    tok = lambda w: pl.BlockSpec((1, tm, w), lambda i, j: (i, j, 0))
    return pl.pallas_call(
        functools.partial(_out_ffn_kernel, final=final),
        grid=(b, t // tm),
        in_specs=[tok(d), tok(hy.shape[-1]), tok(gd.shape[-1]), tok(rt.shape[-1]),
                  pl.BlockSpec((1, N_MOD, d), lambda i, j: (i, 0, 0)),
                  _const_spec(wo16.shape), _const_spec((1, d)), _const_spec(w116.shape),
                  _const_spec(w216.shape), _const_spec((1, d))],
        out_specs=tok(d),
        out_shape=jax.ShapeDtypeStruct((b, t, d), f32),
        compiler_params=_cparams(("parallel", "parallel")),
        name="out_ffn",
    )(x, hy, gd, rt, mod, wo16, g2, w116, w216, gf)


def _hyena_kernel(v_ref, x1_ref, x2_ref, k_ref, b_ref, o_ref, acc_ref, *, nb, cb):
    hb = HY_BLOCK
    rb = v_ref.shape[1] // nb

    def conv(u16, kext):
        m = range(len(u16))
        for d in range(-(nb - 1), nb):
            dd = d % (2 * nb)
            t0, t1 = max(0, d), min(nb - 1, nb - 1 + d) + 1
            for i in m:
                win = jnp.broadcast_to(kext[i][:, hb * dd:hb * dd + 2 * hb], (hb, 2 * hb))
                toep = pltpu.roll(win, 0, 1, stride=1, stride_axis=0)[:, hb:].astype(bf16)
                contrib = jnp.dot(u16[i][rb * (t0 - d):rb * (t1 - d)], toep, preferred_element_type=f32)
                if d == -(nb - 1):
                    acc_ref[i] = jnp.zeros(acc_ref.shape[1:], f32)
                acc_ref[i, rb * t0:rb * t1, :] += contrib
        return [acc_ref[i] for i in m]

    def body(g, carry):
        cs = [HY_PAIR * g + i for i in range(HY_PAIR)]
        z = [v_ref[c] for c in cs]
        for n, gate_ref in enumerate((x1_ref, x2_ref)):
            y = conv([x.astype(bf16) for x in z], [k_ref[n, pl.ds(c, 1), :] for c in cs])
            z = [gate_ref[c] * (y[i] + z[i] * b_ref[n, pl.ds(c, 1), :]) for i, c in enumerate(cs)]
        for i, c in enumerate(cs):
            o_ref[c] = z[i]
        return carry

    lax.fori_loop(0, cb // HY_PAIR, body, 0)


def _hyena_conv(p3, kext, bias, nb, cb=SUBLANES):
    c3, r, hb = p3.shape
    c = c3 // 3
    klen = kext.shape[-1]
    sig = lambda s: pl.BlockSpec((cb, r, hb), lambda i, s=s: (s * (c // cb) + i, 0, 0))
    return pl.pallas_call(
        functools.partial(_hyena_kernel, nb=nb, cb=cb),
        grid=(c // cb,),
        in_specs=[sig(0), sig(1), sig(2),
                  pl.BlockSpec((HY_ORDER, cb, klen), lambda i: (0, i, 0)),
                  pl.BlockSpec((HY_ORDER, cb, hb), lambda i: (0, i, 0))],
        out_specs=pl.BlockSpec((cb, r, hb), lambda i: (i, 0, 0)),
        out_shape=jax.ShapeDtypeStruct((c, r, hb), f32),
        scratch_shapes=[pltpu.VMEM((HY_PAIR, r, hb), f32)],
        compiler_params=_cparams(("parallel",)),
        name=f"hyena_conv_nb{nb}",
    )(p3, p3, p3, kext, bias)


def _hyena_prep_kernel(x_ref, before_ref, after_ref, w_ref, b_ref, o_ref):
    hb = HY_BLOCK
    w = w_ref[...]
    row = lax.broadcasted_iota(jnp.int32, x_ref.shape[1:], 0)
    s = pl.program_id(1)
    has_before = s > 0
    has_after = s < pl.num_programs(1) - 1
    for i in range(x_ref.shape[0]):
        x = x_ref[i]
        prev = jnp.where(has_before, before_ref[i, SUBLANES - 1:SUBLANES, :], 0.0)
        nxt = jnp.where(has_after, after_ref[i, 0:1, :], 0.0)
        xp = jnp.where(row == 0, prev, pltpu.roll(x, 1, 0))
        xn = jnp.where(row == hb - 1, nxt, pltpu.roll(x, hb - 1, 0))
        y = xp * w[0:1] + x * w[1:2] + xn * w[2:3] + b_ref[...]
        o_ref[:, i, :] = y.T


def _hyena_prep(p, conv_w, conv_b):
    b, t, _ = p.shape
    hb = HY_BLOCK
    n = 3 * HY_WIDTH
    per = hb // SUBLANES
    last = t // SUBLANES - 1
    assert OFF_HY == 0 and b % SUBLANES == 0
    return pl.pallas_call(
        _hyena_prep_kernel,
        grid=(n // HY_IO, t // hb),
        in_specs=[pl.BlockSpec((b, hb, HY_IO), lambda j, s: (0, s, j)),
                  pl.BlockSpec((b, SUBLANES, HY_IO), lambda j, s: (0, jnp.maximum(s * per - 1, 0), j)),
                  pl.BlockSpec((b, SUBLANES, HY_IO), lambda j, s: (0, jnp.minimum((s + 1) * per, last), j)),
                  pl.BlockSpec((3, HY_IO), lambda j, s: (0, j)),
                  pl.BlockSpec((1, HY_IO), lambda j, s: (0, j))],
        out_specs=pl.BlockSpec((HY_IO, b, hb), lambda j, s: (j, s, 0)),
        out_shape=jax.ShapeDtypeStruct((n, (t // hb) * b, hb), f32),
        compiler_params=_cparams(("parallel", "parallel")),
        name="hyena_prep",
    )(p, p, p, conv_w, conv_b)


def _hyena_post_kernel(z_ref, o_ref):
    for i in range(z_ref.shape[1]):
        o_ref[i] = z_ref[:, i, :].T


def _hyena_post(z, b):
    c, r, hb = z.shape
    nb = r // b
    return pl.pallas_call(
        _hyena_post_kernel,
        grid=(c // HY_IO, nb),
        in_specs=[pl.BlockSpec((HY_IO, b, hb), lambda j, s: (j, s, 0))],
        out_specs=pl.BlockSpec((b, hb, HY_IO), lambda j, s: (0, s, j)),
        out_shape=jax.ShapeDtypeStruct((b, nb * hb, c), f32),
        compiler_params=_cparams(("parallel", "parallel")),
        name="hyena_post",
    )(z)


def _hyena_filter_kernel(z_ref, w1_ref, b1_ref, f1_ref, w2_ref, b2_ref, f2_ref, w3_ref, win_ref, o_ref):
    hi = lax.Precision.HIGHEST
    hb = HY_BLOCK
    t = z_ref.shape[1]
    for side in range(2):
        h = jnp.sin(f1_ref[...] * (jnp.dot(z_ref[side], w1_ref[...], precision=hi,
                                           preferred_element_type=f32) + b1_ref[...]))
        h = jnp.sin(f2_ref[...] * (jnp.dot(h, w2_ref[...], precision=hi, preferred_element_type=f32)
                                   + b2_ref[...]))
        for n in range(HY_ORDER):
            w3 = w3_ref[:, (2 * n + side) * HY_WIDTH:(2 * n + side + 1) * HY_WIDTH]
            k = lax.dot_general(w3, h, (((0,), (1,)), ((), ())), precision=hi,
                                preferred_element_type=f32) * win_ref[side]
            o_ref[n, :, hb + side * t:hb + (side + 1) * t] = k
            if side == 1:
                o_ref[n, :, 0:hb] = k[:, t - hb:]


def _hyena_tables(t_len):
    pos = np.arange(t_len, dtype=np.float64)
    t = np.linspace(0.0, 1.0, t_len)
    bands = (HY_EMB - 1) // 2
    f = np.linspace(1e-4, bands - 1, bands)
    ang = (2.0 * math.pi / t_len) * pos[:, None] * f[None, :]
    z = np.concatenate([t[:, None], np.cos(ang), -np.sin(ang)], axis=-1)
    max_decay = math.log(HY_TARGET) / HY_FAST_DECAY
    min_decay = math.log(HY_TARGET) / HY_SLOW_DECAY
    deltas = np.abs(np.linspace(min_decay, max_decay, HY_WIDTH))
    window = np.exp(-t[:, None] * deltas[None, :])
    neg = lambda x: np.concatenate([x[:1], x[:0:-1]], axis=0)
    zs = np.pad(np.stack([z, neg(z)]), ((0, 0), (0, 0), (0, LANES - HY_EMB)))
    wneg = neg(window)
    wneg[0] = 0.0
    win = np.stack([window, wneg]).transpose(0, 2, 1)
    return jnp.asarray(zs, f32), jnp.asarray(win, f32)


def _hyena_filters(t_len, lp):
    zs, win = _hyena_tables(t_len)
    w1 = jnp.pad(lp['hy_f_w1'], ((0, LANES - HY_EMB), (0, 0)))
    row = lambda x: x[None]
    args = (zs, w1, row(lp['hy_f_b1']), row(lp['hy_f_freq1']), lp['hy_f_w2'], row(lp['hy_f_b2']),
            row(lp['hy_f_freq2']), lp['hy_f_w3'], win)
    klen = 2 * t_len + HY_BLOCK
    return pl.pallas_call(
        _hyena_filter_kernel,
        grid=(1,),
        in_specs=[_const_spec(a.shape) for a in args],
        out_specs=pl.BlockSpec((HY_ORDER, HY_WIDTH, klen), lambda i: (0, 0, 0)),
        out_shape=jax.ShapeDtypeStruct((HY_ORDER, HY_WIDTH, klen), f32),
        compiler_params=_cparams(("arbitrary",)),
        name="hyena_filters",
    )(*args)


def _gdn_kernel(ql_ref, kl_ref, vl_ref, zl_ref, abl_ref, qc_ref, kc_ref, vc_ref, zc_ref, abc_ref,
                cwq_ref, cwk_ref, cwv_ref, pa_ref, pd_ref, ng_ref,
                ol_ref, oc_ref,
                q_s, k_s, v_s, cf_s, at_s, n_s, mp_s, egl_s, o_s, st_s):
    c = GDN_CHUNK
    rb = GDN_ROWS
    tc, tl = qc_ref.shape[1], ql_ref.shape[1]
    ncc, nc = tc // c, (tc + tl) // c
    dk = GDN_DK

    def conv_rows(x_ref, w_ref, r0, t, ls):
        x = x_ref[0, pl.ds(r0, rb), ls]
        before = pl.multiple_of(jnp.maximum(r0 - SUBLANES, 0), SUBLANES)
        after = pl.multiple_of(jnp.minimum(r0 + rb, t - SUBLANES), SUBLANES)
        prev = x_ref[0, pl.ds(before, SUBLANES), ls][SUBLANES - 1:SUBLANES]
        nxt = x_ref[0, pl.ds(after, SUBLANES), ls][0:1]
        prev = jnp.where(r0 > 0, prev, 0.0)
        nxt = jnp.where(r0 + rb < t, nxt, 0.0)
        row = lax.broadcasted_iota(jnp.int32, x.shape, 0)
        xp = jnp.where(row == 0, prev, pltpu.roll(x, 1, 0))
        xn = jnp.where(row == rb - 1, nxt, pltpu.roll(x, rb - 1, 0))
        w = w_ref[:, ls]
        return _silu(xp * w[0:1] + x * w[1:2] + xn * w[2:3])

    def l2n(x):
        return x * lax.rsqrt(jnp.sum(x * x, axis=-1, keepdims=True) + EPS)

    def prep(hh, q_ref, k_ref, v_ref, ab_ref, t, base):
        ls = slice(hh * LANES, (hh + 1) * LANES)

        def body(i, carry):
            r0 = pl.multiple_of(i * rb, rb)
            dst = pl.ds(pl.multiple_of(base + i * rb, rb), rb)
            q_s[dst, :] = l2n(conv_rows(q_ref, cwq_ref, r0, t, ls)) * (dk ** -0.5)
            k_s[dst, :] = l2n(conv_rows(k_ref, cwk_ref, r0, t, ls))
            v_s[dst, :] = conv_rows(v_ref, cwv_ref, r0, t, ls)
            return carry
        lax.fori_loop(0, t // rb, body, 0)

        ab = ab_ref[0, hh]
        ab = jnp.concatenate([ab, jnp.zeros((SUBLANES - ab.shape[0], t), f32)], axis=0)
        rowi = lax.broadcasted_iota(jnp.int32, ab.shape, 0)
        pos = lax.broadcasted_iota(jnp.int32, ab.shape, 1) % c
        tile = lambda p: jnp.concatenate([p] * (t // LANES), axis=1)
        g = -jnp.exp(tile(pa_ref[hh])) * jax.nn.softplus(ab + tile(pd_ref[hh]))
        acc_f, acc_b = g, g
        s = 1
        while s < c:
            acc_f = acc_f + jnp.where(pos >= s, pltpu.roll(acc_f, s, 1), 0.0)
            acc_b = acc_b + jnp.where(pos < c - s, pltpu.roll(acc_b, t - s, 1), 0.0)
            s *= 2
        cf_t = jnp.where(rowi == 0, acc_f, jnp.where(rowi == 1, acc_b, jax.nn.sigmoid(ab)))
        pad = jnp.zeros((LANES - SUBLANES, LANES), f32)
        for j in range(t // LANES):
            blk = jnp.concatenate([cf_t[:, j * LANES:(j + 1) * LANES], pad], axis=0)
            cf_s[base + j * LANES:base + (j + 1) * LANES, :] = blk.T

    lane = lax.broadcasted_iota(jnp.int32, (c, 2 * c), 1)
    row = lax.broadcasted_iota(jnp.int32, (c, 2 * c), 0)
    fwd = lane < c
    col = lane % c
    incl = (fwd & (row >= col)) | (~fwd & (row <= col))
    strict = (fwd & (row > col)) | (~fwd & (row < col))

    def blockdiag(a):
        return jnp.concatenate([jnp.where(fwd, a, 0.0), jnp.where(fwd, 0.0, a)], axis=0).astype(bf16)

    def chunk_group(hh, cis):
        n = range(len(cis))
        xf, xb = 2 * hh, 2 * hh + 1
        rows = [pl.ds(pl.multiple_of(ci * c, c), c) for ci in cis]
        cf = [cf_s[r, :] for r in rows]
        k = [k_s[r, :] for r in rows]
        q = [q_s[r, :] for r in rows]
        bc = lambda x, j: jnp.broadcast_to(x[:, j:j + 1], (c, LANES))
        gf_c, gb_c = [bc(x, 0) for x in cf], [bc(x, 1) for x in cf]
        bf_c, bb_c = [bc(x, 2) for x in cf], [bc(x, 3) for x in cf]
        cf_t = [jnp.concatenate([x, x], axis=0).T for x in cf]
        g_r = [jnp.where(fwd, jnp.broadcast_to(x[0:1], (c, 2 * c)), jnp.broadcast_to(x[1:2], (c, 2 * c)))
               for x in cf_t]
        decay = [jnp.where(incl, jnp.exp(jnp.where(incl, jnp.where(fwd, gf_c[g], gb_c[g]) - g_r[g], 0.0)), 0.0)
                 for g in n]
        kq = [_dot_nt(jnp.concatenate([k[g], q[g]], axis=0), jnp.concatenate([k[g], k[g]], axis=0))
              for g in n]
        lm = [jnp.where(strict, kq[g][:c] * jnp.where(fwd, bf_c[g], bb_c[g]) * decay[g], 0.0) for g in n]
        for g in n:
            at_s[g] = (kq[g][c:] * decay[g]).astype(bf16)
        a = [jnp.dot(x.astype(bf16), blockdiag(x), preferred_element_type=f32) for x in lm]
        tp = [-x for x in lm]
        p = 2
        while 2 * p < c:
            ta = [jnp.dot(jnp.concatenate([tp[g], a[g]], axis=0).astype(bf16), blockdiag(a[g]),
                          preferred_element_type=f32) for g in n]
            tp = [tp[g] + a[g] + ta[g][:c] for g in n]
            a = [x[c:] for x in ta]
            p *= 2
        tp = [tp[g] + a[g] + jnp.dot(tp[g].astype(bf16), blockdiag(a[g]), preferred_element_type=f32)
              for g in n]
        eg_f, eg_b = [jnp.exp(x) for x in gf_c], [jnp.exp(x) for x in gb_c]
        uw = []
        for g in n:
            v = v_s[rows[g], :]
            rhs = jnp.concatenate([jnp.concatenate([v * bf_c[g], k[g] * (bf_c[g] * eg_f[g])], axis=1),
                                   jnp.concatenate([v * bb_c[g], k[g] * (bb_c[g] * eg_b[g])], axis=1)], axis=0)
            uw.append(rhs + jnp.dot(blockdiag(tp[g]), rhs.astype(bf16), preferred_element_type=f32))
        gl = [(gf_c[g][c - 1:c], gb_c[g][0:1]) for g in n]
        res = []
        for g in n:
            attn = at_s[g]
            for d, (half, g_c) in enumerate(((slice(0, c), gf_c[g]), (slice(c, 2 * c), gb_c[g]))):
                kgt = (k[g] * jnp.exp(gl[g][d] - g_c)).T.astype(bf16)
                res.append(jnp.dot(jnp.concatenate([kgt, attn[:, half]], axis=0), uw[g][half].astype(bf16),
                                   preferred_element_type=f32))
        for g in n:
            ci = cis[g]
            for d, (x, eg) in enumerate(((xf, eg_f[g]), (xb, eg_b[g]))):
                r = res[2 * g + d]
                n_s[x, ci] = r[:dk, :dk]
                mp_s[x, ci] = jnp.concatenate([r[:dk, dk:], q[g] * eg - r[dk:, dk:]], axis=0).astype(bf16)
            egl_s[hh, ci] = jnp.concatenate([jnp.exp(gl[g][0]), jnp.exp(gl[g][1]),
                                             jnp.zeros((SUBLANES - 2, LANES), f32)], axis=0)
            o_s[hh, rows[g], :] = res[2 * g][dk:, :dk] + res[2 * g + 1][dk:, :dk]

    for hh in range(GDN_HP):
        prep(hh, qc_ref, kc_ref, vc_ref, abc_ref, tc, 0)
        prep(hh, ql_ref, kl_ref, vl_ref, abl_ref, tl, tc)

        def chunk_body(i, carry, hh=hh):
            chunk_group(hh, [GDN_GROUP * i + g for g in range(GDN_GROUP)])
            return carry

        lax.fori_loop(0, nc // GDN_GROUP, chunk_body, 0)

    st_s[...] = jnp.zeros_like(st_s)
    chains = [(hh, d) for hh in range(GDN_HP) for d in range(2)]

    def scan_body(i, carry):
        ci = (i, jnp.where(i < ncc, ncc - 1 - i, nc + ncc - 1 - i))
        rows = [pl.ds(pl.multiple_of(x * c, c), c) for x in ci]
        n = range(len(chains))
        s = [st_s[x] for x in n]
        res = [jnp.dot(mp_s[x, ci[chains[x][1]]], s[x].astype(bf16), preferred_element_type=f32) for x in n]
        for x in n:
            hh, d = chains[x]
            st_s[x] = s[x] * egl_s[hh, ci[d]][d:d + 1] + n_s[x, ci[d]] - res[x][:dk]
            o_s[hh, rows[d], :] += res[x][dk:]
        return carry

    lax.fori_loop(0, nc, scan_body, 0)

    def finish(hh, z_ref, o_ref, t, base):
        ls = slice(hh * LANES, (hh + 1) * LANES)

        def body(i, carry):
            r0 = pl.ds(pl.multiple_of(i * rb, rb), rb)
            o = o_s[hh, pl.ds(pl.multiple_of(base + i * rb, rb), rb), :]
            y = o * lax.rsqrt(jnp.mean(o * o, axis=-1, keepdims=True) + EPS) * ng_ref[...]
            o_ref[0, r0, ls] = y * _silu(z_ref[0, r0, ls])
            return carry
        lax.fori_loop(0, t // rb, body, 0)

    for hh in range(GDN_HP):
        finish(hh, zc_ref, oc_ref, tc, 0)
        finish(hh, zl_ref, ol_ref, tl, tc)


def _gdn(p_l, ab_l, p_c, ab_c, conv_w, pa, pd, norm_g):
    b, tl, _ = p_l.shape
    tc = p_c.shape[1]
    h, c, hp = GDN_HEADS, GDN_CHUNK, GDN_HP
    nc = (tl + tc) // c
    assert tl % GDN_ROWS == 0 and tc % GDN_ROWS == 0 and nc % GDN_GROUP == 0
    qb, zb = OFF_QKV // LANES, OFF_Z // LANES
    assert h % hp == 0 and qb % hp == 0 and zb % hp == 0
    w = hp * LANES

    def specs(t):
        heads = lambda off: pl.BlockSpec((1, t, w), lambda i, j, off=off // hp: (i, 0, off + j))
        return [heads(qb), heads(qb + h), heads(qb + 2 * h), heads(zb),
                pl.BlockSpec((1, hp, 4, t), lambda i, j: (i, j, 0, 0))]

    cw = lambda off: pl.BlockSpec((3, w), lambda i, j, off=off // hp: (0, off + j))
    par = pl.BlockSpec((hp, SUBLANES, LANES), lambda i, j: (j, 0, 0))
    out = lambda t: pl.BlockSpec((1, t, w), lambda i, j: (i, 0, j))
    tt = tl + tc
    return pl.pallas_call(
        _gdn_kernel,
        grid=(b, h // hp),
        in_specs=specs(tl) + specs(tc) + [cw(0), cw(h), cw(2 * h), par, par, _const_spec((1, GDN_DV))],
        out_specs=[out(tl), out(tc)],
        out_shape=[jax.ShapeDtypeStruct((b, tl, h * GDN_DV), f32),
                   jax.ShapeDtypeStruct((b, tc, h * GDN_DV), f32)],
        scratch_shapes=[pltpu.VMEM((tt, GDN_DK), f32), pltpu.VMEM((tt, GDN_DK), f32),
                        pltpu.VMEM((tt, GDN_DV), f32), pltpu.VMEM((tt, LANES), f32),
                        pltpu.VMEM((GDN_GROUP, c, 2 * c), bf16),
                        pltpu.VMEM((2 * hp, nc, GDN_DK, GDN_DV), f32),
                        pltpu.VMEM((2 * hp, nc, GDN_DK + c, GDN_DV), bf16),
                        pltpu.VMEM((hp, nc, SUBLANES, LANES), f32), pltpu.VMEM((hp, tt, GDN_DV), f32),
                        pltpu.VMEM((2 * hp, GDN_DK, GDN_DV), f32)],
        compiler_params=pltpu.CompilerParams(dimension_semantics=("parallel", "parallel"),
                                             vmem_limit_bytes=GDN_VMEM_LIMIT),
        name="gated_deltanet",
    )(p_l, p_l, p_l, p_l, ab_l, p_c, p_c, p_c, p_c, ab_c, conv_w, conv_w, conv_w, pa, pd, norm_g)


def _ret_kernel(ql_ref, kl_ref, vl_ref, gl_ref, qc_ref, kc_ref, vc_ref, gc_ref,
                cos_ref, sin_ref, dsum_ref, sc_ref, gch_ref,
                ol_ref, oc_ref, sf_ref, sb_ref, q_s, k_s, o_s):
    c = RET_CHUNK
    w = RET_HEADS * RET_DK
    ri = lax.broadcasted_iota(jnp.int32, (w, w), 0) // RET_DK
    ci = lax.broadcasted_iota(jnp.int32, (w, w), 1) // RET_DK
    blockdiag = ri == ci
    lane_head = lax.broadcasted_iota(jnp.int32, (c, w), 1) // RET_DK

    def intra(q, k, v):
        out = jnp.zeros((c, w), f32)
        for h in range(RET_HEADS):
            m = lane_head == h
            scores = _dot_nt(jnp.where(m, q, 0.0), k) * dsum_ref[h]
            out = out + _dot(scores, jnp.where(m, v, 0.0))
        return out

    def update(s_ref, k, v, d):
        new = jnp.where(blockdiag, _dot_tn(k * sc_ref[2 * d + 1], v), 0.0)
        s_ref[...] = s_ref[...] * gch_ref[d] + new

    first_half = lax.broadcasted_iota(jnp.int32, (c, w), 1) % RET_DK < RET_DK // 2
    avg = jnp.where(blockdiag, 1.0 / RET_DV, 0.0).astype(bf16)

    def rope(x, r):
        swapped = jnp.where(first_half, pltpu.roll(x, w - RET_DK // 2, 1), pltpu.roll(x, RET_DK // 2, 1))
        return x * cos_ref[r, :] + swapped * sin_ref[r, :]

    def head_mean(x):
        hi = x.astype(bf16)
        lo = (x - hi.astype(f32)).astype(bf16)
        return jnp.dot(hi, avg, preferred_element_type=f32) + jnp.dot(lo, avg, preferred_element_type=f32)

    def finish(o, gate):
        d = o - head_mean(o)
        return d * lax.rsqrt(head_mean(d * d) + EPS) * _silu(gate)

    sf_ref[...] = jnp.zeros_like(sf_ref)
    sb_ref[...] = jnp.zeros_like(sb_ref)
    qc, kc, vc = qc_ref[0], kc_ref[0] * (RET_DK ** -0.5), vc_ref[0]
    oc_ref[0] = finish(intra(qc, kc, vc), gc_ref[0])
    update(sf_ref, kc, vc, 0)
    update(sb_ref, kc, vc, 1)
    n = ql_ref.shape[1] // c
    for i in range(n):
        r = slice(i * c, (i + 1) * c)
        q = rope(ql_ref[0, r, :], r)
        k = rope(kl_ref[0, r, :] * (RET_DK ** -0.5), r)
        v = vl_ref[0, r, :]
        q_s[r, :] = q
        k_s[r, :] = k
        o_s[r, :] = intra(q, k, v) + _dot(q * sc_ref[0], sf_ref[...])
        update(sf_ref, k, v, 0)
    for i in range(n - 1, -1, -1):
        r = slice(i * c, (i + 1) * c)
        o = o_s[r, :] + _dot(q_s[r, :] * sc_ref[2], sb_ref[...])
        ol_ref[0, r, :] = finish(o, gl_ref[0, r, :])
        update(sb_ref, k_s[r, :], vl_ref[0, r, :], 1)


def _retention(p_l, p_c, cos, sin, dsum, scales, gch):
    b, tl, _ = p_l.shape
    tc = p_c.shape[1]
    w = RET_HEADS * RET_DK
    assert OFF_RET % w == 0
    blk = lambda t, j: pl.BlockSpec((1, t, w), lambda i, j=j: (i, 0, OFF_RET // w + j))
    tok = lambda t: pl.BlockSpec((1, t, w), lambda i: (i, 0, 0))
    return pl.pallas_call(
        _ret_kernel,
        grid=(b,),
        in_specs=[blk(tl, j) for j in range(4)] + [blk(tc, j) for j in range(4)]
        + [_const_spec(x.shape) for x in (cos, sin, dsum, scales, gch)],
        out_specs=[tok(tl), tok(tc)],
        out_shape=[jax.ShapeDtypeStruct((b, tl, w), f32), jax.ShapeDtypeStruct((b, tc, w), f32)],
        scratch_shapes=[pltpu.VMEM((w, w), f32), pltpu.VMEM((w, w), f32),
                        pltpu.VMEM((tl, w), f32), pltpu.VMEM((tl, w), f32), pltpu.VMEM((tl, w), f32)],
        compiler_params=_cparams(("parallel",)),
        name="retention",
    )(p_l, p_l, p_l, p_l, p_c, p_c, p_c, p_c, cos, sin, dsum, scales, gch)


def _hyena(p, lp):
    b, t, _ = p.shape
    hb, c = HY_BLOCK, HY_WIDTH
    nb = t // hb
    p3 = _hyena_prep(p, lp['hy_conv_w'], lp['hy_conv_b'][None])
    bias = jnp.broadcast_to(lp['hy_bias'][:, :, None], (HY_ORDER, c, hb))
    z = _hyena_conv(p3, _hyena_filters(t, lp), bias, nb)
    return _hyena_post(z, b)


def _gdn_gate_inputs(p):
    b, t, _ = p.shape
    ab = p[..., OFF_AB:OFF_AB + 4 * GDN_HEADS].reshape(b, t, 4, GDN_HEADS)
    return ab.transpose(0, 3, 2, 1)


def _gdn_gate_params(a_log, dt_bias):
    def rows(x):
        x = jnp.pad(x.T, ((0, 0), (0, SUBLANES - 2)))
        return jnp.broadcast_to(x[:, :, None], (GDN_HEADS, SUBLANES, LANES))
    return rows(a_log), rows(dt_bias)


def _rope_tables(t_lat):
    rows_n = t_lat // GRID_W
    row = np.repeat(np.arange(rows_n, dtype=np.float64), GRID_W)
    colp = np.tile(np.arange(GRID_W, dtype=np.float64), rows_n)
    nf = RET_DK // 4
    inv = ROPE_BASE ** (-np.arange(nf, dtype=np.float64) / nf)
    ang = np.concatenate([row[:, None] * inv, colp[:, None] * inv], axis=-1)
    cos, sin = np.cos(ang), np.sin(ang)
    return (jnp.asarray(np.tile(np.concatenate([cos, cos], axis=-1), (1, RET_HEADS)), f32),
            jnp.asarray(np.tile(np.concatenate([-sin, sin], axis=-1), (1, RET_HEADS)), f32))


def _ret_constants(lp):
    c = RET_CHUNK
    lg = jax.nn.log_sigmoid(lp['ret_decay_logit'])
    idx = jnp.arange(c, dtype=f32)
    rel = idx[:, None] - idx[None, :]
    d_f = jnp.exp(jnp.where(rel >= 0, rel * lg[0][:, None, None], -jnp.inf))
    d_b = jnp.exp(jnp.where(rel <= 0, -rel * lg[1][:, None, None], -jnp.inf))
    lane = lambda x: jnp.repeat(x, RET_DK, axis=-1)
    scales = jnp.stack([lane(jnp.exp((idx + 1.0)[:, None] * lg[0])),
                        lane(jnp.exp((c - 1.0 - idx)[:, None] * lg[0])),
                        lane(jnp.exp((c - idx)[:, None] * lg[1])),
                        lane(jnp.exp(idx[:, None] * lg[1]))])
    gch = lane(jnp.exp(c * lg))[:, None, :]
    return d_f + d_b, scales, gch


def _reorder_w_in(w):
    h2 = 2 * GDN_HEADS
    a0 = OFF_Z + GDN_HEADS * GDN_DV
    parts = [w[:, :a0], w[:, a0 + 2 * h2:], w[:, a0:a0 + 2 * h2],
             jnp.zeros((w.shape[0], LANES - 2 * h2), w.dtype)]
    return jnp.concatenate(parts, axis=1)


def kernel(x, c, ctx, c_ctx, mod_w, mod_b, norm1_g, w_in, hy_conv_w, hy_conv_b, hy_f_w1, hy_f_b1, hy_f_freq1, hy_f_w2, hy_f_b2, hy_f_freq2, hy_f_w3, hy_bias, gdn_conv_w, gdn_a_log, gdn_dt_bias, gdn_norm_g, ret_decay_logit, w_out, norm2_g, ffn_w_in, ffn_w_out, final_norm_g):
    bsz, t_lat, d = x.shape
    cos, sin = _rope_tables(t_lat)
    pad_rows = (-(bsz + 1)) % SUBLANES
    cs = jnp.concatenate([c, c_ctx[None], jnp.zeros((pad_rows, d), f32)], axis=0)
    gf = final_norm_g[None]
    for i in range(DEPTH):
        need_ctx = i < DEPTH - 1
        lp = {'hy_conv_w': hy_conv_w[i], 'hy_conv_b': hy_conv_b[i], 'hy_f_w1': hy_f_w1[i],
              'hy_f_b1': hy_f_b1[i], 'hy_f_freq1': hy_f_freq1[i], 'hy_f_w2': hy_f_w2[i],
              'hy_f_b2': hy_f_b2[i], 'hy_f_freq2': hy_f_freq2[i], 'hy_f_w3': hy_f_w3[i],
              'hy_bias': hy_bias[i], 'ret_decay_logit': ret_decay_logit[i]}
        mod = _mod_vectors(cs, mod_w[i].astype(bf16), mod_b[i][None])
        mod_lat = mod[:bsz].reshape(bsz, N_MOD, d)
        mod_ctx = jnp.broadcast_to(mod[bsz].reshape(1, N_MOD, d), (bsz, N_MOD, d))
        w_in16 = _reorder_w_in(w_in[i]).astype(bf16)
        g1 = norm1_g[i][None]
        p_l = _in_proj(x, g1, mod_lat, w_in16, 512)
        p_c = _in_proj(ctx, g1, mod_ctx, w_in16, 256)

        hy_l = _hyena(p_l, lp)
        pa, pd = _gdn_gate_params(gdn_a_log[i], gdn_dt_bias[i])
        gd_l, gd_c = _gdn(p_l, _gdn_gate_inputs(p_l), p_c, _gdn_gate_inputs(p_c), gdn_conv_w[i], pa, pd,
                          gdn_norm_g[i][None])

        dsum, scales, gch = _ret_constants(lp)
        rt_l, rt_c = _retention(p_l, p_c, cos, sin, dsum, scales, gch)

        wo16, w116, w216 = w_out[i].astype(bf16), ffn_w_in[i].astype(bf16), ffn_w_out[i].astype(bf16)
        g2 = norm2_g[i][None]
        x = _out_ffn(x, hy_l, gd_l, rt_l, mod_lat, wo16, g2, w116, w216, gf, 512, final=not need_ctx)
        if need_ctx:
            hy_c = _hyena(p_c, lp)
            ctx = _out_ffn(ctx, hy_c, gd_c, rt_c, mod_ctx, wo16, g2, w116, w216, gf, 256, final=False)
    return x
```

```python
import functools
import math

import jax
import jax.numpy as jnp
import numpy as np
from jax import lax
from jax.experimental import pallas as pl
from jax.experimental.pallas import tpu as pltpu

D_MODEL = 1024
DEPTH = 2
GRID_W = 64
EPS = 1e-6
N_MOD = 6

HY_WIDTH = D_MODEL // 4
HY_ORDER = 2
HY_EMB = 33
HY_FAST_DECAY = 0.3
HY_SLOW_DECAY = 1.5
HY_TARGET = 1e-2
GDN_HEADS = D_MODEL // 256
GDN_DK = 128
GDN_DV = 128
RET_HEADS = D_MODEL // 256
RET_DK = 64
RET_DV = 64
ROPE_BASE = 10000.0
FFN_HIDDEN = ((8 * D_MODEL + 3 * 256 - 1) // (3 * 256)) * 256

GDN_CHUNK = 64
GDN_ROWS = 256
GDN_HP = 2
GDN_GROUP = 18
RET_CHUNK = 256
HY_BLOCK = 256
HY_PAIR = 2
HY_IO = 256
LANES = 128
SUBLANES = 8

GDN_QKV = 2 * GDN_HEADS * GDN_DK + GDN_HEADS * GDN_DV
OFF_HY = 0
OFF_QKV = OFF_HY + 3 * HY_WIDTH
OFF_Z = OFF_QKV + GDN_QKV
OFF_RET = OFF_Z + GDN_HEADS * GDN_DV
OFF_AB = OFF_RET + 4 * RET_HEADS * RET_DK
IN_PAD = OFF_AB + LANES

VMEM_LIMIT = 56 * 1024 * 1024
GDN_VMEM_LIMIT = 60 * 1024 * 1024

f32 = jnp.float32
bf16 = jnp.bfloat16


def _cparams(sem):
    return pltpu.CompilerParams(dimension_semantics=sem, vmem_limit_bytes=VMEM_LIMIT)


def _const_spec(shape):
    nd = len(shape)
    return pl.BlockSpec(shape, lambda *_: (0,) * nd, pipeline_mode=pl.Buffered(1))


def _layer_spec(stacked, layer):
    nd = stacked.ndim - 1
    return pl.BlockSpec((None,) + stacked.shape[1:], lambda *_: (layer,) + (0,) * nd,
                        pipeline_mode=pl.Buffered(1))


def _rms(x, g):
    return x * lax.rsqrt(jnp.mean(x * x, axis=-1, keepdims=True) + EPS) * g


def _silu(x):
    return x * jax.nn.sigmoid(x)


def _dot(a, b):
    return jnp.dot(a.astype(bf16), b.astype(bf16), preferred_element_type=f32)


def _dot_nt(a, b):
    return lax.dot_general(a.astype(bf16), b.astype(bf16), (((1,), (1,)), ((), ())),
                           preferred_element_type=f32)


def _dot_tn(a, b):
    return lax.dot_general(a.astype(bf16), b.astype(bf16), (((0,), (0,)), ((), ())),
                           preferred_element_type=f32)


def _mod_kernel(c_ref, w_ref, b_ref, o_ref):
    o_ref[...] = _dot(_silu(c_ref[...]), w_ref[...]) + b_ref[...]


def _mod_vectors(cs, w16, b, layer):
    rows, d = cs.shape
    n = w16.shape[2]
    tn = 1536
    return pl.pallas_call(
        _mod_kernel,
        grid=(n // tn,),
        in_specs=[pl.BlockSpec((rows, d), lambda j: (0, 0)),
                  pl.BlockSpec((None, d, tn), lambda j: (layer, 0, j)),
                  pl.BlockSpec((1, tn), lambda j: (0, j))],
        out_specs=pl.BlockSpec((rows, tn), lambda j: (0, j)),
        out_shape=jax.ShapeDtypeStruct((rows, n), f32),
        compiler_params=_cparams(("parallel",)),
        name="mod_vectors",
    )(cs, w16, b)


def _in_proj_kernel(x_ref, g_ref, mod_ref, w_ref, o_ref):
    x = x_ref[0]
    mod = mod_ref[0]
    h = _rms(x, g_ref[...]) * (1.0 + mod[1:2]) + mod[0:1]
    h16 = h.astype(bf16)
    n = w_ref.shape[1]
    step = 512
    for j in range(0, n, step):
        e = min(j + step, n)
        o_ref[0, :, j:e] = jnp.dot(h16, w_ref[:, j:e], preferred_element_type=f32)


def _in_proj(x, g, mod, w16, layer, tm):
    b, t, d = x.shape
    n = w16.shape[2]
    return pl.pallas_call(
        _in_proj_kernel,
        grid=(b, t // tm),
        in_specs=[pl.BlockSpec((1, tm, d), lambda i, j: (i, j, 0)),
                  _const_spec((1, d)),
                  pl.BlockSpec((1, N_MOD, d), lambda i, j: (i, 0, 0)),
                  _layer_spec(w16, layer)],
        out_specs=pl.BlockSpec((1, tm, n), lambda i, j: (i, j, 0)),
        out_shape=jax.ShapeDtypeStruct((b, t, n), f32),
        compiler_params=_cparams(("parallel", "parallel")),
        name="in_proj",
    )(x, g, mod, w16)


def _out_ffn_kernel(x_ref, hy_ref, gd_ref, rt_ref, mod_ref, wo_ref, g2_ref, w1_ref, w2_ref, gf_ref, o_ref,
                    *, final):
    x = x_ref[0]
    mod = mod_ref[0]
    o0, o1, o2 = HY_WIDTH, HY_WIDTH + GDN_HEADS * GDN_DV, D_MODEL
    y = (jnp.dot(hy_ref[0].astype(bf16), wo_ref[0:o0, :], preferred_element_type=f32)
         + jnp.dot(gd_ref[0].astype(bf16), wo_ref[o0:o1, :], preferred_element_type=f32)
         + jnp.dot(rt_ref[0].astype(bf16), wo_ref[o1:o2, :], preferred_element_type=f32))
    x = x + mod[2:3] * y
    h16 = (_rms(x, g2_ref[...]) * (1.0 + mod[4:5]) + mod[3:4]).astype(bf16)
    step = 256
    acc = jnp.zeros_like(x)
    for j in range(0, FFN_HIDDEN, step):
        gate = jnp.dot(h16, w1_ref[:, j:j + step], preferred_element_type=f32)
        up = jnp.dot(h16, w1_ref[:, FFN_HIDDEN + j:FFN_HIDDEN + j + step], preferred_element_type=f32)
        acc = acc + jnp.dot((_silu(gate) * up).astype(bf16), w2_ref[j:j + step, :], preferred_element_type=f32)
    x = x + mod[5:6] * acc
    if final:
        x = _rms(x, gf_ref[...])
    o_ref[0] = x


def _out_ffn(x, hy, gd, rt, mod, wo16, g2, w116, w216, gf, layer, tm, final):
    b, t, d = x.shape
    tok = lambda w: pl.BlockSpec((1, tm, w), lambda i, j: (i, j, 0))
    return pl.pallas_call(
        functools.partial(_out_ffn_kernel, final=final),
        grid=(b, t // tm),
        in_specs=[tok(d), tok(hy.shape[-1]), tok(gd.shape[-1]), tok(rt.shape[-1]),
                  pl.BlockSpec((1, N_MOD, d), lambda i, j: (i, 0, 0)),
                  _layer_spec(wo16, layer), _const_spec((1, d)), _layer_spec(w116, layer),
                  _layer_spec(w216, layer), _const_spec((1, d))],
        out_specs=tok(d),
        out_shape=jax.ShapeDtypeStruct((b, t, d), f32),
        compiler_params=_cparams(("parallel", "parallel")),
        name="out_ffn",
    )(x, hy, gd, rt, mod, wo16, g2, w116, w216, gf)


def _hyena_kernel(v_ref, x1_ref, x2_ref, k_ref, b_ref, o_ref, acc_ref, *, nb, cb):
    hb = HY_BLOCK
    rb = v_ref.shape[1] // nb

    def conv(u16, kext):
        m = range(len(u16))
        for d in range(-(nb - 1), nb):
            dd = d % (2 * nb)
            t0, t1 = max(0, d), min(nb - 1, nb - 1 + d) + 1
            for i in m:
                win = jnp.broadcast_to(kext[i][:, hb * dd:hb * dd + 2 * hb], (hb, 2 * hb))
                toep = pltpu.roll(win, 0, 1, stride=1, stride_axis=0)[:, hb:].astype(bf16)
                contrib = jnp.dot(u16[i][rb * (t0 - d):rb * (t1 - d)], toep, preferred_element_type=f32)
                if d == -(nb - 1):
                    acc_ref[i] = jnp.zeros(acc_ref.shape[1:], f32)
                acc_ref[i, rb * t0:rb * t1, :] += contrib
        return [acc_ref[i] for i in m]

    def body(g, carry):
        cs = [HY_PAIR * g + i for i in range(HY_PAIR)]
        z = [v_ref[c] for c in cs]
        for n, gate_ref in enumerate((x1_ref, x2_ref)):
            y = conv([x.astype(bf16) for x in z], [k_ref[n, pl.ds(c, 1), :] for c in cs])
            z = [gate_ref[c] * (y[i] + z[i] * b_ref[n, pl.ds(c, 1), :]) for i, c in enumerate(cs)]
        for i, c in enumerate(cs):
            o_ref[c] = z[i]
        return carry

    lax.fori_loop(0, cb // HY_PAIR, body, 0)


def _hyena_conv(p3, kext, bias, nb, cb=SUBLANES):
    c3, r, hb = p3.shape
    c = c3 // 3
    klen = kext.shape[-1]
    sig = lambda s: pl.BlockSpec((cb, r, hb), lambda i, s=s: (s * (c // cb) + i, 0, 0))
    return pl.pallas_call(
        functools.partial(_hyena_kernel, nb=nb, cb=cb),
        grid=(c // cb,),
        in_specs=[sig(0), sig(1), sig(2),
                  pl.BlockSpec((HY_ORDER, cb, klen), lambda i: (0, i, 0)),
                  pl.BlockSpec((HY_ORDER, cb, hb), lambda i: (0, i, 0))],
        out_specs=pl.BlockSpec((cb, r, hb), lambda i: (i, 0, 0)),
        out_shape=jax.ShapeDtypeStruct((c, r, hb), f32),
        scratch_shapes=[pltpu.VMEM((HY_PAIR, r, hb), f32)],
        compiler_params=_cparams(("parallel",)),
        name=f"hyena_conv_nb{nb}",
    )(p3, p3, p3, kext, bias)


def _hyena_prep_kernel(x_ref, before_ref, after_ref, w_ref, b_ref, o_ref):
    hb = HY_BLOCK
    w = w_ref[...]
    row = lax.broadcasted_iota(jnp.int32, x_ref.shape[1:], 0)
    s = pl.program_id(1)
    has_before = s > 0
    has_after = s < pl.num_programs(1) - 1
    for i in range(x_ref.shape[0]):
        x = x_ref[i]
        prev = jnp.where(has_before, before_ref[i, SUBLANES - 1:SUBLANES, :], 0.0)
        nxt = jnp.where(has_after, after_ref[i, 0:1, :], 0.0)
        xp = jnp.where(row == 0, prev, pltpu.roll(x, 1, 0))
        xn = jnp.where(row == hb - 1, nxt, pltpu.roll(x, hb - 1, 0))
        y = xp * w[0:1] + x * w[1:2] + xn * w[2:3] + b_ref[...]
        o_ref[:, i, :] = y.T


def _hyena_prep(p, conv_w, conv_b):
    b, t, _ = p.shape
    hb = HY_BLOCK
    n = 3 * HY_WIDTH
    per = hb // SUBLANES
    last = t // SUBLANES - 1
    assert OFF_HY == 0 and b % SUBLANES == 0
    return pl.pallas_call(
        _hyena_prep_kernel,
        grid=(n // HY_IO, t // hb),
        in_specs=[pl.BlockSpec((b, hb, HY_IO), lambda j, s: (0, s, j)),
                  pl.BlockSpec((b, SUBLANES, HY_IO), lambda j, s: (0, jnp.maximum(s * per - 1, 0), j)),
                  pl.BlockSpec((b, SUBLANES, HY_IO), lambda j, s: (0, jnp.minimum((s + 1) * per, last), j)),
                  pl.BlockSpec((3, HY_IO), lambda j, s: (0, j)),
                  pl.BlockSpec((1, HY_IO), lambda j, s: (0, j))],
        out_specs=pl.BlockSpec((HY_IO, b, hb), lambda j, s: (j, s, 0)),
        out_shape=jax.ShapeDtypeStruct((n, (t // hb) * b, hb), f32),
        compiler_params=_cparams(("parallel", "parallel")),
        name="hyena_prep",
    )(p, p, p, conv_w, conv_b)


def _hyena_post_kernel(z_ref, o_ref):
    for i in range(z_ref.shape[1]):
        o_ref[i] = z_ref[:, i, :].T


def _hyena_post(z, b):
    c, r, hb = z.shape
    nb = r // b
    return pl.pallas_call(
        _hyena_post_kernel,
        grid=(c // HY_IO, nb),
        in_specs=[pl.BlockSpec((HY_IO, b, hb), lambda j, s: (j, s, 0))],
        out_specs=pl.BlockSpec((b, hb, HY_IO), lambda j, s: (0, s, j)),
        out_shape=jax.ShapeDtypeStruct((b, nb * hb, c), f32),
        compiler_params=_cparams(("parallel", "parallel")),
        name="hyena_post",
    )(z)


def _hyena_filter_kernel(z_ref, w1_ref, b1_ref, f1_ref, w2_ref, b2_ref, f2_ref, w3_ref, win_ref, o_ref):
    hi = lax.Precision.HIGHEST
    hb = HY_BLOCK
    t = z_ref.shape[1]
    for side in range(2):
        h = jnp.sin(f1_ref[...] * (jnp.dot(z_ref[side], w1_ref[...], precision=hi,
                                           preferred_element_type=f32) + b1_ref[...]))
        h = jnp.sin(f2_ref[...] * (jnp.dot(h, w2_ref[...], precision=hi, preferred_element_type=f32)
                                   + b2_ref[...]))
        for n in range(HY_ORDER):
            w3 = w3_ref[:, (2 * n + side) * HY_WIDTH:(2 * n + side + 1) * HY_WIDTH]
            k = lax.dot_general(w3, h, (((0,), (1,)), ((), ())), precision=hi,
                                preferred_element_type=f32) * win_ref[side]
            o_ref[n, :, hb + side * t:hb + (side + 1) * t] = k
            if side == 1:
                o_ref[n, :, 0:hb] = k[:, t - hb:]


def _hyena_tables(t_len):
    pos = np.arange(t_len, dtype=np.float64)
    t = np.linspace(0.0, 1.0, t_len)
    bands = (HY_EMB - 1) // 2
    f = np.linspace(1e-4, bands - 1, bands)
    ang = (2.0 * math.pi / t_len) * pos[:, None] * f[None, :]
    z = np.concatenate([t[:, None], np.cos(ang), -np.sin(ang)], axis=-1)
    max_decay = math.log(HY_TARGET) / HY_FAST_DECAY
    min_decay = math.log(HY_TARGET) / HY_SLOW_DECAY
    deltas = np.abs(np.linspace(min_decay, max_decay, HY_WIDTH))
    window = np.exp(-t[:, None] * deltas[None, :])
    neg = lambda x: np.concatenate([x[:1], x[:0:-1]], axis=0)
    zs = np.pad(np.stack([z, neg(z)]), ((0, 0), (0, 0), (0, LANES - HY_EMB)))
    wneg = neg(window)
    wneg[0] = 0.0
    win = np.stack([window, wneg]).transpose(0, 2, 1)
    return jnp.asarray(zs, f32), jnp.asarray(win, f32)


def _hyena_filters(t_len, lp):
    zs, win = _hyena_tables(t_len)
    w1 = jnp.pad(lp['hy_f_w1'], ((0, LANES - HY_EMB), (0, 0)))
    row = lambda x: x[None]
    args = (zs, w1, row(lp['hy_f_b1']), row(lp['hy_f_freq1']), lp['hy_f_w2'], row(lp['hy_f_b2']),
            row(lp['hy_f_freq2']), lp['hy_f_w3'], win)
    klen = 2 * t_len + HY_BLOCK
    return pl.pallas_call(
        _hyena_filter_kernel,
        grid=(1,),
        in_specs=[_const_spec(a.shape) for a in args],
        out_specs=pl.BlockSpec((HY_ORDER, HY_WIDTH, klen), lambda i: (0, 0, 0)),
        out_shape=jax.ShapeDtypeStruct((HY_ORDER, HY_WIDTH, klen), f32),
        compiler_params=_cparams(("arbitrary",)),
        name="hyena_filters",
    )(*args)


def _gdn_kernel(ql_ref, kl_ref, vl_ref, zl_ref, abl_ref, qc_ref, kc_ref, vc_ref, zc_ref, abc_ref,
                cwq_ref, cwk_ref, cwv_ref, pa_ref, pd_ref, ng_ref,
                ol_ref, oc_ref,
                q_s, k_s, v_s, cf_s, at_s, n_s, mp_s, egl_s, o_s, st_s):
    c = GDN_CHUNK
    rb = GDN_ROWS
    tc, tl = qc_ref.shape[1], ql_ref.shape[1]
    ncc, nc = tc // c, (tc + tl) // c
    dk = GDN_DK

    def conv_rows(x_ref, w_ref, r0, t, ls):
        x = x_ref[0, pl.ds(r0, rb), ls]
        before = pl.multiple_of(jnp.maximum(r0 - SUBLANES, 0), SUBLANES)
        after = pl.multiple_of(jnp.minimum(r0 + rb, t - SUBLANES), SUBLANES)
        prev = x_ref[0, pl.ds(before, SUBLANES), ls][SUBLANES - 1:SUBLANES]
        nxt = x_ref[0, pl.ds(after, SUBLANES), ls][0:1]
        prev = jnp.where(r0 > 0, prev, 0.0)
        nxt = jnp.where(r0 + rb < t, nxt, 0.0)
        row = lax.broadcasted_iota(jnp.int32, x.shape, 0)
        xp = jnp.where(row == 0, prev, pltpu.roll(x, 1, 0))
        xn = jnp.where(row == rb - 1, nxt, pltpu.roll(x, rb - 1, 0))
        w = w_ref[:, ls]
        return _silu(xp * w[0:1] + x * w[1:2] + xn * w[2:3])

    def l2n(x):
        return x * lax.rsqrt(jnp.sum(x * x, axis=-1, keepdims=True) + EPS)

    def prep(hh, q_ref, k_ref, v_ref, ab_ref, t, base):
        ls = slice(hh * LANES, (hh + 1) * LANES)

        def body(i, carry):
            r0 = pl.multiple_of(i * rb, rb)
            dst = pl.ds(pl.multiple_of(base + i * rb, rb), rb)
            q_s[dst, :] = l2n(conv_rows(q_ref, cwq_ref, r0, t, ls)) * (dk ** -0.5)
            k_s[dst, :] = l2n(conv_rows(k_ref, cwk_ref, r0, t, ls))
            v_s[dst, :] = conv_rows(v_ref, cwv_ref, r0, t, ls)
            return carry
        lax.fori_loop(0, t // rb, body, 0)

        ab = ab_ref[0, hh]
        ab = jnp.concatenate([ab, jnp.zeros((SUBLANES - ab.shape[0], t), f32)], axis=0)
        rowi = lax.broadcasted_iota(jnp.int32, ab.shape, 0)
        pos = lax.broadcasted_iota(jnp.int32, ab.shape, 1) % c
        tile = lambda p: jnp.concatenate([p] * (t // LANES), axis=1)
        g = -jnp.exp(tile(pa_ref[hh])) * jax.nn.softplus(ab + tile(pd_ref[hh]))
        acc_f, acc_b = g, g
        s = 1
        while s < c:
            acc_f = acc_f + jnp.where(pos >= s, pltpu.roll(acc_f, s, 1), 0.0)
            acc_b = acc_b + jnp.where(pos < c - s, pltpu.roll(acc_b, t - s, 1), 0.0)
            s *= 2
        cf_t = jnp.where(rowi == 0, acc_f, jnp.where(rowi == 1, acc_b, jax.nn.sigmoid(ab)))
        pad = jnp.zeros((LANES - SUBLANES, LANES), f32)
        for j in range(t // LANES):
            blk = jnp.concatenate([cf_t[:, j * LANES:(j + 1) * LANES], pad], axis=0)
            cf_s[base + j * LANES:base + (j + 1) * LANES, :] = blk.T

    lane = lax.broadcasted_iota(jnp.int32, (c, 2 * c), 1)
    row = lax.broadcasted_iota(jnp.int32, (c, 2 * c), 0)
    fwd = lane < c
    col = lane % c
    incl = (fwd & (row >= col)) | (~fwd & (row <= col))
    strict = (fwd & (row > col)) | (~fwd & (row < col))

    def blockdiag(a):
        return jnp.concatenate([jnp.where(fwd, a, 0.0), jnp.where(fwd, 0.0, a)], axis=0).astype(bf16)

    def chunk_group(hh, cis):
        n = range(len(cis))
        xf, xb = 2 * hh, 2 * hh + 1
        rows = [pl.ds(pl.multiple_of(ci * c, c), c) for ci in cis]
        cf = [cf_s[r, :] for r in rows]
        k = [k_s[r, :] for r in rows]
        q = [q_s[r, :] for r in rows]
        bc = lambda x, j: jnp.broadcast_to(x[:, j:j + 1], (c, LANES))
        gf_c, gb_c = [bc(x, 0) for x in cf], [bc(x, 1) for x in cf]
        bf_c, bb_c = [bc(x, 2) for x in cf], [bc(x, 3) for x in cf]
        cf_t = [jnp.concatenate([x, x], axis=0).T for x in cf]
        g_r = [jnp.where(fwd, jnp.broadcast_to(x[0:1], (c, 2 * c)), jnp.broadcast_to(x[1:2], (c, 2 * c)))
               for x in cf_t]
        decay = [jnp.where(incl, jnp.exp(jnp.where(incl, jnp.where(fwd, gf_c[g], gb_c[g]) - g_r[g], 0.0)), 0.0)
                 for g in n]
        kq = [_dot_nt(jnp.concatenate([k[g], q[g]], axis=0), jnp.concatenate([k[g], k[g]], axis=0))
              for g in n]
        lm = [jnp.where(strict, kq[g][:c] * jnp.where(fwd, bf_c[g], bb_c[g]) * decay[g], 0.0) for g in n]
        for g in n:
            at_s[g] = (kq[g][c:] * decay[g]).astype(bf16)
        a = [jnp.dot(x.astype(bf16), blockdiag(x), preferred_element_type=f32) for x in lm]
        tp = [-x for x in lm]
        p = 2
        while 2 * p < c:
            ta = [jnp.dot(jnp.concatenate([tp[g], a[g]], axis=0).astype(bf16), blockdiag(a[g]),
                          preferred_element_type=f32) for g in n]
            tp = [tp[g] + a[g] + ta[g][:c] for g in n]
            a = [x[c:] for x in ta]
            p *= 2
        tp = [tp[g] + a[g] + jnp.dot(tp[g].astype(bf16), blockdiag(a[g]), preferred_element_type=f32)
              for g in n]
        eg_f, eg_b = [jnp.exp(x) for x in gf_c], [jnp.exp(x) for x in gb_c]
        uw = []
        for g in n:
            v = v_s[rows[g], :]
            rhs = jnp.concatenate([jnp.concatenate([v * bf_c[g], k[g] * (bf_c[g] * eg_f[g])], axis=1),
                                   jnp.concatenate([v * bb_c[g], k[g] * (bb_c[g] * eg_b[g])], axis=1)], axis=0)
            uw.append(rhs + jnp.dot(blockdiag(tp[g]), rhs.astype(bf16), preferred_element_type=f32))
        gl = [(gf_c[g][c - 1:c], gb_c[g][0:1]) for g in n]
        res = []
        for g in n:
            attn = at_s[g]
            for d, (half, g_c) in enumerate(((slice(0, c), gf_c[g]), (slice(c, 2 * c), gb_c[g]))):
                kgt = (k[g] * jnp.exp(gl[g][d] - g_c)).T.astype(bf16)
                res.append(jnp.dot(jnp.concatenate([kgt, attn[:, half]], axis=0), uw[g][half].astype(bf16),
                                   preferred_element_type=f32))
        for g in n:
            ci = cis[g]
            for d, (x, eg) in enumerate(((xf, eg_f[g]), (xb, eg_b[g]))):
                r = res[2 * g + d]
                n_s[x, ci] = r[:dk, :dk]
                mp_s[x, ci] = jnp.concatenate([r[:dk, dk:], q[g] * eg - r[dk:, dk:]], axis=0).astype(bf16)
            egl_s[hh, ci] = jnp.concatenate([jnp.exp(gl[g][0]), jnp.exp(gl[g][1]),
                                             jnp.zeros((SUBLANES - 2, LANES), f32)], axis=0)
            o_s[hh, rows[g], :] = res[2 * g][dk:, :dk] + res[2 * g + 1][dk:, :dk]

    for hh in range(GDN_HP):
        prep(hh, qc_ref, kc_ref, vc_ref, abc_ref, tc, 0)
        prep(hh, ql_ref, kl_ref, vl_ref, abl_ref, tl, tc)

        def chunk_body(i, carry, hh=hh):
            chunk_group(hh, [GDN_GROUP * i + g for g in range(GDN_GROUP)])
            return carry

        lax.fori_loop(0, nc // GDN_GROUP, chunk_body, 0)

    st_s[...] = jnp.zeros_like(st_s)
    chains = [(hh, d) for hh in range(GDN_HP) for d in range(2)]

    def scan_body(i, carry):
        ci = (i, jnp.where(i < ncc, ncc - 1 - i, nc + ncc - 1 - i))
        rows = [pl.ds(pl.multiple_of(x * c, c), c) for x in ci]
        n = range(len(chains))
        s = [st_s[x] for x in n]
        res = [jnp.dot(mp_s[x, ci[chains[x][1]]], s[x].astype(bf16), preferred_element_type=f32) for x in n]
        for x in n:
            hh, d = chains[x]
            st_s[x] = s[x] * egl_s[hh, ci[d]][d:d + 1] + n_s[x, ci[d]] - res[x][:dk]
            o_s[hh, rows[d], :] += res[x][dk:]
        return carry

    lax.fori_loop(0, nc, scan_body, 0)

    def finish(hh, z_ref, o_ref, t, base):
        ls = slice(hh * LANES, (hh + 1) * LANES)

        def body(i, carry):
            r0 = pl.ds(pl.multiple_of(i * rb, rb), rb)
            o = o_s[hh, pl.ds(pl.multiple_of(base + i * rb, rb), rb), :]
            y = o * lax.rsqrt(jnp.mean(o * o, axis=-1, keepdims=True) + EPS) * ng_ref[...]
            o_ref[0, r0, ls] = y * _silu(z_ref[0, r0, ls])
            return carry
        lax.fori_loop(0, t // rb, body, 0)

    for hh in range(GDN_HP):
        finish(hh, zc_ref, oc_ref, tc, 0)
        finish(hh, zl_ref, ol_ref, tl, tc)


def _gdn(p_l, ab_l, p_c, ab_c, conv_w, pa, pd, norm_g):
    b, tl, _ = p_l.shape
    tc = p_c.shape[1]
    h, c, hp = GDN_HEADS, GDN_CHUNK, GDN_HP
    nc = (tl + tc) // c
    assert tl % GDN_ROWS == 0 and tc % GDN_ROWS == 0 and nc % GDN_GROUP == 0
    qb, zb = OFF_QKV // LANES, OFF_Z // LANES
    assert h % hp == 0 and qb % hp == 0 and zb % hp == 0
    w = hp * LANES

    def specs(t):
        heads = lambda off: pl.BlockSpec((1, t, w), lambda i, j, off=off // hp: (i, 0, off + j))
        return [heads(qb), heads(qb + h), heads(qb + 2 * h), heads(zb),
                pl.BlockSpec((1, hp, 4, t), lambda i, j: (i, j, 0, 0))]

    cw = lambda off: pl.BlockSpec((3, w), lambda i, j, off=off // hp: (0, off + j))
    par = pl.BlockSpec((hp, SUBLANES, LANES), lambda i, j: (j, 0, 0))
    out = lambda t: pl.BlockSpec((1, t, w), lambda i, j: (i, 0, j))
    tt = tl + tc
    return pl.pallas_call(
        _gdn_kernel,
        grid=(b, h // hp),
        in_specs=specs(tl) + specs(tc) + [cw(0), cw(h), cw(2 * h), par, par, _const_spec((1, GDN_DV))],
        out_specs=[out(tl), out(tc)],
        out_shape=[jax.ShapeDtypeStruct((b, tl, h * GDN_DV), f32),
                   jax.ShapeDtypeStruct((b, tc, h * GDN_DV), f32)],
        scratch_shapes=[pltpu.VMEM((tt, GDN_DK), f32), pltpu.VMEM((tt, GDN_DK), f32),
                        pltpu.VMEM((tt, GDN_DV), f32), pltpu.VMEM((tt, LANES), f32),
                        pltpu.VMEM((GDN_GROUP, c, 2 * c), bf16),
                        pltpu.VMEM((2 * hp, nc, GDN_DK, GDN_DV), f32),
                        pltpu.VMEM((2 * hp, nc, GDN_DK + c, GDN_DV), bf16),
                        pltpu.VMEM((hp, nc, SUBLANES, LANES), f32), pltpu.VMEM((hp, tt, GDN_DV), f32),
                        pltpu.VMEM((2 * hp, GDN_DK, GDN_DV), f32)],
        compiler_params=pltpu.CompilerParams(dimension_semantics=("parallel", "parallel"),
                                             vmem_limit_bytes=GDN_VMEM_LIMIT),
        name="gated_deltanet",
    )(p_l, p_l, p_l, p_l, ab_l, p_c, p_c, p_c, p_c, ab_c, conv_w, conv_w, conv_w, pa, pd, norm_g)


def _ret_kernel(ql_ref, kl_ref, vl_ref, gl_ref, qc_ref, kc_ref, vc_ref, gc_ref,
                cos_ref, sin_ref, dsum_ref, sc_ref, gch_ref,
                ol_ref, oc_ref, sf_ref, sb_ref, q_s, k_s, o_s):
    c = RET_CHUNK
    w = RET_HEADS * RET_DK
    ri = lax.broadcasted_iota(jnp.int32, (w, w), 0) // RET_DK
    ci = lax.broadcasted_iota(jnp.int32, (w, w), 1) // RET_DK
    blockdiag = ri == ci
    lane_head = lax.broadcasted_iota(jnp.int32, (c, w), 1) // RET_DK

    def intra(q, k, v):
        out = jnp.zeros((c, w), f32)
        for h in range(RET_HEADS):
            m = lane_head == h
            scores = _dot_nt(jnp.where(m, q, 0.0), k) * dsum_ref[h]
            out = out + _dot(scores, jnp.where(m, v, 0.0))
        return out

    def update(s_ref, k, v, d):
        new = jnp.where(blockdiag, _dot_tn(k * sc_ref[2 * d + 1], v), 0.0)
        s_ref[...] = s_ref[...] * gch_ref[d] + new

    first_half = lax.broadcasted_iota(jnp.int32, (c, w), 1) % RET_DK < RET_DK // 2
    avg = jnp.where(blockdiag, 1.0 / RET_DV, 0.0).astype(bf16)

    def rope(x, r):
        swapped = jnp.where(first_half, pltpu.roll(x, w - RET_DK // 2, 1), pltpu.roll(x, RET_DK // 2, 1))
        return x * cos_ref[r, :] + swapped * sin_ref[r, :]

    def head_mean(x):
        hi = x.astype(bf16)
        lo = (x - hi.astype(f32)).astype(bf16)
        return jnp.dot(hi, avg, preferred_element_type=f32) + jnp.dot(lo, avg, preferred_element_type=f32)

    def finish(o, gate):
        d = o - head_mean(o)
        return d * lax.rsqrt(head_mean(d * d) + EPS) * _silu(gate)

    sf_ref[...] = jnp.zeros_like(sf_ref)
    sb_ref[...] = jnp.zeros_like(sb_ref)
    qc, kc, vc = qc_ref[0], kc_ref[0] * (RET_DK ** -0.5), vc_ref[0]
    oc_ref[0] = finish(intra(qc, kc, vc), gc_ref[0])
    update(sf_ref, kc, vc, 0)
    update(sb_ref, kc, vc, 1)
    n = ql_ref.shape[1] // c
    for i in range(n):
        r = slice(i * c, (i + 1) * c)
        q = rope(ql_ref[0, r, :], r)
        k = rope(kl_ref[0, r, :] * (RET_DK ** -0.5), r)
        v = vl_ref[0, r, :]
        q_s[r, :] = q
        k_s[r, :] = k
        o_s[r, :] = intra(q, k, v) + _dot(q * sc_ref[0], sf_ref[...])
        update(sf_ref, k, v, 0)
    for i in range(n - 1, -1, -1):
        r = slice(i * c, (i + 1) * c)
        o = o_s[r, :] + _dot(q_s[r, :] * sc_ref[2], sb_ref[...])
        ol_ref[0, r, :] = finish(o, gl_ref[0, r, :])
        update(sb_ref, k_s[r, :], vl_ref[0, r, :], 1)


def _retention(p_l, p_c, cos, sin, dsum, scales, gch):
    b, tl, _ = p_l.shape
    tc = p_c.shape[1]
    w = RET_HEADS * RET_DK
    assert OFF_RET % w == 0
    blk = lambda t, j: pl.BlockSpec((1, t, w), lambda i, j=j: (i, 0, OFF_RET // w + j))
    tok = lambda t: pl.BlockSpec((1, t, w), lambda i: (i, 0, 0))
    return pl.pallas_call(
        _ret_kernel,
        grid=(b,),
        in_specs=[blk(tl, j) for j in range(4)] + [blk(tc, j) for j in range(4)]
        + [_const_spec(x.shape) for x in (cos, sin, dsum, scales, gch)],
        out_specs=[tok(tl), tok(tc)],
        out_shape=[jax.ShapeDtypeStruct((b, tl, w), f32), jax.ShapeDtypeStruct((b, tc, w), f32)],
        scratch_shapes=[pltpu.VMEM((w, w), f32), pltpu.VMEM((w, w), f32),
                        pltpu.VMEM((tl, w), f32), pltpu.VMEM((tl, w), f32), pltpu.VMEM((tl, w), f32)],
        compiler_params=_cparams(("parallel",)),
        name="retention",
    )(p_l, p_l, p_l, p_l, p_c, p_c, p_c, p_c, cos, sin, dsum, scales, gch)


def _hyena(p, lp):
    b, t, _ = p.shape
    hb, c = HY_BLOCK, HY_WIDTH
    nb = t // hb
    p3 = _hyena_prep(p, lp['hy_conv_w'], lp['hy_conv_b'][None])
    bias = jnp.broadcast_to(lp['hy_bias'][:, :, None], (HY_ORDER, c, hb))
    z = _hyena_conv(p3, _hyena_filters(t, lp), bias, nb)
    return _hyena_post(z, b)


def _gdn_gate_inputs(p):
    b, t, _ = p.shape
    ab = p[..., OFF_AB:OFF_AB + 4 * GDN_HEADS].reshape(b, t, 4, GDN_HEADS)
    return ab.transpose(0, 3, 2, 1)


def _gdn_gate_params(a_log, dt_bias):
    def rows(x):
        x = jnp.pad(x.T, ((0, 0), (0, SUBLANES - 2)))
        return jnp.broadcast_to(x[:, :, None], (GDN_HEADS, SUBLANES, LANES))
    return rows(a_log), rows(dt_bias)


def _rope_tables(t_lat):
    rows_n = t_lat // GRID_W
    row = np.repeat(np.arange(rows_n, dtype=np.float64), GRID_W)
    colp = np.tile(np.arange(GRID_W, dtype=np.float64), rows_n)
    nf = RET_DK // 4
    inv = ROPE_BASE ** (-np.arange(nf, dtype=np.float64) / nf)
    ang = np.concatenate([row[:, None] * inv, colp[:, None] * inv], axis=-1)
    cos, sin = np.cos(ang), np.sin(ang)
    return (jnp.asarray(np.tile(np.concatenate([cos, cos], axis=-1), (1, RET_HEADS)), f32),
            jnp.asarray(np.tile(np.concatenate([-sin, sin], axis=-1), (1, RET_HEADS)), f32))


def _ret_constants(lp):
    c = RET_CHUNK
    lg = jax.nn.log_sigmoid(lp['ret_decay_logit'])
    idx = jnp.arange(c, dtype=f32)
    rel = idx[:, None] - idx[None, :]
    d_f = jnp.exp(jnp.where(rel >= 0, rel * lg[0][:, None, None], -jnp.inf))
    d_b = jnp.exp(jnp.where(rel <= 0, -rel * lg[1][:, None, None], -jnp.inf))
    lane = lambda x: jnp.repeat(x, RET_DK, axis=-1)
    scales = jnp.stack([lane(jnp.exp((idx + 1.0)[:, None] * lg[0])),
                        lane(jnp.exp((c - 1.0 - idx)[:, None] * lg[0])),
                        lane(jnp.exp((c - idx)[:, None] * lg[1])),
                        lane(jnp.exp(idx[:, None] * lg[1]))])
    gch = lane(jnp.exp(c * lg))[:, None, :]
    return d_f + d_b, scales, gch


def _reorder_w_in(w):
    h2 = 2 * GDN_HEADS
    a0 = OFF_Z + GDN_HEADS * GDN_DV
    parts = [w[..., :a0], w[..., a0 + 2 * h2:], w[..., a0:a0 + 2 * h2],
             jnp.zeros(w.shape[:-1] + (LANES - 2 * h2,), w.dtype)]
    return jnp.concatenate(parts, axis=-1)


def kernel(x, c, ctx, c_ctx, mod_w, mod_b, norm1_g, w_in, hy_conv_w, hy_conv_b, hy_f_w1, hy_f_b1, hy_f_freq1, hy_f_w2, hy_f_b2, hy_f_freq2, hy_f_w3, hy_bias, gdn_conv_w, gdn_a_log, gdn_dt_bias, gdn_norm_g, ret_decay_logit, w_out, norm2_g, ffn_w_in, ffn_w_out, final_norm_g):
    bsz, t_lat, d = x.shape
    cos, sin = _rope_tables(t_lat)
    pad_rows = (-(bsz + 1)) % SUBLANES
    cs = jnp.concatenate([c, c_ctx[None], jnp.zeros((pad_rows, d), f32)], axis=0)
    gf = final_norm_g[None]
    mod_w16, w_in16 = mod_w.astype(bf16), _reorder_w_in(w_in).astype(bf16)
    wo16, w116, w216 = w_out.astype(bf16), ffn_w_in.astype(bf16), ffn_w_out.astype(bf16)
    for i in range(DEPTH):
        need_ctx = i < DEPTH - 1
        lp = {'hy_conv_w': hy_conv_w[i], 'hy_conv_b': hy_conv_b[i], 'hy_f_w1': hy_f_w1[i],
              'hy_f_b1': hy_f_b1[i], 'hy_f_freq1': hy_f_freq1[i], 'hy_f_w2': hy_f_w2[i],
              'hy_f_b2': hy_f_b2[i], 'hy_f_freq2': hy_f_freq2[i], 'hy_f_w3': hy_f_w3[i],
              'hy_bias': hy_bias[i], 'ret_decay_logit': ret_decay_logit[i]}
        mod = _mod_vectors(cs, mod_w16, mod_b[i][None], i)
        mod_lat = mod[:bsz].reshape(bsz, N_MOD, d)
        mod_ctx = jnp.broadcast_to(mod[bsz].reshape(1, N_MOD, d), (bsz, N_MOD, d))
        g1 = norm1_g[i][None]
        p_l = _in_proj(x, g1, mod_lat, w_in16, i, 512)
        p_c = _in_proj(ctx, g1, mod_ctx, w_in16, i, 256)

        hy_l = _hyena(p_l, lp)
        pa, pd = _gdn_gate_params(gdn_a_log[i], gdn_dt_bias[i])
        gd_l, gd_c = _gdn(p_l, _gdn_gate_inputs(p_l), p_c, _gdn_gate_inputs(p_c), gdn_conv_w[i], pa, pd,
                          gdn_norm_g[i][None])

        dsum, scales, gch = _ret_constants(lp)
        rt_l, rt_c = _retention(p_l, p_c, cos, sin, dsum, scales, gch)

        g2 = norm2_g[i][None]
        x = _out_ffn(x, hy_l, gd_l, rt_l, mod_lat, wo16, g2, w116, w216, gf, i, 512, final=not need_ctx)
        if need_ctx:
            hy_c = _hyena(p_c, lp)
            ctx = _out_ffn(ctx, hy_c, gd_c, rt_c, mod_ctx, wo16, g2, w116, w216, gf, i, 256, final=False)
    return x
```

```python
import functools
import math

import jax
import jax.numpy as jnp
import numpy as np
from jax import lax
from jax.experimental import pallas as pl
from jax.experimental.pallas import tpu as pltpu

D_MODEL = 1024
DEPTH = 2
GRID_W = 64
EPS = 1e-6
N_MOD = 6

HY_WIDTH = D_MODEL // 4
HY_ORDER = 2
HY_EMB = 33
HY_FAST_DECAY = 0.3
HY_SLOW_DECAY = 1.5
HY_TARGET = 1e-2
GDN_HEADS = D_MODEL // 256
GDN_DK = 128
GDN_DV = 128
RET_HEADS = D_MODEL // 256
RET_DK = 64
RET_DV = 64
ROPE_BASE = 10000.0
FFN_HIDDEN = ((8 * D_MODEL + 3 * 256 - 1) // (3 * 256)) * 256

GDN_CHUNK = 64
GDN_ROWS = 256
GDN_HP = 2
GDN_GROUP = 18
RET_CHUNK = 256
HY_BLOCK = 256
HY_PAIR = 2
HY_IO = 256
LANES = 128
SUBLANES = 8

GDN_QKV = 2 * GDN_HEADS * GDN_DK + GDN_HEADS * GDN_DV
OFF_HY = 0
OFF_QKV = OFF_HY + 3 * HY_WIDTH
OFF_Z = OFF_QKV + GDN_QKV
OFF_RET = OFF_Z + GDN_HEADS * GDN_DV
OFF_AB = OFF_RET + 4 * RET_HEADS * RET_DK
IN_PAD = OFF_AB + LANES

VMEM_LIMIT = 56 * 1024 * 1024
GDN_VMEM_LIMIT = 60 * 1024 * 1024

f32 = jnp.float32
bf16 = jnp.bfloat16


def _cparams(sem):
    return pltpu.CompilerParams(dimension_semantics=sem, vmem_limit_bytes=VMEM_LIMIT)


def _const_spec(shape):
    nd = len(shape)
    return pl.BlockSpec(shape, lambda *_: (0,) * nd, pipeline_mode=pl.Buffered(1))


def _layer_spec(stacked, layer):
    nd = stacked.ndim - 1
    return pl.BlockSpec((None,) + stacked.shape[1:], lambda *_: (layer,) + (0,) * nd,
                        pipeline_mode=pl.Buffered(1))


def _rms(x, g):
    return x * lax.rsqrt(jnp.mean(x * x, axis=-1, keepdims=True) + EPS) * g


def _silu(x):
    return x * jax.nn.sigmoid(x)


def _dot(a, b):
    return jnp.dot(a.astype(bf16), b.astype(bf16), preferred_element_type=f32)


def _dot_nt(a, b):
    return lax.dot_general(a.astype(bf16), b.astype(bf16), (((1,), (1,)), ((), ())),
                           preferred_element_type=f32)


def _dot_tn(a, b):
    return lax.dot_general(a.astype(bf16), b.astype(bf16), (((0,), (0,)), ((), ())),
                           preferred_element_type=f32)


def _mod_kernel(c_ref, w_ref, b_ref, o_ref):
    o_ref[...] = _dot(_silu(c_ref[...]), w_ref[...]) + b_ref[...]


def _mod_vectors(cs, w16, b, layer):
    rows, d = cs.shape
    n = w16.shape[2]
    tn = 1536
    return pl.pallas_call(
        _mod_kernel,
        grid=(n // tn,),
        in_specs=[pl.BlockSpec((rows, d), lambda j: (0, 0)),
                  pl.BlockSpec((None, d, tn), lambda j: (layer, 0, j)),
                  pl.BlockSpec((1, tn), lambda j: (0, j))],
        out_specs=pl.BlockSpec((rows, tn), lambda j: (0, j)),
        out_shape=jax.ShapeDtypeStruct((rows, n), f32),
        compiler_params=_cparams(("parallel",)),
        name="mod_vectors",
    )(cs, w16, b)


def _in_proj_kernel(x_ref, g_ref, mod_ref, w_ref, o_ref):
    x = x_ref[0]
    mod = mod_ref[0]
    h = _rms(x, g_ref[...]) * (1.0 + mod[1:2]) + mod[0:1]
    h16 = h.astype(bf16)
    n = w_ref.shape[1]
    step = 512
    for j in range(0, n, step):
        e = min(j + step, n)
        o_ref[0, :, j:e] = jnp.dot(h16, w_ref[:, j:e], preferred_element_type=f32)


def _in_proj(x, g, mod, w16, layer, tm):
    b, t, d = x.shape
    n = w16.shape[2]
    return pl.pallas_call(
        _in_proj_kernel,
        grid=(b, t // tm),
        in_specs=[pl.BlockSpec((1, tm, d), lambda i, j: (i, j, 0)),
                  _const_spec((1, d)),
                  pl.BlockSpec((1, N_MOD, d), lambda i, j: (i, 0, 0)),
                  _layer_spec(w16, layer)],
        out_specs=pl.BlockSpec((1, tm, n), lambda i, j: (i, j, 0)),
        out_shape=jax.ShapeDtypeStruct((b, t, n), f32),
        compiler_params=_cparams(("parallel", "parallel")),
        name="in_proj",
    )(x, g, mod, w16)


def _out_ffn_kernel(x_ref, hy_ref, gd_ref, rt_ref, mod_ref, wo_ref, g2_ref, w1_ref, w2_ref, gf_ref, o_ref,
                    *, final):
    x = x_ref[0]
    mod = mod_ref[0]
    o0, o1, o2 = HY_WIDTH, HY_WIDTH + GDN_HEADS * GDN_DV, D_MODEL
    y = (jnp.dot(hy_ref[0].astype(bf16), wo_ref[0:o0, :], preferred_element_type=f32)
         + jnp.dot(gd_ref[0].astype(bf16), wo_ref[o0:o1, :], preferred_element_type=f32)
         + jnp.dot(rt_ref[0].astype(bf16), wo_ref[o1:o2, :], preferred_element_type=f32))
    x = x + mod[2:3] * y
    h16 = (_rms(x, g2_ref[...]) * (1.0 + mod[4:5]) + mod[3:4]).astype(bf16)
    step = 256
    acc = jnp.zeros_like(x)
    for j in range(0, FFN_HIDDEN, step):
        gate = jnp.dot(h16, w1_ref[:, j:j + step], preferred_element_type=f32)
        up = jnp.dot(h16, w1_ref[:, FFN_HIDDEN + j:FFN_HIDDEN + j + step], preferred_element_type=f32)
        acc = acc + jnp.dot((_silu(gate) * up).astype(bf16), w2_ref[j:j + step, :], preferred_element_type=f32)
    x = x + mod[5:6] * acc
    if final:
        x = _rms(x, gf_ref[...])
    o_ref[0] = x


def _out_ffn(x, hy, gd, rt, mod, wo16, g2, w116, w216, gf, layer, tm, final):
    b, t, d = x.shape
    tok = lambda w: pl.BlockSpec((1, tm, w), lambda i, j: (i, j, 0))
    return pl.pallas_call(
        functools.partial(_out_ffn_kernel, final=final),
        grid=(b, t // tm),
        in_specs=[tok(d), tok(hy.shape[-1]), tok(gd.shape[-1]), tok(rt.shape[-1]),
                  pl.BlockSpec((1, N_MOD, d), lambda i, j: (i, 0, 0)),
                  _layer_spec(wo16, layer), _const_spec((1, d)), _layer_spec(w116, layer),
                  _layer_spec(w216, layer), _const_spec((1, d))],
        out_specs=tok(d),
        out_shape=jax.ShapeDtypeStruct((b, t, d), f32),
        compiler_params=_cparams(("parallel", "parallel")),
        name="out_ffn",
    )(x, hy, gd, rt, mod, wo16, g2, w116, w216, gf)


def _hyena_kernel(v_ref, x1_ref, x2_ref, k_ref, b_ref, o_ref, acc_ref, *, nb, cb):
    hb = HY_BLOCK
    rb = v_ref.shape[1] // nb

    def conv(u16, kext):
        m = range(len(u16))
        for d in range(-(nb - 1), nb):
            dd = d % (2 * nb)
            t0, t1 = max(0, d), min(nb - 1, nb - 1 + d) + 1
            for i in m:
                win = jnp.broadcast_to(kext[i][:, hb * dd:hb * dd + 2 * hb], (hb, 2 * hb))
                toep = pltpu.roll(win, 0, 1, stride=1, stride_axis=0)[:, hb:].astype(bf16)
                contrib = jnp.dot(u16[i][rb * (t0 - d):rb * (t1 - d)], toep, preferred_element_type=f32)
                if d == -(nb - 1):
                    acc_ref[i] = jnp.zeros(acc_ref.shape[1:], f32)
                acc_ref[i, rb * t0:rb * t1, :] += contrib
        return [acc_ref[i] for i in m]

    def body(g, carry):
        cs = [HY_PAIR * g + i for i in range(HY_PAIR)]
        z = [v_ref[c] for c in cs]
        for n, gate_ref in enumerate((x1_ref, x2_ref)):
            y = conv([x.astype(bf16) for x in z], [k_ref[n, pl.ds(c, 1), :] for c in cs])
            z = [gate_ref[c] * (y[i] + z[i] * b_ref[n, pl.ds(c, 1), :]) for i, c in enumerate(cs)]
        for i, c in enumerate(cs):
            o_ref[c] = z[i]
        return carry

    lax.fori_loop(0, cb // HY_PAIR, body, 0)


def _hyena_conv(p3, kext, bias, nb, cb=SUBLANES):
    c3, r, hb = p3.shape
    c = c3 // 3
    klen = kext.shape[-1]
    sig = lambda s: pl.BlockSpec((cb, r, hb), lambda i, s=s: (s * (c // cb) + i, 0, 0))
    return pl.pallas_call(
        functools.partial(_hyena_kernel, nb=nb, cb=cb),
        grid=(c // cb,),
        in_specs=[sig(0), sig(1), sig(2),
                  pl.BlockSpec((HY_ORDER, cb, klen), lambda i: (0, i, 0)),
                  pl.BlockSpec((HY_ORDER, cb, hb), lambda i: (0, i, 0))],
        out_specs=pl.BlockSpec((cb, r, hb), lambda i: (i, 0, 0)),
        out_shape=jax.ShapeDtypeStruct((c, r, hb), f32),
        scratch_shapes=[pltpu.VMEM((HY_PAIR, r, hb), f32)],
        compiler_params=_cparams(("parallel",)),
        name=f"hyena_conv_nb{nb}",
    )(p3, p3, p3, kext, bias)


def _hyena_prep_kernel(x_ref, before_ref, after_ref, w_ref, b_ref, o_ref):
    hb = HY_BLOCK
    w = w_ref[...]
    row = lax.broadcasted_iota(jnp.int32, x_ref.shape[1:], 0)
    s = pl.program_id(1)
    has_before = s > 0
    has_after = s < pl.num_programs(1) - 1
    for i in range(x_ref.shape[0]):
        x = x_ref[i]
        prev = jnp.where(has_before, before_ref[i, SUBLANES - 1:SUBLANES, :], 0.0)
        nxt = jnp.where(has_after, after_ref[i, 0:1, :], 0.0)
        xp = jnp.where(row == 0, prev, pltpu.roll(x, 1, 0))
        xn = jnp.where(row == hb - 1, nxt, pltpu.roll(x, hb - 1, 0))
        y = xp * w[0:1] + x * w[1:2] + xn * w[2:3] + b_ref[...]
        o_ref[:, i, :] = y.T


def _hyena_prep(p, conv_w, conv_b):
    b, t, _ = p.shape
    hb = HY_BLOCK
    n = 3 * HY_WIDTH
    per = hb // SUBLANES
    last = t // SUBLANES - 1
    assert OFF_HY == 0 and b % SUBLANES == 0
    return pl.pallas_call(
        _hyena_prep_kernel,
        grid=(n // HY_IO, t // hb),
        in_specs=[pl.BlockSpec((b, hb, HY_IO), lambda j, s: (0, s, j)),
                  pl.BlockSpec((b, SUBLANES, HY_IO), lambda j, s: (0, jnp.maximum(s * per - 1, 0), j)),
                  pl.BlockSpec((b, SUBLANES, HY_IO), lambda j, s: (0, jnp.minimum((s + 1) * per, last), j)),
                  pl.BlockSpec((3, HY_IO), lambda j, s: (0, j)),
                  pl.BlockSpec((1, HY_IO), lambda j, s: (0, j))],
        out_specs=pl.BlockSpec((HY_IO, b, hb), lambda j, s: (j, s, 0)),
        out_shape=jax.ShapeDtypeStruct((n, (t // hb) * b, hb), f32),
        compiler_params=_cparams(("parallel", "parallel")),
        name="hyena_prep",
    )(p, p, p, conv_w, conv_b)


def _hyena_post_kernel(z_ref, o_ref):
    for i in range(z_ref.shape[1]):
        o_ref[i] = z_ref[:, i, :].T


def _hyena_post(z, b):
    c, r, hb = z.shape
    nb = r // b
    return pl.pallas_call(
        _hyena_post_kernel,
        grid=(c // HY_IO, nb),
        in_specs=[pl.BlockSpec((HY_IO, b, hb), lambda j, s: (j, s, 0))],
        out_specs=pl.BlockSpec((b, hb, HY_IO), lambda j, s: (0, s, j)),
        out_shape=jax.ShapeDtypeStruct((b, nb * hb, c), f32),
        compiler_params=_cparams(("parallel", "parallel")),
        name="hyena_post",
    )(z)


def _hyena_filter_kernel(z_ref, w1_ref, b1_ref, f1_ref, w2_ref, b2_ref, f2_ref, w3_ref, win_ref, o_ref):
    hi = lax.Precision.HIGHEST
    hb = HY_BLOCK
    t = z_ref.shape[1]
    for side in range(2):
        h = jnp.sin(f1_ref[...] * (jnp.dot(z_ref[side], w1_ref[...], precision=hi,
                                           preferred_element_type=f32) + b1_ref[...]))
        h = jnp.sin(f2_ref[...] * (jnp.dot(h, w2_ref[...], precision=hi, preferred_element_type=f32)
                                   + b2_ref[...]))
        for n in range(HY_ORDER):
            w3 = w3_ref[:, (2 * n + side) * HY_WIDTH:(2 * n + side + 1) * HY_WIDTH]
            k = lax.dot_general(w3, h, (((0,), (1,)), ((), ())), precision=hi,
                                preferred_element_type=f32) * win_ref[side]
            o_ref[n, :, hb + side * t:hb + (side + 1) * t] = k
            if side == 1:
                o_ref[n, :, 0:hb] = k[:, t - hb:]


def _hyena_tables(t_len):
    pos = np.arange(t_len, dtype=np.float64)
    t = np.linspace(0.0, 1.0, t_len)
    bands = (HY_EMB - 1) // 2
    f = np.linspace(1e-4, bands - 1, bands)
    ang = (2.0 * math.pi / t_len) * pos[:, None] * f[None, :]
    z = np.concatenate([t[:, None], np.cos(ang), -np.sin(ang)], axis=-1)
    max_decay = math.log(HY_TARGET) / HY_FAST_DECAY
    min_decay = math.log(HY_TARGET) / HY_SLOW_DECAY
    deltas = np.abs(np.linspace(min_decay, max_decay, HY_WIDTH))
    window = np.exp(-t[:, None] * deltas[None, :])
    neg = lambda x: np.concatenate([x[:1], x[:0:-1]], axis=0)
    zs = np.pad(np.stack([z, neg(z)]), ((0, 0), (0, 0), (0, LANES - HY_EMB)))
    wneg = neg(window)
    wneg[0] = 0.0
    win = np.stack([window, wneg]).transpose(0, 2, 1)
    return jnp.asarray(zs, f32), jnp.asarray(win, f32)


def _hyena_filters(t_len, lp):
    zs, win = _hyena_tables(t_len)
    w1 = jnp.pad(lp['hy_f_w1'], ((0, LANES - HY_EMB), (0, 0)))
    row = lambda x: x[None]
    args = (zs, w1, row(lp['hy_f_b1']), row(lp['hy_f_freq1']), lp['hy_f_w2'], row(lp['hy_f_b2']),
            row(lp['hy_f_freq2']), lp['hy_f_w3'], win)
    klen = 2 * t_len + HY_BLOCK
    return pl.pallas_call(
        _hyena_filter_kernel,
        grid=(1,),
        in_specs=[_const_spec(a.shape) for a in args],
        out_specs=pl.BlockSpec((HY_ORDER, HY_WIDTH, klen), lambda i: (0, 0, 0)),
        out_shape=jax.ShapeDtypeStruct((HY_ORDER, HY_WIDTH, klen), f32),
        compiler_params=_cparams(("arbitrary",)),
        name="hyena_filters",
    )(*args)


def _gdn_kernel(ql_ref, kl_ref, vl_ref, zl_ref, abl_ref, qc_ref, kc_ref, vc_ref, zc_ref, abc_ref,
                cwq_ref, cwk_ref, cwv_ref, pa_ref, pd_ref, ng_ref,
                ol_ref, oc_ref,
                q_s, k_s, v_s, cf_s, at_s, n_s, mp_s, egl_s, o_s, st_s):
    c = GDN_CHUNK
    rb = GDN_ROWS
    tc, tl = qc_ref.shape[1], ql_ref.shape[1]
    ncc, nc = tc // c, (tc + tl) // c
    dk = GDN_DK

    def conv_rows(x_ref, w_ref, r0, t, ls):
        x = x_ref[0, pl.ds(r0, rb), ls]
        before = pl.multiple_of(jnp.maximum(r0 - SUBLANES, 0), SUBLANES)
        after = pl.multiple_of(jnp.minimum(r0 + rb, t - SUBLANES), SUBLANES)
        prev = x_ref[0, pl.ds(before, SUBLANES), ls][SUBLANES - 1:SUBLANES]
        nxt = x_ref[0, pl.ds(after, SUBLANES), ls][0:1]
        prev = jnp.where(r0 > 0, prev, 0.0)
        nxt = jnp.where(r0 + rb < t, nxt, 0.0)
        row = lax.broadcasted_iota(jnp.int32, x.shape, 0)
        xp = jnp.where(row == 0, prev, pltpu.roll(x, 1, 0))
        xn = jnp.where(row == rb - 1, nxt, pltpu.roll(x, rb - 1, 0))
        w = w_ref[:, ls]
        return _silu(xp * w[0:1] + x * w[1:2] + xn * w[2:3])

    def l2n(x):
        return x * lax.rsqrt(jnp.sum(x * x, axis=-1, keepdims=True) + EPS)

    def prep(hh, q_ref, k_ref, v_ref, t, base):
        ls = slice(hh * LANES, (hh + 1) * LANES)

        def body(i, carry):
            r0 = pl.multiple_of(i * rb, rb)
            dst = pl.ds(pl.multiple_of(base + i * rb, rb), rb)
            q_s[dst, :] = l2n(conv_rows(q_ref, cwq_ref, r0, t, ls)) * (dk ** -0.5)
            k_s[dst, :] = l2n(conv_rows(k_ref, cwk_ref, r0, t, ls))
            v_s[dst, :] = conv_rows(v_ref, cwv_ref, r0, t, ls)
            return carry
        lax.fori_loop(0, t // rb, body, 0)

    def gates(ab_ref, t, base):
        ab = jnp.concatenate([ab_ref[0, hh] for hh in range(GDN_HP)], axis=0)
        kind = lax.broadcasted_iota(jnp.int32, ab.shape, 0) % 4
        pos = lax.broadcasted_iota(jnp.int32, ab.shape, 1) % c
        tile = lambda p: jnp.concatenate([p] * (t // LANES), axis=1)
        g = -jnp.exp(tile(pa_ref[0])) * jax.nn.softplus(ab + tile(pd_ref[0]))
        acc_f, acc_b = g, g
        s = 1
        while s < c:
            acc_f = acc_f + jnp.where(pos >= s, pltpu.roll(acc_f, s, 1), 0.0)
            acc_b = acc_b + jnp.where(pos < c - s, pltpu.roll(acc_b, t - s, 1), 0.0)
            s *= 2
        cf_t = jnp.where(kind == 0, acc_f, jnp.where(kind == 1, acc_b, jax.nn.sigmoid(ab)))
        pad = jnp.zeros((LANES - SUBLANES, LANES), f32)
        for j in range(t // LANES):
            blk = jnp.concatenate([cf_t[:, j * LANES:(j + 1) * LANES], pad], axis=0)
            cf_s[base + j * LANES:base + (j + 1) * LANES, :] = blk.T

    gates(abc_ref, tc, 0)
    gates(abl_ref, tl, tc)

    lane = lax.broadcasted_iota(jnp.int32, (c, 2 * c), 1)
    row = lax.broadcasted_iota(jnp.int32, (c, 2 * c), 0)
    fwd = lane < c
    col = lane % c
    incl = (fwd & (row >= col)) | (~fwd & (row <= col))
    strict = (fwd & (row > col)) | (~fwd & (row < col))

    def blockdiag(a):
        return jnp.concatenate([jnp.where(fwd, a, 0.0), jnp.where(fwd, 0.0, a)], axis=0).astype(bf16)

    def chunk_group(hh, cis):
        n = range(len(cis))
        xf, xb = 2 * hh, 2 * hh + 1
        rows = [pl.ds(pl.multiple_of(ci * c, c), c) for ci in cis]
        cf = [cf_s[r, :] for r in rows]
        k = [k_s[r, :] for r in rows]
        q = [q_s[r, :] for r in rows]
        l0 = 4 * hh
        bc = lambda x, j: jnp.broadcast_to(x[:, l0 + j:l0 + j + 1], (c, LANES))
        gf_c, gb_c = [bc(x, 0) for x in cf], [bc(x, 1) for x in cf]
        bf_c, bb_c = [bc(x, 2) for x in cf], [bc(x, 3) for x in cf]
        cf_t = [jnp.concatenate([x, x], axis=0).T for x in cf]
        g_r = [jnp.where(fwd, jnp.broadcast_to(x[l0:l0 + 1], (c, 2 * c)),
                         jnp.broadcast_to(x[l0 + 1:l0 + 2], (c, 2 * c))) for x in cf_t]
        decay = [jnp.where(incl, jnp.exp(jnp.where(incl, jnp.where(fwd, gf_c[g], gb_c[g]) - g_r[g], 0.0)), 0.0)
                 for g in n]
        kq = [_dot_nt(jnp.concatenate([k[g], q[g]], axis=0), jnp.concatenate([k[g], k[g]], axis=0))
              for g in n]
        lm = [jnp.where(strict, kq[g][:c] * jnp.where(fwd, bf_c[g], bb_c[g]) * decay[g], 0.0) for g in n]
        for g in n:
            at_s[g] = (kq[g][c:] * decay[g]).astype(bf16)
        a = [jnp.dot(x.astype(bf16), blockdiag(x), preferred_element_type=f32) for x in lm]
        tp = [-x for x in lm]
        p = 2
        while 2 * p < c:
            ta = [jnp.dot(jnp.concatenate([tp[g], a[g]], axis=0).astype(bf16), blockdiag(a[g]),
                          preferred_element_type=f32) for g in n]
            tp = [tp[g] + a[g] + ta[g][:c] for g in n]
            a = [x[c:] for x in ta]
            p *= 2
        tp = [tp[g] + a[g] + jnp.dot(tp[g].astype(bf16), blockdiag(a[g]), preferred_element_type=f32)
              for g in n]
        eg_f, eg_b = [jnp.exp(x) for x in gf_c], [jnp.exp(x) for x in gb_c]
        uw = []
        for g in n:
            v = v_s[rows[g], :]
            rhs = jnp.concatenate([jnp.concatenate([v * bf_c[g], k[g] * (bf_c[g] * eg_f[g])], axis=1),
                                   jnp.concatenate([v * bb_c[g], k[g] * (bb_c[g] * eg_b[g])], axis=1)], axis=0)
            uw.append(rhs + jnp.dot(blockdiag(tp[g]), rhs.astype(bf16), preferred_element_type=f32))
        gl = [(gf_c[g][c - 1:c], gb_c[g][0:1]) for g in n]
        res = []
        for g in n:
            attn = at_s[g]
            for d, (half, g_c) in enumerate(((slice(0, c), gf_c[g]), (slice(c, 2 * c), gb_c[g]))):
                kgt = (k[g] * jnp.exp(gl[g][d] - g_c)).T.astype(bf16)
                res.append(jnp.dot(jnp.concatenate([kgt, attn[:, half]], axis=0), uw[g][half].astype(bf16),
                                   preferred_element_type=f32))
        for g in n:
            ci = cis[g]
            for d, (x, eg) in enumerate(((xf, eg_f[g]), (xb, eg_b[g]))):
                r = res[2 * g + d]
                n_s[x, ci] = r[:dk, :dk]
                mp_s[x, ci] = jnp.concatenate([r[:dk, dk:], q[g] * eg - r[dk:, dk:]], axis=0).astype(bf16)
            egl_s[hh, ci] = jnp.concatenate([jnp.exp(gl[g][0]), jnp.exp(gl[g][1]),
                                             jnp.zeros((SUBLANES - 2, LANES), f32)], axis=0)
            o_s[hh, rows[g], :] = res[2 * g][dk:, :dk] + res[2 * g + 1][dk:, :dk]

    for hh in range(GDN_HP):
        prep(hh, qc_ref, kc_ref, vc_ref, tc, 0)
        prep(hh, ql_ref, kl_ref, vl_ref, tl, tc)

        def chunk_body(i, carry, hh=hh):
            chunk_group(hh, [GDN_GROUP * i + g for g in range(GDN_GROUP)])
            return carry

        lax.fori_loop(0, nc // GDN_GROUP, chunk_body, 0)

    st_s[...] = jnp.zeros_like(st_s)
    chains = [(hh, d) for hh in range(GDN_HP) for d in range(2)]

    def scan_body(i, carry):
        ci = (i, jnp.where(i < ncc, ncc - 1 - i, nc + ncc - 1 - i))
        rows = [pl.ds(pl.multiple_of(x * c, c), c) for x in ci]
        n = range(len(chains))
        s = [st_s[x] for x in n]
        res = [jnp.dot(mp_s[x, ci[chains[x][1]]], s[x].astype(bf16), preferred_element_type=f32) for x in n]
        for x in n:
            hh, d = chains[x]
            st_s[x] = s[x] * egl_s[hh, ci[d]][d:d + 1] + n_s[x, ci[d]] - res[x][:dk]
            o_s[hh, rows[d], :] += res[x][dk:]
        return carry

    lax.fori_loop(0, nc, scan_body, 0)

    def finish(hh, z_ref, o_ref, t, base):
        ls = slice(hh * LANES, (hh + 1) * LANES)

        def body(i, carry):
            r0 = pl.ds(pl.multiple_of(i * rb, rb), rb)
            o = o_s[hh, pl.ds(pl.multiple_of(base + i * rb, rb), rb), :]
            y = o * lax.rsqrt(jnp.mean(o * o, axis=-1, keepdims=True) + EPS) * ng_ref[...]
            o_ref[0, r0, ls] = y * _silu(z_ref[0, r0, ls])
            return carry
        lax.fori_loop(0, t // rb, body, 0)

    for hh in range(GDN_HP):
        finish(hh, zc_ref, oc_ref, tc, 0)
        finish(hh, zl_ref, ol_ref, tl, tc)


def _gdn(p_l, ab_l, p_c, ab_c, conv_w, pa, pd, norm_g):
    b, tl, _ = p_l.shape
    tc = p_c.shape[1]
    h, c, hp = GDN_HEADS, GDN_CHUNK, GDN_HP
    nc = (tl + tc) // c
    assert tl % GDN_ROWS == 0 and tc % GDN_ROWS == 0 and nc % GDN_GROUP == 0
    qb, zb = OFF_QKV // LANES, OFF_Z // LANES
    assert h % hp == 0 and qb % hp == 0 and zb % hp == 0
    w = hp * LANES

    def specs(t):
        heads = lambda off: pl.BlockSpec((1, t, w), lambda i, j, off=off // hp: (i, 0, off + j))
        return [heads(qb), heads(qb + h), heads(qb + 2 * h), heads(zb),
                pl.BlockSpec((1, hp, 4, t), lambda i, j: (i, j, 0, 0))]

    cw = lambda off: pl.BlockSpec((3, w), lambda i, j, off=off // hp: (0, off + j))
    assert 4 * hp == SUBLANES
    par = pl.BlockSpec((1, SUBLANES, LANES), lambda i, j: (j, 0, 0))
    out = lambda t: pl.BlockSpec((1, t, w), lambda i, j: (i, 0, j))
    tt = tl + tc
    return pl.pallas_call(
        _gdn_kernel,
        grid=(b, h // hp),
        in_specs=specs(tl) + specs(tc) + [cw(0), cw(h), cw(2 * h), par, par, _const_spec((1, GDN_DV))],
        out_specs=[out(tl), out(tc)],
        out_shape=[jax.ShapeDtypeStruct((b, tl, h * GDN_DV), f32),
                   jax.ShapeDtypeStruct((b, tc, h * GDN_DV), f32)],
        scratch_shapes=[pltpu.VMEM((tt, GDN_DK), f32), pltpu.VMEM((tt, GDN_DK), f32),
                        pltpu.VMEM((tt, GDN_DV), f32), pltpu.VMEM((tt, LANES), f32),
                        pltpu.VMEM((GDN_GROUP, c, 2 * c), bf16),
                        pltpu.VMEM((2 * hp, nc, GDN_DK, GDN_DV), f32),
                        pltpu.VMEM((2 * hp, nc, GDN_DK + c, GDN_DV), bf16),
                        pltpu.VMEM((hp, nc, SUBLANES, LANES), f32), pltpu.VMEM((hp, tt, GDN_DV), f32),
                        pltpu.VMEM((2 * hp, GDN_DK, GDN_DV), f32)],
        compiler_params=pltpu.CompilerParams(dimension_semantics=("parallel", "parallel"),
                                             vmem_limit_bytes=GDN_VMEM_LIMIT),
        name="gated_deltanet",
    )(p_l, p_l, p_l, p_l, ab_l, p_c, p_c, p_c, p_c, ab_c, conv_w, conv_w, conv_w, pa, pd, norm_g)


def _ret_kernel(ql_ref, kl_ref, vl_ref, gl_ref, qc_ref, kc_ref, vc_ref, gc_ref,
                cos_ref, sin_ref, dsum_ref, sc_ref, gch_ref,
                ol_ref, oc_ref, sf_ref, sb_ref, q_s, k_s, o_s):
    c = RET_CHUNK
    w = RET_HEADS * RET_DK
    ri = lax.broadcasted_iota(jnp.int32, (w, w), 0) // RET_DK
    ci = lax.broadcasted_iota(jnp.int32, (w, w), 1) // RET_DK
    blockdiag = ri == ci
    lane_head = lax.broadcasted_iota(jnp.int32, (c, w), 1) // RET_DK

    def intra(q, k, v, r, cross):
        k16 = k.astype(bf16)
        for h in range(RET_HEADS):
            m = lane_head == h
            scores = _dot_nt(jnp.where(m, q, 0.0), k16) * dsum_ref[h]
            part = _dot(scores, jnp.where(m, v, 0.0))
            if h == 0:
                o_s[r, :] = part if cross is None else part + cross
            else:
                o_s[r, :] += part

    def update(s_ref, k, v, d):
        new = jnp.where(blockdiag, _dot_tn(k * sc_ref[2 * d + 1], v), 0.0)
        s_ref[...] = s_ref[...] * gch_ref[d] + new

    first_half = lax.broadcasted_iota(jnp.int32, (c, w), 1) % RET_DK < RET_DK // 2
    avg = jnp.where(blockdiag, 1.0 / RET_DV, 0.0).astype(bf16)

    def rope(x, r):
        swapped = jnp.where(first_half, pltpu.roll(x, w - RET_DK // 2, 1), pltpu.roll(x, RET_DK // 2, 1))
        return x * cos_ref[r, :] + swapped * sin_ref[r, :]

    def head_mean(x):
        hi = x.astype(bf16)
        lo = (x - hi.astype(f32)).astype(bf16)
        return jnp.dot(hi, avg, preferred_element_type=f32) + jnp.dot(lo, avg, preferred_element_type=f32)

    def finish(o, gate):
        d = o - head_mean(o)
        return d * lax.rsqrt(head_mean(d * d) + EPS) * _silu(gate)

    sf_ref[...] = jnp.zeros_like(sf_ref)
    sb_ref[...] = jnp.zeros_like(sb_ref)
    qc, kc, vc = qc_ref[0], kc_ref[0] * (RET_DK ** -0.5), vc_ref[0]
    r0 = slice(0, c)
    intra(qc, kc, vc, r0, None)
    oc_ref[0] = finish(o_s[r0, :], gc_ref[0])
    update(sf_ref, kc, vc, 0)
    update(sb_ref, kc, vc, 1)
    n = ql_ref.shape[1] // c
    for i in range(n):
        r = slice(i * c, (i + 1) * c)
        q = rope(ql_ref[0, r, :], r)
        k = rope(kl_ref[0, r, :] * (RET_DK ** -0.5), r)
        v = vl_ref[0, r, :]
        q_s[r, :] = q
        k_s[r, :] = k
        intra(q, k, v, r, _dot(q * sc_ref[0], sf_ref[...]))
        update(sf_ref, k, v, 0)
    for i in range(n - 1, -1, -1):
        r = slice(i * c, (i + 1) * c)
        o = o_s[r, :] + _dot(q_s[r, :] * sc_ref[2], sb_ref[...])
        ol_ref[0, r, :] = finish(o, gl_ref[0, r, :])
        update(sb_ref, k_s[r, :], vl_ref[0, r, :], 1)


def _retention(p_l, p_c, cos, sin, dsum, scales, gch):
    b, tl, _ = p_l.shape
    tc = p_c.shape[1]
    w = RET_HEADS * RET_DK
    assert OFF_RET % w == 0
    blk = lambda t, j: pl.BlockSpec((1, t, w), lambda i, j=j: (i, 0, OFF_RET // w + j))
    tok = lambda t: pl.BlockSpec((1, t, w), lambda i: (i, 0, 0))
    return pl.pallas_call(
        _ret_kernel,
        grid=(b,),
        in_specs=[blk(tl, j) for j in range(4)] + [blk(tc, j) for j in range(4)]
        + [_const_spec(x.shape) for x in (cos, sin, dsum, scales, gch)],
        out_specs=[tok(tl), tok(tc)],
        out_shape=[jax.ShapeDtypeStruct((b, tl, w), f32), jax.ShapeDtypeStruct((b, tc, w), f32)],
        scratch_shapes=[pltpu.VMEM((w, w), f32), pltpu.VMEM((w, w), f32),
                        pltpu.VMEM((tl, w), f32), pltpu.VMEM((tl, w), f32), pltpu.VMEM((tl, w), f32)],
        compiler_params=_cparams(("parallel",)),
        name="retention",
    )(p_l, p_l, p_l, p_l, p_c, p_c, p_c, p_c, cos, sin, dsum, scales, gch)


def _hyena(p, lp):
    b, t, _ = p.shape
    hb, c = HY_BLOCK, HY_WIDTH
    nb = t // hb
    p3 = _hyena_prep(p, lp['hy_conv_w'], lp['hy_conv_b'][None])
    bias = jnp.broadcast_to(lp['hy_bias'][:, :, None], (HY_ORDER, c, hb))
    z = _hyena_conv(p3, _hyena_filters(t, lp), bias, nb)
    return _hyena_post(z, b)


def _gdn_gate_inputs(p):
    b, t, _ = p.shape
    ab = p[..., OFF_AB:OFF_AB + 4 * GDN_HEADS].reshape(b, t, 4, GDN_HEADS)
    return ab.transpose(0, 3, 2, 1)


def _gdn_gate_params(a_log, dt_bias):
    def rows(x):
        x = jnp.pad(x.T, ((0, 0), (0, 2))).reshape(GDN_HEADS // GDN_HP, 4 * GDN_HP)
        return jnp.broadcast_to(x[:, :, None], x.shape + (LANES,))
    return rows(a_log), rows(dt_bias)


def _rope_tables(t_lat):
    rows_n = t_lat // GRID_W
    row = np.repeat(np.arange(rows_n, dtype=np.float64), GRID_W)
    colp = np.tile(np.arange(GRID_W, dtype=np.float64), rows_n)
    nf = RET_DK // 4
    inv = ROPE_BASE ** (-np.arange(nf, dtype=np.float64) / nf)
    ang = np.concatenate([row[:, None] * inv, colp[:, None] * inv], axis=-1)
    cos, sin = np.cos(ang), np.sin(ang)
    return (jnp.asarray(np.tile(np.concatenate([cos, cos], axis=-1), (1, RET_HEADS)), f32),
            jnp.asarray(np.tile(np.concatenate([-sin, sin], axis=-1), (1, RET_HEADS)), f32))


def _ret_constants(lp):
    c = RET_CHUNK
    lg = jax.nn.log_sigmoid(lp['ret_decay_logit'])
    idx = jnp.arange(c, dtype=f32)
    rel = idx[:, None] - idx[None, :]
    d_f = jnp.exp(jnp.where(rel >= 0, rel * lg[0][:, None, None], -jnp.inf))
    d_b = jnp.exp(jnp.where(rel <= 0, -rel * lg[1][:, None, None], -jnp.inf))
    lane = lambda x: jnp.repeat(x, RET_DK, axis=-1)
    scales = jnp.stack([lane(jnp.exp((idx + 1.0)[:, None] * lg[0])),
                        lane(jnp.exp((c - 1.0 - idx)[:, None] * lg[0])),
                        lane(jnp.exp((c - idx)[:, None] * lg[1])),
                        lane(jnp.exp(idx[:, None] * lg[1]))])
    gch = lane(jnp.exp(c * lg))[:, None, :]
    return d_f + d_b, scales, gch


def _reorder_w_in(w):
    h2 = 2 * GDN_HEADS
    a0 = OFF_Z + GDN_HEADS * GDN_DV
    parts = [w[..., :a0], w[..., a0 + 2 * h2:], w[..., a0:a0 + 2 * h2],
             jnp.zeros(w.shape[:-1] + (LANES - 2 * h2,), w.dtype)]
    return jnp.concatenate(parts, axis=-1)


def kernel(x, c, ctx, c_ctx, mod_w, mod_b, norm1_g, w_in, hy_conv_w, hy_conv_b, hy_f_w1, hy_f_b1, hy_f_freq1, hy_f_w2, hy_f_b2, hy_f_freq2, hy_f_w3, hy_bias, gdn_conv_w, gdn_a_log, gdn_dt_bias, gdn_norm_g, ret_decay_logit, w_out, norm2_g, ffn_w_in, ffn_w_out, final_norm_g):
    bsz, t_lat, d = x.shape
    cos, sin = _rope_tables(t_lat)
    pad_rows = (-(bsz + 1)) % SUBLANES
    cs = jnp.concatenate([c, c_ctx[None], jnp.zeros((pad_rows, d), f32)], axis=0)
    gf = final_norm_g[None]
    mod_w16, w_in16 = mod_w.astype(bf16), _reorder_w_in(w_in).astype(bf16)
    wo16, w116, w216 = w_out.astype(bf16), ffn_w_in.astype(bf16), ffn_w_out.astype(bf16)
    for i in range(DEPTH):
        need_ctx = i < DEPTH - 1
        lp = {'hy_conv_w': hy_conv_w[i], 'hy_conv_b': hy_conv_b[i], 'hy_f_w1': hy_f_w1[i],
              'hy_f_b1': hy_f_b1[i], 'hy_f_freq1': hy_f_freq1[i], 'hy_f_w2': hy_f_w2[i],
              'hy_f_b2': hy_f_b2[i], 'hy_f_freq2': hy_f_freq2[i], 'hy_f_w3': hy_f_w3[i],
              'hy_bias': hy_bias[i], 'ret_decay_logit': ret_decay_logit[i]}
        mod = _mod_vectors(cs, mod_w16, mod_b[i][None], i)
        mod_lat = mod[:bsz].reshape(bsz, N_MOD, d)
        mod_ctx = jnp.broadcast_to(mod[bsz].reshape(1, N_MOD, d), (bsz, N_MOD, d))
        g1 = norm1_g[i][None]
        p_l = _in_proj(x, g1, mod_lat, w_in16, i, 512)
        p_c = _in_proj(ctx, g1, mod_ctx, w_in16, i, 256)

        hy_l = _hyena(p_l, lp)
        pa, pd = _gdn_gate_params(gdn_a_log[i], gdn_dt_bias[i])
        gd_l, gd_c = _gdn(p_l, _gdn_gate_inputs(p_l), p_c, _gdn_gate_inputs(p_c), gdn_conv_w[i], pa, pd,
                          gdn_norm_g[i][None])

        dsum, scales, gch = _ret_constants(lp)
        rt_l, rt_c = _retention(p_l, p_c, cos, sin, dsum, scales, gch)

        g2 = norm2_g[i][None]
        x = _out_ffn(x, hy_l, gd_l, rt_l, mod_lat, wo16, g2, w116, w216, gf, i, 512, final=not need_ctx)
        if need_ctx:
            hy_c = _hyena(p_c, lp)
            ctx = _out_ffn(ctx, hy_c, gd_c, rt_c, mod_ctx, wo16, g2, w116, w216, gf, i, 256, final=False)
    return x
```

```python
import functools
import math

import jax
import jax.numpy as jnp
import numpy as np
from jax import lax
from jax.experimental import pallas as pl
from jax.experimental.pallas import tpu as pltpu

D_MODEL = 1024
DEPTH = 2
GRID_W = 64
EPS = 1e-6
N_MOD = 6

HY_WIDTH = D_MODEL // 4
HY_ORDER = 2
HY_EMB = 33
HY_FAST_DECAY = 0.3
HY_SLOW_DECAY = 1.5
HY_TARGET = 1e-2
GDN_HEADS = D_MODEL // 256
GDN_DK = 128
GDN_DV = 128
RET_HEADS = D_MODEL // 256
RET_DK = 64
RET_DV = 64
ROPE_BASE = 10000.0
FFN_HIDDEN = ((8 * D_MODEL + 3 * 256 - 1) // (3 * 256)) * 256

GDN_CHUNK = 64
GDN_ROWS = 256
GDN_HP = 2
GDN_GROUP = 18
RET_CHUNK = 256
HY_BLOCK = 256
HY_PAIR = 2
HY_IO = 256
LANES = 128
SUBLANES = 8

GDN_QKV = 2 * GDN_HEADS * GDN_DK + GDN_HEADS * GDN_DV
OFF_HY = 0
OFF_QKV = OFF_HY + 3 * HY_WIDTH
OFF_Z = OFF_QKV + GDN_QKV
OFF_RET = OFF_Z + GDN_HEADS * GDN_DV
OFF_AB = OFF_RET + 4 * RET_HEADS * RET_DK
IN_PAD = OFF_AB + LANES

VMEM_LIMIT = 56 * 1024 * 1024
GDN_VMEM_LIMIT = 60 * 1024 * 1024

f32 = jnp.float32
bf16 = jnp.bfloat16


def _cparams(sem):
    return pltpu.CompilerParams(dimension_semantics=sem, vmem_limit_bytes=VMEM_LIMIT)


def _const_spec(shape):
    nd = len(shape)
    return pl.BlockSpec(shape, lambda *_: (0,) * nd, pipeline_mode=pl.Buffered(1))


def _layer_spec(stacked, layer):
    nd = stacked.ndim - 1
    return pl.BlockSpec((None,) + stacked.shape[1:], lambda *_: (layer,) + (0,) * nd,
                        pipeline_mode=pl.Buffered(1))


def _rms(x, g):
    return x * lax.rsqrt(jnp.mean(x * x, axis=-1, keepdims=True) + EPS) * g


def _silu(x):
    return x * jax.nn.sigmoid(x)


def _dot(a, b):
    return jnp.dot(a.astype(bf16), b.astype(bf16), preferred_element_type=f32)


def _dot_nt(a, b):
    return lax.dot_general(a.astype(bf16), b.astype(bf16), (((1,), (1,)), ((), ())),
                           preferred_element_type=f32)


def _dot_tn(a, b):
    return lax.dot_general(a.astype(bf16), b.astype(bf16), (((0,), (0,)), ((), ())),
                           preferred_element_type=f32)


def _mod_kernel(c_ref, w_ref, b_ref, o_ref):
    o_ref[...] = _dot(_silu(c_ref[...]), w_ref[...]) + b_ref[...]


def _mod_vectors(cs, w16, b, layer):
    rows, d = cs.shape
    n = w16.shape[2]
    tn = 1536
    return pl.pallas_call(
        _mod_kernel,
        grid=(n // tn,),
        in_specs=[pl.BlockSpec((rows, d), lambda j: (0, 0)),
                  pl.BlockSpec((None, d, tn), lambda j: (layer, 0, j)),
                  pl.BlockSpec((1, tn), lambda j: (0, j))],
        out_specs=pl.BlockSpec((rows, tn), lambda j: (0, j)),
        out_shape=jax.ShapeDtypeStruct((rows, n), f32),
        compiler_params=_cparams(("parallel",)),
        name="mod_vectors",
    )(cs, w16, b)


def _in_proj_kernel(x_ref, g_ref, mod_ref, w_ref, o_ref):
    x = x_ref[0]
    mod = mod_ref[0]
    h = _rms(x, g_ref[...]) * (1.0 + mod[1:2]) + mod[0:1]
    h16 = h.astype(bf16)
    n = w_ref.shape[1]
    step = 512
    for j in range(0, n, step):
        e = min(j + step, n)
        o_ref[0, :, j:e] = jnp.dot(h16, w_ref[:, j:e], preferred_element_type=f32)


def _in_proj(x, g, mod, w16, layer, tm):
    b, t, d = x.shape
    n = w16.shape[2]
    return pl.pallas_call(
        _in_proj_kernel,
        grid=(b, t // tm),
        in_specs=[pl.BlockSpec((1, tm, d), lambda i, j: (i, j, 0)),
                  _const_spec((1, d)),
                  pl.BlockSpec((1, N_MOD, d), lambda i, j: (i, 0, 0)),
                  _layer_spec(w16, layer)],
        out_specs=pl.BlockSpec((1, tm, n), lambda i, j: (i, j, 0)),
        out_shape=jax.ShapeDtypeStruct((b, t, n), f32),
        compiler_params=_cparams(("parallel", "parallel")),
        name="in_proj",
    )(x, g, mod, w16)


def _out_ffn_kernel(x_ref, hy_ref, gd_ref, rt_ref, mod_ref, wo_ref, g2_ref, w1_ref, w2_ref, gf_ref, o_ref,
                    *, final):
    x = x_ref[0]
    mod = mod_ref[0]
    o0, o1, o2 = HY_WIDTH, HY_WIDTH + GDN_HEADS * GDN_DV, D_MODEL
    y = (jnp.dot(hy_ref[0].astype(bf16), wo_ref[0:o0, :], preferred_element_type=f32)
         + jnp.dot(gd_ref[0].astype(bf16), wo_ref[o0:o1, :], preferred_element_type=f32)
         + jnp.dot(rt_ref[0].astype(bf16), wo_ref[o1:o2, :], preferred_element_type=f32))
    x = x + mod[2:3] * y
    h16 = (_rms(x, g2_ref[...]) * (1.0 + mod[4:5]) + mod[3:4]).astype(bf16)
    step = 256
    acc = jnp.zeros_like(x)
    for j in range(0, FFN_HIDDEN, step):
        gate = jnp.dot(h16, w1_ref[:, j:j + step], preferred_element_type=f32)
        up = jnp.dot(h16, w1_ref[:, FFN_HIDDEN + j:FFN_HIDDEN + j + step], preferred_element_type=f32)
        acc = acc + jnp.dot((_silu(gate) * up).astype(bf16), w2_ref[j:j + step, :], preferred_element_type=f32)
    x = x + mod[5:6] * acc
    if final:
        x = _rms(x, gf_ref[...])
    o_ref[0] = x


def _out_ffn(x, hy, gd, rt, mod, wo16, g2, w116, w216, gf, layer, tm, final):
    b, t, d = x.shape
    tok = lambda w: pl.BlockSpec((1, tm, w), lambda i, j: (i, j, 0))
    return pl.pallas_call(
        functools.partial(_out_ffn_kernel, final=final),
        grid=(b, t // tm),
        in_specs=[tok(d), tok(hy.shape[-1]), tok(gd.shape[-1]), tok(rt.shape[-1]),
                  pl.BlockSpec((1, N_MOD, d), lambda i, j: (i, 0, 0)),
                  _layer_spec(wo16, layer), _const_spec((1, d)), _layer_spec(w116, layer),
                  _layer_spec(w216, layer), _const_spec((1, d))],
        out_specs=tok(d),
        out_shape=jax.ShapeDtypeStruct((b, t, d), f32),
        compiler_params=_cparams(("parallel", "parallel")),
        name="out_ffn",
    )(x, hy, gd, rt, mod, wo16, g2, w116, w216, gf)


def _hyena_kernel(v_ref, x1_ref, x2_ref, k_ref, b_ref, o_ref, acc_ref, *, nb, cb):
    hb = HY_BLOCK
    rb = v_ref.shape[1] // nb

    def conv(u16, kext):
        m = range(len(u16))
        for d in range(-(nb - 1), nb):
            dd = d % (2 * nb)
            t0, t1 = max(0, d), min(nb - 1, nb - 1 + d) + 1
            for i in m:
                win = jnp.broadcast_to(kext[i][:, hb * dd:hb * dd + 2 * hb], (hb, 2 * hb))
                toep = pltpu.roll(win, 0, 1, stride=1, stride_axis=0)[:, hb:].astype(bf16)
                contrib = jnp.dot(u16[i][rb * (t0 - d):rb * (t1 - d)], toep, preferred_element_type=f32)
                if d == -(nb - 1):
                    acc_ref[i] = jnp.zeros(acc_ref.shape[1:], f32)
                acc_ref[i, rb * t0:rb * t1, :] += contrib
        return [acc_ref[i] for i in m]

    def body(g, carry):
        cs = [HY_PAIR * g + i for i in range(HY_PAIR)]
        z = [v_ref[c] for c in cs]
        for n, gate_ref in enumerate((x1_ref, x2_ref)):
            y = conv([x.astype(bf16) for x in z], [k_ref[n, pl.ds(c, 1), :] for c in cs])
            z = [gate_ref[c] * (y[i] + z[i] * b_ref[n, pl.ds(c, 1), :]) for i, c in enumerate(cs)]
        for i, c in enumerate(cs):
            o_ref[c] = z[i]
        return carry

    lax.fori_loop(0, cb // HY_PAIR, body, 0)


def _hyena_conv(p3, kext, bias, nb, cb=SUBLANES):
    c3, r, hb = p3.shape
    c = c3 // 3
    klen = kext.shape[-1]
    sig = lambda s: pl.BlockSpec((cb, r, hb), lambda i, s=s: (s * (c // cb) + i, 0, 0))
    return pl.pallas_call(
        functools.partial(_hyena_kernel, nb=nb, cb=cb),
        grid=(c // cb,),
        in_specs=[sig(0), sig(1), sig(2),
                  pl.BlockSpec((HY_ORDER, cb, klen), lambda i: (0, i, 0)),
                  pl.BlockSpec((HY_ORDER, cb, hb), lambda i: (0, i, 0))],
        out_specs=pl.BlockSpec((cb, r, hb), lambda i: (i, 0, 0)),
        out_shape=jax.ShapeDtypeStruct((c, r, hb), f32),
        scratch_shapes=[pltpu.VMEM((HY_PAIR, r, hb), f32)],
        compiler_params=_cparams(("parallel",)),
        name=f"hyena_conv_nb{nb}",
    )(p3, p3, p3, kext, bias)


def _hyena_prep_kernel(x_ref, before_ref, after_ref, w_ref, b_ref, o_ref):
    hb = HY_BLOCK
    w = w_ref[...]
    row = lax.broadcasted_iota(jnp.int32, x_ref.shape[1:], 0)
    s = pl.program_id(1)
    has_before = s > 0
    has_after = s < pl.num_programs(1) - 1
    for i0 in range(0, x_ref.shape[0], SUBLANES):
        tiles = []
        for i in range(i0, i0 + SUBLANES):
            x = x_ref[i]
            prev = jnp.where(has_before, before_ref[i, SUBLANES - 1:SUBLANES, :], 0.0)
            nxt = jnp.where(has_after, after_ref[i, 0:1, :], 0.0)
            xp = jnp.where(row == 0, prev, pltpu.roll(x, 1, 0))
            xn = jnp.where(row == hb - 1, nxt, pltpu.roll(x, hb - 1, 0))
            tiles.append((xp * w[0:1] + x * w[1:2] + xn * w[2:3] + b_ref[...]).T)
        o_ref[:, i0:i0 + SUBLANES, :] = pltpu.einshape("bct->cbt", jnp.stack(tiles, axis=0))


def _hyena_prep(p, conv_w, conv_b):
    b, t, _ = p.shape
    hb = HY_BLOCK
    n = 3 * HY_WIDTH
    per = hb // SUBLANES
    last = t // SUBLANES - 1
    assert OFF_HY == 0 and b % SUBLANES == 0
    return pl.pallas_call(
        _hyena_prep_kernel,
        grid=(n // HY_IO, t // hb),
        in_specs=[pl.BlockSpec((b, hb, HY_IO), lambda j, s: (0, s, j)),
                  pl.BlockSpec((b, SUBLANES, HY_IO), lambda j, s: (0, jnp.maximum(s * per - 1, 0), j)),
                  pl.BlockSpec((b, SUBLANES, HY_IO), lambda j, s: (0, jnp.minimum((s + 1) * per, last), j)),
                  pl.BlockSpec((3, HY_IO), lambda j, s: (0, j)),
                  pl.BlockSpec((1, HY_IO), lambda j, s: (0, j))],
        out_specs=pl.BlockSpec((HY_IO, b, hb), lambda j, s: (j, s, 0)),
        out_shape=jax.ShapeDtypeStruct((n, (t // hb) * b, hb), f32),
        compiler_params=_cparams(("parallel", "parallel")),
        name="hyena_prep",
    )(p, p, p, conv_w, conv_b)


def _hyena_post_kernel(z_ref, o_ref):
    for i0 in range(0, z_ref.shape[1], SUBLANES):
        tiles = pltpu.einshape("cbt->bct", z_ref[:, i0:i0 + SUBLANES, :])
        for i in range(SUBLANES):
            o_ref[i0 + i] = tiles[i].T


def _hyena_post(z, b):
    c, r, hb = z.shape
    nb = r // b
    return pl.pallas_call(
        _hyena_post_kernel,
        grid=(c // HY_IO, nb),
        in_specs=[pl.BlockSpec((HY_IO, b, hb), lambda j, s: (j, s, 0))],
        out_specs=pl.BlockSpec((b, hb, HY_IO), lambda j, s: (0, s, j)),
        out_shape=jax.ShapeDtypeStruct((b, nb * hb, c), f32),
        compiler_params=_cparams(("parallel", "parallel")),
        name="hyena_post",
    )(z)


def _hyena_filter_kernel(z_ref, w1_ref, b1_ref, f1_ref, w2_ref, b2_ref, f2_ref, w3_ref, win_ref, o_ref):
    hi = lax.Precision.HIGHEST
    hb = HY_BLOCK
    t = z_ref.shape[1]
    for side in range(2):
        h = jnp.sin(f1_ref[...] * (jnp.dot(z_ref[side], w1_ref[...], precision=hi,
                                           preferred_element_type=f32) + b1_ref[...]))
        h = jnp.sin(f2_ref[...] * (jnp.dot(h, w2_ref[...], precision=hi, preferred_element_type=f32)
                                   + b2_ref[...]))
        for n in range(HY_ORDER):
            w3 = w3_ref[:, (2 * n + side) * HY_WIDTH:(2 * n + side + 1) * HY_WIDTH]
            k = lax.dot_general(w3, h, (((0,), (1,)), ((), ())), precision=hi,
                                preferred_element_type=f32) * win_ref[side]
            o_ref[n, :, hb + side * t:hb + (side + 1) * t] = k
            if side == 1:
                o_ref[n, :, 0:hb] = k[:, t - hb:]


def _hyena_tables(t_len):
    pos = np.arange(t_len, dtype=np.float64)
    t = np.linspace(0.0, 1.0, t_len)
    bands = (HY_EMB - 1) // 2
    f = np.linspace(1e-4, bands - 1, bands)
    ang = (2.0 * math.pi / t_len) * pos[:, None] * f[None, :]
    z = np.concatenate([t[:, None], np.cos(ang), -np.sin(ang)], axis=-1)
    max_decay = math.log(HY_TARGET) / HY_FAST_DECAY
    min_decay = math.log(HY_TARGET) / HY_SLOW_DECAY
    deltas = np.abs(np.linspace(min_decay, max_decay, HY_WIDTH))
    window = np.exp(-t[:, None] * deltas[None, :])
    neg = lambda x: np.concatenate([x[:1], x[:0:-1]], axis=0)
    zs = np.pad(np.stack([z, neg(z)]), ((0, 0), (0, 0), (0, LANES - HY_EMB)))
    wneg = neg(window)
    wneg[0] = 0.0
    win = np.stack([window, wneg]).transpose(0, 2, 1)
    return jnp.asarray(zs, f32), jnp.asarray(win, f32)


def _hyena_filters(t_len, lp):
    zs, win = _hyena_tables(t_len)
    w1 = jnp.pad(lp['hy_f_w1'], ((0, LANES - HY_EMB), (0, 0)))
    row = lambda x: x[None]
    args = (zs, w1, row(lp['hy_f_b1']), row(lp['hy_f_freq1']), lp['hy_f_w2'], row(lp['hy_f_b2']),
            row(lp['hy_f_freq2']), lp['hy_f_w3'], win)
    klen = 2 * t_len + HY_BLOCK
    return pl.pallas_call(
        _hyena_filter_kernel,
        grid=(1,),
        in_specs=[_const_spec(a.shape) for a in args],
        out_specs=pl.BlockSpec((HY_ORDER, HY_WIDTH, klen), lambda i: (0, 0, 0)),
        out_shape=jax.ShapeDtypeStruct((HY_ORDER, HY_WIDTH, klen), f32),
        compiler_params=_cparams(("arbitrary",)),
        name="hyena_filters",
    )(*args)


def _gdn_kernel(ql_ref, kl_ref, vl_ref, zl_ref, abl_ref, qc_ref, kc_ref, vc_ref, zc_ref, abc_ref,
                cwq_ref, cwk_ref, cwv_ref, pa_ref, pd_ref, ng_ref,
                ol_ref, oc_ref,
                q_s, k_s, v_s, cf_s, at_s, n_s, mp_s, egl_s, o_s, st_s):
    c = GDN_CHUNK
    rb = GDN_ROWS
    tc, tl = qc_ref.shape[1], ql_ref.shape[1]
    ncc, nc = tc // c, (tc + tl) // c
    dk = GDN_DK

    def conv_rows(x_ref, w_ref, r0, t, ls):
        x = x_ref[0, pl.ds(r0, rb), ls]
        before = pl.multiple_of(jnp.maximum(r0 - SUBLANES, 0), SUBLANES)
        after = pl.multiple_of(jnp.minimum(r0 + rb, t - SUBLANES), SUBLANES)
        prev = x_ref[0, pl.ds(before, SUBLANES), ls][SUBLANES - 1:SUBLANES]
        nxt = x_ref[0, pl.ds(after, SUBLANES), ls][0:1]
        prev = jnp.where(r0 > 0, prev, 0.0)
        nxt = jnp.where(r0 + rb < t, nxt, 0.0)
        row = lax.broadcasted_iota(jnp.int32, x.shape, 0)
        xp = jnp.where(row == 0, prev, pltpu.roll(x, 1, 0))
        xn = jnp.where(row == rb - 1, nxt, pltpu.roll(x, rb - 1, 0))
        w = w_ref[:, ls]
        return _silu(xp * w[0:1] + x * w[1:2] + xn * w[2:3])

    def l2n(x):
        return x * lax.rsqrt(jnp.sum(x * x, axis=-1, keepdims=True) + EPS)

    def prep(hh, q_ref, k_ref, v_ref, t, base):
        ls = slice(hh * LANES, (hh + 1) * LANES)

        def body(i, carry):
            r0 = pl.multiple_of(i * rb, rb)
            dst = pl.ds(pl.multiple_of(base + i * rb, rb), rb)
            q_s[dst, :] = l2n(conv_rows(q_ref, cwq_ref, r0, t, ls)) * (dk ** -0.5)
            k_s[dst, :] = l2n(conv_rows(k_ref, cwk_ref, r0, t, ls))
            v_s[dst, :] = conv_rows(v_ref, cwv_ref, r0, t, ls)
            return carry
        lax.fori_loop(0, t // rb, body, 0)

    def gates(ab_ref, t, base):
        ab = jnp.concatenate([ab_ref[0, hh] for hh in range(GDN_HP)], axis=0)
        kind = lax.broadcasted_iota(jnp.int32, ab.shape, 0) % 4
        pos = lax.broadcasted_iota(jnp.int32, ab.shape, 1) % c
        tile = lambda p: jnp.concatenate([p] * (t // LANES), axis=1)
        g = -jnp.exp(tile(pa_ref[0])) * jax.nn.softplus(ab + tile(pd_ref[0]))
        acc_f, acc_b = g, g
        s = 1
        while s < c:
            acc_f = acc_f + jnp.where(pos >= s, pltpu.roll(acc_f, s, 1), 0.0)
            acc_b = acc_b + jnp.where(pos < c - s, pltpu.roll(acc_b, t - s, 1), 0.0)
            s *= 2
        cf_t = jnp.where(kind == 0, acc_f, jnp.where(kind == 1, acc_b, jax.nn.sigmoid(ab)))
        pad = jnp.zeros((LANES - SUBLANES, LANES), f32)
        for j in range(t // LANES):
            blk = jnp.concatenate([cf_t[:, j * LANES:(j + 1) * LANES], pad], axis=0)
            cf_s[base + j * LANES:base + (j + 1) * LANES, :] = blk.T

    gates(abc_ref, tc, 0)
    gates(abl_ref, tl, tc)

    lane = lax.broadcasted_iota(jnp.int32, (c, 2 * c), 1)
    row = lax.broadcasted_iota(jnp.int32, (c, 2 * c), 0)
    fwd = lane < c
    col = lane % c
    incl = (fwd & (row >= col)) | (~fwd & (row <= col))
    strict = (fwd & (row > col)) | (~fwd & (row < col))

    def blockdiag(a):
        return jnp.concatenate([jnp.where(fwd, a, 0.0), jnp.where(fwd, 0.0, a)], axis=0).astype(bf16)

    def chunk_group(hh, cis):
        n = range(len(cis))
        xf, xb = 2 * hh, 2 * hh + 1
        rows = [pl.ds(pl.multiple_of(ci * c, c), c) for ci in cis]
        cf = [cf_s[r, :] for r in rows]
        k = [k_s[r, :] for r in rows]
        q = [q_s[r, :] for r in rows]
        l0 = 4 * hh
        bc = lambda x, j: jnp.broadcast_to(x[:, l0 + j:l0 + j + 1], (c, LANES))
        gf_c, gb_c = [bc(x, 0) for x in cf], [bc(x, 1) for x in cf]
        bf_c, bb_c = [bc(x, 2) for x in cf], [bc(x, 3) for x in cf]
        cf_t = [jnp.concatenate([x, x], axis=0).T for x in cf]
        g_r = [jnp.where(fwd, jnp.broadcast_to(x[l0:l0 + 1], (c, 2 * c)),
                         jnp.broadcast_to(x[l0 + 1:l0 + 2], (c, 2 * c))) for x in cf_t]
        decay = [jnp.where(incl, jnp.exp(jnp.where(incl, jnp.where(fwd, gf_c[g], gb_c[g]) - g_r[g], 0.0)), 0.0)
                 for g in n]
        kq = [_dot_nt(jnp.concatenate([k[g], q[g]], axis=0), jnp.concatenate([k[g], k[g]], axis=0))
              for g in n]
        lm = [jnp.where(strict, kq[g][:c] * jnp.where(fwd, bf_c[g], bb_c[g]) * decay[g], 0.0) for g in n]
        for g in n:
            at_s[g] = (kq[g][c:] * decay[g]).astype(bf16)
        a = [jnp.dot(x.astype(bf16), blockdiag(x), preferred_element_type=f32) for x in lm]
        tp = [-x for x in lm]
        p = 2
        while 2 * p < c:
            ta = [jnp.dot(jnp.concatenate([tp[g], a[g]], axis=0).astype(bf16), blockdiag(a[g]),
                          preferred_element_type=f32) for g in n]
            tp = [tp[g] + a[g] + ta[g][:c] for g in n]
            a = [x[c:] for x in ta]
            p *= 2
        tp = [tp[g] + a[g] + jnp.dot(tp[g].astype(bf16), blockdiag(a[g]), preferred_element_type=f32)
              for g in n]
        eg_f, eg_b = [jnp.exp(x) for x in gf_c], [jnp.exp(x) for x in gb_c]
        uw = []
        for g in n:
            v = v_s[rows[g], :]
            rhs = jnp.concatenate([jnp.concatenate([v * bf_c[g], k[g] * (bf_c[g] * eg_f[g])], axis=1),
                                   jnp.concatenate([v * bb_c[g], k[g] * (bb_c[g] * eg_b[g])], axis=1)], axis=0)
            uw.append(rhs + jnp.dot(blockdiag(tp[g]), rhs.astype(bf16), preferred_element_type=f32))
        gl = [(gf_c[g][c - 1:c], gb_c[g][0:1]) for g in n]
        res = []
        for g in n:
            attn = at_s[g]
            for d, (half, g_c) in enumerate(((slice(0, c), gf_c[g]), (slice(c, 2 * c), gb_c[g]))):
                kgt = (k[g] * jnp.exp(gl[g][d] - g_c)).T.astype(bf16)
                res.append(jnp.dot(jnp.concatenate([kgt, attn[:, half]], axis=0), uw[g][half].astype(bf16),
                                   preferred_element_type=f32))
        for g in n:
            ci = cis[g]
            for d, (x, eg) in enumerate(((xf, eg_f[g]), (xb, eg_b[g]))):
                r = res[2 * g + d]
                n_s[x, ci] = r[:dk, :dk]
                mp_s[x, ci] = jnp.concatenate([r[:dk, dk:], q[g] * eg - r[dk:, dk:]], axis=0).astype(bf16)
            egl_s[hh, ci] = jnp.concatenate([jnp.exp(gl[g][0]), jnp.exp(gl[g][1]),
                                             jnp.zeros((SUBLANES - 2, LANES), f32)], axis=0)
            o_s[hh, rows[g], :] = res[2 * g][dk:, :dk] + res[2 * g + 1][dk:, :dk]

    for hh in range(GDN_HP):
        prep(hh, qc_ref, kc_ref, vc_ref, tc, 0)
        prep(hh, ql_ref, kl_ref, vl_ref, tl, tc)

        def chunk_body(i, carry, hh=hh):
            chunk_group(hh, [GDN_GROUP * i + g for g in range(GDN_GROUP)])
            return carry

        lax.fori_loop(0, nc // GDN_GROUP, chunk_body, 0)

    st_s[...] = jnp.zeros_like(st_s)
    chains = [(hh, d) for hh in range(GDN_HP) for d in range(2)]

    def scan_body(i, carry):
        ci = (i, jnp.where(i < ncc, ncc - 1 - i, nc + ncc - 1 - i))
        rows = [pl.ds(pl.multiple_of(x * c, c), c) for x in ci]
        n = range(len(chains))
        s = [st_s[x] for x in n]
        res = [jnp.dot(mp_s[x, ci[chains[x][1]]], s[x].astype(bf16), preferred_element_type=f32) for x in n]
        for x in n:
            hh, d = chains[x]
            st_s[x] = s[x] * egl_s[hh, ci[d]][d:d + 1] + n_s[x, ci[d]] - res[x][:dk]
            o_s[hh, rows[d], :] += res[x][dk:]
        return carry

    lax.fori_loop(0, nc, scan_body, 0)

    def finish(hh, z_ref, o_ref, t, base):
        ls = slice(hh * LANES, (hh + 1) * LANES)

        def body(i, carry):
            r0 = pl.ds(pl.multiple_of(i * rb, rb), rb)
            o = o_s[hh, pl.ds(pl.multiple_of(base + i * rb, rb), rb), :]
            y = o * lax.rsqrt(jnp.mean(o * o, axis=-1, keepdims=True) + EPS) * ng_ref[...]
            o_ref[0, r0, ls] = y * _silu(z_ref[0, r0, ls])
            return carry
        lax.fori_loop(0, t // rb, body, 0)

    for hh in range(GDN_HP):
        finish(hh, zc_ref, oc_ref, tc, 0)
        finish(hh, zl_ref, ol_ref, tl, tc)


def _gdn(p_l, ab_l, p_c, ab_c, conv_w, pa, pd, norm_g):
    b, tl, _ = p_l.shape
    tc = p_c.shape[1]
    h, c, hp = GDN_HEADS, GDN_CHUNK, GDN_HP
    nc = (tl + tc) // c
    assert tl % GDN_ROWS == 0 and tc % GDN_ROWS == 0 and nc % GDN_GROUP == 0
    qb, zb = OFF_QKV // LANES, OFF_Z // LANES
    assert h % hp == 0 and qb % hp == 0 and zb % hp == 0
    w = hp * LANES

    def specs(t):
        heads = lambda off: pl.BlockSpec((1, t, w), lambda i, j, off=off // hp: (i, 0, off + j))
        return [heads(qb), heads(qb + h), heads(qb + 2 * h), heads(zb),
                pl.BlockSpec((1, hp, 4, t), lambda i, j: (i, j, 0, 0))]

    cw = lambda off: pl.BlockSpec((3, w), lambda i, j, off=off // hp: (0, off + j))
    assert 4 * hp == SUBLANES
    par = pl.BlockSpec((1, SUBLANES, LANES), lambda i, j: (j, 0, 0))
    out = lambda t: pl.BlockSpec((1, t, w), lambda i, j: (i, 0, j))
    tt = tl + tc
    return pl.pallas_call(
        _gdn_kernel,
        grid=(b, h // hp),
        in_specs=specs(tl) + specs(tc) + [cw(0), cw(h), cw(2 * h), par, par, _const_spec((1, GDN_DV))],
        out_specs=[out(tl), out(tc)],
        out_shape=[jax.ShapeDtypeStruct((b, tl, h * GDN_DV), f32),
                   jax.ShapeDtypeStruct((b, tc, h * GDN_DV), f32)],
        scratch_shapes=[pltpu.VMEM((tt, GDN_DK), f32), pltpu.VMEM((tt, GDN_DK), f32),
                        pltpu.VMEM((tt, GDN_DV), f32), pltpu.VMEM((tt, LANES), f32),
                        pltpu.VMEM((GDN_GROUP, c, 2 * c), bf16),
                        pltpu.VMEM((2 * hp, nc, GDN_DK, GDN_DV), f32),
                        pltpu.VMEM((2 * hp, nc, GDN_DK + c, GDN_DV), bf16),
                        pltpu.VMEM((hp, nc, SUBLANES, LANES), f32), pltpu.VMEM((hp, tt, GDN_DV), f32),
                        pltpu.VMEM((2 * hp, GDN_DK, GDN_DV), f32)],
        compiler_params=pltpu.CompilerParams(dimension_semantics=("parallel", "parallel"),
                                             vmem_limit_bytes=GDN_VMEM_LIMIT),
        name="gated_deltanet",
    )(p_l, p_l, p_l, p_l, ab_l, p_c, p_c, p_c, p_c, ab_c, conv_w, conv_w, conv_w, pa, pd, norm_g)


def _ret_kernel(ql_ref, kl_ref, vl_ref, gl_ref, qc_ref, kc_ref, vc_ref, gc_ref,
                cos_ref, sin_ref, dsum_ref, sc_ref, gch_ref,
                ol_ref, oc_ref, sf_ref, sb_ref, q_s, k_s, o_s):
    c = RET_CHUNK
    w = RET_HEADS * RET_DK
    ri = lax.broadcasted_iota(jnp.int32, (w, w), 0) // RET_DK
    ci = lax.broadcasted_iota(jnp.int32, (w, w), 1) // RET_DK
    blockdiag = ri == ci
    lane_head = lax.broadcasted_iota(jnp.int32, (c, w), 1) // RET_DK

    def intra(q, k, v, r, cross):
        k16 = k.astype(bf16)
        for h in range(RET_HEADS):
            m = lane_head == h
            scores = _dot_nt(jnp.where(m, q, 0.0), k16) * dsum_ref[h]
            part = _dot(scores, jnp.where(m, v, 0.0))
            if h == 0:
                o_s[r, :] = part if cross is None else part + cross
            else:
                o_s[r, :] += part

    def update(s_ref, k, v, d):
        new = jnp.where(blockdiag, _dot_tn(k * sc_ref[2 * d + 1], v), 0.0)
        s_ref[...] = s_ref[...] * gch_ref[d] + new

    first_half = lax.broadcasted_iota(jnp.int32, (c, w), 1) % RET_DK < RET_DK // 2
    avg = jnp.where(blockdiag, 1.0 / RET_DV, 0.0).astype(bf16)

    def rope(x, r):
        swapped = jnp.where(first_half, pltpu.roll(x, w - RET_DK // 2, 1), pltpu.roll(x, RET_DK // 2, 1))
        return x * cos_ref[r, :] + swapped * sin_ref[r, :]

    def head_mean(x):
        hi = x.astype(bf16)
        lo = (x - hi.astype(f32)).astype(bf16)
        return jnp.dot(hi, avg, preferred_element_type=f32) + jnp.dot(lo, avg, preferred_element_type=f32)

    def finish(o, gate):
        d = o - head_mean(o)
        return d * lax.rsqrt(head_mean(d * d) + EPS) * _silu(gate)

    sf_ref[...] = jnp.zeros_like(sf_ref)
    sb_ref[...] = jnp.zeros_like(sb_ref)
    qc, kc, vc = qc_ref[0], kc_ref[0] * (RET_DK ** -0.5), vc_ref[0]
    r0 = slice(0, c)
    intra(qc, kc, vc, r0, None)
    oc_ref[0] = finish(o_s[r0, :], gc_ref[0])
    update(sf_ref, kc, vc, 0)
    update(sb_ref, kc, vc, 1)
    n = ql_ref.shape[1] // c
    for i in range(n):
        r = slice(i * c, (i + 1) * c)
        q = rope(ql_ref[0, r, :], r)
        k = rope(kl_ref[0, r, :] * (RET_DK ** -0.5), r)
        v = vl_ref[0, r, :]
        q_s[r, :] = q
        k_s[r, :] = k
        intra(q, k, v, r, _dot(q * sc_ref[0], sf_ref[...]))
        update(sf_ref, k, v, 0)
    for i in range(n - 1, -1, -1):
        r = slice(i * c, (i + 1) * c)
        o = o_s[r, :] + _dot(q_s[r, :] * sc_ref[2], sb_ref[...])
        ol_ref[0, r, :] = finish(o, gl_ref[0, r, :])
        update(sb_ref, k_s[r, :], vl_ref[0, r, :], 1)


def _retention(p_l, p_c, cos, sin, dsum, scales, gch):
    b, tl, _ = p_l.shape
    tc = p_c.shape[1]
    w = RET_HEADS * RET_DK
    assert OFF_RET % w == 0
    blk = lambda t, j: pl.BlockSpec((1, t, w), lambda i, j=j: (i, 0, OFF_RET // w + j))
    tok = lambda t: pl.BlockSpec((1, t, w), lambda i: (i, 0, 0))
    return pl.pallas_call(
        _ret_kernel,
        grid=(b,),
        in_specs=[blk(tl, j) for j in range(4)] + [blk(tc, j) for j in range(4)]
        + [_const_spec(x.shape) for x in (cos, sin, dsum, scales, gch)],
        out_specs=[tok(tl), tok(tc)],
        out_shape=[jax.ShapeDtypeStruct((b, tl, w), f32), jax.ShapeDtypeStruct((b, tc, w), f32)],
        scratch_shapes=[pltpu.VMEM((w, w), f32), pltpu.VMEM((w, w), f32),
                        pltpu.VMEM((tl, w), f32), pltpu.VMEM((tl, w), f32), pltpu.VMEM((tl, w), f32)],
        compiler_params=_cparams(("parallel",)),
        name="retention",
    )(p_l, p_l, p_l, p_l, p_c, p_c, p_c, p_c, cos, sin, dsum, scales, gch)


def _hyena(p, lp):
    b, t, _ = p.shape
    hb, c = HY_BLOCK, HY_WIDTH
    nb = t // hb
    p3 = _hyena_prep(p, lp['hy_conv_w'], lp['hy_conv_b'][None])
    bias = jnp.broadcast_to(lp['hy_bias'][:, :, None], (HY_ORDER, c, hb))
    z = _hyena_conv(p3, _hyena_filters(t, lp), bias, nb)
    return _hyena_post(z, b)


def _gdn_gate_inputs(p):
    b, t, _ = p.shape
    ab = p[..., OFF_AB:OFF_AB + 4 * GDN_HEADS].reshape(b, t, 4, GDN_HEADS)
    return ab.transpose(0, 3, 2, 1)


def _gdn_gate_params(a_log, dt_bias):
    def rows(x):
        x = jnp.pad(x.T, ((0, 0), (0, 2))).reshape(GDN_HEADS // GDN_HP, 4 * GDN_HP)
        return jnp.broadcast_to(x[:, :, None], x.shape + (LANES,))
    return rows(a_log), rows(dt_bias)


def _rope_tables(t_lat):
    rows_n = t_lat // GRID_W
    row = np.repeat(np.arange(rows_n, dtype=np.float64), GRID_W)
    colp = np.tile(np.arange(GRID_W, dtype=np.float64), rows_n)
    nf = RET_DK // 4
    inv = ROPE_BASE ** (-np.arange(nf, dtype=np.float64) / nf)
    ang = np.concatenate([row[:, None] * inv, colp[:, None] * inv], axis=-1)
    cos, sin = np.cos(ang), np.sin(ang)
    return (jnp.asarray(np.tile(np.concatenate([cos, cos], axis=-1), (1, RET_HEADS)), f32),
            jnp.asarray(np.tile(np.concatenate([-sin, sin], axis=-1), (1, RET_HEADS)), f32))


def _ret_constants(lp):
    c = RET_CHUNK
    lg = jax.nn.log_sigmoid(lp['ret_decay_logit'])
    idx = jnp.arange(c, dtype=f32)
    rel = idx[:, None] - idx[None, :]
    d_f = jnp.exp(jnp.where(rel >= 0, rel * lg[0][:, None, None], -jnp.inf))
    d_b = jnp.exp(jnp.where(rel <= 0, -rel * lg[1][:, None, None], -jnp.inf))
    lane = lambda x: jnp.repeat(x, RET_DK, axis=-1)
    scales = jnp.stack([lane(jnp.exp((idx + 1.0)[:, None] * lg[0])),
                        lane(jnp.exp((c - 1.0 - idx)[:, None] * lg[0])),
                        lane(jnp.exp((c - idx)[:, None] * lg[1])),
                        lane(jnp.exp(idx[:, None] * lg[1]))])
    gch = lane(jnp.exp(c * lg))[:, None, :]
    return d_f + d_b, scales, gch


def _reorder_w_in(w):
    h2 = 2 * GDN_HEADS
    a0 = OFF_Z + GDN_HEADS * GDN_DV
    parts = [w[..., :a0], w[..., a0 + 2 * h2:], w[..., a0:a0 + 2 * h2],
             jnp.zeros(w.shape[:-1] + (LANES - 2 * h2,), w.dtype)]
    return jnp.concatenate(parts, axis=-1)


def kernel(x, c, ctx, c_ctx, mod_w, mod_b, norm1_g, w_in, hy_conv_w, hy_conv_b, hy_f_w1, hy_f_b1, hy_f_freq1, hy_f_w2, hy_f_b2, hy_f_freq2, hy_f_w3, hy_bias, gdn_conv_w, gdn_a_log, gdn_dt_bias, gdn_norm_g, ret_decay_logit, w_out, norm2_g, ffn_w_in, ffn_w_out, final_norm_g):
    bsz, t_lat, d = x.shape
    cos, sin = _rope_tables(t_lat)
    pad_rows = (-(bsz + 1)) % SUBLANES
    cs = jnp.concatenate([c, c_ctx[None], jnp.zeros((pad_rows, d), f32)], axis=0)
    gf = final_norm_g[None]
    mod_w16, w_in16 = mod_w.astype(bf16), _reorder_w_in(w_in).astype(bf16)
    wo16, w116, w216 = w_out.astype(bf16), ffn_w_in.astype(bf16), ffn_w_out.astype(bf16)
    for i in range(DEPTH):
        need_ctx = i < DEPTH - 1
        lp = {'hy_conv_w': hy_conv_w[i], 'hy_conv_b': hy_conv_b[i], 'hy_f_w1': hy_f_w1[i],
              'hy_f_b1': hy_f_b1[i], 'hy_f_freq1': hy_f_freq1[i], 'hy_f_w2': hy_f_w2[i],
              'hy_f_b2': hy_f_b2[i], 'hy_f_freq2': hy_f_freq2[i], 'hy_f_w3': hy_f_w3[i],
              'hy_bias': hy_bias[i], 'ret_decay_logit': ret_decay_logit[i]}
        mod = _mod_vectors(cs, mod_w16, mod_b[i][None], i)
        mod_lat = mod[:bsz].reshape(bsz, N_MOD, d)
        mod_ctx = jnp.broadcast_to(mod[bsz].reshape(1, N_MOD, d), (bsz, N_MOD, d))
        g1 = norm1_g[i][None]
        p_l = _in_proj(x, g1, mod_lat, w_in16, i, 512)
        p_c = _in_proj(ctx, g1, mod_ctx, w_in16, i, 256)

        hy_l = _hyena(p_l, lp)
        pa, pd = _gdn_gate_params(gdn_a_log[i], gdn_dt_bias[i])
        gd_l, gd_c = _gdn(p_l, _gdn_gate_inputs(p_l), p_c, _gdn_gate_inputs(p_c), gdn_conv_w[i], pa, pd,
                          gdn_norm_g[i][None])

        dsum, scales, gch = _ret_constants(lp)
        rt_l, rt_c = _retention(p_l, p_c, cos, sin, dsum, scales, gch)

        g2 = norm2_g[i][None]
        x = _out_ffn(x, hy_l, gd_l, rt_l, mod_lat, wo16, g2, w116, w216, gf, i, 512, final=not need_ctx)
        if need_ctx:
            hy_c = _hyena(p_c, lp)
            ctx = _out_ffn(ctx, hy_c, gd_c, rt_c, mod_ctx, wo16, g2, w116, w216, gf, i, 256, final=False)
    return x
```

```python
import functools
import math

import jax
import jax.numpy as jnp
import numpy as np
from jax import lax
from jax.experimental import pallas as pl
from jax.experimental.pallas import tpu as pltpu

D_MODEL = 1024
DEPTH = 2
GRID_W = 64
EPS = 1e-6
N_MOD = 6

HY_WIDTH = D_MODEL // 4
HY_ORDER = 2
HY_EMB = 33
HY_FAST_DECAY = 0.3
HY_SLOW_DECAY = 1.5
HY_TARGET = 1e-2
GDN_HEADS = D_MODEL // 256
GDN_DK = 128
GDN_DV = 128
RET_HEADS = D_MODEL // 256
RET_DK = 64
RET_DV = 64
ROPE_BASE = 10000.0
FFN_HIDDEN = ((8 * D_MODEL + 3 * 256 - 1) // (3 * 256)) * 256

GDN_CHUNK = 64
GDN_ROWS = 256
GDN_HP = 2
GDN_GROUP = 18
RET_CHUNK = 256
HY_BLOCK = 256
HY_PAIR = 4
HY_IO = 256
LANES = 128
SUBLANES = 8
MOD_COLS = 1536
PROJ_COLS = 512
FFN_COLS = 256

GDN_QKV = 2 * GDN_HEADS * GDN_DK + GDN_HEADS * GDN_DV
OFF_HY = 0
OFF_QKV = OFF_HY + 3 * HY_WIDTH
OFF_Z = OFF_QKV + GDN_QKV
OFF_RET = OFF_Z + GDN_HEADS * GDN_DV
OFF_AB = OFF_RET + 4 * RET_HEADS * RET_DK
IN_PAD = OFF_AB + LANES

V7X_VMEM_BYTES = 64 * 1024 * 1024
VMEM_LIMIT = V7X_VMEM_BYTES - 8 * 1024 * 1024
GDN_VMEM_LIMIT = V7X_VMEM_BYTES - 4 * 1024 * 1024

f32 = jnp.float32
bf16 = jnp.bfloat16


def _cparams(sem):
    return pltpu.CompilerParams(dimension_semantics=sem, vmem_limit_bytes=VMEM_LIMIT)


def _const_spec(shape):
    nd = len(shape)
    return pl.BlockSpec(shape, lambda *_: (0,) * nd, pipeline_mode=pl.Buffered(1))


def _layer_spec(stacked, layer):
    nd = stacked.ndim - 1
    return pl.BlockSpec((None,) + stacked.shape[1:], lambda *_: (layer,) + (0,) * nd,
                        pipeline_mode=pl.Buffered(1))


def _rms(x, g):
    return x * lax.rsqrt(jnp.mean(x * x, axis=-1, keepdims=True) + EPS) * g


def _silu(x):
    return x * jax.nn.sigmoid(x)


def _dot(a, b):
    return jnp.dot(a.astype(bf16), b.astype(bf16), preferred_element_type=f32)


def _dot_nt(a, b):
    return lax.dot_general(a.astype(bf16), b.astype(bf16), (((1,), (1,)), ((), ())),
                           preferred_element_type=f32)


def _dot_tn(a, b):
    return lax.dot_general(a.astype(bf16), b.astype(bf16), (((0,), (0,)), ((), ())),
                           preferred_element_type=f32)


def _mod_kernel(c_ref, w_ref, b_ref, o_ref):
    o_ref[...] = _dot(_silu(c_ref[...]), w_ref[...]) + b_ref[...]


def _mod_vectors(cs, w16, b, layer):
    rows, d = cs.shape
    n = w16.shape[2]
    tn = MOD_COLS
    return pl.pallas_call(
        _mod_kernel,
        grid=(n // tn,),
        in_specs=[pl.BlockSpec((rows, d), lambda j: (0, 0)),
                  pl.BlockSpec((None, d, tn), lambda j: (layer, 0, j)),
                  pl.BlockSpec((1, tn), lambda j: (0, j))],
        out_specs=pl.BlockSpec((rows, tn), lambda j: (0, j)),
        out_shape=jax.ShapeDtypeStruct((rows, n), f32),
        compiler_params=_cparams(("parallel",)),
        name="mod_vectors",
    )(cs, w16, b)


def _in_proj_kernel(x_ref, g_ref, mod_ref, w_ref, o_ref):
    x = x_ref[0]
    mod = mod_ref[0]
    h = _rms(x, g_ref[...]) * (1.0 + mod[1:2]) + mod[0:1]
    h16 = h.astype(bf16)
    n = w_ref.shape[1]
    step = PROJ_COLS
    for j in range(0, n, step):
        e = min(j + step, n)
        o_ref[0, :, j:e] = jnp.dot(h16, w_ref[:, j:e], preferred_element_type=f32)


def _in_proj(x, g, mod, w16, layer, tm):
    b, t, d = x.shape
    n = w16.shape[2]
    return pl.pallas_call(
        _in_proj_kernel,
        grid=(b, t // tm),
        in_specs=[pl.BlockSpec((1, tm, d), lambda i, j: (i, j, 0)),
                  _const_spec((1, d)),
                  pl.BlockSpec((1, N_MOD, d), lambda i, j: (i, 0, 0)),
                  _layer_spec(w16, layer)],
        out_specs=pl.BlockSpec((1, tm, n), lambda i, j: (i, j, 0)),
        out_shape=jax.ShapeDtypeStruct((b, t, n), f32),
        compiler_params=_cparams(("parallel", "parallel")),
        name="in_proj",
    )(x, g, mod, w16)


def _out_ffn_kernel(x_ref, hy_ref, gd_ref, rt_ref, mod_ref, wo_ref, g2_ref, w1_ref, w2_ref, gf_ref, o_ref,
                    *, final):
    x = x_ref[0]
    mod = mod_ref[0]
    o0, o1, o2 = HY_WIDTH, HY_WIDTH + GDN_HEADS * GDN_DV, D_MODEL
    y = (jnp.dot(hy_ref[0].astype(bf16), wo_ref[0:o0, :], preferred_element_type=f32)
         + jnp.dot(gd_ref[0].astype(bf16), wo_ref[o0:o1, :], preferred_element_type=f32)
         + jnp.dot(rt_ref[0].astype(bf16), wo_ref[o1:o2, :], preferred_element_type=f32))
    x = x + mod[2:3] * y
    h16 = (_rms(x, g2_ref[...]) * (1.0 + mod[4:5]) + mod[3:4]).astype(bf16)
    step = FFN_COLS
    acc = jnp.zeros_like(x)
    for j in range(0, FFN_HIDDEN, step):
        gate = jnp.dot(h16, w1_ref[:, j:j + step], preferred_element_type=f32)
        up = jnp.dot(h16, w1_ref[:, FFN_HIDDEN + j:FFN_HIDDEN + j + step], preferred_element_type=f32)
        acc = acc + jnp.dot((_silu(gate) * up).astype(bf16), w2_ref[j:j + step, :], preferred_element_type=f32)
    x = x + mod[5:6] * acc
    if final:
        x = _rms(x, gf_ref[...])
    o_ref[0] = x


def _out_ffn(x, hy, gd, rt, mod, wo16, g2, w116, w216, gf, layer, tm, final):
    b, t, d = x.shape
    tok = lambda w: pl.BlockSpec((1, tm, w), lambda i, j: (i, j, 0))
    return pl.pallas_call(
        functools.partial(_out_ffn_kernel, final=final),
        grid=(b, t // tm),
        in_specs=[tok(d), tok(hy.shape[-1]), tok(gd.shape[-1]), tok(rt.shape[-1]),
                  pl.BlockSpec((1, N_MOD, d), lambda i, j: (i, 0, 0)),
                  _layer_spec(wo16, layer), _const_spec((1, d)), _layer_spec(w116, layer),
                  _layer_spec(w216, layer), _const_spec((1, d))],
        out_specs=tok(d),
        out_shape=jax.ShapeDtypeStruct((b, t, d), f32),
        compiler_params=_cparams(("parallel", "parallel")),
        name="out_ffn",
    )(x, hy, gd, rt, mod, wo16, g2, w116, w216, gf)


def _hyena_kernel(v_ref, x1_ref, x2_ref, k_ref, b_ref, o_ref, acc_ref, *, nb, cb):
    hb = HY_BLOCK
    rb = v_ref.shape[1] // nb

    def conv(u16, kext):
        m = range(len(u16))
        for d in range(-(nb - 1), nb):
            dd = d % (2 * nb)
            t0, t1 = max(0, d), min(nb - 1, nb - 1 + d) + 1
            for i in m:
                win = jnp.broadcast_to(kext[i][:, hb * dd:hb * dd + 2 * hb], (hb, 2 * hb))
                toep = pltpu.roll(win, 0, 1, stride=1, stride_axis=0)[:, hb:].astype(bf16)
                contrib = jnp.dot(u16[i][rb * (t0 - d):rb * (t1 - d)], toep, preferred_element_type=f32)
                if d == -(nb - 1):
                    acc_ref[i] = jnp.zeros(acc_ref.shape[1:], f32)
                acc_ref[i, rb * t0:rb * t1, :] += contrib
        return [acc_ref[i] for i in m]

    def body(g, carry):
        cs = [HY_PAIR * g + i for i in range(HY_PAIR)]
        z = [v_ref[c] for c in cs]
        for n, gate_ref in enumerate((x1_ref, x2_ref)):
            y = conv([x.astype(bf16) for x in z], [k_ref[n, pl.ds(c, 1), :] for c in cs])
            z = [gate_ref[c] * (y[i] + z[i] * b_ref[n, pl.ds(c, 1), :]) for i, c in enumerate(cs)]
        for i, c in enumerate(cs):
            o_ref[c] = z[i]
        return carry

    lax.fori_loop(0, cb // HY_PAIR, body, 0)


def _hyena_conv(p3, kext, bias, nb, cb=SUBLANES):
    c3, r, hb = p3.shape
    c = c3 // 3
    klen = kext.shape[-1]
    sig = lambda s: pl.BlockSpec((cb, r, hb), lambda i, s=s: (s * (c // cb) + i, 0, 0))
    return pl.pallas_call(
        functools.partial(_hyena_kernel, nb=nb, cb=cb),
        grid=(c // cb,),
        in_specs=[sig(0), sig(1), sig(2),
                  pl.BlockSpec((HY_ORDER, cb, klen), lambda i: (0, i, 0)),
                  pl.BlockSpec((HY_ORDER, cb, hb), lambda i: (0, i, 0))],
        out_specs=pl.BlockSpec((cb, r, hb), lambda i: (i, 0, 0)),
        out_shape=jax.ShapeDtypeStruct((c, r, hb), f32),
        scratch_shapes=[pltpu.VMEM((HY_PAIR, r, hb), f32)],
        compiler_params=_cparams(("parallel",)),
        name=f"hyena_conv_nb{nb}",
    )(p3, p3, p3, kext, bias)


def _hyena_prep_kernel(x_ref, before_ref, after_ref, w_ref, b_ref, o_ref):
    hb = HY_BLOCK
    w = w_ref[...]
    row = lax.broadcasted_iota(jnp.int32, x_ref.shape[1:], 0)
    s = pl.program_id(1)
    has_before = s > 0
    has_after = s < pl.num_programs(1) - 1
    for i in range(x_ref.shape[0]):
        x = x_ref[i]
        prev = jnp.where(has_before, before_ref[i, SUBLANES - 1:SUBLANES, :], 0.0)
        nxt = jnp.where(has_after, after_ref[i, 0:1, :], 0.0)
        xp = jnp.where(row == 0, prev, pltpu.roll(x, 1, 0))
        xn = jnp.where(row == hb - 1, nxt, pltpu.roll(x, hb - 1, 0))
        y = xp * w[0:1] + x * w[1:2] + xn * w[2:3] + b_ref[...]
        o_ref[:, i, :] = y.T


def _hyena_prep(p, conv_w, conv_b):
    b, t, _ = p.shape
    hb = HY_BLOCK
    n = 3 * HY_WIDTH
    per = hb // SUBLANES
    last = t // SUBLANES - 1
    assert OFF_HY == 0 and b % SUBLANES == 0
    return pl.pallas_call(
        _hyena_prep_kernel,
        grid=(n // HY_IO, t // hb),
        in_specs=[pl.BlockSpec((b, hb, HY_IO), lambda j, s: (0, s, j)),
                  pl.BlockSpec((b, SUBLANES, HY_IO), lambda j, s: (0, jnp.maximum(s * per - 1, 0), j)),
                  pl.BlockSpec((b, SUBLANES, HY_IO), lambda j, s: (0, jnp.minimum((s + 1) * per, last), j)),
                  pl.BlockSpec((3, HY_IO), lambda j, s: (0, j)),
                  pl.BlockSpec((1, HY_IO), lambda j, s: (0, j))],
        out_specs=pl.BlockSpec((HY_IO, b, hb), lambda j, s: (j, s, 0)),
        out_shape=jax.ShapeDtypeStruct((n, (t // hb) * b, hb), f32),
        compiler_params=_cparams(("parallel", "parallel")),
        name="hyena_prep",
    )(p, p, p, conv_w, conv_b)


def _hyena_post_kernel(z_ref, o_ref):
    for i in range(z_ref.shape[1]):
        o_ref[i] = z_ref[:, i, :].T


def _hyena_post(z, b):
    c, r, hb = z.shape
    nb = r // b
    return pl.pallas_call(
        _hyena_post_kernel,
        grid=(c // HY_IO, nb),
        in_specs=[pl.BlockSpec((HY_IO, b, hb), lambda j, s: (j, s, 0))],
        out_specs=pl.BlockSpec((b, hb, HY_IO), lambda j, s: (0, s, j)),
        out_shape=jax.ShapeDtypeStruct((b, nb * hb, c), f32),
        compiler_params=_cparams(("parallel", "parallel")),
        name="hyena_post",
    )(z)


def _hyena_filter_kernel(z_ref, w1_ref, b1_ref, f1_ref, w2_ref, b2_ref, f2_ref, w3_ref, win_ref, o_ref):
    hi = lax.Precision.HIGHEST
    hb = HY_BLOCK
    t = z_ref.shape[1]
    for side in range(2):
        h = jnp.sin(f1_ref[...] * (jnp.dot(z_ref[side], w1_ref[...], precision=hi,
                                           preferred_element_type=f32) + b1_ref[...]))
        h = jnp.sin(f2_ref[...] * (jnp.dot(h, w2_ref[...], precision=hi, preferred_element_type=f32)
                                   + b2_ref[...]))
        for n in range(HY_ORDER):
            w3 = w3_ref[:, (2 * n + side) * HY_WIDTH:(2 * n + side + 1) * HY_WIDTH]
            k = lax.dot_general(w3, h, (((0,), (1,)), ((), ())), precision=hi,
                                preferred_element_type=f32) * win_ref[side]
            o_ref[n, :, hb + side * t:hb + (side + 1) * t] = k
            if side == 1:
                o_ref[n, :, 0:hb] = k[:, t - hb:]


def _hyena_tables(t_len):
    pos = np.arange(t_len, dtype=np.float64)
    t = np.linspace(0.0, 1.0, t_len)
    bands = (HY_EMB - 1) // 2
    f = np.linspace(1e-4, bands - 1, bands)
    ang = (2.0 * math.pi / t_len) * pos[:, None] * f[None, :]
    z = np.concatenate([t[:, None], np.cos(ang), -np.sin(ang)], axis=-1)
    max_decay = math.log(HY_TARGET) / HY_FAST_DECAY
    min_decay = math.log(HY_TARGET) / HY_SLOW_DECAY
    deltas = np.abs(np.linspace(min_decay, max_decay, HY_WIDTH))
    window = np.exp(-t[:, None] * deltas[None, :])
    neg = lambda x: np.concatenate([x[:1], x[:0:-1]], axis=0)
    zs = np.pad(np.stack([z, neg(z)]), ((0, 0), (0, 0), (0, LANES - HY_EMB)))
    wneg = neg(window)
    wneg[0] = 0.0
    win = np.stack([window, wneg]).transpose(0, 2, 1)
    return jnp.asarray(zs, f32), jnp.asarray(win, f32)


def _hyena_filters(t_len, lp):
    zs, win = _hyena_tables(t_len)
    w1 = jnp.pad(lp['hy_f_w1'], ((0, LANES - HY_EMB), (0, 0)))
    row = lambda x: x[None]
    args = (zs, w1, row(lp['hy_f_b1']), row(lp['hy_f_freq1']), lp['hy_f_w2'], row(lp['hy_f_b2']),
            row(lp['hy_f_freq2']), lp['hy_f_w3'], win)
    klen = 2 * t_len + HY_BLOCK
    return pl.pallas_call(
        _hyena_filter_kernel,
        grid=(1,),
        in_specs=[_const_spec(a.shape) for a in args],
        out_specs=pl.BlockSpec((HY_ORDER, HY_WIDTH, klen), lambda i: (0, 0, 0)),
        out_shape=jax.ShapeDtypeStruct((HY_ORDER, HY_WIDTH, klen), f32),
        compiler_params=_cparams(("arbitrary",)),
        name="hyena_filters",
    )(*args)


def _gdn_kernel(ql_ref, kl_ref, vl_ref, zl_ref, abl_ref, qc_ref, kc_ref, vc_ref, zc_ref, abc_ref,
                cwq_ref, cwk_ref, cwv_ref, pa_ref, pd_ref, ng_ref,
                ol_ref, oc_ref,
                q_s, k_s, v_s, cf_s, at_s, n_s, mp_s, egl_s, o_s, st_s):
    c = GDN_CHUNK
    rb = GDN_ROWS
    tc, tl = qc_ref.shape[1], ql_ref.shape[1]
    ncc, nc = tc // c, (tc + tl) // c
    dk = GDN_DK

    def conv_rows(x_ref, w_ref, r0, t, ls):
        x = x_ref[0, pl.ds(r0, rb), ls]
        before = pl.multiple_of(jnp.maximum(r0 - SUBLANES, 0), SUBLANES)
        after = pl.multiple_of(jnp.minimum(r0 + rb, t - SUBLANES), SUBLANES)
        prev = x_ref[0, pl.ds(before, SUBLANES), ls][SUBLANES - 1:SUBLANES]
        nxt = x_ref[0, pl.ds(after, SUBLANES), ls][0:1]
        prev = jnp.where(r0 > 0, prev, 0.0)
        nxt = jnp.where(r0 + rb < t, nxt, 0.0)
        row = lax.broadcasted_iota(jnp.int32, x.shape, 0)
        xp = jnp.where(row == 0, prev, pltpu.roll(x, 1, 0))
        xn = jnp.where(row == rb - 1, nxt, pltpu.roll(x, rb - 1, 0))
        w = w_ref[:, ls]
        return _silu(xp * w[0:1] + x * w[1:2] + xn * w[2:3])

    def l2n(x):
        return x * lax.rsqrt(jnp.sum(x * x, axis=-1, keepdims=True) + EPS)

    def prep(hh, q_ref, k_ref, v_ref, t, base):
        ls = slice(hh * LANES, (hh + 1) * LANES)

        def body(i, carry):
            r0 = pl.multiple_of(i * rb, rb)
            dst = pl.ds(pl.multiple_of(base + i * rb, rb), rb)
            q_s[dst, :] = l2n(conv_rows(q_ref, cwq_ref, r0, t, ls)) * (dk ** -0.5)
            k_s[dst, :] = l2n(conv_rows(k_ref, cwk_ref, r0, t, ls))
            v_s[dst, :] = conv_rows(v_ref, cwv_ref, r0, t, ls)
            return carry
        lax.fori_loop(0, t // rb, body, 0)

    def gates(ab_ref, t, base):
        ab = jnp.concatenate([ab_ref[0, hh] for hh in range(GDN_HP)], axis=0)
        kind = lax.broadcasted_iota(jnp.int32, ab.shape, 0) % 4
        pos = lax.broadcasted_iota(jnp.int32, ab.shape, 1) % c
        tile = lambda p: jnp.concatenate([p] * (t // LANES), axis=1)
        g = -jnp.exp(tile(pa_ref[0])) * jax.nn.softplus(ab + tile(pd_ref[0]))
        acc_f, acc_b = g, g
        s = 1
        while s < c:
            acc_f = acc_f + jnp.where(pos >= s, pltpu.roll(acc_f, s, 1), 0.0)
            acc_b = acc_b + jnp.where(pos < c - s, pltpu.roll(acc_b, t - s, 1), 0.0)
            s *= 2
        cf_t = jnp.where(kind == 0, acc_f, jnp.where(kind == 1, acc_b, jax.nn.sigmoid(ab)))
        pad = jnp.zeros((LANES - SUBLANES, LANES), f32)
        for j in range(t // LANES):
            blk = jnp.concatenate([cf_t[:, j * LANES:(j + 1) * LANES], pad], axis=0)
            cf_s[base + j * LANES:base + (j + 1) * LANES, :] = blk.T

    gates(abc_ref, tc, 0)
    gates(abl_ref, tl, tc)

    lane = lax.broadcasted_iota(jnp.int32, (c, 2 * c), 1)
    row = lax.broadcasted_iota(jnp.int32, (c, 2 * c), 0)
    fwd = lane < c
    col = lane % c
    incl = (fwd & (row >= col)) | (~fwd & (row <= col))
    strict = (fwd & (row > col)) | (~fwd & (row < col))

    def blockdiag(a):
        return jnp.concatenate([jnp.where(fwd, a, 0.0), jnp.where(fwd, 0.0, a)], axis=0).astype(bf16)

    def chunk_group(hh, cis):
        n = range(len(cis))
        xf, xb = 2 * hh, 2 * hh + 1
        rows = [pl.ds(pl.multiple_of(ci * c, c), c) for ci in cis]
        cf = [cf_s[r, :] for r in rows]
        k = [k_s[r, :] for r in rows]
        q = [q_s[r, :] for r in rows]
        l0 = 4 * hh
        bc = lambda x, j: jnp.broadcast_to(x[:, l0 + j:l0 + j + 1], (c, LANES))
        gf_c, gb_c = [bc(x, 0) for x in cf], [bc(x, 1) for x in cf]
        bf_c, bb_c = [bc(x, 2) for x in cf], [bc(x, 3) for x in cf]
        cf_t = [jnp.concatenate([x, x], axis=0).T for x in cf]
        g_r = [jnp.where(fwd, jnp.broadcast_to(x[l0:l0 + 1], (c, 2 * c)),
                         jnp.broadcast_to(x[l0 + 1:l0 + 2], (c, 2 * c))) for x in cf_t]
        decay = [jnp.where(incl, jnp.exp(jnp.where(incl, jnp.where(fwd, gf_c[g], gb_c[g]) - g_r[g], 0.0)), 0.0)
                 for g in n]
        kq = [_dot_nt(jnp.concatenate([k[g], q[g]], axis=0), jnp.concatenate([k[g], k[g]], axis=0))
              for g in n]
        lm = [jnp.where(strict, kq[g][:c] * jnp.where(fwd, bf_c[g], bb_c[g]) * decay[g], 0.0) for g in n]
        for g in n:
            at_s[g] = (kq[g][c:] * decay[g]).astype(bf16)
        a = [jnp.dot(x.astype(bf16), blockdiag(x), preferred_element_type=f32) for x in lm]
        tp = [-x for x in lm]
        p = 2
        while 2 * p < c:
            ta = [jnp.dot(jnp.concatenate([tp[g], a[g]], axis=0).astype(bf16), blockdiag(a[g]),
                          preferred_element_type=f32) for g in n]
            tp = [tp[g] + a[g] + ta[g][:c] for g in n]
            a = [x[c:] for x in ta]
            p *= 2
        tp = [tp[g] + a[g] + jnp.dot(tp[g].astype(bf16), blockdiag(a[g]), preferred_element_type=f32)
              for g in n]
        eg_f, eg_b = [jnp.exp(x) for x in gf_c], [jnp.exp(x) for x in gb_c]
        uw = []
        for g in n:
            v = v_s[rows[g], :]
            rhs = jnp.concatenate([jnp.concatenate([v * bf_c[g], k[g] * (bf_c[g] * eg_f[g])], axis=1),
                                   jnp.concatenate([v * bb_c[g], k[g] * (bb_c[g] * eg_b[g])], axis=1)], axis=0)
            uw.append(rhs + jnp.dot(blockdiag(tp[g]), rhs.astype(bf16), preferred_element_type=f32))
        gl = [(gf_c[g][c - 1:c], gb_c[g][0:1]) for g in n]
        res = []
        for g in n:
            attn = at_s[g]
            for d, (half, g_c) in enumerate(((slice(0, c), gf_c[g]), (slice(c, 2 * c), gb_c[g]))):
                kgt = (k[g] * jnp.exp(gl[g][d] - g_c)).T.astype(bf16)
                res.append(jnp.dot(jnp.concatenate([kgt, attn[:, half]], axis=0), uw[g][half].astype(bf16),
                                   preferred_element_type=f32))
        for g in n:
            ci = cis[g]
            for d, (x, eg) in enumerate(((xf, eg_f[g]), (xb, eg_b[g]))):
                r = res[2 * g + d]
                n_s[x, ci] = r[:dk, :dk]
                mp_s[x, ci] = jnp.concatenate([r[:dk, dk:], q[g] * eg - r[dk:, dk:]], axis=0).astype(bf16)
            egl_s[hh, ci] = jnp.concatenate([jnp.exp(gl[g][0]), jnp.exp(gl[g][1]),
                                             jnp.zeros((SUBLANES - 2, LANES), f32)], axis=0)
            o_s[hh, rows[g], :] = res[2 * g][dk:, :dk] + res[2 * g + 1][dk:, :dk]

    for hh in range(GDN_HP):
        prep(hh, qc_ref, kc_ref, vc_ref, tc, 0)
        prep(hh, ql_ref, kl_ref, vl_ref, tl, tc)

        def chunk_body(i, carry, hh=hh):
            chunk_group(hh, [GDN_GROUP * i + g for g in range(GDN_GROUP)])
            return carry

        lax.fori_loop(0, nc // GDN_GROUP, chunk_body, 0)

    st_s[...] = jnp.zeros_like(st_s)
    chains = [(hh, d) for hh in range(GDN_HP) for d in range(2)]

    def scan_body(i, carry):
        ci = (i, jnp.where(i < ncc, ncc - 1 - i, nc + ncc - 1 - i))
        rows = [pl.ds(pl.multiple_of(x * c, c), c) for x in ci]
        n = range(len(chains))
        s = [st_s[x] for x in n]
        res = [jnp.dot(mp_s[x, ci[chains[x][1]]], s[x].astype(bf16), preferred_element_type=f32) for x in n]
        for x in n:
            hh, d = chains[x]
            st_s[x] = s[x] * egl_s[hh, ci[d]][d:d + 1] + n_s[x, ci[d]] - res[x][:dk]
            o_s[hh, rows[d], :] += res[x][dk:]
        return carry

    lax.fori_loop(0, nc, scan_body, 0)

    def finish(hh, z_ref, o_ref, t, base):
        ls = slice(hh * LANES, (hh + 1) * LANES)

        def body(i, carry):
            r0 = pl.ds(pl.multiple_of(i * rb, rb), rb)
            o = o_s[hh, pl.ds(pl.multiple_of(base + i * rb, rb), rb), :]
            y = o * lax.rsqrt(jnp.mean(o * o, axis=-1, keepdims=True) + EPS) * ng_ref[...]
            o_ref[0, r0, ls] = y * _silu(z_ref[0, r0, ls])
            return carry
        lax.fori_loop(0, t // rb, body, 0)

    for hh in range(GDN_HP):
        finish(hh, zc_ref, oc_ref, tc, 0)
        finish(hh, zl_ref, ol_ref, tl, tc)


def _gdn(p_l, ab_l, p_c, ab_c, conv_w, pa, pd, norm_g):
    b, tl, _ = p_l.shape
    tc = p_c.shape[1]
    h, c, hp = GDN_HEADS, GDN_CHUNK, GDN_HP
    nc = (tl + tc) // c
    assert tl % GDN_ROWS == 0 and tc % GDN_ROWS == 0 and nc % GDN_GROUP == 0
    qb, zb = OFF_QKV // LANES, OFF_Z // LANES
    assert h % hp == 0 and qb % hp == 0 and zb % hp == 0
    w = hp * LANES

    def specs(t):
        heads = lambda off: pl.BlockSpec((1, t, w), lambda i, j, off=off // hp: (i, 0, off + j))
        return [heads(qb), heads(qb + h), heads(qb + 2 * h), heads(zb),
                pl.BlockSpec((1, hp, 4, t), lambda i, j: (i, j, 0, 0))]

    cw = lambda off: pl.BlockSpec((3, w), lambda i, j, off=off // hp: (0, off + j))
    assert 4 * hp == SUBLANES
    par = pl.BlockSpec((1, SUBLANES, LANES), lambda i, j: (j, 0, 0))
    out = lambda t: pl.BlockSpec((1, t, w), lambda i, j: (i, 0, j))
    tt = tl + tc
    return pl.pallas_call(
        _gdn_kernel,
        grid=(b, h // hp),
        in_specs=specs(tl) + specs(tc) + [cw(0), cw(h), cw(2 * h), par, par, _const_spec((1, GDN_DV))],
        out_specs=[out(tl), out(tc)],
        out_shape=[jax.ShapeDtypeStruct((b, tl, h * GDN_DV), f32),
                   jax.ShapeDtypeStruct((b, tc, h * GDN_DV), f32)],
        scratch_shapes=[pltpu.VMEM((tt, GDN_DK), f32), pltpu.VMEM((tt, GDN_DK), f32),
                        pltpu.VMEM((tt, GDN_DV), f32), pltpu.VMEM((tt, LANES), f32),
                        pltpu.VMEM((GDN_GROUP, c, 2 * c), bf16),
                        pltpu.VMEM((2 * hp, nc, GDN_DK, GDN_DV), f32),
                        pltpu.VMEM((2 * hp, nc, GDN_DK + c, GDN_DV), bf16),
                        pltpu.VMEM((hp, nc, SUBLANES, LANES), f32), pltpu.VMEM((hp, tt, GDN_DV), f32),
                        pltpu.VMEM((2 * hp, GDN_DK, GDN_DV), f32)],
        compiler_params=pltpu.CompilerParams(dimension_semantics=("parallel", "parallel"),
                                             vmem_limit_bytes=GDN_VMEM_LIMIT),
        name="gated_deltanet",
    )(p_l, p_l, p_l, p_l, ab_l, p_c, p_c, p_c, p_c, ab_c, conv_w, conv_w, conv_w, pa, pd, norm_g)


def _ret_kernel(ql_ref, kl_ref, vl_ref, gl_ref, qc_ref, kc_ref, vc_ref, gc_ref,
                cos_ref, sin_ref, dsum_ref, sc_ref, gch_ref,
                ol_ref, oc_ref, sf_ref, sb_ref, q_s, k_s, o_s):
    c = RET_CHUNK
    w = RET_HEADS * RET_DK
    ri = lax.broadcasted_iota(jnp.int32, (w, w), 0) // RET_DK
    ci = lax.broadcasted_iota(jnp.int32, (w, w), 1) // RET_DK
    blockdiag = ri == ci
    lane_head = lax.broadcasted_iota(jnp.int32, (c, w), 1) // RET_DK

    def intra(q, k, v, r, cross):
        k16 = k.astype(bf16)
        for h in range(RET_HEADS):
            m = lane_head == h
            scores = _dot_nt(jnp.where(m, q, 0.0), k16) * dsum_ref[h]
            part = _dot(scores, jnp.where(m, v, 0.0))
            if h == 0:
                o_s[r, :] = part if cross is None else part + cross
            else:
                o_s[r, :] += part

    def update(s_ref, k, v, d):
        new = jnp.where(blockdiag, _dot_tn(k * sc_ref[2 * d + 1], v), 0.0)
        s_ref[...] = s_ref[...] * gch_ref[d] + new

    first_half = lax.broadcasted_iota(jnp.int32, (c, w), 1) % RET_DK < RET_DK // 2
    avg = jnp.where(blockdiag, 1.0 / RET_DV, 0.0).astype(bf16)

    def rope(x, r):
        swapped = jnp.where(first_half, pltpu.roll(x, w - RET_DK // 2, 1), pltpu.roll(x, RET_DK // 2, 1))
        return x * cos_ref[r, :] + swapped * sin_ref[r, :]

    def head_mean(x):
        hi = x.astype(bf16)
        lo = (x - hi.astype(f32)).astype(bf16)
        return jnp.dot(hi, avg, preferred_element_type=f32) + jnp.dot(lo, avg, preferred_element_type=f32)

    def finish(o, gate):
        d = o - head_mean(o)
        return d * lax.rsqrt(head_mean(d * d) + EPS) * _silu(gate)

    sf_ref[...] = jnp.zeros_like(sf_ref)
    sb_ref[...] = jnp.zeros_like(sb_ref)
    qc, kc, vc = qc_ref[0], kc_ref[0] * (RET_DK ** -0.5), vc_ref[0]
    r0 = slice(0, c)
    intra(qc, kc, vc, r0, None)
    oc_ref[0] = finish(o_s[r0, :], gc_ref[0])
    update(sf_ref, kc, vc, 0)
    update(sb_ref, kc, vc, 1)
    n = ql_ref.shape[1] // c
    for i in range(n):
        r = slice(i * c, (i + 1) * c)
        q = rope(ql_ref[0, r, :], r)
        k = rope(kl_ref[0, r, :] * (RET_DK ** -0.5), r)
        v = vl_ref[0, r, :]
        q_s[r, :] = q
        k_s[r, :] = k
        intra(q, k, v, r, _dot(q * sc_ref[0], sf_ref[...]))
        update(sf_ref, k, v, 0)
    for i in range(n - 1, -1, -1):
        r = slice(i * c, (i + 1) * c)
        o = o_s[r, :] + _dot(q_s[r, :] * sc_ref[2], sb_ref[...])
        ol_ref[0, r, :] = finish(o, gl_ref[0, r, :])
        update(sb_ref, k_s[r, :], vl_ref[0, r, :], 1)


def _retention(p_l, p_c, cos, sin, dsum, scales, gch):
    b, tl, _ = p_l.shape
    tc = p_c.shape[1]
    w = RET_HEADS * RET_DK
    assert OFF_RET % w == 0
    blk = lambda t, j: pl.BlockSpec((1, t, w), lambda i, j=j: (i, 0, OFF_RET // w + j))
    tok = lambda t: pl.BlockSpec((1, t, w), lambda i: (i, 0, 0))
    return pl.pallas_call(
        _ret_kernel,
        grid=(b,),
        in_specs=[blk(tl, j) for j in range(4)] + [blk(tc, j) for j in range(4)]
        + [_const_spec(x.shape) for x in (cos, sin, dsum, scales, gch)],
        out_specs=[tok(tl), tok(tc)],
        out_shape=[jax.ShapeDtypeStruct((b, tl, w), f32), jax.ShapeDtypeStruct((b, tc, w), f32)],
        scratch_shapes=[pltpu.VMEM((w, w), f32), pltpu.VMEM((w, w), f32),
                        pltpu.VMEM((tl, w), f32), pltpu.VMEM((tl, w), f32), pltpu.VMEM((tl, w), f32)],
        compiler_params=_cparams(("parallel",)),
        name="retention",
    )(p_l, p_l, p_l, p_l, p_c, p_c, p_c, p_c, cos, sin, dsum, scales, gch)


def _hyena(p, lp):
    b, t, _ = p.shape
    hb, c = HY_BLOCK, HY_WIDTH
    nb = t // hb
    p3 = _hyena_prep(p, lp['hy_conv_w'], lp['hy_conv_b'][None])
    bias = jnp.broadcast_to(lp['hy_bias'][:, :, None], (HY_ORDER, c, hb))
    z = _hyena_conv(p3, _hyena_filters(t, lp), bias, nb)
    return _hyena_post(z, b)


def _gdn_gate_inputs(p):
    b, t, _ = p.shape
    ab = p[..., OFF_AB:OFF_AB + 4 * GDN_HEADS].reshape(b, t, 4, GDN_HEADS)
    return ab.transpose(0, 3, 2, 1)


def _gdn_gate_params(a_log, dt_bias):
    def rows(x):
        x = jnp.pad(x.T, ((0, 0), (0, 2))).reshape(GDN_HEADS // GDN_HP, 4 * GDN_HP)
        return jnp.broadcast_to(x[:, :, None], x.shape + (LANES,))
    return rows(a_log), rows(dt_bias)


def _rope_tables(t_lat):
    rows_n = t_lat // GRID_W
    row = np.repeat(np.arange(rows_n, dtype=np.float64), GRID_W)
    colp = np.tile(np.arange(GRID_W, dtype=np.float64), rows_n)
    nf = RET_DK // 4
    inv = ROPE_BASE ** (-np.arange(nf, dtype=np.float64) / nf)
    ang = np.concatenate([row[:, None] * inv, colp[:, None] * inv], axis=-1)
    cos, sin = np.cos(ang), np.sin(ang)
    return (jnp.asarray(np.tile(np.concatenate([cos, cos], axis=-1), (1, RET_HEADS)), f32),
            jnp.asarray(np.tile(np.concatenate([-sin, sin], axis=-1), (1, RET_HEADS)), f32))


def _ret_constants(lp):
    c = RET_CHUNK
    lg = jax.nn.log_sigmoid(lp['ret_decay_logit'])
    idx = jnp.arange(c, dtype=f32)
    rel = idx[:, None] - idx[None, :]
    d_f = jnp.exp(jnp.where(rel >= 0, rel * lg[0][:, None, None], -jnp.inf))
    d_b = jnp.exp(jnp.where(rel <= 0, -rel * lg[1][:, None, None], -jnp.inf))
    lane = lambda x: jnp.repeat(x, RET_DK, axis=-1)
    scales = jnp.stack([lane(jnp.exp((idx + 1.0)[:, None] * lg[0])),
                        lane(jnp.exp((c - 1.0 - idx)[:, None] * lg[0])),
                        lane(jnp.exp((c - idx)[:, None] * lg[1])),
                        lane(jnp.exp(idx[:, None] * lg[1]))])
    gch = lane(jnp.exp(c * lg))[:, None, :]
    return d_f + d_b, scales, gch


def _reorder_w_in(w):
    h2 = 2 * GDN_HEADS
    a0 = OFF_Z + GDN_HEADS * GDN_DV
    parts = [w[..., :a0], w[..., a0 + 2 * h2:], w[..., a0:a0 + 2 * h2],
             jnp.zeros(w.shape[:-1] + (LANES - 2 * h2,), w.dtype)]
    return jnp.concatenate(parts, axis=-1)


def kernel(x, c, ctx, c_ctx, mod_w, mod_b, norm1_g, w_in, hy_conv_w, hy_conv_b, hy_f_w1, hy_f_b1, hy_f_freq1, hy_f_w2, hy_f_b2, hy_f_freq2, hy_f_w3, hy_bias, gdn_conv_w, gdn_a_log, gdn_dt_bias, gdn_norm_g, ret_decay_logit, w_out, norm2_g, ffn_w_in, ffn_w_out, final_norm_g):
    bsz, t_lat, d = x.shape
    cos, sin = _rope_tables(t_lat)
    pad_rows = (-(bsz + 1)) % SUBLANES
    cs = jnp.concatenate([c, c_ctx[None], jnp.zeros((pad_rows, d), f32)], axis=0)
    gf = final_norm_g[None]
    mod_w16, w_in16 = mod_w.astype(bf16), _reorder_w_in(w_in).astype(bf16)
    wo16, w116, w216 = w_out.astype(bf16), ffn_w_in.astype(bf16), ffn_w_out.astype(bf16)
    for i in range(DEPTH):
        need_ctx = i < DEPTH - 1
        lp = {'hy_conv_w': hy_conv_w[i], 'hy_conv_b': hy_conv_b[i], 'hy_f_w1': hy_f_w1[i],
              'hy_f_b1': hy_f_b1[i], 'hy_f_freq1': hy_f_freq1[i], 'hy_f_w2': hy_f_w2[i],
              'hy_f_b2': hy_f_b2[i], 'hy_f_freq2': hy_f_freq2[i], 'hy_f_w3': hy_f_w3[i],
              'hy_bias': hy_bias[i], 'ret_decay_logit': ret_decay_logit[i]}
        mod = _mod_vectors(cs, mod_w16, mod_b[i][None], i)
        mod_lat = mod[:bsz].reshape(bsz, N_MOD, d)
        mod_ctx = jnp.broadcast_to(mod[bsz].reshape(1, N_MOD, d), (bsz, N_MOD, d))
        g1 = norm1_g[i][None]
        p_l = _in_proj(x, g1, mod_lat, w_in16, i, 512)
        p_c = _in_proj(ctx, g1, mod_ctx, w_in16, i, 256)

        hy_l = _hyena(p_l, lp)
        pa, pd = _gdn_gate_params(gdn_a_log[i], gdn_dt_bias[i])
        gd_l, gd_c = _gdn(p_l, _gdn_gate_inputs(p_l), p_c, _gdn_gate_inputs(p_c), gdn_conv_w[i], pa, pd,
                          gdn_norm_g[i][None])

        dsum, scales, gch = _ret_constants(lp)
        rt_l, rt_c = _retention(p_l, p_c, cos, sin, dsum, scales, gch)

        g2 = norm2_g[i][None]
        x = _out_ffn(x, hy_l, gd_l, rt_l, mod_lat, wo16, g2, w116, w216, gf, i, 512, final=not need_ctx)
        if need_ctx:
            hy_c = _hyena(p_c, lp)
            ctx = _out_ffn(ctx, hy_c, gd_c, rt_c, mod_ctx, wo16, g2, w116, w216, gf, i, 256, final=False)
    return x
```

```python
import functools
import math

import jax
import jax.numpy as jnp
import numpy as np
from jax import lax
from jax.experimental import pallas as pl
from jax.experimental.pallas import tpu as pltpu

D_MODEL = 1024
DEPTH = 2
GRID_W = 64
EPS = 1e-6
N_MOD = 6

HY_WIDTH = D_MODEL // 4
HY_ORDER = 2
HY_EMB = 33
HY_FAST_DECAY = 0.3
HY_SLOW_DECAY = 1.5
HY_TARGET = 1e-2
GDN_HEADS = D_MODEL // 256
GDN_DK = 128
GDN_DV = 128
RET_HEADS = D_MODEL // 256
RET_DK = 64
RET_DV = 64
ROPE_BASE = 10000.0
FFN_HIDDEN = ((8 * D_MODEL + 3 * 256 - 1) // (3 * 256)) * 256

GDN_CHUNK = 64
GDN_ROWS = 256
GDN_HP = 2
GDN_GROUP = 18
RET_CHUNK = 256
HY_BLOCK = 256
HY_PAIR = 4
HY_IO = 256
LANES = 128
SUBLANES = 8
MOD_COLS = 1536
PROJ_COLS = 512
FFN_COLS = 256

GDN_QKV = 2 * GDN_HEADS * GDN_DK + GDN_HEADS * GDN_DV
OFF_HY = 0
OFF_QKV = OFF_HY + 3 * HY_WIDTH
OFF_Z = OFF_QKV + GDN_QKV
OFF_RET = OFF_Z + GDN_HEADS * GDN_DV
OFF_AB = OFF_RET + 4 * RET_HEADS * RET_DK
IN_PAD = OFF_AB + LANES

V7X_VMEM_BYTES = 64 * 1024 * 1024
VMEM_LIMIT = V7X_VMEM_BYTES - 8 * 1024 * 1024
GDN_VMEM_LIMIT = V7X_VMEM_BYTES - 4 * 1024 * 1024

f32 = jnp.float32
bf16 = jnp.bfloat16


def _cparams(sem):
    return pltpu.CompilerParams(dimension_semantics=sem, vmem_limit_bytes=VMEM_LIMIT)


def _const_spec(shape):
    nd = len(shape)
    return pl.BlockSpec(shape, lambda *_: (0,) * nd, pipeline_mode=pl.Buffered(1))


def _layer_spec(stacked, layer):
    nd = stacked.ndim - 1
    return pl.BlockSpec((None,) + stacked.shape[1:], lambda *_: (layer,) + (0,) * nd,
                        pipeline_mode=pl.Buffered(1))


def _rms(x, g):
    return x * lax.rsqrt(jnp.mean(x * x, axis=-1, keepdims=True) + EPS) * g


def _silu(x):
    return x * jax.nn.sigmoid(x)


def _dot(a, b):
    return jnp.dot(a.astype(bf16), b.astype(bf16), preferred_element_type=f32)


def _dot_nt(a, b):
    return lax.dot_general(a.astype(bf16), b.astype(bf16), (((1,), (1,)), ((), ())),
                           preferred_element_type=f32)


def _dot_tn(a, b):
    return lax.dot_general(a.astype(bf16), b.astype(bf16), (((0,), (0,)), ((), ())),
                           preferred_element_type=f32)


def _mod_kernel(c_ref, w_ref, b_ref, o_ref):
    o_ref[...] = _dot(_silu(c_ref[...]), w_ref[...]) + b_ref[...]


def _mod_vectors(cs, w16, b, layer):
    rows, d = cs.shape
    n = w16.shape[2]
    tn = MOD_COLS
    return pl.pallas_call(
        _mod_kernel,
        grid=(n // tn,),
        in_specs=[pl.BlockSpec((rows, d), lambda j: (0, 0)),
                  pl.BlockSpec((None, d, tn), lambda j: (layer, 0, j)),
                  pl.BlockSpec((1, tn), lambda j: (0, j))],
        out_specs=pl.BlockSpec((rows, tn), lambda j: (0, j)),
        out_shape=jax.ShapeDtypeStruct((rows, n), f32),
        compiler_params=_cparams(("parallel",)),
        name="mod_vectors",
    )(cs, w16, b)


def _in_proj_kernel(x_ref, g_ref, mod_ref, w_ref, o_ref):
    x = x_ref[0]
    mod = mod_ref[0]
    h = _rms(x, g_ref[...]) * (1.0 + mod[1:2]) + mod[0:1]
    h16 = h.astype(bf16)
    n = w_ref.shape[1]
    step = PROJ_COLS
    for j in range(0, n, step):
        e = min(j + step, n)
        o_ref[0, :, j:e] = jnp.dot(h16, w_ref[:, j:e], preferred_element_type=f32)


def _in_proj(x, g, mod, w16, layer, tm):
    b, t, d = x.shape
    n = w16.shape[2]
    return pl.pallas_call(
        _in_proj_kernel,
        grid=(b, t // tm),
        in_specs=[pl.BlockSpec((1, tm, d), lambda i, j: (i, j, 0)),
                  _const_spec((1, d)),
                  pl.BlockSpec((1, N_MOD, d), lambda i, j: (i, 0, 0)),
                  _layer_spec(w16, layer)],
        out_specs=pl.BlockSpec((1, tm, n), lambda i, j: (i, j, 0)),
        out_shape=jax.ShapeDtypeStruct((b, t, n), f32),
        compiler_params=_cparams(("parallel", "parallel")),
        name="in_proj",
    )(x, g, mod, w16)


def _out_ffn_kernel(x_ref, hy_ref, gd_ref, rt_ref, mod_ref, wo_ref, g2_ref, w1_ref, w2_ref, gf_ref, o_ref,
                    *, final):
    x = x_ref[0]
    mod = mod_ref[0]
    o0, o1, o2 = HY_WIDTH, HY_WIDTH + GDN_HEADS * GDN_DV, D_MODEL
    y = (jnp.dot(hy_ref[0].astype(bf16), wo_ref[0:o0, :], preferred_element_type=f32)
         + jnp.dot(gd_ref[0].astype(bf16), wo_ref[o0:o1, :], preferred_element_type=f32)
         + jnp.dot(rt_ref[0].astype(bf16), wo_ref[o1:o2, :], preferred_element_type=f32))
    x = x + mod[2:3] * y
    h16 = (_rms(x, g2_ref[...]) * (1.0 + mod[4:5]) + mod[3:4]).astype(bf16)
    step = FFN_COLS
    acc = jnp.zeros_like(x)
    for j in range(0, FFN_HIDDEN, step):
        gate = jnp.dot(h16, w1_ref[:, j:j + step], preferred_element_type=f32)
        up = jnp.dot(h16, w1_ref[:, FFN_HIDDEN + j:FFN_HIDDEN + j + step], preferred_element_type=f32)
        acc = acc + jnp.dot((_silu(gate) * up).astype(bf16), w2_ref[j:j + step, :], preferred_element_type=f32)
    x = x + mod[5:6] * acc
    if final:
        x = _rms(x, gf_ref[...])
    o_ref[0] = x


def _out_ffn(x, hy, gd, rt, mod, wo16, g2, w116, w216, gf, layer, tm, final):
    b, t, d = x.shape
    tok = lambda w: pl.BlockSpec((1, tm, w), lambda i, j: (i, j, 0))
    return pl.pallas_call(
        functools.partial(_out_ffn_kernel, final=final),
        grid=(b, t // tm),
        in_specs=[tok(d), tok(hy.shape[-1]), tok(gd.shape[-1]), tok(rt.shape[-1]),
                  pl.BlockSpec((1, N_MOD, d), lambda i, j: (i, 0, 0)),
                  _layer_spec(wo16, layer), _const_spec((1, d)), _layer_spec(w116, layer),
                  _layer_spec(w216, layer), _const_spec((1, d))],
        out_specs=tok(d),
        out_shape=jax.ShapeDtypeStruct((b, t, d), f32),
        compiler_params=_cparams(("parallel", "parallel")),
        name="out_ffn",
    )(x, hy, gd, rt, mod, wo16, g2, w116, w216, gf)


def _hyena_kernel(v_ref, x1_ref, x2_ref, k_ref, b_ref, o_ref, acc_ref, *, nb, cb):
    hb = HY_BLOCK
    rb = v_ref.shape[1] // nb

    def conv(u16, kext):
        m = range(len(u16))
        for d in range(-(nb - 1), nb):
            dd = d % (2 * nb)
            t0, t1 = max(0, d), min(nb - 1, nb - 1 + d) + 1
            for i in m:
                win = jnp.broadcast_to(kext[i][:, hb * dd:hb * dd + 2 * hb], (hb, 2 * hb))
                toep = pltpu.roll(win, 0, 1, stride=1, stride_axis=0)[:, hb:].astype(bf16)
                contrib = jnp.dot(u16[i][rb * (t0 - d):rb * (t1 - d)], toep, preferred_element_type=f32)
                if d == -(nb - 1):
                    acc_ref[i] = jnp.zeros(acc_ref.shape[1:], f32)
                acc_ref[i, rb * t0:rb * t1, :] += contrib
        return [acc_ref[i] for i in m]

    def body(g, carry):
        cs = [HY_PAIR * g + i for i in range(HY_PAIR)]
        z = [v_ref[c] for c in cs]
        for n, gate_ref in enumerate((x1_ref, x2_ref)):
            y = conv([x.astype(bf16) for x in z], [k_ref[n, pl.ds(c, 1), :] for c in cs])
            z = [gate_ref[c] * (y[i] + z[i] * b_ref[n, pl.ds(c, 1), :]) for i, c in enumerate(cs)]
        for i, c in enumerate(cs):
            o_ref[c] = z[i]
        return carry

    lax.fori_loop(0, cb // HY_PAIR, body, 0)


def _hyena_conv(p3, kext, bias, nb, cb=SUBLANES):
    c3, r, hb = p3.shape
    c = c3 // 3
    klen = kext.shape[-1]
    sig = lambda s: pl.BlockSpec((cb, r, hb), lambda i, s=s: (s * (c // cb) + i, 0, 0))
    return pl.pallas_call(
        functools.partial(_hyena_kernel, nb=nb, cb=cb),
        grid=(c // cb,),
        in_specs=[sig(0), sig(1), sig(2),
                  pl.BlockSpec((HY_ORDER, cb, klen), lambda i: (0, i, 0)),
                  pl.BlockSpec((HY_ORDER, cb, hb), lambda i: (0, i, 0))],
        out_specs=pl.BlockSpec((cb, r, hb), lambda i: (i, 0, 0)),
        out_shape=jax.ShapeDtypeStruct((c, r, hb), f32),
        scratch_shapes=[pltpu.VMEM((HY_PAIR, r, hb), f32)],
        compiler_params=_cparams(("parallel",)),
        name=f"hyena_conv_nb{nb}",
    )(p3, p3, p3, kext, bias)


def _hyena_prep_kernel(x_ref, before_ref, after_ref, w_ref, b_ref, o_ref):
    hb = HY_BLOCK
    w = w_ref[...]
    row = lax.broadcasted_iota(jnp.int32, x_ref.shape[1:], 0)
    s = pl.program_id(1)
    has_before = s > 0
    has_after = s < pl.num_programs(1) - 1
    for i in range(x_ref.shape[0]):
        x = x_ref[i]
        prev = jnp.where(has_before, before_ref[i, SUBLANES - 1:SUBLANES, :], 0.0)
        nxt = jnp.where(has_after, after_ref[i, 0:1, :], 0.0)
        xp = jnp.where(row == 0, prev, pltpu.roll(x, 1, 0))
        xn = jnp.where(row == hb - 1, nxt, pltpu.roll(x, hb - 1, 0))
        y = xp * w[0:1] + x * w[1:2] + xn * w[2:3] + b_ref[...]
        o_ref[:, i, :] = y.T


def _hyena_prep(p, conv_w, conv_b):
    b, t, _ = p.shape
    hb = HY_BLOCK
    n = 3 * HY_WIDTH
    per = hb // SUBLANES
    last = t // SUBLANES - 1
    assert OFF_HY == 0 and b % SUBLANES == 0
    return pl.pallas_call(
        _hyena_prep_kernel,
        grid=(n // HY_IO, t // hb),
        in_specs=[pl.BlockSpec((b, hb, HY_IO), lambda j, s: (0, s, j)),
                  pl.BlockSpec((b, SUBLANES, HY_IO), lambda j, s: (0, jnp.maximum(s * per - 1, 0), j)),
                  pl.BlockSpec((b, SUBLANES, HY_IO), lambda j, s: (0, jnp.minimum((s + 1) * per, last), j)),
                  pl.BlockSpec((3, HY_IO), lambda j, s: (0, j)),
                  pl.BlockSpec((1, HY_IO), lambda j, s: (0, j))],
        out_specs=pl.BlockSpec((HY_IO, b, hb), lambda j, s: (j, s, 0)),
        out_shape=jax.ShapeDtypeStruct((n, (t // hb) * b, hb), f32),
        compiler_params=_cparams(("parallel", "parallel")),
        name="hyena_prep",
    )(p, p, p, conv_w, conv_b)


def _hyena_post_kernel(z_ref, o_ref):
    for i in range(z_ref.shape[1]):
        o_ref[i] = z_ref[:, i, :].T


def _hyena_post(z, b):
    c, r, hb = z.shape
    nb = r // b
    return pl.pallas_call(
        _hyena_post_kernel,
        grid=(c // HY_IO, nb),
        in_specs=[pl.BlockSpec((HY_IO, b, hb), lambda j, s: (j, s, 0))],
        out_specs=pl.BlockSpec((b, hb, HY_IO), lambda j, s: (0, s, j)),
        out_shape=jax.ShapeDtypeStruct((b, nb * hb, c), f32),
        compiler_params=_cparams(("parallel", "parallel")),
        name="hyena_post",
    )(z)


def _hyena_filter_kernel(z_ref, w1_ref, b1_ref, f1_ref, w2_ref, b2_ref, f2_ref, w3_ref, win_ref, o_ref):
    hi = lax.Precision.HIGHEST
    hb = HY_BLOCK
    t = z_ref.shape[1]
    for side in range(2):
        h = jnp.sin(f1_ref[...] * (jnp.dot(z_ref[side], w1_ref[...], precision=hi,
                                           preferred_element_type=f32) + b1_ref[...]))
        h = jnp.sin(f2_ref[...] * (jnp.dot(h, w2_ref[...], precision=hi, preferred_element_type=f32)
                                   + b2_ref[...]))
        for n in range(HY_ORDER):
            w3 = w3_ref[:, (2 * n + side) * HY_WIDTH:(2 * n + side + 1) * HY_WIDTH]
            k = lax.dot_general(w3, h, (((0,), (1,)), ((), ())), precision=hi,
                                preferred_element_type=f32) * win_ref[side]
            o_ref[n, :, hb + side * t:hb + (side + 1) * t] = k
            if side == 1:
                o_ref[n, :, 0:hb] = k[:, t - hb:]


def _hyena_tables(t_len):
    pos = np.arange(t_len, dtype=np.float64)
    t = np.linspace(0.0, 1.0, t_len)
    bands = (HY_EMB - 1) // 2
    f = np.linspace(1e-4, bands - 1, bands)
    ang = (2.0 * math.pi / t_len) * pos[:, None] * f[None, :]
    z = np.concatenate([t[:, None], np.cos(ang), -np.sin(ang)], axis=-1)
    max_decay = math.log(HY_TARGET) / HY_FAST_DECAY
    min_decay = math.log(HY_TARGET) / HY_SLOW_DECAY
    deltas = np.abs(np.linspace(min_decay, max_decay, HY_WIDTH))
    window = np.exp(-t[:, None] * deltas[None, :])
    neg = lambda x: np.concatenate([x[:1], x[:0:-1]], axis=0)
    zs = np.pad(np.stack([z, neg(z)]), ((0, 0), (0, 0), (0, LANES - HY_EMB)))
    wneg = neg(window)
    wneg[0] = 0.0
    win = np.stack([window, wneg]).transpose(0, 2, 1)
    return jnp.asarray(zs, f32), jnp.asarray(win, f32)


def _hyena_filters(t_len, lp):
    zs, win = _hyena_tables(t_len)
    w1 = jnp.pad(lp['hy_f_w1'], ((0, LANES - HY_EMB), (0, 0)))
    row = lambda x: x[None]
    args = (zs, w1, row(lp['hy_f_b1']), row(lp['hy_f_freq1']), lp['hy_f_w2'], row(lp['hy_f_b2']),
            row(lp['hy_f_freq2']), lp['hy_f_w3'], win)
    klen = 2 * t_len + HY_BLOCK
    return pl.pallas_call(
        _hyena_filter_kernel,
        grid=(1,),
        in_specs=[_const_spec(a.shape) for a in args],
        out_specs=pl.BlockSpec((HY_ORDER, HY_WIDTH, klen), lambda i: (0, 0, 0)),
        out_shape=jax.ShapeDtypeStruct((HY_ORDER, HY_WIDTH, klen), f32),
        compiler_params=_cparams(("arbitrary",)),
        name="hyena_filters",
    )(*args)


def _gdn_kernel(ql_ref, kl_ref, vl_ref, zl_ref, abl_ref, qc_ref, kc_ref, vc_ref, zc_ref, abc_ref,
                cwq_ref, cwk_ref, cwv_ref, pa_ref, pd_ref, ng_ref,
                ol_ref, oc_ref,
                q_s, k_s, v_s, cf_s, u_s, wq_s, kgt_s, at_s, egl_s, o_s, st_s):
    c = GDN_CHUNK
    rb = GDN_ROWS
    tc, tl = qc_ref.shape[1], ql_ref.shape[1]
    ncc, nc = tc // c, (tc + tl) // c
    dk = GDN_DK

    def conv_rows(x_ref, w_ref, r0, t, ls):
        x = x_ref[0, pl.ds(r0, rb), ls]
        before = pl.multiple_of(jnp.maximum(r0 - SUBLANES, 0), SUBLANES)
        after = pl.multiple_of(jnp.minimum(r0 + rb, t - SUBLANES), SUBLANES)
        prev = x_ref[0, pl.ds(before, SUBLANES), ls][SUBLANES - 1:SUBLANES]
        nxt = x_ref[0, pl.ds(after, SUBLANES), ls][0:1]
        prev = jnp.where(r0 > 0, prev, 0.0)
        nxt = jnp.where(r0 + rb < t, nxt, 0.0)
        row = lax.broadcasted_iota(jnp.int32, x.shape, 0)
        xp = jnp.where(row == 0, prev, pltpu.roll(x, 1, 0))
        xn = jnp.where(row == rb - 1, nxt, pltpu.roll(x, rb - 1, 0))
        w = w_ref[:, ls]
        return _silu(xp * w[0:1] + x * w[1:2] + xn * w[2:3])

    def l2n(x):
        return x * lax.rsqrt(jnp.sum(x * x, axis=-1, keepdims=True) + EPS)

    def prep(hh, q_ref, k_ref, v_ref, t, base):
        ls = slice(hh * LANES, (hh + 1) * LANES)

        def body(i, carry):
            r0 = pl.multiple_of(i * rb, rb)
            dst = pl.ds(pl.multiple_of(base + i * rb, rb), rb)
            q_s[dst, :] = l2n(conv_rows(q_ref, cwq_ref, r0, t, ls)) * (dk ** -0.5)
            k_s[dst, :] = l2n(conv_rows(k_ref, cwk_ref, r0, t, ls))
            v_s[dst, :] = conv_rows(v_ref, cwv_ref, r0, t, ls)
            return carry
        lax.fori_loop(0, t // rb, body, 0)

    def gates(ab_ref, t, base):
        ab = jnp.concatenate([ab_ref[0, hh] for hh in range(GDN_HP)], axis=0)
        kind = lax.broadcasted_iota(jnp.int32, ab.shape, 0) % 4
        pos = lax.broadcasted_iota(jnp.int32, ab.shape, 1) % c
        tile = lambda p: jnp.concatenate([p] * (t // LANES), axis=1)
        g = -jnp.exp(tile(pa_ref[0])) * jax.nn.softplus(ab + tile(pd_ref[0]))
        acc_f, acc_b = g, g
        s = 1
        while s < c:
            acc_f = acc_f + jnp.where(pos >= s, pltpu.roll(acc_f, s, 1), 0.0)
            acc_b = acc_b + jnp.where(pos < c - s, pltpu.roll(acc_b, t - s, 1), 0.0)
            s *= 2
        cf_t = jnp.where(kind == 0, acc_f, jnp.where(kind == 1, acc_b, jax.nn.sigmoid(ab)))
        pad = jnp.zeros((LANES - SUBLANES, LANES), f32)
        for j in range(t // LANES):
            blk = jnp.concatenate([cf_t[:, j * LANES:(j + 1) * LANES], pad], axis=0)
            cf_s[base + j * LANES:base + (j + 1) * LANES, :] = blk.T

    gates(abc_ref, tc, 0)
    gates(abl_ref, tl, tc)

    lane = lax.broadcasted_iota(jnp.int32, (c, 2 * c), 1)
    row = lax.broadcasted_iota(jnp.int32, (c, 2 * c), 0)
    fwd = lane < c
    col = lane % c
    incl = (fwd & (row >= col)) | (~fwd & (row <= col))
    strict = (fwd & (row > col)) | (~fwd & (row < col))

    def blockdiag(a):
        return jnp.concatenate([jnp.where(fwd, a, 0.0), jnp.where(fwd, 0.0, a)], axis=0).astype(bf16)

    def chunk_group(hh, cis):
        n = range(len(cis))
        xf, xb = 2 * hh, 2 * hh + 1
        rows = [pl.ds(pl.multiple_of(ci * c, c), c) for ci in cis]
        cf = [cf_s[r, :] for r in rows]
        k = [k_s[r, :] for r in rows]
        q = [q_s[r, :] for r in rows]
        l0 = 4 * hh
        bc = lambda x, j: jnp.broadcast_to(x[:, l0 + j:l0 + j + 1], (c, LANES))
        gf_c, gb_c = [bc(x, 0) for x in cf], [bc(x, 1) for x in cf]
        bf_c, bb_c = [bc(x, 2) for x in cf], [bc(x, 3) for x in cf]
        cf_t = [jnp.concatenate([x, x], axis=0).T for x in cf]
        g_r = [jnp.where(fwd, jnp.broadcast_to(x[l0:l0 + 1], (c, 2 * c)),
                         jnp.broadcast_to(x[l0 + 1:l0 + 2], (c, 2 * c))) for x in cf_t]
        decay = [jnp.where(incl, jnp.exp(jnp.where(incl, jnp.where(fwd, gf_c[g], gb_c[g]) - g_r[g], 0.0)), 0.0)
                 for g in n]
        kq = [_dot_nt(jnp.concatenate([k[g], q[g]], axis=0), jnp.concatenate([k[g], k[g]], axis=0))
              for g in n]
        lm = [jnp.where(strict, kq[g][:c] * jnp.where(fwd, bf_c[g], bb_c[g]) * decay[g], 0.0) for g in n]
        for g in n:
            attn = (kq[g][c:] * decay[g]).astype(bf16)
            at_s[xf, cis[g]] = attn[:, :c]
            at_s[xb, cis[g]] = attn[:, c:]
        hi = lax.Precision.HIGHEST
        bd32 = lambda x: jnp.concatenate([jnp.where(fwd, x, 0.0), jnp.where(fwd, 0.0, x)], axis=0)
        a = [jnp.dot(x, bd32(x), precision=hi, preferred_element_type=f32) for x in lm]
        tp = [-x for x in lm]
        p = 2
        while 2 * p < c:
            ta = [jnp.dot(jnp.concatenate([tp[g], a[g]], axis=0), bd32(a[g]), precision=hi,
                          preferred_element_type=f32) for g in n]
            tp = [tp[g] + a[g] + ta[g][:c] for g in n]
            a = [x[c:] for x in ta]
            p *= 2
        tp = [tp[g] + a[g] + jnp.dot(tp[g], bd32(a[g]), precision=hi, preferred_element_type=f32)
              for g in n]
        eg_f, eg_b = [jnp.exp(x) for x in gf_c], [jnp.exp(x) for x in gb_c]
        uw = []
        for g in n:
            v = v_s[rows[g], :]
            rhs = jnp.concatenate([jnp.concatenate([v * bf_c[g], k[g] * (bf_c[g] * eg_f[g])], axis=1),
                                   jnp.concatenate([v * bb_c[g], k[g] * (bb_c[g] * eg_b[g])], axis=1)], axis=0)
            uw.append(rhs + jnp.dot(blockdiag(tp[g]), rhs.astype(bf16), preferred_element_type=f32))
        for g in n:
            ci, r = cis[g], rows[g]
            u_s[xf, r, :] = uw[g][:c, :dk]
            u_s[xb, r, :] = uw[g][c:, :dk]
            wq_s[xf, ci] = jnp.concatenate([uw[g][:c, dk:], q[g] * eg_f[g]], axis=0).astype(bf16)
            wq_s[xb, ci] = jnp.concatenate([uw[g][c:, dk:], q[g] * eg_b[g]], axis=0).astype(bf16)
            gl_f, gl_b = gf_c[g][c - 1:c], gb_c[g][0:1]
            kgt_s[xf, ci] = (k[g] * jnp.exp(gl_f - gf_c[g])).T.astype(bf16)
            kgt_s[xb, ci] = (k[g] * jnp.exp(gl_b - gb_c[g])).T.astype(bf16)
            egl_s[hh, ci] = jnp.concatenate([jnp.exp(gl_f), jnp.exp(gl_b),
                                             jnp.zeros((SUBLANES - 2, LANES), f32)], axis=0)
            o_s[hh, r, :] = jnp.zeros((c, GDN_DV), f32)

    for hh in range(GDN_HP):
        prep(hh, qc_ref, kc_ref, vc_ref, tc, 0)
        prep(hh, ql_ref, kl_ref, vl_ref, tl, tc)

        def chunk_body(i, carry, hh=hh):
            chunk_group(hh, [GDN_GROUP * i + g for g in range(GDN_GROUP)])
            return carry

        lax.fori_loop(0, nc // GDN_GROUP, chunk_body, 0)

    st_s[...] = jnp.zeros_like(st_s)
    chains = [(hh, d) for hh in range(GDN_HP) for d in range(2)]

    def scan_body(i, carry):
        ci = (i, jnp.where(i < ncc, ncc - 1 - i, nc + ncc - 1 - i))
        rows = [pl.ds(pl.multiple_of(x * c, c), c) for x in ci]
        n = range(len(chains))
        s = [st_s[x] for x in n]
        ws = [jnp.dot(wq_s[x, ci[chains[x][1]]], s[x].astype(bf16), preferred_element_type=f32) for x in n]
        v_new = [(u_s[x, rows[chains[x][1]], :] - ws[x][:c]).astype(bf16) for x in n]
        upd = [jnp.dot(kgt_s[x, ci[chains[x][1]]], v_new[x], preferred_element_type=f32) for x in n]
        out = [ws[x][c:] + jnp.dot(at_s[x, ci[chains[x][1]]], v_new[x], preferred_element_type=f32) for x in n]
        for x in n:
            hh, d = chains[x]
            st_s[x] = s[x] * egl_s[hh, ci[d]][d:d + 1] + upd[x]
            o_s[hh, rows[d], :] += out[x]
        return carry

    lax.fori_loop(0, nc, scan_body, 0)

    def finish(hh, z_ref, o_ref, t, base):
        ls = slice(hh * LANES, (hh + 1) * LANES)

        def body(i, carry):
            r0 = pl.ds(pl.multiple_of(i * rb, rb), rb)
            o = o_s[hh, pl.ds(pl.multiple_of(base + i * rb, rb), rb), :]
            y = o * lax.rsqrt(jnp.mean(o * o, axis=-1, keepdims=True) + EPS) * ng_ref[...]
            o_ref[0, r0, ls] = y * _silu(z_ref[0, r0, ls])
            return carry
        lax.fori_loop(0, t // rb, body, 0)

    for hh in range(GDN_HP):
        finish(hh, zc_ref, oc_ref, tc, 0)
        finish(hh, zl_ref, ol_ref, tl, tc)


def _gdn(p_l, ab_l, p_c, ab_c, conv_w, pa, pd, norm_g):
    b, tl, _ = p_l.shape
    tc = p_c.shape[1]
    h, c, hp = GDN_HEADS, GDN_CHUNK, GDN_HP
    nc = (tl + tc) // c
    assert tl % GDN_ROWS == 0 and tc % GDN_ROWS == 0 and nc % GDN_GROUP == 0
    qb, zb = OFF_QKV // LANES, OFF_Z // LANES
    assert h % hp == 0 and qb % hp == 0 and zb % hp == 0
    w = hp * LANES

    def specs(t):
        heads = lambda off: pl.BlockSpec((1, t, w), lambda i, j, off=off // hp: (i, 0, off + j))
        return [heads(qb), heads(qb + h), heads(qb + 2 * h), heads(zb),
                pl.BlockSpec((1, hp, 4, t), lambda i, j: (i, j, 0, 0))]

    cw = lambda off: pl.BlockSpec((3, w), lambda i, j, off=off // hp: (0, off + j))
    assert 4 * hp == SUBLANES
    par = pl.BlockSpec((1, SUBLANES, LANES), lambda i, j: (j, 0, 0))
    out = lambda t: pl.BlockSpec((1, t, w), lambda i, j: (i, 0, j))
    tt = tl + tc
    return pl.pallas_call(
        _gdn_kernel,
        grid=(b, h // hp),
        in_specs=specs(tl) + specs(tc) + [cw(0), cw(h), cw(2 * h), par, par, _const_spec((1, GDN_DV))],
        out_specs=[out(tl), out(tc)],
        out_shape=[jax.ShapeDtypeStruct((b, tl, h * GDN_DV), f32),
                   jax.ShapeDtypeStruct((b, tc, h * GDN_DV), f32)],
        scratch_shapes=[pltpu.VMEM((tt, GDN_DK), f32), pltpu.VMEM((tt, GDN_DK), f32),
                        pltpu.VMEM((tt, GDN_DV), f32), pltpu.VMEM((tt, LANES), f32),
                        pltpu.VMEM((2 * hp, tt, GDN_DV), f32), pltpu.VMEM((2 * hp, nc, 2 * c, GDN_DK), bf16),
                        pltpu.VMEM((2 * hp, nc, GDN_DK, c), bf16), pltpu.VMEM((2 * hp, nc, c, c), bf16),
                        pltpu.VMEM((hp, nc, SUBLANES, LANES), f32), pltpu.VMEM((hp, tt, GDN_DV), f32),
                        pltpu.VMEM((2 * hp, GDN_DK, GDN_DV), f32)],
        compiler_params=pltpu.CompilerParams(dimension_semantics=("parallel", "parallel"),
                                             vmem_limit_bytes=GDN_VMEM_LIMIT),
        name="gated_deltanet",
    )(p_l, p_l, p_l, p_l, ab_l, p_c, p_c, p_c, p_c, ab_c, conv_w, conv_w, conv_w, pa, pd, norm_g)


def _ret_kernel(ql_ref, kl_ref, vl_ref, gl_ref, qc_ref, kc_ref, vc_ref, gc_ref,
                cos_ref, sin_ref, dsum_ref, sc_ref, gch_ref,
                ol_ref, oc_ref, sf_ref, sb_ref, q_s, k_s, o_s):
    c = RET_CHUNK
    w = RET_HEADS * RET_DK
    ri = lax.broadcasted_iota(jnp.int32, (w, w), 0) // RET_DK
    ci = lax.broadcasted_iota(jnp.int32, (w, w), 1) // RET_DK
    blockdiag = ri == ci
    lane_head = lax.broadcasted_iota(jnp.int32, (c, w), 1) // RET_DK

    def intra(q, k, v, r, cross):
        k16 = k.astype(bf16)
        for h in range(RET_HEADS):
            m = lane_head == h
            scores = _dot_nt(jnp.where(m, q, 0.0), k16) * dsum_ref[h]
            part = _dot(scores, jnp.where(m, v, 0.0))
            if h == 0:
                o_s[r, :] = part if cross is None else part + cross
            else:
                o_s[r, :] += part

    def update(s_ref, k, v, d):
        new = jnp.where(blockdiag, _dot_tn(k * sc_ref[2 * d + 1], v), 0.0)
        s_ref[...] = s_ref[...] * gch_ref[d] + new

    first_half = lax.broadcasted_iota(jnp.int32, (c, w), 1) % RET_DK < RET_DK // 2
    avg = jnp.where(blockdiag, 1.0 / RET_DV, 0.0).astype(bf16)

    def rope(x, r):
        swapped = jnp.where(first_half, pltpu.roll(x, w - RET_DK // 2, 1), pltpu.roll(x, RET_DK // 2, 1))
        return x * cos_ref[r, :] + swapped * sin_ref[r, :]

    def head_mean(x):
        hi = x.astype(bf16)
        lo = (x - hi.astype(f32)).astype(bf16)
        return jnp.dot(hi, avg, preferred_element_type=f32) + jnp.dot(lo, avg, preferred_element_type=f32)

    def finish(o, gate):
        d = o - head_mean(o)
        return d * lax.rsqrt(head_mean(d * d) + EPS) * _silu(gate)

    sf_ref[...] = jnp.zeros_like(sf_ref)
    sb_ref[...] = jnp.zeros_like(sb_ref)
    qc, kc, vc = qc_ref[0], kc_ref[0] * (RET_DK ** -0.5), vc_ref[0]
    r0 = slice(0, c)
    intra(qc, kc, vc, r0, None)
    oc_ref[0] = finish(o_s[r0, :], gc_ref[0])
    update(sf_ref, kc, vc, 0)
    update(sb_ref, kc, vc, 1)
    n = ql_ref.shape[1] // c
    for i in range(n):
        r = slice(i * c, (i + 1) * c)
        q = rope(ql_ref[0, r, :], r)
        k = rope(kl_ref[0, r, :] * (RET_DK ** -0.5), r)
        v = vl_ref[0, r, :]
        q_s[r, :] = q
        k_s[r, :] = k
        intra(q, k, v, r, _dot(q * sc_ref[0], sf_ref[...]))
        update(sf_ref, k, v, 0)
    for i in range(n - 1, -1, -1):
        r = slice(i * c, (i + 1) * c)
        o = o_s[r, :] + _dot(q_s[r, :] * sc_ref[2], sb_ref[...])
        ol_ref[0, r, :] = finish(o, gl_ref[0, r, :])
        update(sb_ref, k_s[r, :], vl_ref[0, r, :], 1)


def _retention(p_l, p_c, cos, sin, dsum, scales, gch):
    b, tl, _ = p_l.shape
    tc = p_c.shape[1]
    w = RET_HEADS * RET_DK
    assert OFF_RET % w == 0
    blk = lambda t, j: pl.BlockSpec((1, t, w), lambda i, j=j: (i, 0, OFF_RET // w + j))
    tok = lambda t: pl.BlockSpec((1, t, w), lambda i: (i, 0, 0))
    return pl.pallas_call(
        _ret_kernel,
        grid=(b,),
        in_specs=[blk(tl, j) for j in range(4)] + [blk(tc, j) for j in range(4)]
        + [_const_spec(x.shape) for x in (cos, sin, dsum, scales, gch)],
        out_specs=[tok(tl), tok(tc)],
        out_shape=[jax.ShapeDtypeStruct((b, tl, w), f32), jax.ShapeDtypeStruct((b, tc, w), f32)],
        scratch_shapes=[pltpu.VMEM((w, w), f32), pltpu.VMEM((w, w), f32),
                        pltpu.VMEM((tl, w), f32), pltpu.VMEM((tl, w), f32), pltpu.VMEM((tl, w), f32)],
        compiler_params=_cparams(("parallel",)),
        name="retention",
    )(p_l, p_l, p_l, p_l, p_c, p_c, p_c, p_c, cos, sin, dsum, scales, gch)


def _hyena(p, lp):
    b, t, _ = p.shape
    hb, c = HY_BLOCK, HY_WIDTH
    nb = t // hb
    p3 = _hyena_prep(p, lp['hy_conv_w'], lp['hy_conv_b'][None])
    bias = jnp.broadcast_to(lp['hy_bias'][:, :, None], (HY_ORDER, c, hb))
    z = _hyena_conv(p3, _hyena_filters(t, lp), bias, nb)
    return _hyena_post(z, b)


def _gdn_gate_inputs(p):
    b, t, _ = p.shape
    ab = p[..., OFF_AB:OFF_AB + 4 * GDN_HEADS].reshape(b, t, 4, GDN_HEADS)
    return ab.transpose(0, 3, 2, 1)


def _gdn_gate_params(a_log, dt_bias):
    def rows(x):
        x = jnp.pad(x.T, ((0, 0), (0, 2))).reshape(GDN_HEADS // GDN_HP, 4 * GDN_HP)
        return jnp.broadcast_to(x[:, :, None], x.shape + (LANES,))
    return rows(a_log), rows(dt_bias)


def _rope_tables(t_lat):
    rows_n = t_lat // GRID_W
    row = np.repeat(np.arange(rows_n, dtype=np.float64), GRID_W)
    colp = np.tile(np.arange(GRID_W, dtype=np.float64), rows_n)
    nf = RET_DK // 4
    inv = ROPE_BASE ** (-np.arange(nf, dtype=np.float64) / nf)
    ang = np.concatenate([row[:, None] * inv, colp[:, None] * inv], axis=-1)
    cos, sin = np.cos(ang), np.sin(ang)
    return (jnp.asarray(np.tile(np.concatenate([cos, cos], axis=-1), (1, RET_HEADS)), f32),
            jnp.asarray(np.tile(np.concatenate([-sin, sin], axis=-1), (1, RET_HEADS)), f32))


def _ret_constants(lp):
    c = RET_CHUNK
    lg = jax.nn.log_sigmoid(lp['ret_decay_logit'])
    idx = jnp.arange(c, dtype=f32)
    rel = idx[:, None] - idx[None, :]
    d_f = jnp.exp(jnp.where(rel >= 0, rel * lg[0][:, None, None], -jnp.inf))
    d_b = jnp.exp(jnp.where(rel <= 0, -rel * lg[1][:, None, None], -jnp.inf))
    lane = lambda x: jnp.repeat(x, RET_DK, axis=-1)
    scales = jnp.stack([lane(jnp.exp((idx + 1.0)[:, None] * lg[0])),
                        lane(jnp.exp((c - 1.0 - idx)[:, None] * lg[0])),
                        lane(jnp.exp((c - idx)[:, None] * lg[1])),
                        lane(jnp.exp(idx[:, None] * lg[1]))])
    gch = lane(jnp.exp(c * lg))[:, None, :]
    return d_f + d_b, scales, gch


def _reorder_w_in(w):
    h2 = 2 * GDN_HEADS
    a0 = OFF_Z + GDN_HEADS * GDN_DV
    parts = [w[..., :a0], w[..., a0 + 2 * h2:], w[..., a0:a0 + 2 * h2],
             jnp.zeros(w.shape[:-1] + (LANES - 2 * h2,), w.dtype)]
    return jnp.concatenate(parts, axis=-1)


def kernel(x, c, ctx, c_ctx, mod_w, mod_b, norm1_g, w_in, hy_conv_w, hy_conv_b, hy_f_w1, hy_f_b1, hy_f_freq1, hy_f_w2, hy_f_b2, hy_f_freq2, hy_f_w3, hy_bias, gdn_conv_w, gdn_a_log, gdn_dt_bias, gdn_norm_g, ret_decay_logit, w_out, norm2_g, ffn_w_in, ffn_w_out, final_norm_g):
    bsz, t_lat, d = x.shape
    cos, sin = _rope_tables(t_lat)
    pad_rows = (-(bsz + 1)) % SUBLANES
    cs = jnp.concatenate([c, c_ctx[None], jnp.zeros((pad_rows, d), f32)], axis=0)
    gf = final_norm_g[None]
    mod_w16, w_in16 = mod_w.astype(bf16), _reorder_w_in(w_in).astype(bf16)
    wo16, w116, w216 = w_out.astype(bf16), ffn_w_in.astype(bf16), ffn_w_out.astype(bf16)
    for i in range(DEPTH):
        need_ctx = i < DEPTH - 1
        lp = {'hy_conv_w': hy_conv_w[i], 'hy_conv_b': hy_conv_b[i], 'hy_f_w1': hy_f_w1[i],
              'hy_f_b1': hy_f_b1[i], 'hy_f_freq1': hy_f_freq1[i], 'hy_f_w2': hy_f_w2[i],
              'hy_f_b2': hy_f_b2[i], 'hy_f_freq2': hy_f_freq2[i], 'hy_f_w3': hy_f_w3[i],
              'hy_bias': hy_bias[i], 'ret_decay_logit': ret_decay_logit[i]}
        mod = _mod_vectors(cs, mod_w16, mod_b[i][None], i)
        mod_lat = mod[:bsz].reshape(bsz, N_MOD, d)
        mod_ctx = jnp.broadcast_to(mod[bsz].reshape(1, N_MOD, d), (bsz, N_MOD, d))
        g1 = norm1_g[i][None]
        p_l = _in_proj(x, g1, mod_lat, w_in16, i, 512)
        p_c = _in_proj(ctx, g1, mod_ctx, w_in16, i, 256)

        hy_l = _hyena(p_l, lp)
        pa, pd = _gdn_gate_params(gdn_a_log[i], gdn_dt_bias[i])
        gd_l, gd_c = _gdn(p_l, _gdn_gate_inputs(p_l), p_c, _gdn_gate_inputs(p_c), gdn_conv_w[i], pa, pd,
                          gdn_norm_g[i][None])

        dsum, scales, gch = _ret_constants(lp)
        rt_l, rt_c = _retention(p_l, p_c, cos, sin, dsum, scales, gch)

        g2 = norm2_g[i][None]
        x = _out_ffn(x, hy_l, gd_l, rt_l, mod_lat, wo16, g2, w116, w216, gf, i, 512, final=not need_ctx)
        if need_ctx:
            hy_c = _hyena(p_c, lp)
            ctx = _out_ffn(ctx, hy_c, gd_c, rt_c, mod_ctx, wo16, g2, w116, w216, gf, i, 256, final=False)
    return x
```

```python
import functools
import math

import jax
import jax.numpy as jnp
import numpy as np
from jax import lax
from jax.experimental import pallas as pl
from jax.experimental.pallas import tpu as pltpu

D_MODEL = 1024
DEPTH = 2
GRID_W = 64
EPS = 1e-6
N_MOD = 6

HY_WIDTH = D_MODEL // 4
HY_ORDER = 2
HY_EMB = 33
HY_FAST_DECAY = 0.3
HY_SLOW_DECAY = 1.5
HY_TARGET = 1e-2
GDN_HEADS = D_MODEL // 256
GDN_DK = 128
GDN_DV = 128
RET_HEADS = D_MODEL // 256
RET_DK = 64
RET_DV = 64
ROPE_BASE = 10000.0
FFN_HIDDEN = ((8 * D_MODEL + 3 * 256 - 1) // (3 * 256)) * 256

GDN_CHUNK = 64
GDN_ROWS = 256
GDN_HP = 2
GDN_GROUP = 18
RET_CHUNK = 256
HY_BLOCK = 256
HY_PAIR = 4
HY_IO = 256
LANES = 128
SUBLANES = 8
MOD_COLS = 1536
PROJ_COLS = 512
FFN_COLS = 256

GDN_QKV = 2 * GDN_HEADS * GDN_DK + GDN_HEADS * GDN_DV
OFF_HY = 0
OFF_QKV = OFF_HY + 3 * HY_WIDTH
OFF_Z = OFF_QKV + GDN_QKV
OFF_RET = OFF_Z + GDN_HEADS * GDN_DV
OFF_AB = OFF_RET + 4 * RET_HEADS * RET_DK
IN_PAD = OFF_AB + LANES

V7X_VMEM_BYTES = 64 * 1024 * 1024
VMEM_LIMIT = V7X_VMEM_BYTES - 8 * 1024 * 1024
GDN_VMEM_LIMIT = V7X_VMEM_BYTES - 4 * 1024 * 1024

f32 = jnp.float32
bf16 = jnp.bfloat16


def _cparams(sem):
    return pltpu.CompilerParams(dimension_semantics=sem, vmem_limit_bytes=VMEM_LIMIT)


def _const_spec(shape):
    nd = len(shape)
    return pl.BlockSpec(shape, lambda *_: (0,) * nd, pipeline_mode=pl.Buffered(1))


def _layer_spec(stacked, layer):
    nd = stacked.ndim - 1
    return pl.BlockSpec((None,) + stacked.shape[1:], lambda *_: (layer,) + (0,) * nd,
                        pipeline_mode=pl.Buffered(1))


def _rms(x, g):
    return x * lax.rsqrt(jnp.mean(x * x, axis=-1, keepdims=True) + EPS) * g


def _silu(x):
    return x * jax.nn.sigmoid(x)


def _dot(a, b):
    return jnp.dot(a.astype(bf16), b.astype(bf16), preferred_element_type=f32)


def _dot_nt(a, b):
    return lax.dot_general(a.astype(bf16), b.astype(bf16), (((1,), (1,)), ((), ())),
                           preferred_element_type=f32)


def _dot_tn(a, b):
    return lax.dot_general(a.astype(bf16), b.astype(bf16), (((0,), (0,)), ((), ())),
                           preferred_element_type=f32)


def _mod_kernel(c_ref, w_ref, b_ref, o_ref):
    o_ref[...] = _dot(_silu(c_ref[...]), w_ref[...]) + b_ref[...]


def _mod_vectors(cs, w16, b, layer):
    rows, d = cs.shape
    n = w16.shape[2]
    tn = MOD_COLS
    return pl.pallas_call(
        _mod_kernel,
        grid=(n // tn,),
        in_specs=[pl.BlockSpec((rows, d), lambda j: (0, 0)),
                  pl.BlockSpec((None, d, tn), lambda j: (layer, 0, j)),
                  pl.BlockSpec((1, tn), lambda j: (0, j))],
        out_specs=pl.BlockSpec((rows, tn), lambda j: (0, j)),
        out_shape=jax.ShapeDtypeStruct((rows, n), f32),
        compiler_params=_cparams(("parallel",)),
        name="mod_vectors",
    )(cs, w16, b)


def _in_proj_kernel(x_ref, g_ref, mod_ref, w_ref, o_ref):
    x = x_ref[0]
    mod = mod_ref[0]
    h = _rms(x, g_ref[...]) * (1.0 + mod[1:2]) + mod[0:1]
    h16 = h.astype(bf16)
    n = w_ref.shape[1]
    step = PROJ_COLS
    for j in range(0, n, step):
        e = min(j + step, n)
        o_ref[0, :, j:e] = jnp.dot(h16, w_ref[:, j:e], preferred_element_type=f32)


def _in_proj(x, g, mod, w16, layer, tm):
    b, t, d = x.shape
    n = w16.shape[2]
    return pl.pallas_call(
        _in_proj_kernel,
        grid=(b, t // tm),
        in_specs=[pl.BlockSpec((1, tm, d), lambda i, j: (i, j, 0)),
                  _const_spec((1, d)),
                  pl.BlockSpec((1, N_MOD, d), lambda i, j: (i, 0, 0)),
                  _layer_spec(w16, layer)],
        out_specs=pl.BlockSpec((1, tm, n), lambda i, j: (i, j, 0)),
        out_shape=jax.ShapeDtypeStruct((b, t, n), f32),
        compiler_params=_cparams(("parallel", "parallel")),
        name="in_proj",
    )(x, g, mod, w16)


def _out_ffn_kernel(x_ref, hy_ref, gd_ref, rt_ref, mod_ref, wo_ref, g2_ref, w1_ref, w2_ref, gf_ref, o_ref,
                    *, final):
    x = x_ref[0]
    mod = mod_ref[0]
    o0, o1, o2 = HY_WIDTH, HY_WIDTH + GDN_HEADS * GDN_DV, D_MODEL
    y = (jnp.dot(hy_ref[0].astype(bf16), wo_ref[0:o0, :], preferred_element_type=f32)
         + jnp.dot(gd_ref[0].astype(bf16), wo_ref[o0:o1, :], preferred_element_type=f32)
         + jnp.dot(rt_ref[0].astype(bf16), wo_ref[o1:o2, :], preferred_element_type=f32))
    x = x + mod[2:3] * y
    h16 = (_rms(x, g2_ref[...]) * (1.0 + mod[4:5]) + mod[3:4]).astype(bf16)
    step = FFN_COLS
    acc = jnp.zeros_like(x)
    for j in range(0, FFN_HIDDEN, step):
        gate = jnp.dot(h16, w1_ref[:, j:j + step], preferred_element_type=f32)
        up = jnp.dot(h16, w1_ref[:, FFN_HIDDEN + j:FFN_HIDDEN + j + step], preferred_element_type=f32)
        acc = acc + jnp.dot((_silu(gate) * up).astype(bf16), w2_ref[j:j + step, :], preferred_element_type=f32)
    x = x + mod[5:6] * acc
    if final:
        x = _rms(x, gf_ref[...])
    o_ref[0] = x


def _out_ffn(x, hy, gd, rt, mod, wo16, g2, w116, w216, gf, layer, tm, final):
    b, t, d = x.shape
    tok = lambda w: pl.BlockSpec((1, tm, w), lambda i, j: (i, j, 0))
    return pl.pallas_call(
        functools.partial(_out_ffn_kernel, final=final),
        grid=(b, t // tm),
        in_specs=[tok(d), tok(hy.shape[-1]), tok(gd.shape[-1]), tok(rt.shape[-1]),
                  pl.BlockSpec((1, N_MOD, d), lambda i, j: (i, 0, 0)),
                  _layer_spec(wo16, layer), _const_spec((1, d)), _layer_spec(w116, layer),
                  _layer_spec(w216, layer), _const_spec((1, d))],
        out_specs=tok(d),
        out_shape=jax.ShapeDtypeStruct((b, t, d), f32),
        compiler_params=_cparams(("parallel", "parallel")),
        name="out_ffn",
    )(x, hy, gd, rt, mod, wo16, g2, w116, w216, gf)


def _hyena_kernel(v_ref, x1_ref, x2_ref, k_ref, b_ref, o_ref, acc_ref, *, nb, cb):
    hb = HY_BLOCK
    rb = v_ref.shape[1] // nb

    def conv(u16, kext):
        m = range(len(u16))
        for d in range(-(nb - 1), nb):
            dd = d % (2 * nb)
            t0, t1 = max(0, d), min(nb - 1, nb - 1 + d) + 1
            for i in m:
                win = jnp.broadcast_to(kext[i][:, hb * dd:hb * dd + 2 * hb], (hb, 2 * hb))
                toep = pltpu.roll(win, 0, 1, stride=1, stride_axis=0)[:, hb:].astype(bf16)
                contrib = jnp.dot(u16[i][rb * (t0 - d):rb * (t1 - d)], toep, preferred_element_type=f32)
                if d == -(nb - 1):
                    acc_ref[i] = jnp.zeros(acc_ref.shape[1:], f32)
                acc_ref[i, rb * t0:rb * t1, :] += contrib
        return [acc_ref[i] for i in m]

    def body(g, carry):
        cs = [HY_PAIR * g + i for i in range(HY_PAIR)]
        z = [v_ref[c] for c in cs]
        for n, gate_ref in enumerate((x1_ref, x2_ref)):
            y = conv([x.astype(bf16) for x in z], [k_ref[n, pl.ds(c, 1), :] for c in cs])
            z = [gate_ref[c] * (y[i] + z[i] * b_ref[n, pl.ds(c, 1), :]) for i, c in enumerate(cs)]
        for i, c in enumerate(cs):
            o_ref[c] = z[i]
        return carry

    lax.fori_loop(0, cb // HY_PAIR, body, 0)


def _hyena_conv(p3, kext, bias, nb, cb=SUBLANES):
    c3, r, hb = p3.shape
    c = c3 // 3
    klen = kext.shape[-1]
    sig = lambda s: pl.BlockSpec((cb, r, hb), lambda i, s=s: (s * (c // cb) + i, 0, 0))
    return pl.pallas_call(
        functools.partial(_hyena_kernel, nb=nb, cb=cb),
        grid=(c // cb,),
        in_specs=[sig(0), sig(1), sig(2),
                  pl.BlockSpec((HY_ORDER, cb, klen), lambda i: (0, i, 0)),
                  pl.BlockSpec((HY_ORDER, cb, hb), lambda i: (0, i, 0))],
        out_specs=pl.BlockSpec((cb, r, hb), lambda i: (i, 0, 0)),
        out_shape=jax.ShapeDtypeStruct((c, r, hb), f32),
        scratch_shapes=[pltpu.VMEM((HY_PAIR, r, hb), f32)],
        compiler_params=_cparams(("parallel",)),
        name=f"hyena_conv_nb{nb}",
    )(p3, p3, p3, kext, bias)


def _hyena_prep_kernel(x_ref, before_ref, after_ref, w_ref, b_ref, o_ref):
    hb = HY_BLOCK
    w = w_ref[...]
    row = lax.broadcasted_iota(jnp.int32, x_ref.shape[1:], 0)
    s = pl.program_id(1)
    has_before = s > 0
    has_after = s < pl.num_programs(1) - 1
    for i in range(x_ref.shape[0]):
        x = x_ref[i]
        prev = jnp.where(has_before, before_ref[i, SUBLANES - 1:SUBLANES, :], 0.0)
        nxt = jnp.where(has_after, after_ref[i, 0:1, :], 0.0)
        xp = jnp.where(row == 0, prev, pltpu.roll(x, 1, 0))
        xn = jnp.where(row == hb - 1, nxt, pltpu.roll(x, hb - 1, 0))
        y = xp * w[0:1] + x * w[1:2] + xn * w[2:3] + b_ref[...]
        o_ref[:, i, :] = y.T


def _hyena_prep(p, conv_w, conv_b):
    b, t, _ = p.shape
    hb = HY_BLOCK
    n = 3 * HY_WIDTH
    per = hb // SUBLANES
    last = t // SUBLANES - 1
    assert OFF_HY == 0 and b % SUBLANES == 0
    return pl.pallas_call(
        _hyena_prep_kernel,
        grid=(n // HY_IO, t // hb),
        in_specs=[pl.BlockSpec((b, hb, HY_IO), lambda j, s: (0, s, j)),
                  pl.BlockSpec((b, SUBLANES, HY_IO), lambda j, s: (0, jnp.maximum(s * per - 1, 0), j)),
                  pl.BlockSpec((b, SUBLANES, HY_IO), lambda j, s: (0, jnp.minimum((s + 1) * per, last), j)),
                  pl.BlockSpec((3, HY_IO), lambda j, s: (0, j)),
                  pl.BlockSpec((1, HY_IO), lambda j, s: (0, j))],
        out_specs=pl.BlockSpec((HY_IO, b, hb), lambda j, s: (j, s, 0)),
        out_shape=jax.ShapeDtypeStruct((n, (t // hb) * b, hb), f32),
        compiler_params=_cparams(("parallel", "parallel")),
        name="hyena_prep",
    )(p, p, p, conv_w, conv_b)


def _hyena_post_kernel(z_ref, o_ref):
    for i in range(z_ref.shape[1]):
        o_ref[i] = z_ref[:, i, :].T


def _hyena_post(z, b):
    c, r, hb = z.shape
    nb = r // b
    return pl.pallas_call(
        _hyena_post_kernel,
        grid=(c // HY_IO, nb),
        in_specs=[pl.BlockSpec((HY_IO, b, hb), lambda j, s: (j, s, 0))],
        out_specs=pl.BlockSpec((b, hb, HY_IO), lambda j, s: (0, s, j)),
        out_shape=jax.ShapeDtypeStruct((b, nb * hb, c), f32),
        compiler_params=_cparams(("parallel", "parallel")),
        name="hyena_post",
    )(z)


def _hyena_filter_kernel(z_ref, w1_ref, b1_ref, f1_ref, w2_ref, b2_ref, f2_ref, w3_ref, win_ref, o_ref):
    hi = lax.Precision.HIGHEST
    hb = HY_BLOCK
    t = z_ref.shape[1]
    for side in range(2):
        h = jnp.sin(f1_ref[...] * (jnp.dot(z_ref[side], w1_ref[...], precision=hi,
                                           preferred_element_type=f32) + b1_ref[...]))
        h = jnp.sin(f2_ref[...] * (jnp.dot(h, w2_ref[...], precision=hi, preferred_element_type=f32)
                                   + b2_ref[...]))
        for n in range(HY_ORDER):
            w3 = w3_ref[:, (2 * n + side) * HY_WIDTH:(2 * n + side + 1) * HY_WIDTH]
            k = lax.dot_general(w3, h, (((0,), (1,)), ((), ())), precision=hi,
                                preferred_element_type=f32) * win_ref[side]
            o_ref[n, :, hb + side * t:hb + (side + 1) * t] = k
            if side == 1:
                o_ref[n, :, 0:hb] = k[:, t - hb:]


def _hyena_tables(t_len):
    pos = np.arange(t_len, dtype=np.float64)
    t = np.linspace(0.0, 1.0, t_len)
    bands = (HY_EMB - 1) // 2
    f = np.linspace(1e-4, bands - 1, bands)
    ang = (2.0 * math.pi / t_len) * pos[:, None] * f[None, :]
    z = np.concatenate([t[:, None], np.cos(ang), -np.sin(ang)], axis=-1)
    max_decay = math.log(HY_TARGET) / HY_FAST_DECAY
    min_decay = math.log(HY_TARGET) / HY_SLOW_DECAY
    deltas = np.abs(np.linspace(min_decay, max_decay, HY_WIDTH))
    window = np.exp(-t[:, None] * deltas[None, :])
    neg = lambda x: np.concatenate([x[:1], x[:0:-1]], axis=0)
    zs = np.pad(np.stack([z, neg(z)]), ((0, 0), (0, 0), (0, LANES - HY_EMB)))
    wneg = neg(window)
    wneg[0] = 0.0
    win = np.stack([window, wneg]).transpose(0, 2, 1)
    return jnp.asarray(zs, f32), jnp.asarray(win, f32)


def _hyena_filters(t_len, lp):
    zs, win = _hyena_tables(t_len)
    w1 = jnp.pad(lp['hy_f_w1'], ((0, LANES - HY_EMB), (0, 0)))
    row = lambda x: x[None]
    args = (zs, w1, row(lp['hy_f_b1']), row(lp['hy_f_freq1']), lp['hy_f_w2'], row(lp['hy_f_b2']),
            row(lp['hy_f_freq2']), lp['hy_f_w3'], win)
    klen = 2 * t_len + HY_BLOCK
    return pl.pallas_call(
        _hyena_filter_kernel,
        grid=(1,),
        in_specs=[_const_spec(a.shape) for a in args],
        out_specs=pl.BlockSpec((HY_ORDER, HY_WIDTH, klen), lambda i: (0, 0, 0)),
        out_shape=jax.ShapeDtypeStruct((HY_ORDER, HY_WIDTH, klen), f32),
        compiler_params=_cparams(("arbitrary",)),
        name="hyena_filters",
    )(*args)


def _gdn_kernel(ql_ref, kl_ref, vl_ref, zl_ref, abl_ref, qc_ref, kc_ref, vc_ref, zc_ref, abc_ref,
                cwq_ref, cwk_ref, cwv_ref, pa_ref, pd_ref, ng_ref,
                ol_ref, oc_ref,
                q_s, k_s, v_s, cf_s, u_s, wq_s, kgt_s, at_s, egl_s, o_s, st_s):
    c = GDN_CHUNK
    rb = GDN_ROWS
    tc, tl = qc_ref.shape[1], ql_ref.shape[1]
    ncc, nc = tc // c, (tc + tl) // c
    dk = GDN_DK

    def conv_rows(x_ref, w_ref, r0, t, ls):
        x = x_ref[0, pl.ds(r0, rb), ls]
        before = pl.multiple_of(jnp.maximum(r0 - SUBLANES, 0), SUBLANES)
        after = pl.multiple_of(jnp.minimum(r0 + rb, t - SUBLANES), SUBLANES)
        prev = x_ref[0, pl.ds(before, SUBLANES), ls][SUBLANES - 1:SUBLANES]
        nxt = x_ref[0, pl.ds(after, SUBLANES), ls][0:1]
        prev = jnp.where(r0 > 0, prev, 0.0)
        nxt = jnp.where(r0 + rb < t, nxt, 0.0)
        row = lax.broadcasted_iota(jnp.int32, x.shape, 0)
        xp = jnp.where(row == 0, prev, pltpu.roll(x, 1, 0))
        xn = jnp.where(row == rb - 1, nxt, pltpu.roll(x, rb - 1, 0))
        w = w_ref[:, ls]
        return _silu(xp * w[0:1] + x * w[1:2] + xn * w[2:3])

    def l2n(x):
        return x * lax.rsqrt(jnp.sum(x * x, axis=-1, keepdims=True) + EPS)

    def prep(hh, q_ref, k_ref, v_ref, t, base):
        ls = slice(hh * LANES, (hh + 1) * LANES)

        def body(i, carry):
            r0 = pl.multiple_of(i * rb, rb)
            dst = pl.ds(pl.multiple_of(base + i * rb, rb), rb)
            q_s[dst, :] = l2n(conv_rows(q_ref, cwq_ref, r0, t, ls)) * (dk ** -0.5)
            k_s[dst, :] = l2n(conv_rows(k_ref, cwk_ref, r0, t, ls))
            v_s[dst, :] = conv_rows(v_ref, cwv_ref, r0, t, ls)
            return carry
        lax.fori_loop(0, t // rb, body, 0)

    def gates(ab_ref, t, base):
        ab = jnp.concatenate([ab_ref[0, hh] for hh in range(GDN_HP)], axis=0)
        kind = lax.broadcasted_iota(jnp.int32, ab.shape, 0) % 4
        pos = lax.broadcasted_iota(jnp.int32, ab.shape, 1) % c
        tile = lambda p: jnp.concatenate([p] * (t // LANES), axis=1)
        g = -jnp.exp(tile(pa_ref[0])) * jax.nn.softplus(ab + tile(pd_ref[0]))
        acc_f, acc_b = g, g
        s = 1
        while s < c:
            acc_f = acc_f + jnp.where(pos >= s, pltpu.roll(acc_f, s, 1), 0.0)
            acc_b = acc_b + jnp.where(pos < c - s, pltpu.roll(acc_b, t - s, 1), 0.0)
            s *= 2
        cf_t = jnp.where(kind == 0, acc_f, jnp.where(kind == 1, acc_b, jax.nn.sigmoid(ab)))
        pad = jnp.zeros((LANES - SUBLANES, LANES), f32)
        for j in range(t // LANES):
            blk = jnp.concatenate([cf_t[:, j * LANES:(j + 1) * LANES], pad], axis=0)
            cf_s[base + j * LANES:base + (j + 1) * LANES, :] = blk.T

    gates(abc_ref, tc, 0)
    gates(abl_ref, tl, tc)

    lane = lax.broadcasted_iota(jnp.int32, (c, 2 * c), 1)
    row = lax.broadcasted_iota(jnp.int32, (c, 2 * c), 0)
    fwd = lane < c
    col = lane % c
    incl = (fwd & (row >= col)) | (~fwd & (row <= col))
    strict = (fwd & (row > col)) | (~fwd & (row < col))

    def blockdiag(a):
        return jnp.concatenate([jnp.where(fwd, a, 0.0), jnp.where(fwd, 0.0, a)], axis=0).astype(bf16)

    def chunk_group(hh, cis):
        n = range(len(cis))
        xf, xb = 2 * hh, 2 * hh + 1
        rows = [pl.ds(pl.multiple_of(ci * c, c), c) for ci in cis]
        cf = [cf_s[r, :] for r in rows]
        k = [k_s[r, :] for r in rows]
        q = [q_s[r, :] for r in rows]
        l0 = 4 * hh
        bc = lambda x, j: jnp.broadcast_to(x[:, l0 + j:l0 + j + 1], (c, LANES))
        gf_c, gb_c = [bc(x, 0) for x in cf], [bc(x, 1) for x in cf]
        bf_c, bb_c = [bc(x, 2) for x in cf], [bc(x, 3) for x in cf]
        cf_t = [jnp.concatenate([x, x], axis=0).T for x in cf]
        g_r = [jnp.where(fwd, jnp.broadcast_to(x[l0:l0 + 1], (c, 2 * c)),
                         jnp.broadcast_to(x[l0 + 1:l0 + 2], (c, 2 * c))) for x in cf_t]
        decay = [jnp.where(incl, jnp.exp(jnp.where(incl, jnp.where(fwd, gf_c[g], gb_c[g]) - g_r[g], 0.0)), 0.0)
                 for g in n]
        kq = [_dot_nt(jnp.concatenate([k[g], q[g]], axis=0), jnp.concatenate([k[g], k[g]], axis=0))
              for g in n]
        lm = [jnp.where(strict, kq[g][:c] * jnp.where(fwd, bf_c[g], bb_c[g]) * decay[g], 0.0) for g in n]
        for g in n:
            attn = (kq[g][c:] * decay[g]).astype(bf16)
            at_s[xf, cis[g]] = attn[:, :c]
            at_s[xb, cis[g]] = attn[:, c:]
        def split(x):
            h = x.astype(bf16)
            return h, (x - h.astype(f32)).astype(bf16)

        def times_blockdiag(lhs, x):
            lh, ll = split(lhs)
            xh, xl = split(x)
            m = lhs.shape[0]
            main = jnp.dot(jnp.concatenate([lh, ll], axis=0), blockdiag(xh), preferred_element_type=f32)
            return main[:m] + main[m:] + jnp.dot(lh, blockdiag(xl), preferred_element_type=f32)

        a = [times_blockdiag(x, x) for x in lm]
        tp = [-x for x in lm]
        p = 2
        while 2 * p < c:
            ta = [times_blockdiag(jnp.concatenate([tp[g], a[g]], axis=0), a[g]) for g in n]
            tp = [tp[g] + a[g] + ta[g][:c] for g in n]
            a = [x[c:] for x in ta]
            p *= 2
        tp = [tp[g] + a[g] + times_blockdiag(tp[g], a[g]) for g in n]
        eg_f, eg_b = [jnp.exp(x) for x in gf_c], [jnp.exp(x) for x in gb_c]
        uw = []
        for g in n:
            v = v_s[rows[g], :]
            rhs = jnp.concatenate([jnp.concatenate([v * bf_c[g], k[g] * (bf_c[g] * eg_f[g])], axis=1),
                                   jnp.concatenate([v * bb_c[g], k[g] * (bb_c[g] * eg_b[g])], axis=1)], axis=0)
            uw.append(rhs + jnp.dot(blockdiag(tp[g]), rhs.astype(bf16), preferred_element_type=f32))
        for g in n:
            ci, r = cis[g], rows[g]
            u_s[xf, r, :] = uw[g][:c, :dk]
            u_s[xb, r, :] = uw[g][c:, :dk]
            wq_s[xf, ci] = jnp.concatenate([uw[g][:c, dk:], q[g] * eg_f[g]], axis=0).astype(bf16)
            wq_s[xb, ci] = jnp.concatenate([uw[g][c:, dk:], q[g] * eg_b[g]], axis=0).astype(bf16)
            gl_f, gl_b = gf_c[g][c - 1:c], gb_c[g][0:1]
            kgt_s[xf, ci] = (k[g] * jnp.exp(gl_f - gf_c[g])).T.astype(bf16)
            kgt_s[xb, ci] = (k[g] * jnp.exp(gl_b - gb_c[g])).T.astype(bf16)
            egl_s[hh, ci] = jnp.concatenate([jnp.exp(gl_f), jnp.exp(gl_b),
                                             jnp.zeros((SUBLANES - 2, LANES), f32)], axis=0)
            o_s[hh, r, :] = jnp.zeros((c, GDN_DV), f32)

    for hh in range(GDN_HP):
        prep(hh, qc_ref, kc_ref, vc_ref, tc, 0)
        prep(hh, ql_ref, kl_ref, vl_ref, tl, tc)

        def chunk_body(i, carry, hh=hh):
            chunk_group(hh, [GDN_GROUP * i + g for g in range(GDN_GROUP)])
            return carry

        lax.fori_loop(0, nc // GDN_GROUP, chunk_body, 0)

    st_s[...] = jnp.zeros_like(st_s)
    chains = [(hh, d) for hh in range(GDN_HP) for d in range(2)]

    def scan_body(i, carry):
        ci = (i, jnp.where(i < ncc, ncc - 1 - i, nc + ncc - 1 - i))
        rows = [pl.ds(pl.multiple_of(x * c, c), c) for x in ci]
        n = range(len(chains))
        s = [st_s[x] for x in n]
        ws = [jnp.dot(wq_s[x, ci[chains[x][1]]], s[x].astype(bf16), preferred_element_type=f32) for x in n]
        v_new = [(u_s[x, rows[chains[x][1]], :] - ws[x][:c]).astype(bf16) for x in n]
        upd = [jnp.dot(kgt_s[x, ci[chains[x][1]]], v_new[x], preferred_element_type=f32) for x in n]
        out = [ws[x][c:] + jnp.dot(at_s[x, ci[chains[x][1]]], v_new[x], preferred_element_type=f32) for x in n]
        for x in n:
            hh, d = chains[x]
            st_s[x] = s[x] * egl_s[hh, ci[d]][d:d + 1] + upd[x]
            o_s[hh, rows[d], :] += out[x]
        return carry

    lax.fori_loop(0, nc, scan_body, 0)

    def finish(hh, z_ref, o_ref, t, base):
        ls = slice(hh * LANES, (hh + 1) * LANES)

        def body(i, carry):
            r0 = pl.ds(pl.multiple_of(i * rb, rb), rb)
            o = o_s[hh, pl.ds(pl.multiple_of(base + i * rb, rb), rb), :]
            y = o * lax.rsqrt(jnp.mean(o * o, axis=-1, keepdims=True) + EPS) * ng_ref[...]
            o_ref[0, r0, ls] = y * _silu(z_ref[0, r0, ls])
            return carry
        lax.fori_loop(0, t // rb, body, 0)

    for hh in range(GDN_HP):
        finish(hh, zc_ref, oc_ref, tc, 0)
        finish(hh, zl_ref, ol_ref, tl, tc)


def _gdn(p_l, ab_l, p_c, ab_c, conv_w, pa, pd, norm_g):
    b, tl, _ = p_l.shape
    tc = p_c.shape[1]
    h, c, hp = GDN_HEADS, GDN_CHUNK, GDN_HP
    nc = (tl + tc) // c
    assert tl % GDN_ROWS == 0 and tc % GDN_ROWS == 0 and nc % GDN_GROUP == 0
    qb, zb = OFF_QKV // LANES, OFF_Z // LANES
    assert h % hp == 0 and qb % hp == 0 and zb % hp == 0
    w = hp * LANES

    def specs(t):
        heads = lambda off: pl.BlockSpec((1, t, w), lambda i, j, off=off // hp: (i, 0, off + j))
        return [heads(qb), heads(qb + h), heads(qb + 2 * h), heads(zb),
                pl.BlockSpec((1, hp, 4, t), lambda i, j: (i, j, 0, 0))]

    cw = lambda off: pl.BlockSpec((3, w), lambda i, j, off=off // hp: (0, off + j))
    assert 4 * hp == SUBLANES
    par = pl.BlockSpec((1, SUBLANES, LANES), lambda i, j: (j, 0, 0))
    out = lambda t: pl.BlockSpec((1, t, w), lambda i, j: (i, 0, j))
    tt = tl + tc
    return pl.pallas_call(
        _gdn_kernel,
        grid=(b, h // hp),
        in_specs=specs(tl) + specs(tc) + [cw(0), cw(h), cw(2 * h), par, par, _const_spec((1, GDN_DV))],
        out_specs=[out(tl), out(tc)],
        out_shape=[jax.ShapeDtypeStruct((b, tl, h * GDN_DV), f32),
                   jax.ShapeDtypeStruct((b, tc, h * GDN_DV), f32)],
        scratch_shapes=[pltpu.VMEM((tt, GDN_DK), f32), pltpu.VMEM((tt, GDN_DK), f32),
                        pltpu.VMEM((tt, GDN_DV), f32), pltpu.VMEM((tt, LANES), f32),
                        pltpu.VMEM((2 * hp, tt, GDN_DV), f32), pltpu.VMEM((2 * hp, nc, 2 * c, GDN_DK), bf16),
                        pltpu.VMEM((2 * hp, nc, GDN_DK, c), bf16), pltpu.VMEM((2 * hp, nc, c, c), bf16),
                        pltpu.VMEM((hp, nc, SUBLANES, LANES), f32), pltpu.VMEM((hp, tt, GDN_DV), f32),
                        pltpu.VMEM((2 * hp, GDN_DK, GDN_DV), f32)],
        compiler_params=pltpu.CompilerParams(dimension_semantics=("parallel", "parallel"),
                                             vmem_limit_bytes=GDN_VMEM_LIMIT),
        name="gated_deltanet",
    )(p_l, p_l, p_l, p_l, ab_l, p_c, p_c, p_c, p_c, ab_c, conv_w, conv_w, conv_w, pa, pd, norm_g)


def _ret_kernel(ql_ref, kl_ref, vl_ref, gl_ref, qc_ref, kc_ref, vc_ref, gc_ref,
                cos_ref, sin_ref, dsum_ref, sc_ref, gch_ref,
                ol_ref, oc_ref, sf_ref, sb_ref, q_s, k_s, o_s):
    c = RET_CHUNK
    w = RET_HEADS * RET_DK
    ri = lax.broadcasted_iota(jnp.int32, (w, w), 0) // RET_DK
    ci = lax.broadcasted_iota(jnp.int32, (w, w), 1) // RET_DK
    blockdiag = ri == ci
    lane_head = lax.broadcasted_iota(jnp.int32, (c, w), 1) // RET_DK

    def intra(q, k, v, r, cross):
        k16 = k.astype(bf16)
        for h in range(RET_HEADS):
            m = lane_head == h
            scores = _dot_nt(jnp.where(m, q, 0.0), k16) * dsum_ref[h]
            part = _dot(scores, jnp.where(m, v, 0.0))
            if h == 0:
                o_s[r, :] = part if cross is None else part + cross
            else:
                o_s[r, :] += part

    def update(s_ref, k, v, d):
        new = jnp.where(blockdiag, _dot_tn(k * sc_ref[2 * d + 1], v), 0.0)
        s_ref[...] = s_ref[...] * gch_ref[d] + new

    first_half = lax.broadcasted_iota(jnp.int32, (c, w), 1) % RET_DK < RET_DK // 2
    avg = jnp.where(blockdiag, 1.0 / RET_DV, 0.0).astype(bf16)

    def rope(x, r):
        swapped = jnp.where(first_half, pltpu.roll(x, w - RET_DK // 2, 1), pltpu.roll(x, RET_DK // 2, 1))
        return x * cos_ref[r, :] + swapped * sin_ref[r, :]

    def head_mean(x):
        hi = x.astype(bf16)
        lo = (x - hi.astype(f32)).astype(bf16)
        return jnp.dot(hi, avg, preferred_element_type=f32) + jnp.dot(lo, avg, preferred_element_type=f32)

    def finish(o, gate):
        d = o - head_mean(o)
        return d * lax.rsqrt(head_mean(d * d) + EPS) * _silu(gate)

    sf_ref[...] = jnp.zeros_like(sf_ref)
    sb_ref[...] = jnp.zeros_like(sb_ref)
    qc, kc, vc = qc_ref[0], kc_ref[0] * (RET_DK ** -0.5), vc_ref[0]
    r0 = slice(0, c)
    intra(qc, kc, vc, r0, None)
    oc_ref[0] = finish(o_s[r0, :], gc_ref[0])
    update(sf_ref, kc, vc, 0)
    update(sb_ref, kc, vc, 1)
    n = ql_ref.shape[1] // c
    for i in range(n):
        r = slice(i * c, (i + 1) * c)
        q = rope(ql_ref[0, r, :], r)
        k = rope(kl_ref[0, r, :] * (RET_DK ** -0.5), r)
        v = vl_ref[0, r, :]
        q_s[r, :] = q
        k_s[r, :] = k
        intra(q, k, v, r, _dot(q * sc_ref[0], sf_ref[...]))
        update(sf_ref, k, v, 0)
    for i in range(n - 1, -1, -1):
        r = slice(i * c, (i + 1) * c)
        o = o_s[r, :] + _dot(q_s[r, :] * sc_ref[2], sb_ref[...])
        ol_ref[0, r, :] = finish(o, gl_ref[0, r, :])
        update(sb_ref, k_s[r, :], vl_ref[0, r, :], 1)


def _retention(p_l, p_c, cos, sin, dsum, scales, gch):
    b, tl, _ = p_l.shape
    tc = p_c.shape[1]
    w = RET_HEADS * RET_DK
    assert OFF_RET % w == 0
    blk = lambda t, j: pl.BlockSpec((1, t, w), lambda i, j=j: (i, 0, OFF_RET // w + j))
    tok = lambda t: pl.BlockSpec((1, t, w), lambda i: (i, 0, 0))
    return pl.pallas_call(
        _ret_kernel,
        grid=(b,),
        in_specs=[blk(tl, j) for j in range(4)] + [blk(tc, j) for j in range(4)]
        + [_const_spec(x.shape) for x in (cos, sin, dsum, scales, gch)],
        out_specs=[tok(tl), tok(tc)],
        out_shape=[jax.ShapeDtypeStruct((b, tl, w), f32), jax.ShapeDtypeStruct((b, tc, w), f32)],
        scratch_shapes=[pltpu.VMEM((w, w), f32), pltpu.VMEM((w, w), f32),
                        pltpu.VMEM((tl, w), f32), pltpu.VMEM((tl, w), f32), pltpu.VMEM((tl, w), f32)],
        compiler_params=_cparams(("parallel",)),
        name="retention",
    )(p_l, p_l, p_l, p_l, p_c, p_c, p_c, p_c, cos, sin, dsum, scales, gch)


def _hyena(p, lp):
    b, t, _ = p.shape
    hb, c = HY_BLOCK, HY_WIDTH
    nb = t // hb
    p3 = _hyena_prep(p, lp['hy_conv_w'], lp['hy_conv_b'][None])
    bias = jnp.broadcast_to(lp['hy_bias'][:, :, None], (HY_ORDER, c, hb))
    z = _hyena_conv(p3, _hyena_filters(t, lp), bias, nb)
    return _hyena_post(z, b)


def _gdn_gate_inputs(p):
    b, t, _ = p.shape
    ab = p[..., OFF_AB:OFF_AB + 4 * GDN_HEADS].reshape(b, t, 4, GDN_HEADS)
    return ab.transpose(0, 3, 2, 1)


def _gdn_gate_params(a_log, dt_bias):
    def rows(x):
        x = jnp.pad(x.T, ((0, 0), (0, 2))).reshape(GDN_HEADS // GDN_HP, 4 * GDN_HP)
        return jnp.broadcast_to(x[:, :, None], x.shape + (LANES,))
    return rows(a_log), rows(dt_bias)


def _rope_tables(t_lat):
    rows_n = t_lat // GRID_W
    row = np.repeat(np.arange(rows_n, dtype=np.float64), GRID_W)
    colp = np.tile(np.arange(GRID_W, dtype=np.float64), rows_n)
    nf = RET_DK // 4
    inv = ROPE_BASE ** (-np.arange(nf, dtype=np.float64) / nf)
    ang = np.concatenate([row[:, None] * inv, colp[:, None] * inv], axis=-1)
    cos, sin = np.cos(ang), np.sin(ang)
    return (jnp.asarray(np.tile(np.concatenate([cos, cos], axis=-1), (1, RET_HEADS)), f32),
            jnp.asarray(np.tile(np.concatenate([-sin, sin], axis=-1), (1, RET_HEADS)), f32))


def _ret_constants(lp):
    c = RET_CHUNK
    lg = jax.nn.log_sigmoid(lp['ret_decay_logit'])
    idx = jnp.arange(c, dtype=f32)
    rel = idx[:, None] - idx[None, :]
    d_f = jnp.exp(jnp.where(rel >= 0, rel * lg[0][:, None, None], -jnp.inf))
    d_b = jnp.exp(jnp.where(rel <= 0, -rel * lg[1][:, None, None], -jnp.inf))
    lane = lambda x: jnp.repeat(x, RET_DK, axis=-1)
    scales = jnp.stack([lane(jnp.exp((idx + 1.0)[:, None] * lg[0])),
                        lane(jnp.exp((c - 1.0 - idx)[:, None] * lg[0])),
                        lane(jnp.exp((c - idx)[:, None] * lg[1])),
                        lane(jnp.exp(idx[:, None] * lg[1]))])
    gch = lane(jnp.exp(c * lg))[:, None, :]
    return d_f + d_b, scales, gch


def _reorder_w_in(w):
    h2 = 2 * GDN_HEADS
    a0 = OFF_Z + GDN_HEADS * GDN_DV
    parts = [w[..., :a0], w[..., a0 + 2 * h2:], w[..., a0:a0 + 2 * h2],
             jnp.zeros(w.shape[:-1] + (LANES - 2 * h2,), w.dtype)]
    return jnp.concatenate(parts, axis=-1)


def kernel(x, c, ctx, c_ctx, mod_w, mod_b, norm1_g, w_in, hy_conv_w, hy_conv_b, hy_f_w1, hy_f_b1, hy_f_freq1, hy_f_w2, hy_f_b2, hy_f_freq2, hy_f_w3, hy_bias, gdn_conv_w, gdn_a_log, gdn_dt_bias, gdn_norm_g, ret_decay_logit, w_out, norm2_g, ffn_w_in, ffn_w_out, final_norm_g):
    bsz, t_lat, d = x.shape
    cos, sin = _rope_tables(t_lat)
    pad_rows = (-(bsz + 1)) % SUBLANES
    cs = jnp.concatenate([c, c_ctx[None], jnp.zeros((pad_rows, d), f32)], axis=0)
    gf = final_norm_g[None]
    mod_w16, w_in16 = mod_w.astype(bf16), _reorder_w_in(w_in).astype(bf16)
    wo16, w116, w216 = w_out.astype(bf16), ffn_w_in.astype(bf16), ffn_w_out.astype(bf16)
    for i in range(DEPTH):
        need_ctx = i < DEPTH - 1
        lp = {'hy_conv_w': hy_conv_w[i], 'hy_conv_b': hy_conv_b[i], 'hy_f_w1': hy_f_w1[i],
              'hy_f_b1': hy_f_b1[i], 'hy_f_freq1': hy_f_freq1[i], 'hy_f_w2': hy_f_w2[i],
              'hy_f_b2': hy_f_b2[i], 'hy_f_freq2': hy_f_freq2[i], 'hy_f_w3': hy_f_w3[i],
              'hy_bias': hy_bias[i], 'ret_decay_logit': ret_decay_logit[i]}
        mod = _mod_vectors(cs, mod_w16, mod_b[i][None], i)
        mod_lat = mod[:bsz].reshape(bsz, N_MOD, d)
        mod_ctx = jnp.broadcast_to(mod[bsz].reshape(1, N_MOD, d), (bsz, N_MOD, d))
        g1 = norm1_g[i][None]
        p_l = _in_proj(x, g1, mod_lat, w_in16, i, 512)
        p_c = _in_proj(ctx, g1, mod_ctx, w_in16, i, 256)

        hy_l = _hyena(p_l, lp)
        pa, pd = _gdn_gate_params(gdn_a_log[i], gdn_dt_bias[i])
        gd_l, gd_c = _gdn(p_l, _gdn_gate_inputs(p_l), p_c, _gdn_gate_inputs(p_c), gdn_conv_w[i], pa, pd,
                          gdn_norm_g[i][None])

        dsum, scales, gch = _ret_constants(lp)
        rt_l, rt_c = _retention(p_l, p_c, cos, sin, dsum, scales, gch)

        g2 = norm2_g[i][None]
        x = _out_ffn(x, hy_l, gd_l, rt_l, mod_lat, wo16, g2, w116, w216, gf, i, 512, final=not need_ctx)
        if need_ctx:
            hy_c = _hyena(p_c, lp)
            ctx = _out_ffn(ctx, hy_c, gd_c, rt_c, mod_ctx, wo16, g2, w116, w216, gf, i, 256, final=False)
    return x
```
